```python
import math
import jax, jax.numpy as jnp
from jax import lax
import numpy as np

D_MODEL = 1024
BATCH = 32
SEQ = 2048
DEPTH = 4

SSM_HEAD_DIM = 64
SSM_HEADS = D_MODEL // SSM_HEAD_DIM
SSM_WIDTH = SSM_HEADS * SSM_HEAD_DIM
SSM_GROUPS = 2
SSM_STATE = 128
SSM_CONV = 7
SSM_CHUNK = 128
ATTN_HEAD_DIM = 64
ATTN_HEADS = D_MODEL // ATTN_HEAD_DIM
ATTN_KV_HEADS = ATTN_HEADS // 4
ATTN_WIDTH = ATTN_HEADS * ATTN_HEAD_DIM
KV_WIDTH = ATTN_KV_HEADS * ATTN_HEAD_DIM
WINDOW = 128
ATTN_BLOCK = 128
KEY_SPAN = ATTN_BLOCK + 2 * WINDOW
REL_BUCKETS = 32
REL_MAX_DIST = 128
MIX_WIDTH = SSM_WIDTH + ATTN_WIDTH
D_FF = 256 * ((8 * D_MODEL // 3 + 255) // 256)
FFN_CONV = 3
NORM_EPS = 1e-6

BC_WIDTH = SSM_GROUPS * SSM_STATE
CONV_CH = SSM_WIDTH + 2 * BC_WIDTH
Z_END = SSM_WIDTH
XBC_END = Z_END + CONV_CH
DT_END = XBC_END + 2 * SSM_HEADS
Q_END = DT_END + ATTN_WIDTH
K_END = Q_END + KV_WIDTH
IN_COLS = K_END + KV_WIDTH

kernel_name = "hymba_ssd_swa_convffn_encoder"


def rms_norm(x, w):
    xf = x.astype(jnp.float32)
    y = xf * lax.rsqrt(jnp.mean(xf * xf, axis=-1, keepdims=True) + NORM_EPS)
    return (y * w.astype(jnp.float32)).astype(x.dtype)


def depthwise_conv_centered(x, w, b):
    k, ch = w.shape
    pad = k // 2
    y = lax.conv_general_dilated(
        x, w[:, None, :].astype(x.dtype), window_strides=(1,), padding=[(pad, pad)],
        dimension_numbers=("NWC", "WIO", "NWC"), feature_group_count=ch)
    return y + b.astype(x.dtype)


def t5_bucket(rel):
    half = REL_BUCKETS // 2
    max_exact = half // 2
    ret = jnp.where(rel > 0, half, 0)
    n = jnp.abs(rel)
    nf = jnp.maximum(n, 1).astype(jnp.float32)
    large = max_exact + (jnp.log(nf / max_exact) / math.log(REL_MAX_DIST / max_exact)
                         * (half - max_exact)).astype(jnp.int32)
    large = jnp.minimum(large, half - 1)
    return ret + jnp.where(n < max_exact, n, large)


def ssd_chunked(x, dt, a, b, c):
    f32 = jnp.float32
    bsz, seq, nh, hp = x.shape
    ng, ns = b.shape[-2:]
    rep = nh // ng
    nc, cl = seq // SSM_CHUNK, SSM_CHUNK
    xdt = (x.astype(f32) * dt[..., None]).reshape(bsz, nc, cl, ng, rep, hp)
    a_dt = (dt * a).reshape(bsz, nc, cl, ng, rep).transpose(0, 3, 4, 1, 2)
    a_cum = jnp.cumsum(a_dt, axis=-1)
    b = b.astype(f32).reshape(bsz, nc, cl, ng, ns)
    c = c.astype(f32).reshape(bsz, nc, cl, ng, ns)
    seg = a_cum[..., :, None] - a_cum[..., None, :]
    lower = jnp.tril(jnp.ones((cl, cl), dtype=bool))
    decay = jnp.exp(jnp.where(lower, seg, -jnp.inf))
    cb = jnp.einsum("bclgn,bcsgn->bgcls", c, b)
    mix = cb[:, :, None] * decay
    y_diag = jnp.einsum("bgrcls,bcsgrp->bclgrp", mix, xdt)
    decay_states = jnp.exp(a_cum[..., -1:] - a_cum)
    states = jnp.einsum("bclgn,bgrcl,bclgrp->cbgrpn", b, decay_states, xdt)
    chunk_decay = jnp.exp(a_cum[..., -1]).transpose(3, 0, 1, 2)

    def step(h, inp):
        st, dec = inp
        return h * dec[..., None, None] + st, h

    h0 = jnp.zeros((bsz, ng, rep, hp, ns), f32)
    _, prev = lax.scan(step, h0, (states, chunk_decay))
    y_off = jnp.einsum("bclgn,cbgrpn,bgrcl->bclgrp", c, prev, jnp.exp(a_cum))
    return (y_diag + y_off).reshape(bsz, seq, nh, hp)


def windowed_gqa(q, k, v, sink, band_bias):
    bsz, seq, nh, hd = q.shape
    nkv = k.shape[2]
    rep = nh // nkv
    nblk = seq // ATTN_BLOCK
    qb_all = q.reshape(bsz, nblk, ATTN_BLOCK, nkv, rep, hd).transpose(1, 0, 2, 3, 4, 5)
    kp = jnp.pad(k, ((0, 0), (WINDOW, WINDOW), (0, 0), (0, 0)))
    vp = jnp.pad(v, ((0, 0), (WINDOW, WINDOW), (0, 0), (0, 0)))
    rel = jnp.arange(KEY_SPAN)[None, :] - WINDOW - jnp.arange(ATTN_BLOCK)[:, None]
    band = jnp.abs(rel) <= WINDOW
    bias = band_bias.reshape(nkv, rep, ATTN_BLOCK, KEY_SPAN)
    sink_l = sink.astype(jnp.float32).reshape(nkv, rep, 1, 1)
    scale = hd ** -0.5

    def one_block(args):
        qb, n = args
        start = n * ATTN_BLOCK
        kb = lax.dynamic_slice_in_dim(kp, start, KEY_SPAN, axis=1)
        vb = lax.dynamic_slice_in_dim(vp, start, KEY_SPAN, axis=1)
        kpos = start - WINDOW + jnp.arange(KEY_SPAN)
        valid = band & ((kpos >= 0) & (kpos < seq))[None, :]
        s = jnp.einsum("bqgrd,bkgd->bgrqk", qb, kb,
                       preferred_element_type=jnp.float32) * scale + bias
        s = jnp.where(valid, s, -jnp.inf)
        m = jnp.maximum(jnp.max(s, axis=-1, keepdims=True), sink_l)
        p = jnp.exp(s - m)
        denom = jnp.sum(p, axis=-1, keepdims=True) + jnp.exp(sink_l - m)
        return jnp.einsum("bgrqk,bkgd->bqgrd", (p / denom).astype(vb.dtype), vb)

    out = lax.map(one_block, (qb_all, jnp.arange(nblk)))
    return out.transpose(1, 0, 2, 3, 4, 5).reshape(bsz, seq, nh * hd)


def _fwd_setup_inputs(seed: int = 0) -> dict:
    key = jax.random.key(seed)
    ks = jax.random.split(key, 20)
    f32 = jnp.float32

    def nrm(k, shape, scale):
        return jax.random.normal(k, shape, f32) * scale

    dt0 = jnp.exp(jax.random.uniform(ks[6], (DEPTH, 2, SSM_HEADS), f32,
                                     math.log(1e-3), math.log(1e-1)))
    return {
        "x": nrm(ks[0], (BATCH, SEQ, D_MODEL), 1.0),
        "rel_bias": nrm(ks[1], (REL_BUCKETS, ATTN_HEADS), 0.5),
        "norm1_w": 1.0 + nrm(ks[2], (DEPTH, D_MODEL), 0.05),
        "w_in": nrm(ks[3], (DEPTH, D_MODEL, IN_COLS), D_MODEL ** -0.5),
        "conv_w": nrm(ks[4], (DEPTH, SSM_CONV, CONV_CH), SSM_CONV ** -0.5),
        "conv_b": nrm(ks[5], (DEPTH, CONV_CH), 0.01),
        "dt_bias": dt0 + jnp.log(-jnp.expm1(-dt0)),
        "a_log": jnp.log(jax.random.uniform(ks[7], (DEPTH, 2, SSM_HEADS), f32, 1.0, 16.0)),
        "d_skip": 1.0 + nrm(ks[8], (DEPTH, SSM_HEADS), 0.1),
        "ssm_norm_w": 1.0 + nrm(ks[9], (DEPTH, SSM_WIDTH), 0.05),
        "attn_sink": nrm(ks[10], (DEPTH, ATTN_HEADS), 0.5),
        "w_out": nrm(ks[11], (DEPTH, MIX_WIDTH, D_MODEL), MIX_WIDTH ** -0.5),
        "norm2_w": 1.0 + nrm(ks[12], (DEPTH, D_MODEL), 0.05),
        "w_up": nrm(ks[13], (DEPTH, D_MODEL, 2 * D_FF), D_MODEL ** -0.5),
        "ffn_conv_w": nrm(ks[14], (DEPTH, FFN_CONV, D_FF), FFN_CONV ** -0.5),
        "ffn_conv_b": nrm(ks[15], (DEPTH, D_FF), 0.01),
        "w_down": nrm(ks[16], (DEPTH, D_FF, D_MODEL), D_FF ** -0.5),
        "final_norm_w": 1.0 + nrm(ks[17], (D_MODEL,), 0.05),
    }


def _fwd_reference(x, rel_bias, norm1_w, w_in, conv_w, conv_b, dt_bias, a_log, d_skip,
              ssm_norm_w, attn_sink, w_out, norm2_w, w_up, ffn_conv_w, ffn_conv_b,
              w_down, final_norm_w):
    bsz, seq, _ = x.shape
    f32 = jnp.float32
    rel = jnp.arange(KEY_SPAN)[None, :] - WINDOW - jnp.arange(ATTN_BLOCK)[:, None]
    band_bias = rel_bias.astype(f32)[t5_bucket(rel)].transpose(2, 0, 1)

    for i in range(DEPTH):
        h = rms_norm(x, norm1_w[i])
        proj = h @ w_in[i]
        z, xbc, dt_raw, q, k, v = jnp.split(proj, [Z_END, XBC_END, DT_END, Q_END, K_END], axis=-1)

        xbc = jax.nn.silu(depthwise_conv_centered(xbc, conv_w[i], conv_b[i]))
        xs, bm, cm = jnp.split(xbc, [SSM_WIDTH, SSM_WIDTH + BC_WIDTH], axis=-1)
        xs = xs.reshape(bsz, seq, SSM_HEADS, SSM_HEAD_DIM)
        bm = bm.reshape(bsz, seq, SSM_GROUPS, SSM_STATE)
        cm = cm.reshape(bsz, seq, SSM_GROUPS, SSM_STATE)
        dt = jax.nn.softplus(dt_raw.astype(f32).reshape(bsz, seq, 2, SSM_HEADS)
                             + dt_bias[i].astype(f32))
        a = -jnp.exp(a_log[i].astype(f32))
        y_fwd = ssd_chunked(xs, dt[:, :, 0], a[0], bm, cm)
        y_bwd = jnp.flip(ssd_chunked(jnp.flip(xs, 1), jnp.flip(dt[:, :, 1], 1), a[1],
                                     jnp.flip(bm, 1), jnp.flip(cm, 1)), 1)
        y_ssm = y_fwd + y_bwd + d_skip[i].astype(f32)[:, None] * xs.astype(f32)
        y_ssm = y_ssm.reshape(bsz, seq, SSM_GROUPS, SSM_WIDTH // SSM_GROUPS) \
            * jax.nn.silu(z.astype(f32)).reshape(bsz, seq, SSM_GROUPS, SSM_WIDTH // SSM_GROUPS)
        y_ssm = rms_norm(y_ssm, ssm_norm_w[i].reshape(SSM_GROUPS, -1)).reshape(bsz, seq, SSM_WIDTH)

        y_attn = windowed_gqa(q.reshape(bsz, seq, ATTN_HEADS, ATTN_HEAD_DIM),
                              k.reshape(bsz, seq, ATTN_KV_HEADS, ATTN_HEAD_DIM),
                              v.reshape(bsz, seq, ATTN_KV_HEADS, ATTN_HEAD_DIM),
                              attn_sink[i], band_bias)

        mixed = jnp.concatenate([y_ssm.astype(x.dtype), y_attn.astype(x.dtype)], axis=-1)
        x = x + mixed @ w_out[i]

        h = rms_norm(x, norm2_w[i])
        g, u = jnp.split(h @ w_up[i], [D_FF], axis=-1)
        g = depthwise_conv_centered(g, ffn_conv_w[i], ffn_conv_b[i])
        x = x + (jax.nn.silu(g) * u) @ w_down[i]

    return rms_norm(x, final_norm_w)


import jax as _jax
import jax.numpy as _jnp

TWIN_FORMAT = 'train_step'
FWD_PARAMS = ['x', 'rel_bias', 'norm1_w', 'w_in', 'conv_w', 'conv_b', 'dt_bias', 'a_log', 'd_skip', 'ssm_norm_w', 'attn_sink', 'w_out', 'norm2_w', 'w_up', 'ffn_conv_w', 'ffn_conv_b', 'w_down', 'final_norm_w']
TWIN_WEIGHTS = ['rel_bias', 'norm1_w', 'w_in', 'conv_w', 'conv_b', 'dt_bias', 'a_log', 'd_skip', 'ssm_norm_w', 'attn_sink', 'w_out', 'norm2_w', 'w_up', 'ffn_conv_w', 'ffn_conv_b', 'w_down', 'final_norm_w']
TWIN_DIFF_INPUT = 'x'
TWIN_INPUTS = ['x', 'rel_bias', 'norm1_w', 'w_in', 'conv_w', 'conv_b', 'dt_bias', 'a_log', 'd_skip', 'ssm_norm_w', 'attn_sink', 'w_out', 'norm2_w', 'w_up', 'ffn_conv_w', 'ffn_conv_b', 'w_down', 'final_norm_w', 'loss_target', 'm_rel_bias', 'm_norm1_w', 'm_w_in', 'm_conv_w', 'm_conv_b', 'm_dt_bias', 'm_a_log', 'm_d_skip', 'm_ssm_norm_w', 'm_attn_sink', 'm_w_out', 'm_norm2_w', 'm_w_up', 'm_ffn_conv_w', 'm_ffn_conv_b', 'm_w_down', 'm_final_norm_w', 'v_rel_bias', 'v_norm1_w', 'v_w_in', 'v_conv_w', 'v_conv_b', 'v_dt_bias', 'v_a_log', 'v_d_skip', 'v_ssm_norm_w', 'v_attn_sink', 'v_w_out', 'v_norm2_w', 'v_w_up', 'v_ffn_conv_w', 'v_ffn_conv_b', 'v_w_down', 'v_final_norm_w']
TWIN_OUTPUTS = ['loss', 'grad_x', 'grad_rel_bias', 'grad_norm1_w', 'grad_w_in', 'grad_conv_w', 'grad_conv_b', 'grad_dt_bias', 'grad_a_log', 'grad_d_skip', 'grad_ssm_norm_w', 'grad_attn_sink', 'grad_w_out', 'grad_norm2_w', 'grad_w_up', 'grad_ffn_conv_w', 'grad_ffn_conv_b', 'grad_w_down', 'grad_final_norm_w', 'delta_rel_bias', 'delta_norm1_w', 'delta_w_in', 'delta_conv_w', 'delta_conv_b', 'delta_dt_bias', 'delta_a_log', 'delta_d_skip', 'delta_ssm_norm_w', 'delta_attn_sink', 'delta_w_out', 'delta_norm2_w', 'delta_w_up', 'delta_ffn_conv_w', 'delta_ffn_conv_b', 'delta_w_down', 'delta_final_norm_w', 'new_m_rel_bias', 'new_m_norm1_w', 'new_m_w_in', 'new_m_conv_w', 'new_m_conv_b', 'new_m_dt_bias', 'new_m_a_log', 'new_m_d_skip', 'new_m_ssm_norm_w', 'new_m_attn_sink', 'new_m_w_out', 'new_m_norm2_w', 'new_m_w_up', 'new_m_ffn_conv_w', 'new_m_ffn_conv_b', 'new_m_w_down', 'new_m_final_norm_w', 'new_v_rel_bias', 'new_v_norm1_w', 'new_v_w_in', 'new_v_conv_w', 'new_v_conv_b', 'new_v_dt_bias', 'new_v_a_log', 'new_v_d_skip', 'new_v_ssm_norm_w', 'new_v_attn_sink', 'new_v_w_out', 'new_v_norm2_w', 'new_v_w_up', 'new_v_ffn_conv_w', 'new_v_ffn_conv_b', 'new_v_w_down', 'new_v_final_norm_w']
TWIN_LEAF_KINDS = {'loss': 'loss', 'grad_x': 'grad_x', 'grad_rel_bias': 'grad_w', 'grad_norm1_w': 'grad_w', 'grad_w_in': 'grad_w', 'grad_conv_w': 'grad_w', 'grad_conv_b': 'grad_w', 'grad_dt_bias': 'grad_w', 'grad_a_log': 'grad_w', 'grad_d_skip': 'grad_w', 'grad_ssm_norm_w': 'grad_w', 'grad_attn_sink': 'grad_w', 'grad_w_out': 'grad_w', 'grad_norm2_w': 'grad_w', 'grad_w_up': 'grad_w', 'grad_ffn_conv_w': 'grad_w', 'grad_ffn_conv_b': 'grad_w', 'grad_w_down': 'grad_w', 'grad_final_norm_w': 'grad_w', 'delta_rel_bias': 'delta_w', 'delta_norm1_w': 'delta_w', 'delta_w_in': 'delta_w', 'delta_conv_w': 'delta_w', 'delta_conv_b': 'delta_w', 'delta_dt_bias': 'delta_w', 'delta_a_log': 'delta_w', 'delta_d_skip': 'delta_w', 'delta_ssm_norm_w': 'delta_w', 'delta_attn_sink': 'delta_w', 'delta_w_out': 'delta_w', 'delta_norm2_w': 'delta_w', 'delta_w_up': 'delta_w', 'delta_ffn_conv_w': 'delta_w', 'delta_ffn_conv_b': 'delta_w', 'delta_w_down': 'delta_w', 'delta_final_norm_w': 'delta_w', 'new_m_rel_bias': 'new_m', 'new_m_norm1_w': 'new_m', 'new_m_w_in': 'new_m', 'new_m_conv_w': 'new_m', 'new_m_conv_b': 'new_m', 'new_m_dt_bias': 'new_m', 'new_m_a_log': 'new_m', 'new_m_d_skip': 'new_m', 'new_m_ssm_norm_w': 'new_m', 'new_m_attn_sink': 'new_m', 'new_m_w_out': 'new_m', 'new_m_norm2_w': 'new_m', 'new_m_w_up': 'new_m', 'new_m_ffn_conv_w': 'new_m', 'new_m_ffn_conv_b': 'new_m', 'new_m_w_down': 'new_m', 'new_m_final_norm_w': 'new_m', 'new_v_rel_bias': 'new_v', 'new_v_norm1_w': 'new_v', 'new_v_w_in': 'new_v', 'new_v_conv_w': 'new_v', 'new_v_conv_b': 'new_v', 'new_v_dt_bias': 'new_v', 'new_v_a_log': 'new_v', 'new_v_d_skip': 'new_v', 'new_v_ssm_norm_w': 'new_v', 'new_v_attn_sink': 'new_v', 'new_v_w_out': 'new_v', 'new_v_norm2_w': 'new_v', 'new_v_w_up': 'new_v', 'new_v_ffn_conv_w': 'new_v', 'new_v_ffn_conv_b': 'new_v', 'new_v_w_down': 'new_v', 'new_v_final_norm_w': 'new_v'}


def _forward(args):
    return _fwd_reference(*[args[k] for k in FWD_PARAMS])


def _output_shape():
    out = _jax.eval_shape(lambda: _forward(_fwd_setup_inputs(0)))
    return out.shape, out.dtype

N_MICROBATCH = 1
ADAM_LR = 0.001
ADAM_B1 = 0.9
ADAM_B2 = 0.999
ADAM_EPS = 1e-08
ADAM_WD = 0.01
ADAM_STEP = 10
PER_EXAMPLE_BATCH_AXIS = {'x': 0, 'loss_target': 0}
SHARED_INPUTS = []
_WEIGHT_DTYPES = {'rel_bias': _jnp.float32, 'norm1_w': _jnp.float32, 'w_in': _jnp.float32, 'conv_w': _jnp.float32, 'conv_b': _jnp.float32, 'dt_bias': _jnp.float32, 'a_log': _jnp.float32, 'd_skip': _jnp.float32, 'ssm_norm_w': _jnp.float32, 'attn_sink': _jnp.float32, 'w_out': _jnp.float32, 'norm2_w': _jnp.float32, 'w_up': _jnp.float32, 'ffn_conv_w': _jnp.float32, 'ffn_conv_b': _jnp.float32, 'w_down': _jnp.float32, 'final_norm_w': _jnp.float32}
MOMENT_SCALE = {'rel_bias': 5.870385e-02, 'norm1_w': 2.450652e-01, 'w_in': 1.228518e-01, 'conv_w': 1.363827e-01, 'conv_b': 2.308013e-01, 'dt_bias': 4.417526e-01, 'a_log': 4.241068e-01, 'd_skip': 7.438641e-01, 'ssm_norm_w': 1.790268e-01, 'attn_sink': 1.202931e-03, 'w_out': 1.699329e-01, 'norm2_w': 1.722361e-01, 'w_up': 7.289787e-02, 'ffn_conv_w': 7.390088e-02, 'ffn_conv_b': 7.252380e-02, 'w_down': 1.193025e-01, 'final_norm_w': 6.396017e+01}


def _to_microbatches(a, axis):
    t = _jnp.moveaxis(a, axis, 0)
    t = t.reshape((N_MICROBATCH, t.shape[0] // N_MICROBATCH) + t.shape[1:])
    return _jnp.moveaxis(t, 1, axis + 1)


def setup_inputs(seed: int = 0) -> dict:
    inp = _fwd_setup_inputs(seed)
    key = _jax.random.fold_in(_jax.random.key(seed), 7919)
    shape, _ = _output_shape()
    out = dict(inp)
    out["loss_target"] = _jax.random.normal(_jax.random.fold_in(key, 0), shape, _jnp.float32)
    for i, name in enumerate(TWIN_WEIGHTS):
        w = inp[name].astype(_jnp.float32)
        if MOMENT_SCALE is None:
            s = _jnp.sqrt(_jnp.mean(_jnp.square(w)) + 1e-30)
        else:
            s = MOMENT_SCALE[name]
        km, kv = _jax.random.split(_jax.random.fold_in(key, i + 1))
        out[name] = w
        out["m_" + name] = s * _jax.random.normal(km, w.shape, _jnp.float32)
        out["v_" + name] = (s * s) * _jax.random.uniform(kv, w.shape, _jnp.float32, 0.5, 1.5)
    if N_MICROBATCH > 1:
        for name, axis in PER_EXAMPLE_BATCH_AXIS.items():
            out[name] = _to_microbatches(out[name], axis)
    return {'x': out['x'], 'rel_bias': out['rel_bias'], 'norm1_w': out['norm1_w'], 'w_in': out['w_in'], 'conv_w': out['conv_w'], 'conv_b': out['conv_b'], 'dt_bias': out['dt_bias'], 'a_log': out['a_log'], 'd_skip': out['d_skip'], 'ssm_norm_w': out['ssm_norm_w'], 'attn_sink': out['attn_sink'], 'w_out': out['w_out'], 'norm2_w': out['norm2_w'], 'w_up': out['w_up'], 'ffn_conv_w': out['ffn_conv_w'], 'ffn_conv_b': out['ffn_conv_b'], 'w_down': out['w_down'], 'final_norm_w': out['final_norm_w'], 'loss_target': out['loss_target'], 'm_rel_bias': out['m_rel_bias'], 'm_norm1_w': out['m_norm1_w'], 'm_w_in': out['m_w_in'], 'm_conv_w': out['m_conv_w'], 'm_conv_b': out['m_conv_b'], 'm_dt_bias': out['m_dt_bias'], 'm_a_log': out['m_a_log'], 'm_d_skip': out['m_d_skip'], 'm_ssm_norm_w': out['m_ssm_norm_w'], 'm_attn_sink': out['m_attn_sink'], 'm_w_out': out['m_w_out'], 'm_norm2_w': out['m_norm2_w'], 'm_w_up': out['m_w_up'], 'm_ffn_conv_w': out['m_ffn_conv_w'], 'm_ffn_conv_b': out['m_ffn_conv_b'], 'm_w_down': out['m_w_down'], 'm_final_norm_w': out['m_final_norm_w'], 'v_rel_bias': out['v_rel_bias'], 'v_norm1_w': out['v_norm1_w'], 'v_w_in': out['v_w_in'], 'v_conv_w': out['v_conv_w'], 'v_conv_b': out['v_conv_b'], 'v_dt_bias': out['v_dt_bias'], 'v_a_log': out['v_a_log'], 'v_d_skip': out['v_d_skip'], 'v_ssm_norm_w': out['v_ssm_norm_w'], 'v_attn_sink': out['v_attn_sink'], 'v_w_out': out['v_w_out'], 'v_norm2_w': out['v_norm2_w'], 'v_w_up': out['v_w_up'], 'v_ffn_conv_w': out['v_ffn_conv_w'], 'v_ffn_conv_b': out['v_ffn_conv_b'], 'v_w_down': out['v_w_down'], 'v_final_norm_w': out['v_final_norm_w']}


def _loss(weights, diff, rest, loss_target):
    with _jax.named_scope("forward"):
        args = {**rest, TWIN_DIFF_INPUT: diff, **{k: w.astype(_WEIGHT_DTYPES[k]) for k, w in weights.items()}}
        y = _forward(args)
    with _jax.named_scope("loss_head"):
        err = _jnp.square(y.astype(_jnp.float32) - loss_target)
        return 0.5 * _jnp.sum(_jnp.mean(err, axis=-1)) if err.ndim else 0.5 * err


def _adamw(w, g, m, v):
    m = ADAM_B1 * m + (1.0 - ADAM_B1) * g
    v = ADAM_B2 * v + (1.0 - ADAM_B2) * _jnp.square(g)
    m_hat = m / (1.0 - ADAM_B1 ** ADAM_STEP)
    v_hat = v / (1.0 - ADAM_B2 ** ADAM_STEP)
    delta = -ADAM_LR * (m_hat / (_jnp.sqrt(v_hat) + ADAM_EPS) + ADAM_WD * w)
    return delta, m, v


def reference(x, rel_bias, norm1_w, w_in, conv_w, conv_b, dt_bias, a_log, d_skip, ssm_norm_w, attn_sink, w_out, norm2_w, w_up, ffn_conv_w, ffn_conv_b, w_down, final_norm_w, loss_target, m_rel_bias, m_norm1_w, m_w_in, m_conv_w, m_conv_b, m_dt_bias, m_a_log, m_d_skip, m_ssm_norm_w, m_attn_sink, m_w_out, m_norm2_w, m_w_up, m_ffn_conv_w, m_ffn_conv_b, m_w_down, m_final_norm_w, v_rel_bias, v_norm1_w, v_w_in, v_conv_w, v_conv_b, v_dt_bias, v_a_log, v_d_skip, v_ssm_norm_w, v_attn_sink, v_w_out, v_norm2_w, v_w_up, v_ffn_conv_w, v_ffn_conv_b, v_w_down, v_final_norm_w):
    given = dict(x=x, rel_bias=rel_bias, norm1_w=norm1_w, w_in=w_in, conv_w=conv_w, conv_b=conv_b, dt_bias=dt_bias, a_log=a_log, d_skip=d_skip, ssm_norm_w=ssm_norm_w, attn_sink=attn_sink, w_out=w_out, norm2_w=norm2_w, w_up=w_up, ffn_conv_w=ffn_conv_w, ffn_conv_b=ffn_conv_b, w_down=w_down, final_norm_w=final_norm_w, loss_target=loss_target, m_rel_bias=m_rel_bias, m_norm1_w=m_norm1_w, m_w_in=m_w_in, m_conv_w=m_conv_w, m_conv_b=m_conv_b, m_dt_bias=m_dt_bias, m_a_log=m_a_log, m_d_skip=m_d_skip, m_ssm_norm_w=m_ssm_norm_w, m_attn_sink=m_attn_sink, m_w_out=m_w_out, m_norm2_w=m_norm2_w, m_w_up=m_w_up, m_ffn_conv_w=m_ffn_conv_w, m_ffn_conv_b=m_ffn_conv_b, m_w_down=m_w_down, m_final_norm_w=m_final_norm_w, v_rel_bias=v_rel_bias, v_norm1_w=v_norm1_w, v_w_in=v_w_in, v_conv_w=v_conv_w, v_conv_b=v_conv_b, v_dt_bias=v_dt_bias, v_a_log=v_a_log, v_d_skip=v_d_skip, v_ssm_norm_w=v_ssm_norm_w, v_attn_sink=v_attn_sink, v_w_out=v_w_out, v_norm2_w=v_norm2_w, v_w_up=v_w_up, v_ffn_conv_w=v_ffn_conv_w, v_ffn_conv_b=v_ffn_conv_b, v_w_down=v_w_down, v_final_norm_w=v_final_norm_w)
    weights = {n: given[n] for n in TWIN_WEIGHTS}
    shared = {n: given[n] for n in SHARED_INPUTS}
    per_example = {n: given[n] for n in ['x']}
    grad_fn = _jax.value_and_grad(_loss, argnums=(0, 1))

    def one_microbatch(ex, loss_target):
        ex = dict(ex)
        diff = ex.pop(TWIN_DIFF_INPUT)
        return grad_fn(weights, diff, {**shared, **ex}, loss_target)

    if N_MICROBATCH == 1:
        loss, (grad_w, grad_x) = one_microbatch(per_example, given["loss_target"])
    else:
        def body(carry, xs):
            loss_sum, grad_sum = carry
            l_k, (gw_k, gx_k) = one_microbatch(xs[0], xs[1])
            with _jax.named_scope("update"):
                return (loss_sum + l_k, _jax.tree.map(_jnp.add, grad_sum, gw_k)), gx_k

        init = (_jnp.zeros((), _jnp.float32), _jax.tree.map(_jnp.zeros_like, weights))
        (loss, grad_w), grad_x = _jax.lax.scan(body, init, (per_example, given["loss_target"]))
    with _jax.named_scope("update"):
        delta_w, new_m, new_v = {}, {}, {}
        for n in TWIN_WEIGHTS:
            delta_w[n], new_m[n], new_v[n] = _adamw(weights[n], grad_w[n], given["m_" + n], given["v_" + n])
    return (loss, grad_x, *[grad_w[n] for n in TWIN_WEIGHTS], *[delta_w[n] for n in TWIN_WEIGHTS],
            *[new_m[n] for n in TWIN_WEIGHTS], *[new_v[n] for n in TWIN_WEIGHTS])
```

```python
import math

import numpy as np
import jax
import jax.numpy as jnp
from jax import lax
from jax.experimental import pallas as pl
from jax.experimental.pallas import tpu as pltpu

F32, BF16 = jnp.float32, jnp.bfloat16
HIGHEST = lax.Precision.HIGHEST

D_MODEL = 1024
HEAD_DIM = 64
N_HEADS = 16
N_GROUPS = 2
HEADS_PER_GROUP = N_HEADS // N_GROUPS
N_STATE = 128
SSM_WIDTH = 1024
BC_WIDTH = 256
CONV_CH = SSM_WIDTH + 2 * BC_WIDTH
SSM_TAPS = 7
CHUNK = 128
KV_HEADS = 4
KV_WIDTH = 256
Q_PER_KV = N_HEADS // KV_HEADS
KEY_SPAN = 3 * CHUNK
REL_BUCKETS = 32
D_FF = 2816
FFN_TAPS = 3
IN_COLS = 4128
NORM_EPS = 1e-6
N_DEV = 8

LANES = 128
SUBLANES = 8
VMEM_LIMIT_BYTES = 56 * 1024 * 1024

PZ, PXS, PQ, PB, PC, PK, PV, PDT, PROJ_W = 0, 1024, 2048, 3072, 3328, 3584, 3840, 4096, 4224
OZ, OXBC, ODT, OQ, OK_, OV = 0, 1024, 2560, 2592, 3616, 3872

ADAM_LR, ADAM_B1, ADAM_B2, ADAM_EPS, ADAM_WD, ADAM_STEP = 0.001, 0.9, 0.999, 1e-08, 0.01, 10


def _params(*sem):
    return pltpu.CompilerParams(dimension_semantics=sem, vmem_limit_bytes=VMEM_LIMIT_BYTES)


def _sigmoid(x):
    return 1.0 / (1.0 + jnp.exp(-x))


def _softplus(x):
    return jnp.maximum(x, 0.0) + jnp.log(1.0 + jnp.exp(-jnp.abs(x)))


def _dot(a, b, dims):
    return lax.dot_general(a, b, (dims, ((), ())), preferred_element_type=F32)


NN = ((1,), (0,))
NT = ((1,), (1,))
TN = ((0,), (0,))


def _matmul(a, b, mode, *, name, tm, tn, tk, res=None, out_dtype=F32, precision=None):
    if mode == "nn":
        (m, k), (k2, n) = a.shape, b.shape
        a_spec = pl.BlockSpec((tm, tk), lambda i, j, kk: (i, kk))
        b_spec = pl.BlockSpec((tk, tn), lambda i, j, kk: (kk, j))
        dims = NN
    elif mode == "nt":
        (m, k), (n, k2) = a.shape, b.shape
        a_spec = pl.BlockSpec((tm, tk), lambda i, j, kk: (i, kk))
        b_spec = pl.BlockSpec((tn, tk), lambda i, j, kk: (j, kk))
        dims = NT
    else:
        (k, m), (k2, n) = a.shape, b.shape
        a_spec = pl.BlockSpec((tk, tm), lambda i, j, kk: (kk, i))
        b_spec = pl.BlockSpec((tk, tn), lambda i, j, kk: (kk, j))
        dims = TN
    assert k == k2 and m % tm == 0 and n % tn == 0 and k % tk == 0, (name, a.shape, b.shape, tm, tn, tk)
    nk = k // tk
    has_res = res is not None

    def body(*refs):
        if has_res:
            a_ref, b_ref, r_ref, o_ref, acc = refs
        else:
            a_ref, b_ref, o_ref, acc = refs
        kk = pl.program_id(2)

        @pl.when(kk == 0)
        def _():
            acc[...] = jnp.zeros_like(acc)

        if precision is None:
            part = _dot(a_ref[...].astype(BF16), b_ref[...].astype(BF16), dims)
        else:
            part = lax.dot_general(a_ref[...], b_ref[...], (dims, ((), ())), precision=precision,
                                   preferred_element_type=F32)
        acc[...] += part

        @pl.when(kk == nk - 1)
        def _():
            r = acc[...]
            if has_res:
                r = r + r_ref[...].astype(F32)
            o_ref[...] = r.astype(out_dtype)

    in_specs = [a_spec, b_spec]
    args = [a, b]
    if has_res:
        in_specs.append(pl.BlockSpec((tm, tn), lambda i, j, kk: (i, j)))
        args.append(res)
    return pl.pallas_call(
        body, name=name, grid=(m // tm, n // tn, nk),
        in_specs=in_specs, out_specs=pl.BlockSpec((tm, tn), lambda i, j, kk: (i, j)),
        out_shape=jax.ShapeDtypeStruct((m, n), out_dtype),
        scratch_shapes=[pltpu.VMEM((tm, tn), F32)],
        compiler_params=_params("parallel", "parallel", "arbitrary"),
    )(*args)


def _rms_matmul(x, nw, w, *, name, tm, tn):
    t, d = x.shape
    n = w.shape[1]
    assert t % tm == 0 and n % tn == 0

    def body(x_ref, nw_ref, w_ref, o_ref, h_ref):
        @pl.when(pl.program_id(1) == 0)
        def _():
            xv = x_ref[...]
            r = lax.rsqrt(jnp.mean(xv * xv, axis=-1, keepdims=True) + NORM_EPS)
            h_ref[...] = (xv * r * nw_ref[...]).astype(BF16)

        o_ref[...] = _dot(h_ref[...], w_ref[...].astype(BF16), NN)

    return pl.pallas_call(
        body, name=name, grid=(t // tm, n // tn),
        in_specs=[pl.BlockSpec((tm, d), lambda i, j: (i, 0)),
                  pl.BlockSpec((1, d), lambda i, j: (0, 0)),
                  pl.BlockSpec((d, tn), lambda i, j: (0, j))],
        out_specs=[pl.BlockSpec((tm, tn), lambda i, j: (i, j)),
                   pl.BlockSpec((tm, d), lambda i, j: (i, 0))],
        out_shape=[jax.ShapeDtypeStruct((t, n), F32), jax.ShapeDtypeStruct((t, d), BF16)],
        compiler_params=_params("parallel", "arbitrary"),
    )(x, nw, w)


def _shifted(v, offset, seq):
    if offset == 0:
        return v
    rolled = pltpu.roll(v, (-offset) % seq, 0)
    t = lax.broadcasted_iota(jnp.int32, v.shape, 0)
    ok = (t + offset >= 0) & (t + offset < seq)
    return jnp.where(ok, rolled, 0.0)


def _conv_taps(v, w_ref, taps, seq):
    pad = taps // 2
    acc = None
    for k in range(taps):
        term = _shifted(v, k - pad, seq) * w_ref[k:k + 1, :]
        acc = term if acc is None else acc + term
    return acc


def _ssm_conv_fwd(proj, cw, cb, *, nb, seq):
    width = 512

    def body(x_ref, w_ref, b_ref, o_ref):
        g = _conv_taps(x_ref[...], w_ref, SSM_TAPS, seq) + b_ref[...]
        o_ref[...] = g * _sigmoid(g)

    def col(j):
        return jnp.where(j < 2, j + PXS // width, PB // width)

    return pl.pallas_call(
        body, name="ssm_conv_fwd", grid=(nb, CONV_CH // width),
        in_specs=[pl.BlockSpec((seq, width), lambda b, j: (b, col(j))),
                  pl.BlockSpec((SSM_TAPS, width), lambda b, j: (0, j)),
                  pl.BlockSpec((1, width), lambda b, j: (0, j))],
        out_specs=pl.BlockSpec((seq, width), lambda b, j: (b, j)),
        out_shape=jax.ShapeDtypeStruct((nb * seq, CONV_CH), F32),
        compiler_params=_params("parallel", "parallel"),
    )(proj, cw, cb)


def _ffn_act_fwd(gu, cw, cb, *, nb, seq):
    width = 256
    nj = D_FF // width

    def body(g_ref, u_ref, w_ref, b_ref, o_ref):
        g = _conv_taps(g_ref[...], w_ref, FFN_TAPS, seq) + b_ref[...]
        o_ref[...] = (g * _sigmoid(g) * u_ref[...]).astype(BF16)

    return pl.pallas_call(
        body, name="ffn_act_fwd", grid=(nb, nj),
        in_specs=[pl.BlockSpec((seq, width), lambda b, j: (b, j)),
                  pl.BlockSpec((seq, width), lambda b, j: (b, j + nj)),
                  pl.BlockSpec((FFN_TAPS, width), lambda b, j: (0, j)),
                  pl.BlockSpec((1, width), lambda b, j: (0, j))],
        out_specs=pl.BlockSpec((seq, width), lambda b, j: (b, j)),
        out_shape=jax.ShapeDtypeStruct((nb * seq, D_FF), BF16),
        compiler_params=_params("parallel", "parallel"),
    )(gu, gu, cw, cb)


def _scan_setup(d, dt_ref, dtb_ref, alog_ref, z_ref, zt_ref, *, lower_when_dir0, inclusive):
    is0 = d == 0
    dt_all = _softplus(dt_ref[...] + dtb_ref[...])
    adt_all = dt_all * (-jnp.exp(alog_ref[...]))
    li = lax.broadcasted_iota(jnp.int32, (CHUNK, CHUNK), 0)
    si = lax.broadcasted_iota(jnp.int32, (CHUNK, CHUNK), 1)
    lower = is0 if lower_when_dir0 else jnp.logical_not(is0)
    ahead = jnp.where(lower, li - si, si - li)
    mask = ahead >= 0
    tri = mask if inclusive else ahead > 0
    z_all = jnp.dot(tri.astype(F32), adt_all, precision=HIGHEST, preferred_element_type=F32)
    zt_all = z_all.T
    z_ref[...] = jnp.where(is0, z_all[:, 0:N_HEADS], z_all[:, N_HEADS:2 * N_HEADS])
    zt_ref[...] = jnp.where(is0, zt_all[0:N_HEADS, :], zt_all[N_HEADS:2 * N_HEADS, :])
    dt = jnp.where(is0, dt_all[:, 0:N_HEADS], dt_all[:, N_HEADS:2 * N_HEADS])
    adt = jnp.where(is0, adt_all[:, 0:N_HEADS], adt_all[:, N_HEADS:2 * N_HEADS])
    tot = jnp.sum(adt, axis=0, keepdims=True)
    return ahead, dt, tot, dt_all


def _chunk_index(nchunk, forward_when_dir0):
    def idx(d, b, c):
        fwd = (d == 0) if forward_when_dir0 else (d != 0)
        return b * nchunk + jnp.where(fwd, c, nchunk - 1 - c)
    return idx


def _ssd_fwd(xbc, proj, dtb, alog, *, nb, seq):
    nchunk = seq // CHUNK
    t = nb * seq
    row = _chunk_index(nchunk, True)

    def body(xs_ref, bc_ref, dt_ref, dtb_ref, alog_ref, o_ref, hs_ref, h_ref, z_ref, zt_ref, dts_ref):
        d, c = pl.program_id(0), pl.program_id(2)

        @pl.when(c == 0)
        def _():
            h_ref[...] = jnp.zeros_like(h_ref)

        ahead, dt, tot, _ = _scan_setup(d, dt_ref, dtb_ref, alog_ref, z_ref, zt_ref,
                                        lower_when_dir0=True, inclusive=True)
        mask = ahead >= 0
        dts_ref[...] = dt
        e_tot = jnp.exp(tot)
        for g in range(N_GROUPS):
            bg = bc_ref[:, g * N_STATE:(g + 1) * N_STATE]
            cg = bc_ref[:, BC_WIDTH + g * N_STATE:BC_WIDTH + (g + 1) * N_STATE]
            cb = _dot(cg.astype(BF16), bg.astype(BF16), NT)
            for r in range(HEADS_PER_GROUP):
                h = g * HEADS_PER_GROUP + r
                zc = z_ref[:, h:h + 1]
                zr = zt_ref[h:h + 1, :]
                decay = jnp.exp(jnp.where(mask, zc - zr, -jnp.inf))
                u = (xs_ref[:, h * HEAD_DIM:(h + 1) * HEAD_DIM] * dts_ref[:, h:h + 1]).astype(BF16)
                y = _dot((cb * decay).astype(BF16), u, NN)
                state = h_ref[h]
                hs_ref[0, 0, h] = state
                y = y + _dot((cg * jnp.exp(zc)).astype(BF16), state.astype(BF16), NN)
                bw = (bg * jnp.exp(tot[:, h:h + 1] - zc)).astype(BF16)
                h_ref[h] = state * e_tot[:, h:h + 1] + _dot(bw, u, TN)
                o_ref[0, :, h * HEAD_DIM:(h + 1) * HEAD_DIM] = y

    return pl.pallas_call(
        body, name="ssd_fwd", grid=(2, nb, nchunk),
        in_specs=[pl.BlockSpec((CHUNK, SSM_WIDTH), lambda d, b, c: (row(d, b, c), 0)),
                  pl.BlockSpec((CHUNK, 2 * BC_WIDTH), lambda d, b, c: (row(d, b, c), SSM_WIDTH // (2 * BC_WIDTH))),
                  pl.BlockSpec((CHUNK, LANES), lambda d, b, c: (row(d, b, c), PDT // LANES)),
                  pl.BlockSpec((1, LANES), lambda d, b, c: (0, 0)),
                  pl.BlockSpec((1, LANES), lambda d, b, c: (0, 0))],
        out_specs=[pl.BlockSpec((1, CHUNK, SSM_WIDTH), lambda d, b, c: (d, row(d, b, c), 0)),
                   pl.BlockSpec((1, 1, N_HEADS, N_STATE, HEAD_DIM), lambda d, b, c: (d, row(d, b, c), 0, 0, 0))],
        out_shape=[jax.ShapeDtypeStruct((2, t, SSM_WIDTH), F32),
                   jax.ShapeDtypeStruct((2, nb * nchunk, N_HEADS, N_STATE, HEAD_DIM), F32)],
        scratch_shapes=[pltpu.VMEM((N_HEADS, N_STATE, HEAD_DIM), F32),
                        pltpu.VMEM((CHUNK, N_HEADS), F32), pltpu.VMEM((N_HEADS, CHUNK), F32),
                        pltpu.VMEM((CHUNK, N_HEADS), F32)],
        compiler_params=_params("arbitrary", "arbitrary", "arbitrary"),
    )(xbc, xbc, proj, dtb, alog)


def _gate_norm_fwd(y2, xbc, proj, dvec, nw, *, tm):
    t = xbc.shape[0]
    half = SSM_WIDTH // N_GROUPS

    def body(y_ref, xs_ref, z_ref, d_ref, w_ref, o_ref):
        z = z_ref[...]
        p = (y_ref[0] + y_ref[1] + d_ref[...] * xs_ref[...]) * (z * _sigmoid(z))
        for g in range(N_GROUPS):
            pg = p[:, g * half:(g + 1) * half]
            r = lax.rsqrt(jnp.mean(pg * pg, axis=-1, keepdims=True) + NORM_EPS)
            o_ref[:, g * half:(g + 1) * half] = (pg * r * w_ref[:, g * half:(g + 1) * half]).astype(BF16)

    return pl.pallas_call(
        body, name="gate_norm_fwd", grid=(t // tm,),
        in_specs=[pl.BlockSpec((2, tm, SSM_WIDTH), lambda i: (0, i, 0)),
                  pl.BlockSpec((tm, SSM_WIDTH), lambda i: (i, 0)),
                  pl.BlockSpec((tm, SSM_WIDTH), lambda i: (i, PZ // SSM_WIDTH)),
                  pl.BlockSpec((1, SSM_WIDTH), lambda i: (0, 0)),
                  pl.BlockSpec((1, SSM_WIDTH), lambda i: (0, 0))],
        out_specs=pl.BlockSpec((tm, SSM_WIDTH), lambda i: (i, 0)),
        out_shape=jax.ShapeDtypeStruct((t, SSM_WIDTH), BF16),
        compiler_params=_params("parallel"),
    )(y2, xbc, proj, dvec, nw)


def _band_mask(n, nblk):
    qi = lax.broadcasted_iota(jnp.int32, (CHUNK, KEY_SPAN), 0)
    kj = lax.broadcasted_iota(jnp.int32, (CHUNK, KEY_SPAN), 1)
    rel = kj - CHUNK - qi
    kpos = (n - 1) * CHUNK + kj
    return (jnp.abs(rel) <= CHUNK) & (kpos >= 0) & (kpos < nblk * CHUNK)


def _kv_specs(nblk):
    kvb = PK // (2 * KV_WIDTH)

    def at(off):
        def idx(b, n):
            return (b * nblk + jnp.clip(n + off, 0, nblk - 1), kvb)
        return pl.BlockSpec((CHUNK, 2 * KV_WIDTH), idx)
    return [at(-1), at(0), at(1)]


def _attn_fwd(proj, bias, sink, *, nb, seq):
    nblk = seq // CHUNK
    t = nb * seq
    scale = HEAD_DIM ** -0.5

    def body(q_ref, kp_ref, kc_ref, kn_ref, bias_ref, sink_ref, o_ref, lse_ref):
        n = pl.program_id(1)
        valid = _band_mask(n, nblk)
        lses = []
        for g in range(KV_HEADS):
            ks = slice(g * HEAD_DIM, (g + 1) * HEAD_DIM)
            vs = slice(KV_WIDTH + g * HEAD_DIM, KV_WIDTH + (g + 1) * HEAD_DIM)
            kcat = jnp.concatenate([kp_ref[:, ks], kc_ref[:, ks], kn_ref[:, ks]], axis=0).astype(BF16)
            vcat = jnp.concatenate([kp_ref[:, vs], kc_ref[:, vs], kn_ref[:, vs]], axis=0).astype(BF16)
            for r in range(Q_PER_KV):
                h = g * Q_PER_KV + r
                q = q_ref[:, h * HEAD_DIM:(h + 1) * HEAD_DIM].astype(BF16)
                s = _dot(q, kcat, NT) * scale + bias_ref[h]
                s = jnp.where(valid, s, -jnp.inf)
                sk = sink_ref[:, h:h + 1]
                m = jnp.maximum(jnp.max(s, axis=-1, keepdims=True), sk)
                p = jnp.exp(s - m)
                denom = jnp.sum(p, axis=-1, keepdims=True) + jnp.exp(sk - m)
                o = _dot((p / denom).astype(BF16), vcat, NN)
                o_ref[:, h * HEAD_DIM:(h + 1) * HEAD_DIM] = o.astype(BF16)
                lses.append(m + jnp.log(denom))
        lse_ref[...] = jnp.concatenate(lses, axis=1)

    return pl.pallas_call(
        body, name="attn_fwd", grid=(nb, nblk),
        in_specs=[pl.BlockSpec((CHUNK, D_MODEL), lambda b, n: (b * nblk + n, PQ // D_MODEL))] + _kv_specs(nblk) + [
            pl.BlockSpec((N_HEADS, CHUNK, KEY_SPAN), lambda b, n: (0, 0, 0)),
            pl.BlockSpec((1, LANES), lambda b, n: (0, 0))],
        out_specs=[pl.BlockSpec((CHUNK, D_MODEL), lambda b, n: (b * nblk + n, 0)),
                   pl.BlockSpec((CHUNK, N_HEADS), lambda b, n: (b * nblk + n, 0))],
        out_shape=[jax.ShapeDtypeStruct((t, D_MODEL), BF16), jax.ShapeDtypeStruct((t, N_HEADS), F32)],
        compiler_params=_params("parallel", "parallel"),
    )(proj, proj, proj, proj, bias, sink)


def _loss_head(x, tgt, nw, *, tm):
    t, d = x.shape

    def body(x_ref, t_ref, w_ref, dx_ref, dw_ref, l_ref):
        @pl.when(pl.program_id(0) == 0)
        def _():
            dw_ref[...] = jnp.zeros_like(dw_ref)
            l_ref[...] = jnp.zeros_like(l_ref)

        xv = x_ref[...]
        w = w_ref[...]
        r = lax.rsqrt(jnp.mean(xv * xv, axis=-1, keepdims=True) + NORM_EPS)
        xh = xv * r
        err = xh * w - t_ref[...]
        l_ref[...] += jnp.sum(err * err) * (0.5 / d)
        dy = err * (1.0 / d)
        gw = dy * w
        dx_ref[...] = r * (gw - xh * jnp.mean(gw * xh, axis=-1, keepdims=True))
        dw_ref[...] += jnp.sum(dy * xh, axis=0, keepdims=True)

    return pl.pallas_call(
        body, name="loss_head", grid=(t // tm,),
        in_specs=[pl.BlockSpec((tm, d), lambda i: (i, 0)), pl.BlockSpec((tm, d), lambda i: (i, 0)),
                  pl.BlockSpec((1, d), lambda i: (0, 0))],
        out_specs=[pl.BlockSpec((tm, d), lambda i: (i, 0)), pl.BlockSpec((1, d), lambda i: (0, 0)),
                   pl.BlockSpec((1, LANES), lambda i: (0, 0))],
        out_shape=[jax.ShapeDtypeStruct((t, d), F32), jax.ShapeDtypeStruct((1, d), F32),
                   jax.ShapeDtypeStruct((1, LANES), F32)],
        compiler_params=_params("arbitrary"),
    )(x, tgt, nw)


def _to_proj_layout(w):
    pad = jnp.zeros(w.shape[:-1] + (PROJ_W - IN_COLS,), w.dtype)
    return jnp.concatenate([w[..., OZ:OXBC], w[..., OXBC:OXBC + SSM_WIDTH], w[..., OQ:OK_],
                            w[..., OXBC + SSM_WIDTH:ODT], w[..., OK_:IN_COLS], w[..., ODT:OQ], pad], axis=-1)


def _from_proj_layout(g):
    return jnp.concatenate([g[..., PZ:PZ + 2 * SSM_WIDTH], g[..., PB:PB + 2 * BC_WIDTH], g[..., PDT:PDT + 2 * N_HEADS],
                            g[..., PQ:PQ + D_MODEL], g[..., PK:PK + 2 * KV_WIDTH]], axis=-1)


def _pad_lanes(v):
    return jnp.pad(v.reshape(1, -1), ((0, 0), (0, LANES - v.size)))


def _layer_fwd(x, p, band_bias, *, nb, seq):
    proj, h1 = _rms_matmul(x, p["n1"], p["w_in"], name="in_proj", tm=512, tn=1408)
    xbc = _ssm_conv_fwd(proj, p["conv_w"], p["conv_b"], nb=nb, seq=seq)
    y2, states = _ssd_fwd(xbc, proj, p["dtb"], p["alog"], nb=nb, seq=seq)
    y_ssm = _gate_norm_fwd(y2, xbc, proj, p["dvec"], p["ssm_nw"], tm=256)
    y_att, lse = _attn_fwd(proj, band_bias, p["sink"], nb=nb, seq=seq)
    x1 = _matmul(y_ssm, p["w_out"][:SSM_WIDTH], "nn", name="out_proj_ssm", tm=512, tn=1024, tk=1024, res=x)
    x1 = _matmul(y_att, p["w_out"][SSM_WIDTH:], "nn", name="out_proj_att", tm=512, tn=1024, tk=1024, res=x1)
    gu, h2 = _rms_matmul(x1, p["n2"], p["w_up"], name="up_proj", tm=512, tn=1408)
    act = _ffn_act_fwd(gu, p["ffn_cw"], p["ffn_cb"], nb=nb, seq=seq)
    x2 = _matmul(act, p["w_down"], "nn", name="down_proj", tm=512, tn=1024, tk=1408, res=x1)
    saved = dict(x=x, proj=proj, h1=h1, xbc=xbc, y2=y2, states=states, y_ssm=y_ssm, y_att=y_att, lse=lse, x1=x1, gu=gu, h2=h2, act=act)
    return x2, saved


def _rms_bwd(x, dh, nw, dres, *, tm, name):
    t, d = x.shape

    def body(x_ref, dh_ref, w_ref, r_ref, dx_ref, dw_ref):
        @pl.when(pl.program_id(0) == 0)
        def _():
            dw_ref[...] = jnp.zeros_like(dw_ref)

        xv = x_ref[...]
        dh_v = dh_ref[...].astype(F32)
        r = lax.rsqrt(jnp.mean(xv * xv, axis=-1, keepdims=True) + NORM_EPS)
        xh = xv * r
        gw = dh_v * w_ref[...]
        dx_ref[...] = r_ref[...] + r * (gw - xh * jnp.mean(gw * xh, axis=-1, keepdims=True))
        dw_ref[...] += jnp.sum(dh_v * xh, axis=0, keepdims=True)

    row = pl.BlockSpec((tm, d), lambda i: (i, 0))
    vec = pl.BlockSpec((1, d), lambda i: (0, 0))
    return pl.pallas_call(
        body, name=name, grid=(t // tm,), in_specs=[row, row, vec, row], out_specs=[row, vec],
        out_shape=[jax.ShapeDtypeStruct((t, d), F32), jax.ShapeDtypeStruct((1, d), F32)],
        compiler_params=_params("arbitrary"),
    )(x, dh, nw, dres)


def _dsilu(g, sg):
    return sg * (1.0 + g * (1.0 - sg))


def _conv_taps_bwd(gpre, dg, w_ref, dwb_ref, taps, seq):
    pad = taps // 2
    dpre = None
    for k in range(taps):
        term = _shifted(dg, pad - k, seq) * w_ref[k:k + 1, :]
        dpre = term if dpre is None else dpre + term
        dwb_ref[k:k + 1, :] += jnp.sum(dg * _shifted(gpre, k - pad, seq), axis=0, keepdims=True)
    dwb_ref[SUBLANES - 1:SUBLANES, :] += jnp.sum(dg, axis=0, keepdims=True)
    return dpre


def _ffn_act_bwd(gu, dact, cw, cb, *, nb, seq):
    width = 256
    nj = D_FF // width

    def body(g_ref, u_ref, da_ref, w_ref, b_ref, dg_ref, du_ref, dwb_ref):
        @pl.when(pl.program_id(1) == 0)
        def _():
            dwb_ref[...] = jnp.zeros_like(dwb_ref)

        gpre = g_ref[...]
        g = _conv_taps(gpre, w_ref, FFN_TAPS, seq) + b_ref[...]
        sg = _sigmoid(g)
        da = da_ref[...].astype(F32)
        du_ref[...] = (da * g * sg).astype(BF16)
        dgc = da * u_ref[...] * _dsilu(g, sg)
        dg_ref[...] = _conv_taps_bwd(gpre, dgc, w_ref, dwb_ref, FFN_TAPS, seq).astype(BF16)

    blk = lambda off: pl.BlockSpec((seq, width), lambda j, b: (b, j + off))
    return pl.pallas_call(
        body, name="ffn_act_bwd", grid=(nj, nb),
        in_specs=[blk(0), blk(nj), blk(0),
                  pl.BlockSpec((FFN_TAPS, width), lambda j, b: (0, j)),
                  pl.BlockSpec((1, width), lambda j, b: (0, j))],
        out_specs=[blk(0), blk(0), pl.BlockSpec((SUBLANES, width), lambda j, b: (0, j))],
        out_shape=[jax.ShapeDtypeStruct((nb * seq, D_FF), BF16), jax.ShapeDtypeStruct((nb * seq, D_FF), BF16),
                   jax.ShapeDtypeStruct((SUBLANES, D_FF), F32)],
        compiler_params=_params("parallel", "arbitrary"),
    )(gu, gu, dact, cw, cb)


def _ssm_conv_bwd(proj, pair, cw, cb, *, nb, seq, name, width, proj_col, conv_col, ncol, extra=None, scale=None):
    has_extra = extra is not None

    def body(*refs):
        if has_extra:
            x_ref, p_ref, w_ref, b_ref, e_ref, s_ref, dx_ref, dwb_ref = refs
        else:
            x_ref, p_ref, w_ref, b_ref, dx_ref, dwb_ref = refs

        @pl.when(pl.program_id(1) == 0)
        def _():
            dwb_ref[...] = jnp.zeros_like(dwb_ref)

        gpre = x_ref[...]
        g = _conv_taps(gpre, w_ref, SSM_TAPS, seq) + b_ref[...]
        sg = _sigmoid(g)
        da = p_ref[0] + p_ref[1]
        if has_extra:
            da = da + e_ref[...] * s_ref[...]
        dx_ref[...] = _conv_taps_bwd(gpre, da * _dsilu(g, sg), w_ref, dwb_ref, SSM_TAPS, seq)

    in_specs = [pl.BlockSpec((seq, width), lambda j, b: (b, j + proj_col)),
                pl.BlockSpec((2, seq, width), lambda j, b: (0, b, j)),
                pl.BlockSpec((SSM_TAPS, width), lambda j, b: (0, j + conv_col)),
                pl.BlockSpec((1, width), lambda j, b: (0, j + conv_col))]
    args = [proj, pair, cw, cb]
    if has_extra:
        in_specs += [pl.BlockSpec((seq, width), lambda j, b: (b, j)), pl.BlockSpec((1, width), lambda j, b: (0, j))]
        args += [extra, scale]
    return pl.pallas_call(
        body, name=name, grid=(ncol, nb), in_specs=in_specs,
        out_specs=[pl.BlockSpec((seq, width), lambda j, b: (b, j)), pl.BlockSpec((SUBLANES, width), lambda j, b: (0, j))],
        out_shape=[jax.ShapeDtypeStruct((nb * seq, ncol * width), F32), jax.ShapeDtypeStruct((SUBLANES, ncol * width), F32)],
        compiler_params=_params("parallel", "arbitrary"),
    )(*args)


def _attn_bwd(proj, dmix, y_att, lse, bias, sink, dbias_in, *, nb, seq):
    nblk = seq // CHUNK
    t = nb * seq
    scale = HEAD_DIM ** -0.5

    def body(q_ref, kp_ref, kc_ref, kn_ref, do_ref, o_ref, lse_ref, bias_ref, sink_ref, dbin_ref,
             dq_ref, dkv_ref, dbias_ref, dsink_ref):
        b, n = pl.program_id(0), pl.program_id(1)

        @pl.when(n == 0)
        def _():
            dkv_ref[...] = jnp.zeros_like(dkv_ref)

        @pl.when((n == 0) & (b == 0))
        def _():
            dbias_ref[...] = dbin_ref[...]
            dsink_ref[...] = jnp.zeros_like(dsink_ref)

        valid = _band_mask(n, nblk)
        lane = lax.broadcasted_iota(jnp.int32, (1, LANES), 1)
        dsink = jnp.zeros((1, LANES), F32)
        rows = pl.ds(pl.multiple_of(n * CHUNK, CHUNK), KEY_SPAN)
        for g in range(KV_HEADS):
            ks = slice(g * HEAD_DIM, (g + 1) * HEAD_DIM)
            vs = slice(KV_WIDTH + g * HEAD_DIM, KV_WIDTH + (g + 1) * HEAD_DIM)
            kcat = jnp.concatenate([kp_ref[:, ks], kc_ref[:, ks], kn_ref[:, ks]], axis=0).astype(BF16)
            vcat = jnp.concatenate([kp_ref[:, vs], kc_ref[:, vs], kn_ref[:, vs]], axis=0).astype(BF16)
            dk = jnp.zeros((KEY_SPAN, HEAD_DIM), F32)
            dv = jnp.zeros((KEY_SPAN, HEAD_DIM), F32)
            for r in range(Q_PER_KV):
                h = g * Q_PER_KV + r
                hs = slice(h * HEAD_DIM, (h + 1) * HEAD_DIM)
                q = q_ref[:, hs].astype(BF16)
                do = do_ref[:, hs]
                s = _dot(q, kcat, NT) * scale + bias_ref[h]
                s = jnp.where(valid, s, -jnp.inf)
                lse_h = lse_ref[:, h:h + 1]
                p = jnp.exp(s - lse_h)
                delta = jnp.sum(do * o_ref[:, hs].astype(F32), axis=-1, keepdims=True)
                do16 = do.astype(BF16)
                ds = p * (_dot(do16, vcat, NT) - delta)
                dbias_ref[h] += ds
                dsink_h = -jnp.sum(jnp.exp(sink_ref[:, h:h + 1] - lse_h) * delta, axis=0, keepdims=True)
                dsink = dsink + jnp.where(lane == h, dsink_h, 0.0)
                ds16 = (ds * scale).astype(BF16)
                dq_ref[:, hs] = _dot(ds16, kcat, NN)
                dk = dk + _dot(ds16, q, TN)
                dv = dv + _dot(p.astype(BF16), do16, TN)
            dkv_ref[0, rows, ks] += dk
            dkv_ref[0, rows, vs] += dv
        dsink_ref[...] += dsink

    blk = lambda cb: pl.BlockSpec((CHUNK, D_MODEL), lambda b, n: (b * nblk + n, cb))
    whole = pl.BlockSpec((N_HEADS, CHUNK, KEY_SPAN), lambda b, n: (0, 0, 0))
    vec = pl.BlockSpec((1, LANES), lambda b, n: (0, 0))
    return pl.pallas_call(
        body, name="attn_bwd", grid=(nb, nblk),
        in_specs=[blk(PQ // D_MODEL)] + _kv_specs(nblk) + [
            blk(1), blk(0), pl.BlockSpec((CHUNK, N_HEADS), lambda b, n: (b * nblk + n, 0)), whole, vec, whole],
        out_specs=[blk(0), pl.BlockSpec((1, seq + 2 * CHUNK, 2 * KV_WIDTH), lambda b, n: (b, 0, 0)), whole, vec],
        out_shape=[jax.ShapeDtypeStruct((t, D_MODEL), F32),
                   jax.ShapeDtypeStruct((nb, seq + 2 * CHUNK, 2 * KV_WIDTH), F32),
                   jax.ShapeDtypeStruct((N_HEADS, CHUNK, KEY_SPAN), F32),
                   jax.ShapeDtypeStruct((1, LANES), F32)],
        compiler_params=_params("arbitrary", "arbitrary"),
    )(proj, proj, proj, proj, dmix, y_att, lse, bias, sink, dbias_in)


def _gate_norm_bwd(y2, xbc, proj, dmix, dvec, nw, *, tm):
    t = xbc.shape[0]
    half = SSM_WIDTH // N_GROUPS

    def body(y_ref, xs_ref, z_ref, do_ref, d_ref, w_ref, dyv_ref, dz_ref, dd_ref, dw_ref):
        @pl.when(pl.program_id(0) == 0)
        def _():
            dd_ref[...] = jnp.zeros_like(dd_ref)
            dw_ref[...] = jnp.zeros_like(dw_ref)

        z = z_ref[...]
        xs = xs_ref[...]
        sg = _sigmoid(z)
        gz = z * sg
        yv = y_ref[0] + y_ref[1] + d_ref[...] * xs
        p = yv * gz
        do = do_ref[...]
        for g in range(N_GROUPS):
            cs = slice(g * half, (g + 1) * half)
            pg = p[:, cs]
            r = lax.rsqrt(jnp.mean(pg * pg, axis=-1, keepdims=True) + NORM_EPS)
            ph = pg * r
            gw = do[:, cs] * w_ref[:, cs]
            dp = r * (gw - ph * jnp.mean(gw * ph, axis=-1, keepdims=True))
            dyv = dp * gz[:, cs]
            dyv_ref[:, cs] = dyv
            dz_ref[:, cs] = dp * yv[:, cs] * _dsilu(z[:, cs], sg[:, cs])
            dw_ref[:, cs] += jnp.sum(do[:, cs] * ph, axis=0, keepdims=True)
            dd_ref[:, cs] += jnp.sum(dyv * xs[:, cs], axis=0, keepdims=True)

    row = lambda cb: pl.BlockSpec((tm, SSM_WIDTH), lambda i: (i, cb))
    vec = pl.BlockSpec((1, SSM_WIDTH), lambda i: (0, 0))
    return pl.pallas_call(
        body, name="gate_norm_bwd", grid=(t // tm,),
        in_specs=[pl.BlockSpec((2, tm, SSM_WIDTH), lambda i: (0, i, 0)), row(0), row(PZ // SSM_WIDTH), row(0), vec, vec],
        out_specs=[row(0), row(0), vec, vec],
        out_shape=[jax.ShapeDtypeStruct((t, SSM_WIDTH), F32), jax.ShapeDtypeStruct((t, SSM_WIDTH), F32),
                   jax.ShapeDtypeStruct((1, SSM_WIDTH), F32), jax.ShapeDtypeStruct((1, SSM_WIDTH), F32)],
        compiler_params=_params("arbitrary"),
    )(y2, xbc, proj, dmix, dvec, nw)


def _ssd_specs(nchunk, row):
    return [pl.BlockSpec((CHUNK, SSM_WIDTH), lambda d, b, c: (row(d, b, c), 0)),
            pl.BlockSpec((CHUNK, 2 * BC_WIDTH), lambda d, b, c: (row(d, b, c), SSM_WIDTH // (2 * BC_WIDTH))),
            pl.BlockSpec((CHUNK, LANES), lambda d, b, c: (row(d, b, c), PDT // LANES)),
            pl.BlockSpec((1, LANES), lambda d, b, c: (0, 0)),
            pl.BlockSpec((1, LANES), lambda d, b, c: (0, 0)),
            pl.BlockSpec((CHUNK, SSM_WIDTH), lambda d, b, c: (row(d, b, c), 0))]


_SSD_SCRATCH = [pltpu.VMEM((N_HEADS, N_STATE, HEAD_DIM), F32),
                pltpu.VMEM((CHUNK, N_HEADS), F32), pltpu.VMEM((N_HEADS, CHUNK), F32),
                pltpu.VMEM((CHUNK, N_HEADS), F32)]


def _ssd_bwd_c(xbc, proj, dtb, alog, dyv, *, nb, seq):
    nchunk = seq // CHUNK
    t = nb * seq
    row = _chunk_index(nchunk, True)

    def body(xs_ref, bc_ref, dt_ref, dtb_ref, alog_ref, dy_ref, dc_ref, h_ref, z_ref, zt_ref, dts_ref):
        d, c = pl.program_id(0), pl.program_id(2)

        @pl.when(c == 0)
        def _():
            h_ref[...] = jnp.zeros_like(h_ref)

        ahead, dt, tot, _ = _scan_setup(d, dt_ref, dtb_ref, alog_ref, z_ref, zt_ref, lower_when_dir0=True, inclusive=True)
        mask = ahead >= 0
        dts_ref[...] = dt
        e_tot = jnp.exp(tot)
        for g in range(N_GROUPS):
            bg = bc_ref[:, g * N_STATE:(g + 1) * N_STATE]
            cg = bc_ref[:, BC_WIDTH + g * N_STATE:BC_WIDTH + (g + 1) * N_STATE]
            bg16 = bg.astype(BF16)
            dcg = jnp.zeros((CHUNK, N_STATE), F32)
            for r in range(HEADS_PER_GROUP):
                h = g * HEADS_PER_GROUP + r
                hs = slice(h * HEAD_DIM, (h + 1) * HEAD_DIM)
                zc = z_ref[:, h:h + 1]
                zr = zt_ref[h:h + 1, :]
                decay = jnp.exp(jnp.where(mask, zc - zr, -jnp.inf))
                u = (xs_ref[:, hs] * dts_ref[:, h:h + 1]).astype(BF16)
                dy = dy_ref[:, hs].astype(BF16)
                state = h_ref[h]
                w1 = (_dot(dy, u, NT) * decay).astype(BF16)
                dch = _dot(w1, bg16, NN) + jnp.exp(zc) * _dot(dy, state.astype(BF16), NT)
                dcg = dcg + dch
                bw = (bg * jnp.exp(tot[:, h:h + 1] - zc)).astype(BF16)
                h_ref[h] = state * e_tot[:, h:h + 1] + _dot(bw, u, TN)
            dc_ref[0, :, g * N_STATE:(g + 1) * N_STATE] = dcg

    return pl.pallas_call(
        body, name="ssd_bwd_c", grid=(2, nb, nchunk), in_specs=_ssd_specs(nchunk, row),
        out_specs=pl.BlockSpec((1, CHUNK, BC_WIDTH), lambda d, b, c: (d, row(d, b, c), 0)),
        out_shape=jax.ShapeDtypeStruct((2, t, BC_WIDTH), F32),
        scratch_shapes=_SSD_SCRATCH,
        compiler_params=_params("arbitrary", "arbitrary", "arbitrary"),
    )(xbc, xbc, proj, dtb, alog, dyv)


def _ssd_bwd_x(xbc, proj, dtb, alog, dyv, states, *, nb, seq):
    nchunk = seq // CHUNK
    t = nb * seq
    row = _chunk_index(nchunk, False)

    def body(xs_ref, bc_ref, dt_ref, dtb_ref, alog_ref, dy_ref, hs_ref, dx_ref, db_ref, draw_ref, da_ref, dbias_ref,
             h_ref, z_ref, zt_ref, dts_ref):
        d, b, c = pl.program_id(0), pl.program_id(1), pl.program_id(2)

        @pl.when(c == 0)
        def _():
            h_ref[...] = jnp.zeros_like(h_ref)

        @pl.when((c == 0) & (b == 0) & (d == 0))
        def _():
            da_ref[...] = jnp.zeros_like(da_ref)
            dbias_ref[...] = jnp.zeros_like(dbias_ref)

        ahead, dt, tot, dt_all = _scan_setup(d, dt_ref, dtb_ref, alog_ref, z_ref, zt_ref,
                                             lower_when_dir0=False, inclusive=False)
        mask = ahead >= 0
        mask_t = ahead <= 0
        dts_ref[...] = dt
        e_tot = jnp.exp(tot)
        lane = lax.broadcasted_iota(jnp.int32, (1, LANES), 1)
        ddt = jnp.zeros((CHUNK, LANES), F32)
        inner = jnp.zeros((CHUNK, LANES), F32)
        outer = jnp.zeros((CHUNK, LANES), F32)
        span = jnp.zeros((1, LANES), F32)
        for g in range(N_GROUPS):
            bg = bc_ref[:, g * N_STATE:(g + 1) * N_STATE]
            cg = bc_ref[:, BC_WIDTH + g * N_STATE:BC_WIDTH + (g + 1) * N_STATE]
            bg16 = bg.astype(BF16)
            cg16 = cg.astype(BF16)
            bc_t = _dot(bg16, cg16, NT)
            cb = _dot(cg16, bg16, NT)
            dbg = jnp.zeros((CHUNK, N_STATE), F32)
            for r in range(HEADS_PER_GROUP):
                h = g * HEADS_PER_GROUP + r
                hs = slice(h * HEAD_DIM, (h + 1) * HEAD_DIM)
                here = lane == d * N_HEADS + h
                zc = z_ref[:, h:h + 1]
                zr = zt_ref[h:h + 1, :]
                decay = jnp.exp(jnp.where(mask, zc - zr, -jnp.inf))
                decay_t = jnp.exp(jnp.where(mask_t, zr - zc, -jnp.inf))
                x_h = xs_ref[:, hs]
                dt_h = dts_ref[:, h:h + 1]
                u = (x_h * dt_h).astype(BF16)
                dy = dy_ref[:, hs].astype(BF16)
                state = h_ref[h]
                st16 = state.astype(BF16)
                fstate = hs_ref[0, 0, h]
                e_z = jnp.exp(zc)
                e_tz = jnp.exp(tot[:, h:h + 1] - zc)
                du = _dot((bc_t * decay).astype(BF16), dy, NN) + _dot((bg * e_z).astype(BF16), st16, NN)
                w2 = _dot(u, dy, NT) * decay
                db_out = e_z * _dot(u, st16, NT)
                dbg = dbg + _dot(w2.astype(BF16), cg16, NN) + db_out
                col_in = jnp.sum(w2 * bc_t, axis=-1, keepdims=True)
                row_in = jnp.sum(_dot(dy, u, NT) * decay_t * cb, axis=-1, keepdims=True)
                row_out = e_tz * jnp.sum(cg * _dot(dy, fstate.astype(BF16), NT), axis=-1, keepdims=True)
                col_out = jnp.sum(db_out * bg, axis=-1, keepdims=True)
                inner = inner + jnp.where(here, row_in + row_out - col_in, 0.0)
                outer = outer + jnp.where(here, col_out, 0.0)
                span = span + jnp.where(here, e_tot[:, h:h + 1] * jnp.sum(fstate * state), 0.0)
                ddt = ddt + jnp.where(here, jnp.sum(du * x_h, axis=-1, keepdims=True), 0.0)
                dx_ref[0, :, hs] = du * dt_h
                h_ref[h] = state * e_tot[:, h:h + 1] + _dot((cg * e_tz).astype(BF16), dy, TN)
            db_ref[0, :, g * N_STATE:(g + 1) * N_STATE] = dbg
        tri = mask.astype(F32)
        dadt = (jnp.dot(tri, inner, precision=HIGHEST, preferred_element_type=F32)
                + jnp.dot(1.0 - tri, outer, precision=HIGHEST, preferred_element_type=F32) + span)
        a = -jnp.exp(alog_ref[...])
        draw = (ddt + a * dadt) * _sigmoid(dt_ref[...] + dtb_ref[...])
        draw_ref[0] = draw
        da_ref[...] += jnp.sum(dt_all * dadt, axis=0, keepdims=True) * a
        dbias_ref[...] += jnp.sum(draw, axis=0, keepdims=True)

    out_row = lambda w: pl.BlockSpec((1, CHUNK, w), lambda d, b, c: (d, row(d, b, c), 0))
    vec = pl.BlockSpec((1, LANES), lambda d, b, c: (0, 0))
    return pl.pallas_call(
        body, name="ssd_bwd_x", grid=(2, nb, nchunk),
        in_specs=_ssd_specs(nchunk, row) + [
            pl.BlockSpec((1, 1, N_HEADS, N_STATE, HEAD_DIM), lambda d, b, c: (d, row(d, b, c), 0, 0, 0))],
        out_specs=[out_row(SSM_WIDTH), out_row(BC_WIDTH), out_row(LANES), vec, vec],
        out_shape=[jax.ShapeDtypeStruct((2, t, SSM_WIDTH), F32), jax.ShapeDtypeStruct((2, t, BC_WIDTH), F32),
                   jax.ShapeDtypeStruct((2, t, LANES), F32), jax.ShapeDtypeStruct((1, LANES), F32),
                   jax.ShapeDtypeStruct((1, LANES), F32)],
        scratch_shapes=_SSD_SCRATCH,
        compiler_params=_params("arbitrary", "arbitrary", "arbitrary"),
    )(xbc, xbc, proj, dtb, alog, dyv, states)


def _layer_bwd(dx2, p, s, band_bias, dbias_in, *, nb, seq):
    t = nb * seq
    x, proj, xbc, x1 = s["x"], s["proj"], s["xbc"], s["x1"]
    dact = _matmul(dx2, p["w_down"], "nt", name="down_proj_dx", tm=512, tn=1408, tk=1024, out_dtype=BF16)
    g_w_down = _matmul(s["act"], dx2, "tn", name="down_proj_dw", tm=1408, tn=1024, tk=1024)
    dg, du, dwb_ffn = _ffn_act_bwd(s["gu"], dact, p["ffn_cw"], p["ffn_cb"], nb=nb, seq=seq)
    dh2 = _matmul(dg, p["w_up"][:, :D_FF], "nt", name="up_proj_dx_g", tm=512, tn=1024, tk=1408)
    dh2 = _matmul(du, p["w_up"][:, D_FF:], "nt", name="up_proj_dx_u", tm=512, tn=1024, tk=1408, res=dh2)
    g_w_up = jnp.concatenate([_matmul(s["h2"], dg, "tn", name="up_proj_dw_g", tm=1024, tn=1408, tk=1024),
                              _matmul(s["h2"], du, "tn", name="up_proj_dw_u", tm=1024, tn=1408, tk=1024)], axis=1)
    dx1, g_n2 = _rms_bwd(x1, dh2, p["n2"], dx2, tm=512, name="norm2_bwd")
    dmix = _matmul(dx1, p["w_out"], "nt", name="out_proj_dx", tm=512, tn=1024, tk=1024)
    g_w_out = jnp.concatenate([_matmul(s["y_ssm"], dx1, "tn", name="out_proj_dw_ssm", tm=1024, tn=1024, tk=1024),
                               _matmul(s["y_att"], dx1, "tn", name="out_proj_dw_att", tm=1024, tn=1024, tk=1024)], axis=0)
    dq, dkv, dbias, dsink = _attn_bwd(proj, dmix, s["y_att"], s["lse"], band_bias, p["sink"], dbias_in, nb=nb, seq=seq)
    dkv = dkv[:, CHUNK:CHUNK + seq, :].reshape(t, 2 * KV_WIDTH)
    dyv, dz, g_dvec, g_ssm_nw = _gate_norm_bwd(s["y2"], xbc, proj, dmix, p["dvec"], p["ssm_nw"], tm=256)
    dc2 = _ssd_bwd_c(xbc, proj, p["dtb"], p["alog"], dyv, nb=nb, seq=seq)
    dxs2, db2, draw2, g_alog, g_dtb = _ssd_bwd_x(xbc, proj, p["dtb"], p["alog"], dyv, s["states"], nb=nb, seq=seq)
    ddt_raw = draw2[0] + draw2[1]
    conv = dict(nb=nb, seq=seq)
    dxs_pre, dwb_xs = _ssm_conv_bwd(proj, dxs2, p["conv_w"], p["conv_b"], name="ssm_conv_bwd_x", width=256,
                                    proj_col=PXS // 256, conv_col=0, ncol=4, extra=dyv, scale=p["dvec"], **conv)
    db_pre, dwb_b = _ssm_conv_bwd(proj, db2, p["conv_w"], p["conv_b"], name="ssm_conv_bwd_b", width=256,
                                  proj_col=PB // 256, conv_col=SSM_WIDTH // 256, ncol=1, **conv)
    dc_pre, dwb_c = _ssm_conv_bwd(proj, dc2, p["conv_w"], p["conv_b"], name="ssm_conv_bwd_c", width=256,
                                  proj_col=PC // 256, conv_col=(SSM_WIDTH + BC_WIDTH) // 256, ncol=1, **conv)
    dwb_ssm = jnp.concatenate([dwb_xs, dwb_b, dwb_c], axis=1)
    dproj = jnp.concatenate([dz, dxs_pre, dq, db_pre, dc_pre, dkv, ddt_raw], axis=1).astype(BF16)
    dh1 = _matmul(dproj, p["w_in"], "nt", name="in_proj_dx", tm=512, tn=1024, tk=1408)
    g_w_in = _matmul(s["h1"], dproj, "tn", name="in_proj_dw", tm=1024, tn=1408, tk=1024)
    dx, g_n1 = _rms_bwd(x, dh1, p["n1"], dx1, tm=512, name="norm1_bwd")
    grads = dict(n1=g_n1, w_in=g_w_in, conv_w=dwb_ssm[:SSM_TAPS], conv_b=dwb_ssm[SUBLANES - 1], dtb=g_dtb, alog=g_alog,
                 dvec=g_dvec, ssm_nw=g_ssm_nw, sink=dsink, w_out=g_w_out, n2=g_n2, w_up=g_w_up,
                 ffn_cw=dwb_ffn[:FFN_TAPS], ffn_cb=dwb_ffn[SUBLANES - 1], w_down=g_w_down)
    return dx, dbias, grads


def _rel_bias_grad(dbias, bucket):
    def body(d_ref, b_ref, o_ref):
        o_ref[...] = jnp.zeros_like(o_ref)
        lane = lax.broadcasted_iota(jnp.int32, (1, LANES), 1)

        def per_bucket(k, carry):
            hit = b_ref[...] == k
            for h in range(N_HEADS):
                part = jnp.sum(jnp.where(hit, d_ref[h], 0.0), axis=1, keepdims=True)
                o_ref[h:h + 1, :] += jnp.where(lane == k, jnp.sum(part, axis=0, keepdims=True), 0.0)
            return carry

        lax.fori_loop(0, REL_BUCKETS, per_bucket, 0)

    return pl.pallas_call(
        body, name="rel_bias_grad", out_shape=jax.ShapeDtypeStruct((N_HEADS, LANES), F32),
        compiler_params=pltpu.CompilerParams(vmem_limit_bytes=VMEM_LIMIT_BYTES),
    )(dbias, bucket)


def _exchange(arrs, *, gather, name):
    n = len(arrs)
    npeer = N_DEV - 1

    def body(*refs):
        ins, outs = refs[:n], refs[n:2 * n]
        send_sems, recv_sems, local_sems = refs[2 * n:]
        x, y, c = lax.axis_index("x"), lax.axis_index("y"), lax.axis_index("c")
        me = 4 * x + 2 * y + c
        peers = []
        for k in range(1, N_DEV):
            px, py, pc = x ^ ((k >> 2) & 1), y ^ ((k >> 1) & 1), c ^ (k & 1)
            peers.append(((px, py, pc), 4 * px + 2 * py + pc))
        local, sends, recvs = [], [], []
        for i in range(n):
            mine = ins[i] if gather else ins[i].at[me]
            local.append(pltpu.make_async_copy(mine, outs[i].at[me], local_sems.at[i]))
            for k, (pid, pslot) in enumerate(peers):
                src = ins[i] if gather else ins[i].at[pslot]
                sem = i * npeer + k
                sends.append(pltpu.make_async_remote_copy(
                    src_ref=src, dst_ref=outs[i].at[me], send_sem=send_sems.at[sem], recv_sem=recv_sems.at[sem],
                    device_id=pid, device_id_type=pl.DeviceIdType.MESH))
                recvs.append(pltpu.make_async_remote_copy(
                    src_ref=src, dst_ref=outs[i].at[pslot], send_sem=send_sems.at[sem], recv_sem=recv_sems.at[sem],
                    device_id=pid, device_id_type=pl.DeviceIdType.MESH))
        for cp in local + sends:
            cp.start()
        for cp in recvs:
            cp.wait_recv()
        for cp in sends:
            cp.wait_send()
        for cp in local:
            cp.wait()

    any_spec = pl.BlockSpec(memory_space=pl.ANY)
    out_shape = [jax.ShapeDtypeStruct((N_DEV,) + (a.shape if gather else a.shape[1:]), a.dtype) for a in arrs]
    return pl.pallas_call(
        body, name=name, in_specs=[any_spec] * n, out_specs=[any_spec] * n, out_shape=out_shape,
        scratch_shapes=[pltpu.SemaphoreType.DMA((n * npeer,)), pltpu.SemaphoreType.DMA((n * npeer,)),
                        pltpu.SemaphoreType.DMA((n,))],
        compiler_params=pltpu.CompilerParams(has_side_effects=True),
    )(*arrs)


def _adamw(parts, w, m, v, *, name, tr):
    r, c = w.shape
    assert r % tr == 0
    c1 = 1.0 - ADAM_B1 ** ADAM_STEP
    c2 = 1.0 - ADAM_B2 ** ADAM_STEP

    def body(p_ref, w_ref, m_ref, v_ref, g_ref, d_ref, nm_ref, nv_ref):
        g = p_ref[0].astype(F32)
        for j in range(1, N_DEV):
            g = g + p_ref[j].astype(F32)
        nm = ADAM_B1 * m_ref[...] + (1.0 - ADAM_B1) * g
        nv = ADAM_B2 * v_ref[...] + (1.0 - ADAM_B2) * (g * g)
        g_ref[...] = g
        nm_ref[...] = nm
        nv_ref[...] = nv
        d_ref[...] = -ADAM_LR * ((nm / c1) / (jnp.sqrt(nv / c2) + ADAM_EPS) + ADAM_WD * w_ref[...])

    blk = pl.BlockSpec((tr, c), lambda i: (i, 0))
    return pl.pallas_call(
        body, name=name, grid=(r // tr,),
        in_specs=[pl.BlockSpec((N_DEV, tr, c), lambda i: (0, i, 0)), blk, blk, blk],
        out_specs=[blk] * 4, out_shape=[jax.ShapeDtypeStruct((r, c), F32)] * 4,
        compiler_params=_params("parallel"),
    )(parts, w, m, v)


def _t5_bucket(rel):
    half = REL_BUCKETS // 2
    max_exact = half // 2
    ret = jnp.where(rel > 0, half, 0)
    n = jnp.abs(rel)
    nf = jnp.maximum(n, 1).astype(F32)
    large = max_exact + (jnp.log(nf / max_exact) / math.log(CHUNK / max_exact) * (half - max_exact)).astype(jnp.int32)
    large = jnp.minimum(large, half - 1)
    return ret + jnp.where(n < max_exact, n, large)


def _split16(w):
    hi = w.astype(BF16)
    return hi, (w - hi.astype(F32)).astype(BF16)


def _cols_to_blocks(g, depth):
    _, r, c8 = g.shape
    return g.reshape(depth, r, N_DEV, c8 // N_DEV).transpose(2, 0, 1, 3).reshape(N_DEV, depth * r, c8 // N_DEV)


def _rows_to_blocks(g, depth):
    _, r8, c = g.shape
    return g.reshape(depth, N_DEV, r8 // N_DEV, c).transpose(1, 0, 2, 3).reshape(N_DEV, depth * r8 // N_DEV, c)


def _blocks_to_cols(a, depth):
    _, dr, c = a.shape
    r = dr // depth
    return a.reshape(N_DEV, depth, r, c).transpose(1, 2, 0, 3).reshape(depth, r, N_DEV * c)


def _blocks_to_rows(a, depth):
    _, dr, c = a.shape
    r = dr // depth
    return a.reshape(N_DEV, depth, r, c).transpose(1, 0, 2, 3).reshape(depth, N_DEV * r, c)


_SMALL = ("rel_bias", "norm1_w", "conv_b", "dt_bias", "a_log", "d_skip", "ssm_norm_w", "attn_sink", "norm2_w",
          "ffn_conv_b", "final_norm_w")
_SHARDED = ("w_in", "conv_w", "w_out", "w_up", "ffn_conv_w", "w_down")
_ORDER = ("rel_bias", "norm1_w", "w_in", "conv_w", "conv_b", "dt_bias", "a_log", "d_skip", "ssm_norm_w", "attn_sink",
          "w_out", "norm2_w", "w_up", "ffn_conv_w", "ffn_conv_b", "w_down", "final_norm_w")


def _pack_small(d):
    flat = jnp.concatenate([d[k].reshape(-1).astype(F32) for k in _SMALL])
    rows = -(-flat.size // (LANES * SUBLANES)) * SUBLANES
    return jnp.pad(flat, (0, rows * LANES - flat.size)).reshape(rows, LANES)


def _unpack_small(packed, like):
    flat = packed.reshape(-1)
    out, off = {}, 0
    for k in _SMALL:
        out[k] = flat[off:off + like[k].size].reshape(like[k].shape)
        off += like[k].size
    return out


def kernel(x, rel_bias, norm1_w, w_in, conv_w, conv_b, dt_bias, a_log, d_skip, ssm_norm_w, attn_sink, w_out, norm2_w, w_up, ffn_conv_w, ffn_conv_b, w_down, final_norm_w, loss_target, m_rel_bias, m_norm1_w, m_w_in, m_conv_w, m_conv_b, m_dt_bias, m_a_log, m_d_skip, m_ssm_norm_w, m_attn_sink, m_w_out, m_norm2_w, m_w_up, m_ffn_conv_w, m_ffn_conv_b, m_w_down, m_final_norm_w, v_rel_bias, v_norm1_w, v_w_in, v_conv_w, v_conv_b, v_dt_bias, v_a_log, v_d_skip, v_ssm_norm_w, v_attn_sink, v_w_out, v_norm2_w, v_w_up, v_ffn_conv_w, v_ffn_conv_b, v_w_down, v_final_norm_w):
    w = dict(rel_bias=rel_bias, norm1_w=norm1_w, w_in=w_in, conv_w=conv_w, conv_b=conv_b, dt_bias=dt_bias, a_log=a_log,
             d_skip=d_skip, ssm_norm_w=ssm_norm_w, attn_sink=attn_sink, w_out=w_out, norm2_w=norm2_w, w_up=w_up,
             ffn_conv_w=ffn_conv_w, ffn_conv_b=ffn_conv_b, w_down=w_down, final_norm_w=final_norm_w)
    m = dict(rel_bias=m_rel_bias, norm1_w=m_norm1_w, w_in=m_w_in, conv_w=m_conv_w, conv_b=m_conv_b, dt_bias=m_dt_bias,
             a_log=m_a_log, d_skip=m_d_skip, ssm_norm_w=m_ssm_norm_w, attn_sink=m_attn_sink, w_out=m_w_out,
             norm2_w=m_norm2_w, w_up=m_w_up, ffn_conv_w=m_ffn_conv_w, ffn_conv_b=m_ffn_conv_b, w_down=m_w_down,
             final_norm_w=m_final_norm_w)
    v = dict(rel_bias=v_rel_bias, norm1_w=v_norm1_w, w_in=v_w_in, conv_w=v_conv_w, conv_b=v_conv_b, dt_bias=v_dt_bias,
             a_log=v_a_log, d_skip=v_d_skip, ssm_norm_w=v_ssm_norm_w, attn_sink=v_attn_sink, w_out=v_w_out,
             norm2_w=v_norm2_w, w_up=v_w_up, ffn_conv_w=v_ffn_conv_w, ffn_conv_b=v_ffn_conv_b, w_down=v_w_down,
             final_norm_w=v_final_norm_w)
    nb, seq, _ = x.shape
    t = nb * seq
    depth = w_in.shape[0]

    flat2 = lambda a: a.reshape(-1, a.shape[-1])
    cw_hi, cw_lo = _split16(flat2(conv_w))
    fw_hi, fw_lo = _split16(flat2(ffn_conv_w))
    shards = [flat2(w_in).astype(BF16), flat2(w_out).astype(BF16), flat2(w_up).astype(BF16), flat2(w_down).astype(BF16),
              cw_hi, cw_lo, fw_hi, fw_lo]
    g_in, g_out, g_up, g_down, g_cwh, g_cwl, g_fwh, g_fwl = _exchange(shards, gather=True, name="gather_weights")
    full_w_in = _to_proj_layout(_blocks_to_cols(g_in, depth))
    full_w_out = _blocks_to_rows(g_out, depth)
    full_w_up = _blocks_to_cols(g_up, depth)
    full_w_down = _blocks_to_rows(g_down, depth)
    full_conv_w = _blocks_to_cols(g_cwh.astype(F32) + g_cwl.astype(F32), depth)
    full_ffn_cw = _blocks_to_cols(g_fwh.astype(F32) + g_fwl.astype(F32), depth)

    rel = jnp.arange(KEY_SPAN)[None, :] - CHUNK - jnp.arange(CHUNK)[:, None]
    bucket = _t5_bucket(rel)
    band_bias = rel_bias[bucket].transpose(2, 0, 1)

    def layer_params(i):
        return dict(n1=norm1_w[i][None], w_in=full_w_in[i], conv_w=full_conv_w[i], conv_b=conv_b[i][None],
                    dtb=_pad_lanes(dt_bias[i].reshape(-1)), alog=_pad_lanes(a_log[i].reshape(-1)),
                    dvec=jnp.repeat(d_skip[i], HEAD_DIM)[None], ssm_nw=ssm_norm_w[i][None], sink=_pad_lanes(attn_sink[i]),
                    w_out=full_w_out[i], n2=norm2_w[i][None], w_up=full_w_up[i], ffn_cw=full_ffn_cw[i],
                    ffn_cb=ffn_conv_b[i][None], w_down=full_w_down[i])

    h = x.reshape(t, D_MODEL)
    params, saved = [], []
    for i in range(depth):
        params.append(layer_params(i))
        h, s = _layer_fwd(h, params[i], band_bias, nb=nb, seq=seq)
        saved.append(s)
    dh, g_final, loss_part = _loss_head(h, loss_target.reshape(t, D_MODEL), final_norm_w[None], tm=512)
    loss = lax.psum(loss_part[0, 0], ("x", "y", "c"))

    dbias = jnp.zeros((N_HEADS, CHUNK, KEY_SPAN), F32)
    grads = [None] * depth
    for i in reversed(range(depth)):
        dh, dbias, grads[i] = _layer_bwd(dh, params[i], saved[i], band_bias, dbias, nb=nb, seq=seq)
    grad_x = dh.reshape(nb, seq, D_MODEL)
    stack = lambda k: jnp.stack([g[k] for g in grads])

    blocks = [_cols_to_blocks(_from_proj_layout(stack("w_in")), depth), _cols_to_blocks(stack("conv_w"), depth),
              _rows_to_blocks(stack("w_out"), depth), _cols_to_blocks(stack("w_up"), depth),
              _cols_to_blocks(stack("ffn_cw"), depth), _rows_to_blocks(stack("w_down"), depth)]
    parts = _exchange([b.astype(BF16) for b in blocks], gather=False, name="scatter_grads")
    out = {}
    for k, p8 in zip(_SHARDED, parts):
        rows = p8.shape[1]
        tr = 256 if rows % 256 == 0 else (128 if rows % 128 == 0 else rows)
        res = _adamw(p8, flat2(w[k]), flat2(m[k]), flat2(v[k]), name="adamw_" + k, tr=tr)
        out[k] = [a.reshape(w[k].shape) for a in res]

    small = dict(rel_bias=_rel_bias_grad(dbias, bucket)[:, :REL_BUCKETS].T, norm1_w=stack("n1"), conv_b=stack("conv_b"),
                 dt_bias=stack("dtb")[:, 0, :2 * N_HEADS], a_log=stack("alog")[:, 0, :2 * N_HEADS],
                 d_skip=stack("dvec").reshape(depth, N_HEADS, HEAD_DIM).sum(-1), ssm_norm_w=stack("ssm_nw"),
                 attn_sink=stack("sink")[:, 0, :N_HEADS], norm2_w=stack("n2"), ffn_conv_b=stack("ffn_cb"),
                 final_norm_w=g_final)
    (small_parts,) = _exchange([_pack_small(small)], gather=True, name="gather_small_grads")
    res = _adamw(small_parts, _pack_small(w), _pack_small(m), _pack_small(v), name="adamw_small", tr=small_parts.shape[1])
    unpacked = [_unpack_small(a, w) for a in res]
    for k in _SMALL:
        out[k] = [u[k] for u in unpacked]

    return (loss, grad_x, *[out[k][0] for k in _ORDER], *[out[k][1] for k in _ORDER],
            *[out[k][2] for k in _ORDER], *[out[k][3] for k in _ORDER])
```

```python
import math

import numpy as np
import jax
import jax.numpy as jnp
from jax import lax
from jax.experimental import pallas as pl
from jax.experimental.pallas import tpu as pltpu

F32, BF16 = jnp.float32, jnp.bfloat16
HIGHEST = lax.Precision.HIGHEST

D_MODEL = 1024
HEAD_DIM = 64
N_HEADS = 16
N_GROUPS = 2
HEADS_PER_GROUP = N_HEADS // N_GROUPS
N_STATE = 128
SSM_WIDTH = 1024
BC_WIDTH = 256
CONV_CH = SSM_WIDTH + 2 * BC_WIDTH
SSM_TAPS = 7
CHUNK = 128
KV_HEADS = 4
KV_WIDTH = 256
Q_PER_KV = N_HEADS // KV_HEADS
KEY_SPAN = 3 * CHUNK
REL_BUCKETS = 32
D_FF = 2816
FFN_TAPS = 3
IN_COLS = 4128
NORM_EPS = 1e-6
N_DEV = 8

LANES = 128
SUBLANES = 8
VMEM_LIMIT_BYTES = 56 * 1024 * 1024

PZ, PXS, PQ, PB, PC, PK, PV, PDT, PROJ_W = 0, 1024, 2048, 3072, 3328, 3584, 3840, 4096, 4224
OZ, OXBC, ODT, OQ, OK_, OV = 0, 1024, 2560, 2592, 3616, 3872

ADAM_LR, ADAM_B1, ADAM_B2, ADAM_EPS, ADAM_WD, ADAM_STEP = 0.001, 0.9, 0.999, 1e-08, 0.01, 10


def _params(*sem):
    return pltpu.CompilerParams(dimension_semantics=sem, vmem_limit_bytes=VMEM_LIMIT_BYTES)


def _sigmoid(x):
    return 1.0 / (1.0 + jnp.exp(-x))


def _softplus(x):
    return jnp.maximum(x, 0.0) + jnp.log(1.0 + jnp.exp(-jnp.abs(x)))


def _dot(a, b, dims):
    return lax.dot_general(a, b, (dims, ((), ())), preferred_element_type=F32)


NN = ((1,), (0,))
NT = ((1,), (1,))
TN = ((0,), (0,))


def _matmul(a, b, mode, *, name, tm, tn, tk, res=None, out_dtype=F32, precision=None):
    if mode == "nn":
        (m, k), (k2, n) = a.shape, b.shape
        a_spec = pl.BlockSpec((tm, tk), lambda i, j, kk: (i, kk))
        b_spec = pl.BlockSpec((tk, tn), lambda i, j, kk: (kk, j))
        dims = NN
    elif mode == "nt":
        (m, k), (n, k2) = a.shape, b.shape
        a_spec = pl.BlockSpec((tm, tk), lambda i, j, kk: (i, kk))
        b_spec = pl.BlockSpec((tn, tk), lambda i, j, kk: (j, kk))
        dims = NT
    else:
        (k, m), (k2, n) = a.shape, b.shape
        a_spec = pl.BlockSpec((tk, tm), lambda i, j, kk: (kk, i))
        b_spec = pl.BlockSpec((tk, tn), lambda i, j, kk: (kk, j))
        dims = TN
    assert k == k2 and m % tm == 0 and n % tn == 0 and k % tk == 0, (name, a.shape, b.shape, tm, tn, tk)
    nk = k // tk
    has_res = res is not None

    def body(*refs):
        if has_res:
            a_ref, b_ref, r_ref, o_ref, acc = refs
        else:
            a_ref, b_ref, o_ref, acc = refs
        kk = pl.program_id(2)

        @pl.when(kk == 0)
        def _():
            acc[...] = jnp.zeros_like(acc)

        if precision is None:
            part = _dot(a_ref[...].astype(BF16), b_ref[...].astype(BF16), dims)
        else:
            part = lax.dot_general(a_ref[...], b_ref[...], (dims, ((), ())), precision=precision,
                                   preferred_element_type=F32)
        acc[...] += part

        @pl.when(kk == nk - 1)
        def _():
            r = acc[...]
            if has_res:
                r = r + r_ref[...].astype(F32)
            o_ref[...] = r.astype(out_dtype)

    in_specs = [a_spec, b_spec]
    args = [a, b]
    if has_res:
        in_specs.append(pl.BlockSpec((tm, tn), lambda i, j, kk: (i, j)))
        args.append(res)
    return pl.pallas_call(
        body, name=name, grid=(m // tm, n // tn, nk),
        in_specs=in_specs, out_specs=pl.BlockSpec((tm, tn), lambda i, j, kk: (i, j)),
        out_shape=jax.ShapeDtypeStruct((m, n), out_dtype),
        scratch_shapes=[pltpu.VMEM((tm, tn), F32)],
        compiler_params=_params("parallel", "parallel", "arbitrary"),
    )(*args)


def _rms_matmul(x, nw, w, *, name, tm, tn):
    t, d = x.shape
    n = w.shape[1]
    assert t % tm == 0 and n % tn == 0

    def body(x_ref, nw_ref, w_ref, o_ref, h_ref):
        @pl.when(pl.program_id(1) == 0)
        def _():
            xv = x_ref[...]
            r = lax.rsqrt(jnp.mean(xv * xv, axis=-1, keepdims=True) + NORM_EPS)
            h_ref[...] = (xv * r * nw_ref[...]).astype(BF16)

        o_ref[...] = _dot(h_ref[...], w_ref[...].astype(BF16), NN)

    return pl.pallas_call(
        body, name=name, grid=(t // tm, n // tn),
        in_specs=[pl.BlockSpec((tm, d), lambda i, j: (i, 0)),
                  pl.BlockSpec((1, d), lambda i, j: (0, 0)),
                  pl.BlockSpec((d, tn), lambda i, j: (0, j))],
        out_specs=[pl.BlockSpec((tm, tn), lambda i, j: (i, j)),
                   pl.BlockSpec((tm, d), lambda i, j: (i, 0))],
        out_shape=[jax.ShapeDtypeStruct((t, n), F32), jax.ShapeDtypeStruct((t, d), BF16)],
        compiler_params=_params("parallel", "arbitrary"),
    )(x, nw, w)


def _zero_ext(v):
    z = jnp.zeros((SUBLANES, v.shape[1]), v.dtype)
    return jnp.concatenate([z, v, z], axis=0)


def _shifted(v_ext, offset, seq):
    if offset == 0:
        return v_ext[SUBLANES:SUBLANES + seq]
    return pltpu.roll(v_ext, (-offset) % (seq + 2 * SUBLANES), 0)[SUBLANES:SUBLANES + seq]


def _conv_taps(v, w_ref, taps, seq):
    pad = taps // 2
    v_ext = _zero_ext(v)
    acc = None
    for k in range(taps):
        term = _shifted(v_ext, k - pad, seq) * w_ref[k:k + 1, :]
        acc = term if acc is None else acc + term
    return acc


def _ssm_conv_fwd(proj, cw, cb, *, nb, seq):
    width = 512

    def body(x_ref, w_ref, b_ref, o_ref):
        g = _conv_taps(x_ref[...], w_ref, SSM_TAPS, seq) + b_ref[...]
        o_ref[...] = g * _sigmoid(g)

    def col(j):
        return jnp.where(j < 2, j + PXS // width, PB // width)

    return pl.pallas_call(
        body, name="ssm_conv_fwd", grid=(nb, CONV_CH // width),
        in_specs=[pl.BlockSpec((seq, width), lambda b, j: (b, col(j))),
                  pl.BlockSpec((SSM_TAPS, width), lambda b, j: (0, j)),
                  pl.BlockSpec((1, width), lambda b, j: (0, j))],
        out_specs=pl.BlockSpec((seq, width), lambda b, j: (b, j)),
        out_shape=jax.ShapeDtypeStruct((nb * seq, CONV_CH), F32),
        compiler_params=_params("parallel", "parallel"),
    )(proj, cw, cb)


def _ffn_act_fwd(gu, cw, cb, *, nb, seq):
    width = 256
    nj = D_FF // width

    def body(g_ref, u_ref, w_ref, b_ref, o_ref):
        g = _conv_taps(g_ref[...], w_ref, FFN_TAPS, seq) + b_ref[...]
        o_ref[...] = (g * _sigmoid(g) * u_ref[...]).astype(BF16)

    return pl.pallas_call(
        body, name="ffn_act_fwd", grid=(nb, nj),
        in_specs=[pl.BlockSpec((seq, width), lambda b, j: (b, j)),
                  pl.BlockSpec((seq, width), lambda b, j: (b, j + nj)),
                  pl.BlockSpec((FFN_TAPS, width), lambda b, j: (0, j)),
                  pl.BlockSpec((1, width), lambda b, j: (0, j))],
        out_specs=pl.BlockSpec((seq, width), lambda b, j: (b, j)),
        out_shape=jax.ShapeDtypeStruct((nb * seq, D_FF), BF16),
        compiler_params=_params("parallel", "parallel"),
    )(gu, gu, cw, cb)


def _scan_setup(d, dt_ref, dtb_ref, alog_ref, z_ref, zt_ref, *, lower_when_dir0, inclusive):
    is0 = d == 0
    dt_all = _softplus(dt_ref[...] + dtb_ref[...])
    adt_all = dt_all * (-jnp.exp(alog_ref[...]))
    li = lax.broadcasted_iota(jnp.int32, (CHUNK, CHUNK), 0)
    si = lax.broadcasted_iota(jnp.int32, (CHUNK, CHUNK), 1)
    lower = is0 if lower_when_dir0 else jnp.logical_not(is0)
    ahead = jnp.where(lower, li - si, si - li)
    mask = ahead >= 0
    tri = mask if inclusive else ahead > 0
    z_all = jnp.dot(tri.astype(F32), adt_all, precision=HIGHEST, preferred_element_type=F32)
    zt_all = z_all.T
    z_ref[...] = jnp.where(is0, z_all[:, 0:N_HEADS], z_all[:, N_HEADS:2 * N_HEADS])
    zt_ref[...] = jnp.where(is0, zt_all[0:N_HEADS, :], zt_all[N_HEADS:2 * N_HEADS, :])
    dt = jnp.where(is0, dt_all[:, 0:N_HEADS], dt_all[:, N_HEADS:2 * N_HEADS])
    adt = jnp.where(is0, adt_all[:, 0:N_HEADS], adt_all[:, N_HEADS:2 * N_HEADS])
    tot = jnp.sum(adt, axis=0, keepdims=True)
    return ahead, dt, tot, dt_all


def _chunk_index(nchunk, forward_when_dir0):
    def idx(d, b, c):
        fwd = (d == 0) if forward_when_dir0 else (d != 0)
        return b * nchunk + jnp.where(fwd, c, nchunk - 1 - c)
    return idx


def _ssd_fwd(xbc, proj, dtb, alog, *, nb, seq):
    nchunk = seq // CHUNK
    t = nb * seq
    row = _chunk_index(nchunk, True)

    def body(xs_ref, bc_ref, dt_ref, dtb_ref, alog_ref, o_ref, hs_ref, h_ref, z_ref, zt_ref, dts_ref):
        d, c = pl.program_id(0), pl.program_id(2)

        @pl.when(c == 0)
        def _():
            h_ref[...] = jnp.zeros_like(h_ref)

        ahead, dt, tot, _ = _scan_setup(d, dt_ref, dtb_ref, alog_ref, z_ref, zt_ref,
                                        lower_when_dir0=True, inclusive=True)
        mask = ahead >= 0
        dts_ref[...] = dt
        e_tot = jnp.exp(tot)
        for g in range(N_GROUPS):
            bg = bc_ref[:, g * N_STATE:(g + 1) * N_STATE]
            cg = bc_ref[:, BC_WIDTH + g * N_STATE:BC_WIDTH + (g + 1) * N_STATE]
            cb = _dot(cg.astype(BF16), bg.astype(BF16), NT)
            for r in range(HEADS_PER_GROUP):
                h = g * HEADS_PER_GROUP + r
                zc = jnp.broadcast_to(z_ref[:, h:h + 1], (CHUNK, CHUNK))
                decay = jnp.exp(jnp.where(mask, zc - zt_ref[h:h + 1, :], -jnp.inf))
                u = (xs_ref[:, h * HEAD_DIM:(h + 1) * HEAD_DIM] * dts_ref[:, h:h + 1]).astype(BF16)
                state = h_ref[h]
                hs_ref[0, 0, h] = state
                y = _dot((cb * decay).astype(BF16), u, NN) + _dot((cg * jnp.exp(zc)).astype(BF16), state.astype(BF16), NT)
                bw = (bg * jnp.exp(tot[:, h:h + 1] - zc)).astype(BF16)
                h_ref[h] = state * e_tot[:, h:h + 1] + _dot(u, bw, TN)
                o_ref[0, :, h * HEAD_DIM:(h + 1) * HEAD_DIM] = y

    return pl.pallas_call(
        body, name="ssd_fwd", grid=(2, nb, nchunk),
        in_specs=[pl.BlockSpec((CHUNK, SSM_WIDTH), lambda d, b, c: (row(d, b, c), 0)),
                  pl.BlockSpec((CHUNK, 2 * BC_WIDTH), lambda d, b, c: (row(d, b, c), SSM_WIDTH // (2 * BC_WIDTH))),
                  pl.BlockSpec((CHUNK, LANES), lambda d, b, c: (row(d, b, c), PDT // LANES)),
                  pl.BlockSpec((1, LANES), lambda d, b, c: (0, 0)),
                  pl.BlockSpec((1, LANES), lambda d, b, c: (0, 0))],
        out_specs=[pl.BlockSpec((1, CHUNK, SSM_WIDTH), lambda d, b, c: (d, row(d, b, c), 0)),
                   pl.BlockSpec((1, 1, N_HEADS, HEAD_DIM, N_STATE), lambda d, b, c: (d, row(d, b, c), 0, 0, 0))],
        out_shape=[jax.ShapeDtypeStruct((2, t, SSM_WIDTH), F32),
                   jax.ShapeDtypeStruct((2, nb * nchunk, N_HEADS, HEAD_DIM, N_STATE), F32)],
        scratch_shapes=[pltpu.VMEM((N_HEADS, HEAD_DIM, N_STATE), F32),
                        pltpu.VMEM((CHUNK, N_HEADS), F32), pltpu.VMEM((N_HEADS, CHUNK), F32),
                        pltpu.VMEM((CHUNK, N_HEADS), F32)],
        compiler_params=_params("arbitrary", "arbitrary", "arbitrary"),
    )(xbc, xbc, proj, dtb, alog)


def _gate_norm_fwd(y2, xbc, proj, dvec, nw, *, tm):
    t = xbc.shape[0]
    half = SSM_WIDTH // N_GROUPS

    def body(y_ref, xs_ref, z_ref, d_ref, w_ref, o_ref):
        z = z_ref[...]
        p = (y_ref[0] + y_ref[1] + d_ref[...] * xs_ref[...]) * (z * _sigmoid(z))
        for g in range(N_GROUPS):
            pg = p[:, g * half:(g + 1) * half]
            r = lax.rsqrt(jnp.mean(pg * pg, axis=-1, keepdims=True) + NORM_EPS)
            o_ref[:, g * half:(g + 1) * half] = (pg * r * w_ref[:, g * half:(g + 1) * half]).astype(BF16)

    return pl.pallas_call(
        body, name="gate_norm_fwd", grid=(t // tm,),
        in_specs=[pl.BlockSpec((2, tm, SSM_WIDTH), lambda i: (0, i, 0)),
                  pl.BlockSpec((tm, SSM_WIDTH), lambda i: (i, 0)),
                  pl.BlockSpec((tm, SSM_WIDTH), lambda i: (i, PZ // SSM_WIDTH)),
                  pl.BlockSpec((1, SSM_WIDTH), lambda i: (0, 0)),
                  pl.BlockSpec((1, SSM_WIDTH), lambda i: (0, 0))],
        out_specs=pl.BlockSpec((tm, SSM_WIDTH), lambda i: (i, 0)),
        out_shape=jax.ShapeDtypeStruct((t, SSM_WIDTH), BF16),
        compiler_params=_params("parallel"),
    )(y2, xbc, proj, dvec, nw)


GROUP_ROWS = Q_PER_KV * CHUNK


def _band_mask(n, nblk):
    qi = lax.broadcasted_iota(jnp.int32, (GROUP_ROWS, KEY_SPAN), 0) & (CHUNK - 1)
    kj = lax.broadcasted_iota(jnp.int32, (GROUP_ROWS, KEY_SPAN), 1)
    rel = kj - CHUNK - qi
    kpos = (n - 1) * CHUNK + kj
    return (jnp.abs(rel) <= CHUNK) & (kpos >= 0) & (kpos < nblk * CHUNK)


def _per_head_column(ref, g):
    blk = lax.broadcasted_iota(jnp.int32, (GROUP_ROWS, 1), 0) // CHUNK
    col = jnp.zeros((GROUP_ROWS, 1), F32)
    for r in range(Q_PER_KV):
        h = g * Q_PER_KV + r
        col = jnp.where(blk == r, ref[:, h:h + 1], col)
    return col


def _stack_heads(ref, g, dtype):
    return jnp.concatenate([ref[:, (g * Q_PER_KV + r) * HEAD_DIM:(g * Q_PER_KV + r + 1) * HEAD_DIM].astype(dtype)
                            for r in range(Q_PER_KV)], axis=0)


def _kv_specs(nblk):
    kvb = PK // (2 * KV_WIDTH)

    def at(off):
        def idx(b, n):
            return (b * nblk + jnp.clip(n + off, 0, nblk - 1), kvb)
        return pl.BlockSpec((CHUNK, 2 * KV_WIDTH), idx)
    return [at(-1), at(0), at(1)]


def _attn_fwd(proj, bias, sink, *, nb, seq):
    nblk = seq // CHUNK
    t = nb * seq
    scale = HEAD_DIM ** -0.5

    def body(q_ref, kp_ref, kc_ref, kn_ref, bias_ref, sink_ref, o_ref, lse_ref):
        n = pl.program_id(1)
        valid = _band_mask(n, nblk)
        lses = []
        for g in range(KV_HEADS):
            ks = slice(g * HEAD_DIM, (g + 1) * HEAD_DIM)
            vs = slice(KV_WIDTH + g * HEAD_DIM, KV_WIDTH + (g + 1) * HEAD_DIM)
            kcat = jnp.concatenate([kp_ref[:, ks], kc_ref[:, ks], kn_ref[:, ks]], axis=0).astype(BF16)
            vcat = jnp.concatenate([kp_ref[:, vs], kc_ref[:, vs], kn_ref[:, vs]], axis=0).astype(BF16)
            q = _stack_heads(q_ref, g, BF16)
            bias = bias_ref[g * Q_PER_KV:(g + 1) * Q_PER_KV].reshape(GROUP_ROWS, KEY_SPAN)
            s = jnp.where(valid, _dot(q, kcat, NT) * scale + bias, -jnp.inf)
            sk = _per_head_column(sink_ref, g)
            m = jnp.maximum(jnp.max(s, axis=-1, keepdims=True), sk)
            p = jnp.exp(s - m)
            denom = jnp.sum(p, axis=-1, keepdims=True) + jnp.exp(sk - m)
            o = _dot((p * (1.0 / denom)).astype(BF16), vcat, NN).astype(BF16)
            lse = m + jnp.log(denom)
            for r in range(Q_PER_KV):
                h = g * Q_PER_KV + r
                o_ref[:, h * HEAD_DIM:(h + 1) * HEAD_DIM] = o[r * CHUNK:(r + 1) * CHUNK]
                lses.append(lse[r * CHUNK:(r + 1) * CHUNK])
        lse_ref[...] = jnp.concatenate(lses, axis=1)

    return pl.pallas_call(
        body, name="attn_fwd", grid=(nb, nblk),
        in_specs=[pl.BlockSpec((CHUNK, D_MODEL), lambda b, n: (b * nblk + n, PQ // D_MODEL))] + _kv_specs(nblk) + [
            pl.BlockSpec((N_HEADS, CHUNK, KEY_SPAN), lambda b, n: (0, 0, 0)),
            pl.BlockSpec((1, LANES), lambda b, n: (0, 0))],
        out_specs=[pl.BlockSpec((CHUNK, D_MODEL), lambda b, n: (b * nblk + n, 0)),
                   pl.BlockSpec((CHUNK, N_HEADS), lambda b, n: (b * nblk + n, 0))],
        out_shape=[jax.ShapeDtypeStruct((t, D_MODEL), BF16), jax.ShapeDtypeStruct((t, N_HEADS), F32)],
        compiler_params=_params("parallel", "parallel"),
    )(proj, proj, proj, proj, bias, sink)


def _loss_head(x, tgt, nw, *, tm):
    t, d = x.shape

    def body(x_ref, t_ref, w_ref, dx_ref, dw_ref, l_ref):
        @pl.when(pl.program_id(0) == 0)
        def _():
            dw_ref[...] = jnp.zeros_like(dw_ref)
            l_ref[...] = jnp.zeros_like(l_ref)

        xv = x_ref[...]
        w = w_ref[...]
        r = lax.rsqrt(jnp.mean(xv * xv, axis=-1, keepdims=True) + NORM_EPS)
        xh = xv * r
        err = xh * w - t_ref[...]
        l_ref[...] += jnp.sum(err * err) * (0.5 / d)
        dy = err * (1.0 / d)
        gw = dy * w
        dx_ref[...] = r * (gw - xh * jnp.mean(gw * xh, axis=-1, keepdims=True))
        dw_ref[...] += jnp.sum(dy * xh, axis=0, keepdims=True)

    return pl.pallas_call(
        body, name="loss_head", grid=(t // tm,),
        in_specs=[pl.BlockSpec((tm, d), lambda i: (i, 0)), pl.BlockSpec((tm, d), lambda i: (i, 0)),
                  pl.BlockSpec((1, d), lambda i: (0, 0))],
        out_specs=[pl.BlockSpec((tm, d), lambda i: (i, 0)), pl.BlockSpec((1, d), lambda i: (0, 0)),
                   pl.BlockSpec((1, LANES), lambda i: (0, 0))],
        out_shape=[jax.ShapeDtypeStruct((t, d), F32), jax.ShapeDtypeStruct((1, d), F32),
                   jax.ShapeDtypeStruct((1, LANES), F32)],
        compiler_params=_params("arbitrary"),
    )(x, tgt, nw)


def _to_proj_layout(w):
    pad = jnp.zeros(w.shape[:-1] + (PROJ_W - IN_COLS,), w.dtype)
    return jnp.concatenate([w[..., OZ:OXBC], w[..., OXBC:OXBC + SSM_WIDTH], w[..., OQ:OK_],
                            w[..., OXBC + SSM_WIDTH:ODT], w[..., OK_:IN_COLS], w[..., ODT:OQ], pad], axis=-1)


def _from_proj_layout(g):
    return jnp.concatenate([g[..., PZ:PZ + 2 * SSM_WIDTH], g[..., PB:PB + 2 * BC_WIDTH], g[..., PDT:PDT + 2 * N_HEADS],
                            g[..., PQ:PQ + D_MODEL], g[..., PK:PK + 2 * KV_WIDTH]], axis=-1)


def _pad_lanes(v):
    return jnp.pad(v.reshape(1, -1), ((0, 0), (0, LANES - v.size)))


def _layer_fwd(x, p, band_bias, *, nb, seq):
    proj, h1 = _rms_matmul(x, p["n1"], p["w_in"], name="in_proj", tm=512, tn=1408)
    xbc = _ssm_conv_fwd(proj, p["conv_w"], p["conv_b"], nb=nb, seq=seq)
    y2, states = _ssd_fwd(xbc, proj, p["dtb"], p["alog"], nb=nb, seq=seq)
    y_ssm = _gate_norm_fwd(y2, xbc, proj, p["dvec"], p["ssm_nw"], tm=256)
    y_att, lse = _attn_fwd(proj, band_bias, p["sink"], nb=nb, seq=seq)
    x1 = _matmul(y_ssm, p["w_out"][:SSM_WIDTH], "nn", name="out_proj_ssm", tm=512, tn=1024, tk=1024, res=x)
    x1 = _matmul(y_att, p["w_out"][SSM_WIDTH:], "nn", name="out_proj_att", tm=512, tn=1024, tk=1024, res=x1)
    gu, h2 = _rms_matmul(x1, p["n2"], p["w_up"], name="up_proj", tm=512, tn=1408)
    act = _ffn_act_fwd(gu, p["ffn_cw"], p["ffn_cb"], nb=nb, seq=seq)
    x2 = _matmul(act, p["w_down"], "nn", name="down_proj", tm=512, tn=1024, tk=1408, res=x1)
    saved = dict(x=x, proj=proj, h1=h1, xbc=xbc, y2=y2, states=states, y_ssm=y_ssm, y_att=y_att, lse=lse, x1=x1, gu=gu, h2=h2, act=act)
    return x2, saved


def _rms_bwd(x, dh, nw, dres, *, tm, name):
    t, d = x.shape

    def body(x_ref, dh_ref, w_ref, r_ref, dx_ref, dw_ref):
        @pl.when(pl.program_id(0) == 0)
        def _():
            dw_ref[...] = jnp.zeros_like(dw_ref)

        xv = x_ref[...]
        dh_v = dh_ref[...].astype(F32)
        r = lax.rsqrt(jnp.mean(xv * xv, axis=-1, keepdims=True) + NORM_EPS)
        xh = xv * r
        gw = dh_v * w_ref[...]
        dx_ref[...] = r_ref[...] + r * (gw - xh * jnp.mean(gw * xh, axis=-1, keepdims=True))
        dw_ref[...] += jnp.sum(dh_v * xh, axis=0, keepdims=True)

    row = pl.BlockSpec((tm, d), lambda i: (i, 0))
    vec = pl.BlockSpec((1, d), lambda i: (0, 0))
    return pl.pallas_call(
        body, name=name, grid=(t // tm,), in_specs=[row, row, vec, row], out_specs=[row, vec],
        out_shape=[jax.ShapeDtypeStruct((t, d), F32), jax.ShapeDtypeStruct((1, d), F32)],
        compiler_params=_params("arbitrary"),
    )(x, dh, nw, dres)


def _dsilu(g, sg):
    return sg * (1.0 + g * (1.0 - sg))


def _conv_taps_bwd(gpre, dg, w_ref, dwb_ref, taps, seq):
    pad = taps // 2
    dg_ext, gpre_ext = _zero_ext(dg), _zero_ext(gpre)
    dpre = None
    for k in range(taps):
        term = _shifted(dg_ext, pad - k, seq) * w_ref[k:k + 1, :]
        dpre = term if dpre is None else dpre + term
        dwb_ref[k:k + 1, :] += jnp.sum(dg * _shifted(gpre_ext, k - pad, seq), axis=0, keepdims=True)
    dwb_ref[SUBLANES - 1:SUBLANES, :] += jnp.sum(dg, axis=0, keepdims=True)
    return dpre


def _ffn_act_bwd(gu, dact, cw, cb, *, nb, seq):
    width = 256
    nj = D_FF // width

    def body(g_ref, u_ref, da_ref, w_ref, b_ref, dg_ref, du_ref, dwb_ref):
        @pl.when(pl.program_id(1) == 0)
        def _():
            dwb_ref[...] = jnp.zeros_like(dwb_ref)

        gpre = g_ref[...]
        g = _conv_taps(gpre, w_ref, FFN_TAPS, seq) + b_ref[...]
        sg = _sigmoid(g)
        da = da_ref[...].astype(F32)
        du_ref[...] = (da * g * sg).astype(BF16)
        dgc = da * u_ref[...] * _dsilu(g, sg)
        dg_ref[...] = _conv_taps_bwd(gpre, dgc, w_ref, dwb_ref, FFN_TAPS, seq).astype(BF16)

    blk = lambda off: pl.BlockSpec((seq, width), lambda j, b: (b, j + off))
    return pl.pallas_call(
        body, name="ffn_act_bwd", grid=(nj, nb),
        in_specs=[blk(0), blk(nj), blk(0),
                  pl.BlockSpec((FFN_TAPS, width), lambda j, b: (0, j)),
                  pl.BlockSpec((1, width), lambda j, b: (0, j))],
        out_specs=[blk(0), blk(0), pl.BlockSpec((SUBLANES, width), lambda j, b: (0, j))],
        out_shape=[jax.ShapeDtypeStruct((nb * seq, D_FF), BF16), jax.ShapeDtypeStruct((nb * seq, D_FF), BF16),
                   jax.ShapeDtypeStruct((SUBLANES, D_FF), F32)],
        compiler_params=_params("parallel", "arbitrary"),
    )(gu, gu, dact, cw, cb)


def _ssm_conv_bwd(proj, pair, cw, cb, *, nb, seq, name, width, proj_col, conv_col, ncol, extra=None, scale=None):
    has_extra = extra is not None

    def body(*refs):
        if has_extra:
            x_ref, p_ref, w_ref, b_ref, e_ref, s_ref, dx_ref, dwb_ref = refs
        else:
            x_ref, p_ref, w_ref, b_ref, dx_ref, dwb_ref = refs

        @pl.when(pl.program_id(1) == 0)
        def _():
            dwb_ref[...] = jnp.zeros_like(dwb_ref)

        gpre = x_ref[...]
        g = _conv_taps(gpre, w_ref, SSM_TAPS, seq) + b_ref[...]
        sg = _sigmoid(g)
        da = p_ref[0] + p_ref[1]
        if has_extra:
            da = da + e_ref[...] * s_ref[...]
        dx_ref[...] = _conv_taps_bwd(gpre, da * _dsilu(g, sg), w_ref, dwb_ref, SSM_TAPS, seq)

    in_specs = [pl.BlockSpec((seq, width), lambda j, b: (b, j + proj_col)),
                pl.BlockSpec((2, seq, width), lambda j, b: (0, b, j)),
                pl.BlockSpec((SSM_TAPS, width), lambda j, b: (0, j + conv_col)),
                pl.BlockSpec((1, width), lambda j, b: (0, j + conv_col))]
    args = [proj, pair, cw, cb]
    if has_extra:
        in_specs += [pl.BlockSpec((seq, width), lambda j, b: (b, j)), pl.BlockSpec((1, width), lambda j, b: (0, j))]
        args += [extra, scale]
    return pl.pallas_call(
        body, name=name, grid=(ncol, nb), in_specs=in_specs,
        out_specs=[pl.BlockSpec((seq, width), lambda j, b: (b, j)), pl.BlockSpec((SUBLANES, width), lambda j, b: (0, j))],
        out_shape=[jax.ShapeDtypeStruct((nb * seq, ncol * width), F32), jax.ShapeDtypeStruct((SUBLANES, ncol * width), F32)],
        compiler_params=_params("parallel", "arbitrary"),
    )(*args)


def _attn_bwd(proj, dmix, y_att, lse, bias, sink, dbias_in, *, nb, seq):
    nblk = seq // CHUNK
    t = nb * seq
    scale = HEAD_DIM ** -0.5

    def body(q_ref, kp_ref, kc_ref, kn_ref, do_ref, o_ref, lse_ref, bias_ref, sink_ref, dbin_ref,
             dq_ref, dkv_ref, dbias_ref, dsink_ref):
        b, n = pl.program_id(0), pl.program_id(1)

        @pl.when(n == 0)
        def _():
            dkv_ref[...] = jnp.zeros_like(dkv_ref)

        @pl.when((n == 0) & (b == 0))
        def _():
            dbias_ref[...] = dbin_ref[...]
            dsink_ref[...] = jnp.zeros_like(dsink_ref)

        valid = _band_mask(n, nblk)
        lane = lax.broadcasted_iota(jnp.int32, (1, LANES), 1)
        dsink = jnp.zeros((1, LANES), F32)
        rows = pl.ds(pl.multiple_of(n * CHUNK, CHUNK), KEY_SPAN)
        for g in range(KV_HEADS):
            ks = slice(g * HEAD_DIM, (g + 1) * HEAD_DIM)
            vs = slice(KV_WIDTH + g * HEAD_DIM, KV_WIDTH + (g + 1) * HEAD_DIM)
            kcat = jnp.concatenate([kp_ref[:, ks], kc_ref[:, ks], kn_ref[:, ks]], axis=0).astype(BF16)
            vcat = jnp.concatenate([kp_ref[:, vs], kc_ref[:, vs], kn_ref[:, vs]], axis=0).astype(BF16)
            q = _stack_heads(q_ref, g, BF16)
            do = _stack_heads(do_ref, g, F32)
            lse = jnp.concatenate([lse_ref[:, g * Q_PER_KV + r:g * Q_PER_KV + r + 1] for r in range(Q_PER_KV)], axis=0)
            bias = bias_ref[g * Q_PER_KV:(g + 1) * Q_PER_KV].reshape(GROUP_ROWS, KEY_SPAN)
            s = jnp.where(valid, _dot(q, kcat, NT) * scale + bias, -jnp.inf)
            p = jnp.exp(s - lse)
            delta = jnp.sum(do * _stack_heads(o_ref, g, F32), axis=-1, keepdims=True)
            do16 = do.astype(BF16)
            ds = p * (_dot(do16, vcat, NT) - delta)
            dbias_ref[g * Q_PER_KV:(g + 1) * Q_PER_KV] += ds.reshape(Q_PER_KV, CHUNK, KEY_SPAN)
            sink_part = jnp.exp(_per_head_column(sink_ref, g) - lse) * delta
            ds16 = (ds * scale).astype(BF16)
            dq = _dot(ds16, kcat, NN)
            for r in range(Q_PER_KV):
                h = g * Q_PER_KV + r
                dq_ref[:, h * HEAD_DIM:(h + 1) * HEAD_DIM] = dq[r * CHUNK:(r + 1) * CHUNK]
                dsink = dsink - jnp.where(lane == h, jnp.sum(sink_part[r * CHUNK:(r + 1) * CHUNK], axis=0, keepdims=True), 0.0)
            dkv_ref[0, rows, ks] += _dot(ds16, q, TN)
            dkv_ref[0, rows, vs] += _dot(p.astype(BF16), do16, TN)
        dsink_ref[...] += dsink

    blk = lambda cb: pl.BlockSpec((CHUNK, D_MODEL), lambda b, n: (b * nblk + n, cb))
    whole = pl.BlockSpec((N_HEADS, CHUNK, KEY_SPAN), lambda b, n: (0, 0, 0))
    vec = pl.BlockSpec((1, LANES), lambda b, n: (0, 0))
    return pl.pallas_call(
        body, name="attn_bwd", grid=(nb, nblk),
        in_specs=[blk(PQ // D_MODEL)] + _kv_specs(nblk) + [
            blk(1), blk(0), pl.BlockSpec((CHUNK, N_HEADS), lambda b, n: (b * nblk + n, 0)), whole, vec, whole],
        out_specs=[blk(0), pl.BlockSpec((1, seq + 2 * CHUNK, 2 * KV_WIDTH), lambda b, n: (b, 0, 0)), whole, vec],
        out_shape=[jax.ShapeDtypeStruct((t, D_MODEL), F32),
                   jax.ShapeDtypeStruct((nb, seq + 2 * CHUNK, 2 * KV_WIDTH), F32),
                   jax.ShapeDtypeStruct((N_HEADS, CHUNK, KEY_SPAN), F32),
                   jax.ShapeDtypeStruct((1, LANES), F32)],
        compiler_params=_params("arbitrary", "arbitrary"),
    )(proj, proj, proj, proj, dmix, y_att, lse, bias, sink, dbias_in)


def _gate_norm_bwd(y2, xbc, proj, dmix, dvec, nw, *, tm):
    t = xbc.shape[0]
    half = SSM_WIDTH // N_GROUPS

    def body(y_ref, xs_ref, z_ref, do_ref, d_ref, w_ref, dyv_ref, dz_ref, dd_ref, dw_ref):
        @pl.when(pl.program_id(0) == 0)
        def _():
            dd_ref[...] = jnp.zeros_like(dd_ref)
            dw_ref[...] = jnp.zeros_like(dw_ref)

        z = z_ref[...]
        xs = xs_ref[...]
        sg = _sigmoid(z)
        gz = z * sg
        yv = y_ref[0] + y_ref[1] + d_ref[...] * xs
        p = yv * gz
        do = do_ref[...]
        for g in range(N_GROUPS):
            cs = slice(g * half, (g + 1) * half)
            pg = p[:, cs]
            r = lax.rsqrt(jnp.mean(pg * pg, axis=-1, keepdims=True) + NORM_EPS)
            ph = pg * r
            gw = do[:, cs] * w_ref[:, cs]
            dp = r * (gw - ph * jnp.mean(gw * ph, axis=-1, keepdims=True))
            dyv = dp * gz[:, cs]
            dyv_ref[:, cs] = dyv
            dz_ref[:, cs] = dp * yv[:, cs] * _dsilu(z[:, cs], sg[:, cs])
            dw_ref[:, cs] += jnp.sum(do[:, cs] * ph, axis=0, keepdims=True)
            dd_ref[:, cs] += jnp.sum(dyv * xs[:, cs], axis=0, keepdims=True)

    row = lambda cb: pl.BlockSpec((tm, SSM_WIDTH), lambda i: (i, cb))
    vec = pl.BlockSpec((1, SSM_WIDTH), lambda i: (0, 0))
    return pl.pallas_call(
        body, name="gate_norm_bwd", grid=(t // tm,),
        in_specs=[pl.BlockSpec((2, tm, SSM_WIDTH), lambda i: (0, i, 0)), row(0), row(PZ // SSM_WIDTH), row(0), vec, vec],
        out_specs=[row(0), row(0), vec, vec],
        out_shape=[jax.ShapeDtypeStruct((t, SSM_WIDTH), F32), jax.ShapeDtypeStruct((t, SSM_WIDTH), F32),
                   jax.ShapeDtypeStruct((1, SSM_WIDTH), F32), jax.ShapeDtypeStruct((1, SSM_WIDTH), F32)],
        compiler_params=_params("arbitrary"),
    )(y2, xbc, proj, dmix, dvec, nw)


def _ssd_specs(nchunk, row):
    return [pl.BlockSpec((CHUNK, SSM_WIDTH), lambda d, b, c: (row(d, b, c), 0)),
            pl.BlockSpec((CHUNK, 2 * BC_WIDTH), lambda d, b, c: (row(d, b, c), SSM_WIDTH // (2 * BC_WIDTH))),
            pl.BlockSpec((CHUNK, LANES), lambda d, b, c: (row(d, b, c), PDT // LANES)),
            pl.BlockSpec((1, LANES), lambda d, b, c: (0, 0)),
            pl.BlockSpec((1, LANES), lambda d, b, c: (0, 0)),
            pl.BlockSpec((CHUNK, SSM_WIDTH), lambda d, b, c: (row(d, b, c), 0))]


_SSD_SCRATCH = [pltpu.VMEM((N_HEADS, HEAD_DIM, N_STATE), F32),
                pltpu.VMEM((CHUNK, N_HEADS), F32), pltpu.VMEM((N_HEADS, CHUNK), F32),
                pltpu.VMEM((CHUNK, N_HEADS), F32)]


def _ssd_bwd(xbc, proj, dtb, alog, dyv, states, *, nb, seq):
    nchunk = seq // CHUNK
    t = nb * seq
    row = _chunk_index(nchunk, False)

    def body(xs_ref, bc_ref, dt_ref, dtb_ref, alog_ref, dy_ref, hs_ref, dx_ref, db_ref, dc_ref, draw_ref, da_ref,
             dbias_ref, h_ref, z_ref, zt_ref, dts_ref):
        d, b, c = pl.program_id(0), pl.program_id(1), pl.program_id(2)

        @pl.when(c == 0)
        def _():
            h_ref[...] = jnp.zeros_like(h_ref)

        @pl.when((c == 0) & (b == 0) & (d == 0))
        def _():
            da_ref[...] = jnp.zeros_like(da_ref)
            dbias_ref[...] = jnp.zeros_like(dbias_ref)

        ahead, dt, tot, dt_all = _scan_setup(d, dt_ref, dtb_ref, alog_ref, z_ref, zt_ref,
                                             lower_when_dir0=False, inclusive=False)
        mask = ahead >= 0
        dts_ref[...] = dt
        e_tot = jnp.exp(tot)
        lane = lax.broadcasted_iota(jnp.int32, (1, LANES), 1)
        ddt = jnp.zeros((CHUNK, LANES), F32)
        head_row = lax.broadcasted_iota(jnp.int32, (LANES, 1), 0)
        inner = jnp.zeros((CHUNK, LANES), F32)
        inner_t = jnp.zeros((LANES, CHUNK), F32)
        outer = jnp.zeros((CHUNK, LANES), F32)
        span = jnp.zeros((1, LANES), F32)
        for g in range(N_GROUPS):
            bg = bc_ref[:, g * N_STATE:(g + 1) * N_STATE]
            cg = bc_ref[:, BC_WIDTH + g * N_STATE:BC_WIDTH + (g + 1) * N_STATE]
            bg16 = bg.astype(BF16)
            cg16 = cg.astype(BF16)
            bc_t = _dot(bg16, cg16, NT)
            dbg = jnp.zeros((CHUNK, N_STATE), F32)
            dcg = jnp.zeros((CHUNK, N_STATE), F32)
            for r in range(HEADS_PER_GROUP):
                h = g * HEADS_PER_GROUP + r
                hs = slice(h * HEAD_DIM, (h + 1) * HEAD_DIM)
                here = lane == d * N_HEADS + h
                here_t = head_row == d * N_HEADS + h
                zc = jnp.broadcast_to(z_ref[:, h:h + 1], (CHUNK, CHUNK))
                decay = jnp.exp(jnp.where(mask, zc - zt_ref[h:h + 1, :], -jnp.inf))
                e_z = jnp.exp(zc)
                e_tz = jnp.exp(tot[:, h:h + 1] - zc)
                x_h = xs_ref[:, hs]
                dt_h = dts_ref[:, h:h + 1]
                u = (x_h * dt_h).astype(BF16)
                dy = dy_ref[:, hs].astype(BF16)
                state = h_ref[h]
                st16 = state.astype(BF16)
                fstate = hs_ref[0, 0, h]
                du = _dot((bc_t * decay).astype(BF16), dy, NN) + _dot((bg * e_z).astype(BF16), st16, NT)
                w2f = _dot(u, dy, NT) * decay
                w2 = w2f.astype(BF16)
                db_out = e_z * _dot(u, st16, NN)
                dc_out = e_tz * _dot(dy, fstate.astype(BF16), NN)
                dbg = dbg + _dot(w2, cg16, NN) + db_out
                dcg = dcg + _dot(w2, bg16, TN) + dc_out
                pairs = w2f * bc_t
                col_in = jnp.sum(pairs, axis=-1, keepdims=True)
                row_in = jnp.sum(pairs, axis=0, keepdims=True)
                row_out = jnp.sum(dc_out * cg, axis=-1, keepdims=True)
                col_out = jnp.sum(db_out * bg, axis=-1, keepdims=True)
                inner = inner + jnp.where(here, row_out - col_in, 0.0)
                inner_t = inner_t + jnp.where(here_t, row_in, 0.0)
                outer = outer + jnp.where(here, col_out, 0.0)
                span = span + jnp.where(here, e_tot[:, h:h + 1] * jnp.sum(fstate * state), 0.0)
                ddt = ddt + jnp.where(here, jnp.sum(du * x_h, axis=-1, keepdims=True), 0.0)
                dx_ref[0, :, hs] = du * dt_h
                h_ref[h] = state * e_tot[:, h:h + 1] + _dot(dy, (cg * e_tz).astype(BF16), TN)
            db_ref[0, :, g * N_STATE:(g + 1) * N_STATE] = dbg
            dc_ref[0, :, g * N_STATE:(g + 1) * N_STATE] = dcg
        tri = mask.astype(F32)
        dadt = (jnp.dot(tri, inner + inner_t.T, precision=HIGHEST, preferred_element_type=F32)
                + jnp.dot(1.0 - tri, outer, precision=HIGHEST, preferred_element_type=F32) + span)
        a = -jnp.exp(alog_ref[...])
        draw = (ddt + a * dadt) * _sigmoid(dt_ref[...] + dtb_ref[...])
        draw_ref[0] = draw
        da_ref[...] += jnp.sum(dt_all * dadt, axis=0, keepdims=True) * a
        dbias_ref[...] += jnp.sum(draw, axis=0, keepdims=True)

    out_row = lambda w: pl.BlockSpec((1, CHUNK, w), lambda d, b, c: (d, row(d, b, c), 0))
    vec = pl.BlockSpec((1, LANES), lambda d, b, c: (0, 0))
    return pl.pallas_call(
        body, name="ssd_bwd", grid=(2, nb, nchunk),
        in_specs=_ssd_specs(nchunk, row) + [
            pl.BlockSpec((1, 1, N_HEADS, HEAD_DIM, N_STATE), lambda d, b, c: (d, row(d, b, c), 0, 0, 0))],
        out_specs=[out_row(SSM_WIDTH), out_row(BC_WIDTH), out_row(BC_WIDTH), out_row(LANES), vec, vec],
        out_shape=[jax.ShapeDtypeStruct((2, t, SSM_WIDTH), F32), jax.ShapeDtypeStruct((2, t, BC_WIDTH), F32),
                   jax.ShapeDtypeStruct((2, t, BC_WIDTH), F32),
                   jax.ShapeDtypeStruct((2, t, LANES), F32), jax.ShapeDtypeStruct((1, LANES), F32),
                   jax.ShapeDtypeStruct((1, LANES), F32)],
        scratch_shapes=_SSD_SCRATCH,
        compiler_params=_params("arbitrary", "arbitrary", "arbitrary"),
    )(xbc, xbc, proj, dtb, alog, dyv, states)


def _layer_bwd(dx2, p, s, band_bias, dbias_in, *, nb, seq):
    t = nb * seq
    x, proj, xbc, x1 = s["x"], s["proj"], s["xbc"], s["x1"]
    dact = _matmul(dx2, p["w_down"], "nt", name="down_proj_dx", tm=512, tn=1408, tk=1024, out_dtype=BF16)
    g_w_down = _matmul(s["act"], dx2, "tn", name="down_proj_dw", tm=1408, tn=1024, tk=1024)
    dg, du, dwb_ffn = _ffn_act_bwd(s["gu"], dact, p["ffn_cw"], p["ffn_cb"], nb=nb, seq=seq)
    dh2 = _matmul(dg, p["w_up"][:, :D_FF], "nt", name="up_proj_dx_g", tm=512, tn=1024, tk=1408)
    dh2 = _matmul(du, p["w_up"][:, D_FF:], "nt", name="up_proj_dx_u", tm=512, tn=1024, tk=1408, res=dh2)
    g_w_up = jnp.concatenate([_matmul(s["h2"], dg, "tn", name="up_proj_dw_g", tm=1024, tn=1408, tk=1024),
                              _matmul(s["h2"], du, "tn", name="up_proj_dw_u", tm=1024, tn=1408, tk=1024)], axis=1)
    dx1, g_n2 = _rms_bwd(x1, dh2, p["n2"], dx2, tm=512, name="norm2_bwd")
    dmix = _matmul(dx1, p["w_out"], "nt", name="out_proj_dx", tm=512, tn=1024, tk=1024)
    g_w_out = jnp.concatenate([_matmul(s["y_ssm"], dx1, "tn", name="out_proj_dw_ssm", tm=1024, tn=1024, tk=1024),
                               _matmul(s["y_att"], dx1, "tn", name="out_proj_dw_att", tm=1024, tn=1024, tk=1024)], axis=0)
    dq, dkv, dbias, dsink = _attn_bwd(proj, dmix, s["y_att"], s["lse"], band_bias, p["sink"], dbias_in, nb=nb, seq=seq)
    dkv = dkv[:, CHUNK:CHUNK + seq, :].reshape(t, 2 * KV_WIDTH)
    dyv, dz, g_dvec, g_ssm_nw = _gate_norm_bwd(s["y2"], xbc, proj, dmix, p["dvec"], p["ssm_nw"], tm=256)
    dxs2, db2, dc2, draw2, g_alog, g_dtb = _ssd_bwd(xbc, proj, p["dtb"], p["alog"], dyv, s["states"], nb=nb, seq=seq)
    ddt_raw = draw2[0] + draw2[1]
    conv = dict(nb=nb, seq=seq)
    dxs_pre, dwb_xs = _ssm_conv_bwd(proj, dxs2, p["conv_w"], p["conv_b"], name="ssm_conv_bwd_x", width=256,
                                    proj_col=PXS // 256, conv_col=0, ncol=4, extra=dyv, scale=p["dvec"], **conv)
    db_pre, dwb_b = _ssm_conv_bwd(proj, db2, p["conv_w"], p["conv_b"], name="ssm_conv_bwd_b", width=256,
                                  proj_col=PB // 256, conv_col=SSM_WIDTH // 256, ncol=1, **conv)
    dc_pre, dwb_c = _ssm_conv_bwd(proj, dc2, p["conv_w"], p["conv_b"], name="ssm_conv_bwd_c", width=256,
                                  proj_col=PC // 256, conv_col=(SSM_WIDTH + BC_WIDTH) // 256, ncol=1, **conv)
    dwb_ssm = jnp.concatenate([dwb_xs, dwb_b, dwb_c], axis=1)
    dproj = jnp.concatenate([dz, dxs_pre, dq, db_pre, dc_pre, dkv, ddt_raw], axis=1).astype(BF16)
    dh1 = _matmul(dproj, p["w_in"], "nt", name="in_proj_dx", tm=512, tn=1024, tk=1408)
    g_w_in = _matmul(s["h1"], dproj, "tn", name="in_proj_dw", tm=1024, tn=1408, tk=1024)
    dx, g_n1 = _rms_bwd(x, dh1, p["n1"], dx1, tm=512, name="norm1_bwd")
    grads = dict(n1=g_n1, w_in=g_w_in, conv_w=dwb_ssm[:SSM_TAPS], conv_b=dwb_ssm[SUBLANES - 1], dtb=g_dtb, alog=g_alog,
                 dvec=g_dvec, ssm_nw=g_ssm_nw, sink=dsink, w_out=g_w_out, n2=g_n2, w_up=g_w_up,
                 ffn_cw=dwb_ffn[:FFN_TAPS], ffn_cb=dwb_ffn[SUBLANES - 1], w_down=g_w_down)
    return dx, dbias, grads


def _band_bias(rel_bias, bucket):
    def body(rb_ref, b_ref, o_ref):
        o_ref[...] = jnp.zeros_like(o_ref)

        def per_bucket(k, carry):
            hit = b_ref[...] == k
            for h in range(N_HEADS):
                o_ref[h] = jnp.where(hit, rb_ref[k, h], o_ref[h])
            return carry

        lax.fori_loop(0, REL_BUCKETS, per_bucket, 0)

    return pl.pallas_call(
        body, name="band_bias", out_shape=jax.ShapeDtypeStruct((N_HEADS, CHUNK, KEY_SPAN), F32),
        in_specs=[pl.BlockSpec(memory_space=pltpu.SMEM), pl.BlockSpec(memory_space=pltpu.VMEM)],
        out_specs=pl.BlockSpec(memory_space=pltpu.VMEM),
    )(rel_bias, bucket)


def _rel_bias_grad(dbias, bucket):
    def body(d_ref, b_ref, o_ref):
        o_ref[...] = jnp.zeros_like(o_ref)
        lane = lax.broadcasted_iota(jnp.int32, (1, LANES), 1)

        def per_bucket(k, carry):
            hit = b_ref[...] == k
            for h in range(N_HEADS):
                part = jnp.sum(jnp.where(hit, d_ref[h], 0.0), axis=1, keepdims=True)
                o_ref[h:h + 1, :] += jnp.where(lane == k, jnp.sum(part, axis=0, keepdims=True), 0.0)
            return carry

        lax.fori_loop(0, REL_BUCKETS, per_bucket, 0)

    return pl.pallas_call(
        body, name="rel_bias_grad", out_shape=jax.ShapeDtypeStruct((N_HEADS, LANES), F32),
        compiler_params=pltpu.CompilerParams(vmem_limit_bytes=VMEM_LIMIT_BYTES),
    )(dbias, bucket)


def _exchange(arrs, *, gather, name):
    n = len(arrs)
    npeer = N_DEV - 1

    def body(*refs):
        ins, outs = refs[:n], refs[n:2 * n]
        send_sems, recv_sems, local_sems = refs[2 * n:]
        x, y, c = lax.axis_index("x"), lax.axis_index("y"), lax.axis_index("c")
        me = 4 * x + 2 * y + c
        peers = []
        for k in range(1, N_DEV):
            px, py, pc = x ^ ((k >> 2) & 1), y ^ ((k >> 1) & 1), c ^ (k & 1)
            peers.append(((px, py, pc), 4 * px + 2 * py + pc))
        local, sends, recvs = [], [], []
        for i in range(n):
            mine = ins[i] if gather else ins[i].at[me]
            local.append(pltpu.make_async_copy(mine, outs[i].at[me], local_sems.at[i]))
            for k, (pid, pslot) in enumerate(peers):
                src = ins[i] if gather else ins[i].at[pslot]
                sem = i * npeer + k
                sends.append(pltpu.make_async_remote_copy(
                    src_ref=src, dst_ref=outs[i].at[me], send_sem=send_sems.at[sem], recv_sem=recv_sems.at[sem],
                    device_id=pid, device_id_type=pl.DeviceIdType.MESH))
                recvs.append(pltpu.make_async_remote_copy(
                    src_ref=src, dst_ref=outs[i].at[pslot], send_sem=send_sems.at[sem], recv_sem=recv_sems.at[sem],
                    device_id=pid, device_id_type=pl.DeviceIdType.MESH))
        for cp in local + sends:
            cp.start()
        for cp in recvs:
            cp.wait_recv()
        for cp in sends:
            cp.wait_send()
        for cp in local:
            cp.wait()

    any_spec = pl.BlockSpec(memory_space=pl.ANY)
    out_shape = [jax.ShapeDtypeStruct((N_DEV,) + (a.shape if gather else a.shape[1:]), a.dtype) for a in arrs]
    return pl.pallas_call(
        body, name=name, in_specs=[any_spec] * n, out_specs=[any_spec] * n, out_shape=out_shape,
        scratch_shapes=[pltpu.SemaphoreType.DMA((n * npeer,)), pltpu.SemaphoreType.DMA((n * npeer,)),
                        pltpu.SemaphoreType.DMA((n,))],
        compiler_params=pltpu.CompilerParams(has_side_effects=True),
    )(*arrs)


def _adamw(parts, w, m, v, *, name, tr):
    r, c = w.shape
    assert r % tr == 0
    c1 = 1.0 - ADAM_B1 ** ADAM_STEP
    c2 = 1.0 - ADAM_B2 ** ADAM_STEP

    def body(p_ref, w_ref, m_ref, v_ref, g_ref, d_ref, nm_ref, nv_ref):
        g = p_ref[0].astype(F32)
        for j in range(1, N_DEV):
            g = g + p_ref[j].astype(F32)
        nm = ADAM_B1 * m_ref[...] + (1.0 - ADAM_B1) * g
        nv = ADAM_B2 * v_ref[...] + (1.0 - ADAM_B2) * (g * g)
        g_ref[...] = g
        nm_ref[...] = nm
        nv_ref[...] = nv
        d_ref[...] = -ADAM_LR * ((nm / c1) / (jnp.sqrt(nv / c2) + ADAM_EPS) + ADAM_WD * w_ref[...])

    blk = pl.BlockSpec((tr, c), lambda i: (i, 0))
    return pl.pallas_call(
        body, name=name, grid=(r // tr,),
        in_specs=[pl.BlockSpec((N_DEV, tr, c), lambda i: (0, i, 0)), blk, blk, blk],
        out_specs=[blk] * 4, out_shape=[jax.ShapeDtypeStruct((r, c), F32)] * 4,
        compiler_params=_params("parallel"),
    )(parts, w, m, v)


def _t5_bucket(rel):
    half = REL_BUCKETS // 2
    max_exact = half // 2
    ret = jnp.where(rel > 0, half, 0)
    n = jnp.abs(rel)
    nf = jnp.maximum(n, 1).astype(F32)
    large = max_exact + (jnp.log(nf / max_exact) / math.log(CHUNK / max_exact) * (half - max_exact)).astype(jnp.int32)
    large = jnp.minimum(large, half - 1)
    return ret + jnp.where(n < max_exact, n, large)


def _split16(w):
    hi = w.astype(BF16)
    return hi, (w - hi.astype(F32)).astype(BF16)


def _cols_to_blocks(g, depth):
    _, r, c8 = g.shape
    return g.reshape(depth, r, N_DEV, c8 // N_DEV).transpose(2, 0, 1, 3).reshape(N_DEV, depth * r, c8 // N_DEV)


def _rows_to_blocks(g, depth):
    _, r8, c = g.shape
    return g.reshape(depth, N_DEV, r8 // N_DEV, c).transpose(1, 0, 2, 3).reshape(N_DEV, depth * r8 // N_DEV, c)


def _blocks_to_cols(a, depth):
    _, dr, c = a.shape
    r = dr // depth
    return a.reshape(N_DEV, depth, r, c).transpose(1, 2, 0, 3).reshape(depth, r, N_DEV * c)


def _blocks_to_rows(a, depth):
    _, dr, c = a.shape
    r = dr // depth
    return a.reshape(N_DEV, depth, r, c).transpose(1, 0, 2, 3).reshape(depth, N_DEV * r, c)


_SMALL = ("rel_bias", "norm1_w", "conv_b", "dt_bias", "a_log", "d_skip", "ssm_norm_w", "attn_sink", "norm2_w",
          "ffn_conv_b", "final_norm_w")
_SHARDED = ("w_in", "conv_w", "w_out", "w_up", "ffn_conv_w", "w_down")
_ORDER = ("rel_bias", "norm1_w", "w_in", "conv_w", "conv_b", "dt_bias", "a_log", "d_skip", "ssm_norm_w", "attn_sink",
          "w_out", "norm2_w", "w_up", "ffn_conv_w", "ffn_conv_b", "w_down", "final_norm_w")


def _pack_small(d):
    flat = jnp.concatenate([d[k].reshape(-1).astype(F32) for k in _SMALL])
    rows = -(-flat.size // (LANES * SUBLANES)) * SUBLANES
    return jnp.pad(flat, (0, rows * LANES - flat.size)).reshape(rows, LANES)


def _unpack_small(packed, like):
    flat = packed.reshape(-1)
    out, off = {}, 0
    for k in _SMALL:
        out[k] = flat[off:off + like[k].size].reshape(like[k].shape)
        off += like[k].size
    return out


def kernel(x, rel_bias, norm1_w, w_in, conv_w, conv_b, dt_bias, a_log, d_skip, ssm_norm_w, attn_sink, w_out, norm2_w, w_up, ffn_conv_w, ffn_conv_b, w_down, final_norm_w, loss_target, m_rel_bias, m_norm1_w, m_w_in, m_conv_w, m_conv_b, m_dt_bias, m_a_log, m_d_skip, m_ssm_norm_w, m_attn_sink, m_w_out, m_norm2_w, m_w_up, m_ffn_conv_w, m_ffn_conv_b, m_w_down, m_final_norm_w, v_rel_bias, v_norm1_w, v_w_in, v_conv_w, v_conv_b, v_dt_bias, v_a_log, v_d_skip, v_ssm_norm_w, v_attn_sink, v_w_out, v_norm2_w, v_w_up, v_ffn_conv_w, v_ffn_conv_b, v_w_down, v_final_norm_w):
    w = dict(rel_bias=rel_bias, norm1_w=norm1_w, w_in=w_in, conv_w=conv_w, conv_b=conv_b, dt_bias=dt_bias, a_log=a_log,
             d_skip=d_skip, ssm_norm_w=ssm_norm_w, attn_sink=attn_sink, w_out=w_out, norm2_w=norm2_w, w_up=w_up,
             ffn_conv_w=ffn_conv_w, ffn_conv_b=ffn_conv_b, w_down=w_down, final_norm_w=final_norm_w)
    m = dict(rel_bias=m_rel_bias, norm1_w=m_norm1_w, w_in=m_w_in, conv_w=m_conv_w, conv_b=m_conv_b, dt_bias=m_dt_bias,
             a_log=m_a_log, d_skip=m_d_skip, ssm_norm_w=m_ssm_norm_w, attn_sink=m_attn_sink, w_out=m_w_out,
             norm2_w=m_norm2_w, w_up=m_w_up, ffn_conv_w=m_ffn_conv_w, ffn_conv_b=m_ffn_conv_b, w_down=m_w_down,
             final_norm_w=m_final_norm_w)
    v = dict(rel_bias=v_rel_bias, norm1_w=v_norm1_w, w_in=v_w_in, conv_w=v_conv_w, conv_b=v_conv_b, dt_bias=v_dt_bias,
             a_log=v_a_log, d_skip=v_d_skip, ssm_norm_w=v_ssm_norm_w, attn_sink=v_attn_sink, w_out=v_w_out,
             norm2_w=v_norm2_w, w_up=v_w_up, ffn_conv_w=v_ffn_conv_w, ffn_conv_b=v_ffn_conv_b, w_down=v_w_down,
             final_norm_w=v_final_norm_w)
    nb, seq, _ = x.shape
    t = nb * seq
    depth = w_in.shape[0]

    flat2 = lambda a: a.reshape(-1, a.shape[-1])
    cw_hi, cw_lo = _split16(flat2(conv_w))
    fw_hi, fw_lo = _split16(flat2(ffn_conv_w))
    shards = [flat2(w_in).astype(BF16), flat2(w_out).astype(BF16), flat2(w_up).astype(BF16), flat2(w_down).astype(BF16),
              cw_hi, cw_lo, fw_hi, fw_lo]
    g_in, g_out, g_up, g_down, g_cwh, g_cwl, g_fwh, g_fwl = _exchange(shards, gather=True, name="gather_weights")
    full_w_in = _to_proj_layout(_blocks_to_cols(g_in, depth))
    full_w_out = _blocks_to_rows(g_out, depth)
    full_w_up = _blocks_to_cols(g_up, depth)
    full_w_down = _blocks_to_rows(g_down, depth)
    full_conv_w = _blocks_to_cols(g_cwh.astype(F32) + g_cwl.astype(F32), depth)
    full_ffn_cw = _blocks_to_cols(g_fwh.astype(F32) + g_fwl.astype(F32), depth)

    rel = jnp.arange(KEY_SPAN)[None, :] - CHUNK - jnp.arange(CHUNK)[:, None]
    bucket = _t5_bucket(rel)
    band_bias = _band_bias(rel_bias, bucket)

    def layer_params(i):
        return dict(n1=norm1_w[i][None], w_in=full_w_in[i], conv_w=full_conv_w[i], conv_b=conv_b[i][None],
                    dtb=_pad_lanes(dt_bias[i].reshape(-1)), alog=_pad_lanes(a_log[i].reshape(-1)),
                    dvec=jnp.repeat(d_skip[i], HEAD_DIM)[None], ssm_nw=ssm_norm_w[i][None], sink=_pad_lanes(attn_sink[i]),
                    w_out=full_w_out[i], n2=norm2_w[i][None], w_up=full_w_up[i], ffn_cw=full_ffn_cw[i],
                    ffn_cb=ffn_conv_b[i][None], w_down=full_w_down[i])

    h = x.reshape(t, D_MODEL)
    params, saved = [], []
    for i in range(depth):
        params.append(layer_params(i))
        h, s = _layer_fwd(h, params[i], band_bias, nb=nb, seq=seq)
        saved.append(s)
    dh, g_final, loss_part = _loss_head(h, loss_target.reshape(t, D_MODEL), final_norm_w[None], tm=512)
    loss = lax.psum(loss_part[0, 0], ("x", "y", "c"))

    dbias = jnp.zeros((N_HEADS, CHUNK, KEY_SPAN), F32)
    grads = [None] * depth
    for i in reversed(range(depth)):
        dh, dbias, grads[i] = _layer_bwd(dh, params[i], saved[i], band_bias, dbias, nb=nb, seq=seq)
    grad_x = dh.reshape(nb, seq, D_MODEL)
    stack = lambda k: jnp.stack([g[k] for g in grads])

    blocks = [_cols_to_blocks(_from_proj_layout(stack("w_in")), depth), _cols_to_blocks(stack("conv_w"), depth),
              _rows_to_blocks(stack("w_out"), depth), _cols_to_blocks(stack("w_up"), depth),
              _cols_to_blocks(stack("ffn_cw"), depth), _rows_to_blocks(stack("w_down"), depth)]
    parts = _exchange([b.astype(BF16) for b in blocks], gather=False, name="scatter_grads")
    out = {}
    for k, p8 in zip(_SHARDED, parts):
        rows = p8.shape[1]
        tr = 256 if rows % 256 == 0 else (128 if rows % 128 == 0 else rows)
        res = _adamw(p8, flat2(w[k]), flat2(m[k]), flat2(v[k]), name="adamw_" + k, tr=tr)
        out[k] = [a.reshape(w[k].shape) for a in res]

    small = dict(rel_bias=_rel_bias_grad(dbias, bucket)[:, :REL_BUCKETS].T, norm1_w=stack("n1"), conv_b=stack("conv_b"),
                 dt_bias=stack("dtb")[:, 0, :2 * N_HEADS], a_log=stack("alog")[:, 0, :2 * N_HEADS],
                 d_skip=stack("dvec").reshape(depth, N_HEADS, HEAD_DIM).sum(-1), ssm_norm_w=stack("ssm_nw"),
                 attn_sink=stack("sink")[:, 0, :N_HEADS], norm2_w=stack("n2"), ffn_conv_b=stack("ffn_cb"),
                 final_norm_w=g_final)
    (small_parts,) = _exchange([_pack_small(small)], gather=True, name="gather_small_grads")
    res = _adamw(small_parts, _pack_small(w), _pack_small(m), _pack_small(v), name="adamw_small", tr=small_parts.shape[1])
    unpacked = [_unpack_small(a, w) for a in res]
    for k in _SMALL:
        out[k] = [u[k] for u in unpacked]

    return (loss, grad_x, *[out[k][0] for k in _ORDER], *[out[k][1] for k in _ORDER],
            *[out[k][2] for k in _ORDER], *[out[k][3] for k in _ORDER])
```

```python
import math

import numpy as np
import jax
import jax.numpy as jnp
from jax import lax
from jax.experimental import pallas as pl
from jax.experimental.pallas import tpu as pltpu

F32, BF16 = jnp.float32, jnp.bfloat16
HIGHEST = lax.Precision.HIGHEST

D_MODEL = 1024
HEAD_DIM = 64
N_HEADS = 16
N_GROUPS = 2
HEADS_PER_GROUP = N_HEADS // N_GROUPS
N_STATE = 128
SSM_WIDTH = 1024
BC_WIDTH = 256
CONV_CH = SSM_WIDTH + 2 * BC_WIDTH
SSM_TAPS = 7
CHUNK = 128
KV_HEADS = 4
KV_WIDTH = 256
Q_PER_KV = N_HEADS // KV_HEADS
KEY_SPAN = 3 * CHUNK
REL_BUCKETS = 32
D_FF = 2816
FFN_TAPS = 3
IN_COLS = 4128
NORM_EPS = 1e-6
N_DEV = 8

LANES = 128
SUBLANES = 8
VMEM_LIMIT_BYTES = 56 * 1024 * 1024
MM_ROWS = 1024

PZ, PXS, PQ, PB, PC, PK, PV, PDT, PROJ_W = 0, 1024, 2048, 3072, 3328, 3584, 3840, 4096, 4224
OZ, OXBC, ODT, OQ, OK_, OV = 0, 1024, 2560, 2592, 3616, 3872

ADAM_LR, ADAM_B1, ADAM_B2, ADAM_EPS, ADAM_WD, ADAM_STEP = 0.001, 0.9, 0.999, 1e-08, 0.01, 10


def _params(*sem):
    return pltpu.CompilerParams(dimension_semantics=sem, vmem_limit_bytes=VMEM_LIMIT_BYTES)


def _sigmoid(x):
    return 1.0 / (1.0 + jnp.exp(-x))


def _softplus(x):
    return jnp.maximum(x, 0.0) + jnp.log(1.0 + jnp.exp(-jnp.abs(x)))


def _dot(a, b, dims):
    return lax.dot_general(a, b, (dims, ((), ())), preferred_element_type=F32)


NN = ((1,), (0,))
NT = ((1,), (1,))
TN = ((0,), (0,))


def _matmul(a, b, mode, *, name, tm, tn, tk, res=None, out_dtype=F32, precision=None):
    if mode == "nn":
        (m, k), (k2, n) = a.shape, b.shape
        a_spec = pl.BlockSpec((tm, tk), lambda i, j, kk: (i, kk))
        b_spec = pl.BlockSpec((tk, tn), lambda i, j, kk: (kk, j))
        dims = NN
    elif mode == "nt":
        (m, k), (n, k2) = a.shape, b.shape
        a_spec = pl.BlockSpec((tm, tk), lambda i, j, kk: (i, kk))
        b_spec = pl.BlockSpec((tn, tk), lambda i, j, kk: (j, kk))
        dims = NT
    else:
        (k, m), (k2, n) = a.shape, b.shape
        a_spec = pl.BlockSpec((tk, tm), lambda i, j, kk: (kk, i))
        b_spec = pl.BlockSpec((tk, tn), lambda i, j, kk: (kk, j))
        dims = TN
    assert k == k2 and m % tm == 0 and n % tn == 0 and k % tk == 0, (name, a.shape, b.shape, tm, tn, tk)
    nk = k // tk
    has_res = res is not None

    def body(*refs):
        if has_res:
            a_ref, b_ref, r_ref, o_ref, acc = refs
        else:
            a_ref, b_ref, o_ref, acc = refs
        kk = pl.program_id(2)

        @pl.when(kk == 0)
        def _():
            acc[...] = jnp.zeros_like(acc)

        if precision is None:
            part = _dot(a_ref[...].astype(BF16), b_ref[...].astype(BF16), dims)
        else:
            part = lax.dot_general(a_ref[...], b_ref[...], (dims, ((), ())), precision=precision,
                                   preferred_element_type=F32)
        acc[...] += part

        @pl.when(kk == nk - 1)
        def _():
            r = acc[...]
            if has_res:
                r = r + r_ref[...].astype(F32)
            o_ref[...] = r.astype(out_dtype)

    in_specs = [a_spec, b_spec]
    args = [a, b]
    if has_res:
        in_specs.append(pl.BlockSpec((tm, tn), lambda i, j, kk: (i, j)))
        args.append(res)
    return pl.pallas_call(
        body, name=name, grid=(m // tm, n // tn, nk),
        in_specs=in_specs, out_specs=pl.BlockSpec((tm, tn), lambda i, j, kk: (i, j)),
        out_shape=jax.ShapeDtypeStruct((m, n), out_dtype),
        scratch_shapes=[pltpu.VMEM((tm, tn), F32)],
        compiler_params=_params("parallel", "parallel", "arbitrary"),
    )(*args)


def _rms_matmul(x, nw, w, *, name, tm, tn):
    t, d = x.shape
    n = w.shape[1]
    assert t % tm == 0 and n % tn == 0

    def body(x_ref, nw_ref, w_ref, o_ref, h_ref):
        @pl.when(pl.program_id(1) == 0)
        def _():
            xv = x_ref[...]
            r = lax.rsqrt(jnp.mean(xv * xv, axis=-1, keepdims=True) + NORM_EPS)
            h_ref[...] = (xv * r * nw_ref[...]).astype(BF16)

        o_ref[...] = _dot(h_ref[...], w_ref[...].astype(BF16), NN)

    return pl.pallas_call(
        body, name=name, grid=(t // tm, n // tn),
        in_specs=[pl.BlockSpec((tm, d), lambda i, j: (i, 0)),
                  pl.BlockSpec((1, d), lambda i, j: (0, 0)),
                  pl.BlockSpec((d, tn), lambda i, j: (0, j))],
        out_specs=[pl.BlockSpec((tm, tn), lambda i, j: (i, j)),
                   pl.BlockSpec((tm, d), lambda i, j: (i, 0))],
        out_shape=[jax.ShapeDtypeStruct((t, n), F32), jax.ShapeDtypeStruct((t, d), BF16)],
        compiler_params=_params("parallel", "arbitrary"),
    )(x, nw, w)


def _zero_ext(v):
    z = jnp.zeros((SUBLANES, v.shape[1]), v.dtype)
    return jnp.concatenate([z, v, z], axis=0)


def _shifted(v_ext, offset, seq):
    if offset == 0:
        return v_ext[SUBLANES:SUBLANES + seq]
    return pltpu.roll(v_ext, (-offset) % (seq + 2 * SUBLANES), 0)[SUBLANES:SUBLANES + seq]


def _conv_taps(v, w_ref, taps, seq):
    pad = taps // 2
    v_ext = _zero_ext(v)
    acc = None
    for k in range(taps):
        term = _shifted(v_ext, k - pad, seq) * w_ref[k:k + 1, :]
        acc = term if acc is None else acc + term
    return acc


def _ssm_conv_fwd(proj, cw, cb, *, nb, seq):
    width = 512

    def body(x_ref, w_ref, b_ref, o_ref):
        g = _conv_taps(x_ref[...], w_ref, SSM_TAPS, seq) + b_ref[...]
        o_ref[...] = g * _sigmoid(g)

    def col(j):
        return jnp.where(j < 2, j + PXS // width, PB // width)

    return pl.pallas_call(
        body, name="ssm_conv_fwd", grid=(nb, CONV_CH // width),
        in_specs=[pl.BlockSpec((seq, width), lambda b, j: (b, col(j))),
                  pl.BlockSpec((SSM_TAPS, width), lambda b, j: (0, j)),
                  pl.BlockSpec((1, width), lambda b, j: (0, j))],
        out_specs=pl.BlockSpec((seq, width), lambda b, j: (b, j)),
        out_shape=jax.ShapeDtypeStruct((nb * seq, CONV_CH), F32),
        compiler_params=_params("parallel", "parallel"),
    )(proj, cw, cb)


def _ffn_act_fwd(gu, cw, cb, *, nb, seq):
    width = 256
    nj = D_FF // width

    def body(g_ref, u_ref, w_ref, b_ref, o_ref):
        g = _conv_taps(g_ref[...], w_ref, FFN_TAPS, seq) + b_ref[...]
        o_ref[...] = (g * _sigmoid(g) * u_ref[...]).astype(BF16)

    return pl.pallas_call(
        body, name="ffn_act_fwd", grid=(nb, nj),
        in_specs=[pl.BlockSpec((seq, width), lambda b, j: (b, j)),
                  pl.BlockSpec((seq, width), lambda b, j: (b, j + nj)),
                  pl.BlockSpec((FFN_TAPS, width), lambda b, j: (0, j)),
                  pl.BlockSpec((1, width), lambda b, j: (0, j))],
        out_specs=pl.BlockSpec((seq, width), lambda b, j: (b, j)),
        out_shape=jax.ShapeDtypeStruct((nb * seq, D_FF), BF16),
        compiler_params=_params("parallel", "parallel"),
    )(gu, gu, cw, cb)


def _scan_setup(d, dt_ref, dtb_ref, alog_ref, z_ref, zt_ref, *, lower_when_dir0, inclusive):
    is0 = d == 0
    dt_all = _softplus(dt_ref[...] + dtb_ref[...])
    adt_all = dt_all * (-jnp.exp(alog_ref[...]))
    li = lax.broadcasted_iota(jnp.int32, (CHUNK, CHUNK), 0)
    si = lax.broadcasted_iota(jnp.int32, (CHUNK, CHUNK), 1)
    lower = is0 if lower_when_dir0 else jnp.logical_not(is0)
    ahead = jnp.where(lower, li - si, si - li)
    mask = ahead >= 0
    tri = mask if inclusive else ahead > 0
    z_all = jnp.dot(tri.astype(F32), adt_all, precision=HIGHEST, preferred_element_type=F32)
    zt_all = z_all.T
    z_ref[...] = jnp.where(is0, z_all[:, 0:N_HEADS], z_all[:, N_HEADS:2 * N_HEADS])
    zt_ref[...] = jnp.where(is0, zt_all[0:N_HEADS, :], zt_all[N_HEADS:2 * N_HEADS, :])
    dt = jnp.where(is0, dt_all[:, 0:N_HEADS], dt_all[:, N_HEADS:2 * N_HEADS])
    adt = jnp.where(is0, adt_all[:, 0:N_HEADS], adt_all[:, N_HEADS:2 * N_HEADS])
    tot = jnp.sum(adt, axis=0, keepdims=True)
    return ahead, dt, tot, dt_all


def _chunk_index(nchunk, forward_when_dir0):
    def idx(d, b, c):
        fwd = (d == 0) if forward_when_dir0 else (d != 0)
        return b * nchunk + jnp.where(fwd, c, nchunk - 1 - c)
    return idx


def _ssd_fwd(xbc, proj, dtb, alog, *, nb, seq):
    nchunk = seq // CHUNK
    t = nb * seq
    row = _chunk_index(nchunk, True)

    def body(xs_ref, bc_ref, dt_ref, dtb_ref, alog_ref, o_ref, hs_ref, h_ref, z_ref, zt_ref, dts_ref):
        d, c = pl.program_id(0), pl.program_id(2)

        @pl.when(c == 0)
        def _():
            h_ref[...] = jnp.zeros_like(h_ref)

        ahead, dt, tot, _ = _scan_setup(d, dt_ref, dtb_ref, alog_ref, z_ref, zt_ref,
                                        lower_when_dir0=True, inclusive=True)
        mask = ahead >= 0
        dts_ref[...] = dt
        e_tot = jnp.exp(tot)
        for g in range(N_GROUPS):
            bg = bc_ref[:, g * N_STATE:(g + 1) * N_STATE]
            cg = bc_ref[:, BC_WIDTH + g * N_STATE:BC_WIDTH + (g + 1) * N_STATE]
            cb = _dot(cg.astype(BF16), bg.astype(BF16), NT)
            for r in range(HEADS_PER_GROUP):
                h = g * HEADS_PER_GROUP + r
                zc = jnp.broadcast_to(z_ref[:, h:h + 1], (CHUNK, CHUNK))
                decay = jnp.exp(jnp.where(mask, zc - zt_ref[h:h + 1, :], -jnp.inf))
                u = (xs_ref[:, h * HEAD_DIM:(h + 1) * HEAD_DIM] * dts_ref[:, h:h + 1]).astype(BF16)
                state = h_ref[h]
                hs_ref[0, 0, h] = state
                y = _dot((cb * decay).astype(BF16), u, NN) + _dot((cg * jnp.exp(zc)).astype(BF16), state.astype(BF16), NT)
                bw = (bg * jnp.exp(tot[:, h:h + 1] - zc)).astype(BF16)
                h_ref[h] = state * e_tot[:, h:h + 1] + _dot(u, bw, TN)
                o_ref[0, :, h * HEAD_DIM:(h + 1) * HEAD_DIM] = y

    return pl.pallas_call(
        body, name="ssd_fwd", grid=(2, nb, nchunk),
        in_specs=[pl.BlockSpec((CHUNK, SSM_WIDTH), lambda d, b, c: (row(d, b, c), 0)),
                  pl.BlockSpec((CHUNK, 2 * BC_WIDTH), lambda d, b, c: (row(d, b, c), SSM_WIDTH // (2 * BC_WIDTH))),
                  pl.BlockSpec((CHUNK, LANES), lambda d, b, c: (row(d, b, c), PDT // LANES)),
                  pl.BlockSpec((1, LANES), lambda d, b, c: (0, 0)),
                  pl.BlockSpec((1, LANES), lambda d, b, c: (0, 0))],
        out_specs=[pl.BlockSpec((1, CHUNK, SSM_WIDTH), lambda d, b, c: (d, row(d, b, c), 0)),
                   pl.BlockSpec((1, 1, N_HEADS, HEAD_DIM, N_STATE), lambda d, b, c: (d, row(d, b, c), 0, 0, 0))],
        out_shape=[jax.ShapeDtypeStruct((2, t, SSM_WIDTH), F32),
                   jax.ShapeDtypeStruct((2, nb * nchunk, N_HEADS, HEAD_DIM, N_STATE), F32)],
        scratch_shapes=[pltpu.VMEM((N_HEADS, HEAD_DIM, N_STATE), F32),
                        pltpu.VMEM((CHUNK, N_HEADS), F32), pltpu.VMEM((N_HEADS, CHUNK), F32),
                        pltpu.VMEM((CHUNK, N_HEADS), F32)],
        compiler_params=_params("arbitrary", "arbitrary", "arbitrary"),
    )(xbc, xbc, proj, dtb, alog)


def _gate_norm_fwd(y2, xbc, proj, dvec, nw, *, tm):
    t = xbc.shape[0]
    half = SSM_WIDTH // N_GROUPS

    def body(y_ref, xs_ref, z_ref, d_ref, w_ref, o_ref):
        z = z_ref[...]
        p = (y_ref[0] + y_ref[1] + d_ref[...] * xs_ref[...]) * (z * _sigmoid(z))
        for g in range(N_GROUPS):
            pg = p[:, g * half:(g + 1) * half]
            r = lax.rsqrt(jnp.mean(pg * pg, axis=-1, keepdims=True) + NORM_EPS)
            o_ref[:, g * half:(g + 1) * half] = (pg * r * w_ref[:, g * half:(g + 1) * half]).astype(BF16)

    return pl.pallas_call(
        body, name="gate_norm_fwd", grid=(t // tm,),
        in_specs=[pl.BlockSpec((2, tm, SSM_WIDTH), lambda i: (0, i, 0)),
                  pl.BlockSpec((tm, SSM_WIDTH), lambda i: (i, 0)),
                  pl.BlockSpec((tm, SSM_WIDTH), lambda i: (i, PZ // SSM_WIDTH)),
                  pl.BlockSpec((1, SSM_WIDTH), lambda i: (0, 0)),
                  pl.BlockSpec((1, SSM_WIDTH), lambda i: (0, 0))],
        out_specs=pl.BlockSpec((tm, SSM_WIDTH), lambda i: (i, 0)),
        out_shape=jax.ShapeDtypeStruct((t, SSM_WIDTH), BF16),
        compiler_params=_params("parallel"),
    )(y2, xbc, proj, dvec, nw)


GROUP_ROWS = Q_PER_KV * CHUNK


def _band_mask(n, nblk):
    qi = lax.broadcasted_iota(jnp.int32, (GROUP_ROWS, KEY_SPAN), 0) & (CHUNK - 1)
    kj = lax.broadcasted_iota(jnp.int32, (GROUP_ROWS, KEY_SPAN), 1)
    rel = kj - CHUNK - qi
    kpos = (n - 1) * CHUNK + kj
    return (jnp.abs(rel) <= CHUNK) & (kpos >= 0) & (kpos < nblk * CHUNK)


def _per_head_column(ref, g):
    blk = lax.broadcasted_iota(jnp.int32, (GROUP_ROWS, 1), 0) // CHUNK
    col = jnp.zeros((GROUP_ROWS, 1), F32)
    for r in range(Q_PER_KV):
        h = g * Q_PER_KV + r
        col = jnp.where(blk == r, ref[:, h:h + 1], col)
    return col


def _stack_heads(ref, g, dtype):
    return jnp.concatenate([ref[:, (g * Q_PER_KV + r) * HEAD_DIM:(g * Q_PER_KV + r + 1) * HEAD_DIM].astype(dtype)
                            for r in range(Q_PER_KV)], axis=0)


def _kv_specs(nblk):
    kvb = PK // (2 * KV_WIDTH)

    def at(off):
        def idx(b, n):
            return (b * nblk + jnp.clip(n + off, 0, nblk - 1), kvb)
        return pl.BlockSpec((CHUNK, 2 * KV_WIDTH), idx)
    return [at(-1), at(0), at(1)]


def _attn_fwd(proj, bias, sink, *, nb, seq, gather=()):
    nblk = seq // CHUNK
    t = nb * seq
    scale = HEAD_DIM ** -0.5
    ng = len(gather)

    def body(*refs):
        q_ref, kp_ref, kc_ref, kn_ref, bias_ref, sink_ref = refs[:6]
        o_ref, lse_ref = refs[6 + ng:8 + ng]
        n = pl.program_id(1)
        if ng:
            step = pl.program_id(0) * nblk + n
            copies = _exchange_copies(refs[6:6 + ng], refs[8 + ng:8 + 2 * ng], *refs[8 + 2 * ng:], gather=True)
            pl.when(step == 0)(lambda: _exchange_start(copies))
        valid = _band_mask(n, nblk)
        lses = []
        for g in range(KV_HEADS):
            ks = slice(g * HEAD_DIM, (g + 1) * HEAD_DIM)
            vs = slice(KV_WIDTH + g * HEAD_DIM, KV_WIDTH + (g + 1) * HEAD_DIM)
            kcat = jnp.concatenate([kp_ref[:, ks], kc_ref[:, ks], kn_ref[:, ks]], axis=0).astype(BF16)
            vcat = jnp.concatenate([kp_ref[:, vs], kc_ref[:, vs], kn_ref[:, vs]], axis=0).astype(BF16)
            q = _stack_heads(q_ref, g, BF16)
            bias = bias_ref[g * Q_PER_KV:(g + 1) * Q_PER_KV].reshape(GROUP_ROWS, KEY_SPAN)
            s = jnp.where(valid, _dot(q, kcat, NT) * scale + bias, -jnp.inf)
            sk = _per_head_column(sink_ref, g)
            m = jnp.maximum(jnp.max(s, axis=-1, keepdims=True), sk)
            p = jnp.exp(s - m)
            denom = jnp.sum(p, axis=-1, keepdims=True) + jnp.exp(sk - m)
            o = _dot((p * (1.0 / denom)).astype(BF16), vcat, NN).astype(BF16)
            lse = m + jnp.log(denom)
            for r in range(Q_PER_KV):
                h = g * Q_PER_KV + r
                o_ref[:, h * HEAD_DIM:(h + 1) * HEAD_DIM] = o[r * CHUNK:(r + 1) * CHUNK]
                lses.append(lse[r * CHUNK:(r + 1) * CHUNK])
        lse_ref[...] = jnp.concatenate(lses, axis=1)
        if ng:
            pl.when(step == nb * nblk - 1)(lambda: _exchange_wait(copies))

    any_spec = pl.BlockSpec(memory_space=pl.ANY)
    res = pl.pallas_call(
        body, name="attn_fwd_gather" if ng else "attn_fwd", grid=(nb, nblk),
        in_specs=[pl.BlockSpec((CHUNK, D_MODEL), lambda b, n: (b * nblk + n, PQ // D_MODEL))] + _kv_specs(nblk) + [
            pl.BlockSpec((N_HEADS, CHUNK, KEY_SPAN), lambda b, n: (0, 0, 0)),
            pl.BlockSpec((1, LANES), lambda b, n: (0, 0))] + [any_spec] * ng,
        out_specs=[pl.BlockSpec((CHUNK, D_MODEL), lambda b, n: (b * nblk + n, 0)),
                   pl.BlockSpec((CHUNK, N_HEADS), lambda b, n: (b * nblk + n, 0))] + [any_spec] * ng,
        out_shape=[jax.ShapeDtypeStruct((t, D_MODEL), BF16), jax.ShapeDtypeStruct((t, N_HEADS), F32)]
        + _exchange_out_shapes(gather, gather=True),
        scratch_shapes=_exchange_sems(ng),
        compiler_params=_params("arbitrary", "arbitrary"),
    )(proj, proj, proj, proj, bias, sink, *gather)
    return res[0], res[1], list(res[2:])


def _loss_head(x, tgt, nw, *, tm):
    t, d = x.shape

    def body(x_ref, t_ref, w_ref, dx_ref, dw_ref, l_ref):
        @pl.when(pl.program_id(0) == 0)
        def _():
            dw_ref[...] = jnp.zeros_like(dw_ref)
            l_ref[...] = jnp.zeros_like(l_ref)

        xv = x_ref[...]
        w = w_ref[...]
        r = lax.rsqrt(jnp.mean(xv * xv, axis=-1, keepdims=True) + NORM_EPS)
        xh = xv * r
        err = xh * w - t_ref[...]
        l_ref[...] += jnp.sum(err * err) * (0.5 / d)
        dy = err * (1.0 / d)
        gw = dy * w
        dx_ref[...] = r * (gw - xh * jnp.mean(gw * xh, axis=-1, keepdims=True))
        dw_ref[...] += jnp.sum(dy * xh, axis=0, keepdims=True)

    return pl.pallas_call(
        body, name="loss_head", grid=(t // tm,),
        in_specs=[pl.BlockSpec((tm, d), lambda i: (i, 0)), pl.BlockSpec((tm, d), lambda i: (i, 0)),
                  pl.BlockSpec((1, d), lambda i: (0, 0))],
        out_specs=[pl.BlockSpec((tm, d), lambda i: (i, 0)), pl.BlockSpec((1, d), lambda i: (0, 0)),
                   pl.BlockSpec((1, LANES), lambda i: (0, 0))],
        out_shape=[jax.ShapeDtypeStruct((t, d), F32), jax.ShapeDtypeStruct((1, d), F32),
                   jax.ShapeDtypeStruct((1, LANES), F32)],
        compiler_params=_params("arbitrary"),
    )(x, tgt, nw)


def _to_proj_layout(w):
    pad = jnp.zeros(w.shape[:-1] + (PROJ_W - IN_COLS,), w.dtype)
    return jnp.concatenate([w[..., OZ:OXBC], w[..., OXBC:OXBC + SSM_WIDTH], w[..., OQ:OK_],
                            w[..., OXBC + SSM_WIDTH:ODT], w[..., OK_:IN_COLS], w[..., ODT:OQ], pad], axis=-1)


def _from_proj_layout(g):
    return jnp.concatenate([g[..., PZ:PZ + 2 * SSM_WIDTH], g[..., PB:PB + 2 * BC_WIDTH], g[..., PDT:PDT + 2 * N_HEADS],
                            g[..., PQ:PQ + D_MODEL], g[..., PK:PK + 2 * KV_WIDTH]], axis=-1)


def _pad_lanes(v):
    return jnp.pad(v.reshape(1, -1), ((0, 0), (0, LANES - v.size)))


def _layer_fwd(x, p, band_bias, *, nb, seq, gather=()):
    proj, h1 = _rms_matmul(x, p["n1"], p["w_in"], name="in_proj", tm=MM_ROWS, tn=1408)
    xbc = _ssm_conv_fwd(proj, p["conv_w"], p["conv_b"], nb=nb, seq=seq)
    y2, states = _ssd_fwd(xbc, proj, p["dtb"], p["alog"], nb=nb, seq=seq)
    y_ssm = _gate_norm_fwd(y2, xbc, proj, p["dvec"], p["ssm_nw"], tm=256)
    y_att, lse, gathered = _attn_fwd(proj, band_bias, p["sink"], nb=nb, seq=seq, gather=gather)
    x1 = _matmul(y_ssm, p["w_out"][:SSM_WIDTH], "nn", name="out_proj_ssm", tm=MM_ROWS, tn=1024, tk=1024, res=x)
    x1 = _matmul(y_att, p["w_out"][SSM_WIDTH:], "nn", name="out_proj_att", tm=MM_ROWS, tn=1024, tk=1024, res=x1)
    gu, h2 = _rms_matmul(x1, p["n2"], p["w_up"], name="up_proj", tm=MM_ROWS, tn=1408)
    act = _ffn_act_fwd(gu, p["ffn_cw"], p["ffn_cb"], nb=nb, seq=seq)
    x2 = _matmul(act, p["w_down"], "nn", name="down_proj", tm=MM_ROWS, tn=1024, tk=1408, res=x1)
    saved = dict(x=x, proj=proj, h1=h1, xbc=xbc, y2=y2, states=states, y_ssm=y_ssm, y_att=y_att, lse=lse, x1=x1, gu=gu, h2=h2, act=act)
    return x2, saved, gathered


def _rms_bwd(x, dh, nw, dres, *, tm, name):
    t, d = x.shape

    def body(x_ref, dh_ref, w_ref, r_ref, dx_ref, dw_ref):
        @pl.when(pl.program_id(0) == 0)
        def _():
            dw_ref[...] = jnp.zeros_like(dw_ref)

        xv = x_ref[...]
        dh_v = dh_ref[...].astype(F32)
        r = lax.rsqrt(jnp.mean(xv * xv, axis=-1, keepdims=True) + NORM_EPS)
        xh = xv * r
        gw = dh_v * w_ref[...]
        dx_ref[...] = r_ref[...] + r * (gw - xh * jnp.mean(gw * xh, axis=-1, keepdims=True))
        dw_ref[...] += jnp.sum(dh_v * xh, axis=0, keepdims=True)

    row = pl.BlockSpec((tm, d), lambda i: (i, 0))
    vec = pl.BlockSpec((1, d), lambda i: (0, 0))
    return pl.pallas_call(
        body, name=name, grid=(t // tm,), in_specs=[row, row, vec, row], out_specs=[row, vec],
        out_shape=[jax.ShapeDtypeStruct((t, d), F32), jax.ShapeDtypeStruct((1, d), F32)],
        compiler_params=_params("arbitrary"),
    )(x, dh, nw, dres)


def _dsilu(g, sg):
    return sg * (1.0 + g * (1.0 - sg))


def _conv_taps_bwd(gpre, dg, w_ref, dwb_ref, taps, seq):
    pad = taps // 2
    dg_ext, gpre_ext = _zero_ext(dg), _zero_ext(gpre)
    dpre = None
    for k in range(taps):
        term = _shifted(dg_ext, pad - k, seq) * w_ref[k:k + 1, :]
        dpre = term if dpre is None else dpre + term
        dwb_ref[k:k + 1, :] += jnp.sum(dg * _shifted(gpre_ext, k - pad, seq), axis=0, keepdims=True)
    dwb_ref[SUBLANES - 1:SUBLANES, :] += jnp.sum(dg, axis=0, keepdims=True)
    return dpre


def _ffn_act_bwd(gu, dact, cw, cb, *, nb, seq):
    width = 256
    nj = D_FF // width

    def body(g_ref, u_ref, da_ref, w_ref, b_ref, dg_ref, du_ref, dwb_ref):
        @pl.when(pl.program_id(1) == 0)
        def _():
            dwb_ref[...] = jnp.zeros_like(dwb_ref)

        gpre = g_ref[...]
        g = _conv_taps(gpre, w_ref, FFN_TAPS, seq) + b_ref[...]
        sg = _sigmoid(g)
        da = da_ref[...].astype(F32)
        du_ref[...] = (da * g * sg).astype(BF16)
        dgc = da * u_ref[...] * _dsilu(g, sg)
        dg_ref[...] = _conv_taps_bwd(gpre, dgc, w_ref, dwb_ref, FFN_TAPS, seq).astype(BF16)

    blk = lambda off: pl.BlockSpec((seq, width), lambda j, b: (b, j + off))
    return pl.pallas_call(
        body, name="ffn_act_bwd", grid=(nj, nb),
        in_specs=[blk(0), blk(nj), blk(0),
                  pl.BlockSpec((FFN_TAPS, width), lambda j, b: (0, j)),
                  pl.BlockSpec((1, width), lambda j, b: (0, j))],
        out_specs=[blk(0), blk(0), pl.BlockSpec((SUBLANES, width), lambda j, b: (0, j))],
        out_shape=[jax.ShapeDtypeStruct((nb * seq, D_FF), BF16), jax.ShapeDtypeStruct((nb * seq, D_FF), BF16),
                   jax.ShapeDtypeStruct((SUBLANES, D_FF), F32)],
        compiler_params=_params("parallel", "arbitrary"),
    )(gu, gu, dact, cw, cb)


def _ssm_conv_bwd(proj, pair, cw, cb, *, nb, seq, name, width, proj_col, conv_col, ncol, extra=None, scale=None):
    has_extra = extra is not None

    def body(*refs):
        if has_extra:
            x_ref, p_ref, w_ref, b_ref, e_ref, s_ref, dx_ref, dwb_ref = refs
        else:
            x_ref, p_ref, w_ref, b_ref, dx_ref, dwb_ref = refs

        @pl.when(pl.program_id(1) == 0)
        def _():
            dwb_ref[...] = jnp.zeros_like(dwb_ref)

        gpre = x_ref[...]
        g = _conv_taps(gpre, w_ref, SSM_TAPS, seq) + b_ref[...]
        sg = _sigmoid(g)
        da = p_ref[0] + p_ref[1]
        if has_extra:
            da = da + e_ref[...] * s_ref[...]
        dx_ref[...] = _conv_taps_bwd(gpre, da * _dsilu(g, sg), w_ref, dwb_ref, SSM_TAPS, seq)

    in_specs = [pl.BlockSpec((seq, width), lambda j, b: (b, j + proj_col)),
                pl.BlockSpec((2, seq, width), lambda j, b: (0, b, j)),
                pl.BlockSpec((SSM_TAPS, width), lambda j, b: (0, j + conv_col)),
                pl.BlockSpec((1, width), lambda j, b: (0, j + conv_col))]
    args = [proj, pair, cw, cb]
    if has_extra:
        in_specs += [pl.BlockSpec((seq, width), lambda j, b: (b, j)), pl.BlockSpec((1, width), lambda j, b: (0, j))]
        args += [extra, scale]
    return pl.pallas_call(
        body, name=name, grid=(ncol, nb), in_specs=in_specs,
        out_specs=[pl.BlockSpec((seq, width), lambda j, b: (b, j)), pl.BlockSpec((SUBLANES, width), lambda j, b: (0, j))],
        out_shape=[jax.ShapeDtypeStruct((nb * seq, ncol * width), F32), jax.ShapeDtypeStruct((SUBLANES, ncol * width), F32)],
        compiler_params=_params("parallel", "arbitrary"),
    )(*args)


def _attn_bwd(proj, dmix, y_att, lse, bias, sink, dbias_in, *, nb, seq):
    nblk = seq // CHUNK
    t = nb * seq
    scale = HEAD_DIM ** -0.5

    def body(q_ref, kp_ref, kc_ref, kn_ref, do_ref, o_ref, lse_ref, bias_ref, sink_ref, dbin_ref,
             dq_ref, dkv_ref, dbias_ref, dsink_ref):
        b, n = pl.program_id(0), pl.program_id(1)

        @pl.when(n == 0)
        def _():
            dkv_ref[...] = jnp.zeros_like(dkv_ref)

        @pl.when((n == 0) & (b == 0))
        def _():
            dbias_ref[...] = dbin_ref[...]
            dsink_ref[...] = jnp.zeros_like(dsink_ref)

        valid = _band_mask(n, nblk)
        lane = lax.broadcasted_iota(jnp.int32, (1, LANES), 1)
        dsink = jnp.zeros((1, LANES), F32)
        rows = pl.ds(pl.multiple_of(n * CHUNK, CHUNK), KEY_SPAN)
        for g in range(KV_HEADS):
            ks = slice(g * HEAD_DIM, (g + 1) * HEAD_DIM)
            vs = slice(KV_WIDTH + g * HEAD_DIM, KV_WIDTH + (g + 1) * HEAD_DIM)
            kcat = jnp.concatenate([kp_ref[:, ks], kc_ref[:, ks], kn_ref[:, ks]], axis=0).astype(BF16)
            vcat = jnp.concatenate([kp_ref[:, vs], kc_ref[:, vs], kn_ref[:, vs]], axis=0).astype(BF16)
            q = _stack_heads(q_ref, g, BF16)
            do = _stack_heads(do_ref, g, F32)
            lse = jnp.concatenate([lse_ref[:, g * Q_PER_KV + r:g * Q_PER_KV + r + 1] for r in range(Q_PER_KV)], axis=0)
            bias = bias_ref[g * Q_PER_KV:(g + 1) * Q_PER_KV].reshape(GROUP_ROWS, KEY_SPAN)
            s = jnp.where(valid, _dot(q, kcat, NT) * scale + bias, -jnp.inf)
            p = jnp.exp(s - lse)
            delta = jnp.sum(do * _stack_heads(o_ref, g, F32), axis=-1, keepdims=True)
            do16 = do.astype(BF16)
            ds = p * (_dot(do16, vcat, NT) - delta)
            dbias_ref[g * Q_PER_KV:(g + 1) * Q_PER_KV] += ds.reshape(Q_PER_KV, CHUNK, KEY_SPAN)
            sink_part = jnp.exp(_per_head_column(sink_ref, g) - lse) * delta
            ds16 = (ds * scale).astype(BF16)
            dq = _dot(ds16, kcat, NN)
            for r in range(Q_PER_KV):
                h = g * Q_PER_KV + r
                dq_ref[:, h * HEAD_DIM:(h + 1) * HEAD_DIM] = dq[r * CHUNK:(r + 1) * CHUNK]
                dsink = dsink - jnp.where(lane == h, jnp.sum(sink_part[r * CHUNK:(r + 1) * CHUNK], axis=0, keepdims=True), 0.0)
            dkv_ref[0, rows, ks] += _dot(ds16, q, TN)
            dkv_ref[0, rows, vs] += _dot(p.astype(BF16), do16, TN)
        dsink_ref[...] += dsink

    blk = lambda cb: pl.BlockSpec((CHUNK, D_MODEL), lambda b, n: (b * nblk + n, cb))
    whole = pl.BlockSpec((N_HEADS, CHUNK, KEY_SPAN), lambda b, n: (0, 0, 0))
    vec = pl.BlockSpec((1, LANES), lambda b, n: (0, 0))
    return pl.pallas_call(
        body, name="attn_bwd", grid=(nb, nblk),
        in_specs=[blk(PQ // D_MODEL)] + _kv_specs(nblk) + [
            blk(1), blk(0), pl.BlockSpec((CHUNK, N_HEADS), lambda b, n: (b * nblk + n, 0)), whole, vec, whole],
        out_specs=[blk(0), pl.BlockSpec((1, seq + 2 * CHUNK, 2 * KV_WIDTH), lambda b, n: (b, 0, 0)), whole, vec],
        out_shape=[jax.ShapeDtypeStruct((t, D_MODEL), F32),
                   jax.ShapeDtypeStruct((nb, seq + 2 * CHUNK, 2 * KV_WIDTH), F32),
                   jax.ShapeDtypeStruct((N_HEADS, CHUNK, KEY_SPAN), F32),
                   jax.ShapeDtypeStruct((1, LANES), F32)],
        compiler_params=_params("arbitrary", "arbitrary"),
    )(proj, proj, proj, proj, dmix, y_att, lse, bias, sink, dbias_in)


def _gate_norm_bwd(y2, xbc, proj, dmix, dvec, nw, *, tm):
    t = xbc.shape[0]
    half = SSM_WIDTH // N_GROUPS

    def body(y_ref, xs_ref, z_ref, do_ref, d_ref, w_ref, dyv_ref, dz_ref, dd_ref, dw_ref):
        @pl.when(pl.program_id(0) == 0)
        def _():
            dd_ref[...] = jnp.zeros_like(dd_ref)
            dw_ref[...] = jnp.zeros_like(dw_ref)

        z = z_ref[...]
        xs = xs_ref[...]
        sg = _sigmoid(z)
        gz = z * sg
        yv = y_ref[0] + y_ref[1] + d_ref[...] * xs
        p = yv * gz
        do = do_ref[...]
        for g in range(N_GROUPS):
            cs = slice(g * half, (g + 1) * half)
            pg = p[:, cs]
            r = lax.rsqrt(jnp.mean(pg * pg, axis=-1, keepdims=True) + NORM_EPS)
            ph = pg * r
            gw = do[:, cs] * w_ref[:, cs]
            dp = r * (gw - ph * jnp.mean(gw * ph, axis=-1, keepdims=True))
            dyv = dp * gz[:, cs]
            dyv_ref[:, cs] = dyv
            dz_ref[:, cs] = dp * yv[:, cs] * _dsilu(z[:, cs], sg[:, cs])
            dw_ref[:, cs] += jnp.sum(do[:, cs] * ph, axis=0, keepdims=True)
            dd_ref[:, cs] += jnp.sum(dyv * xs[:, cs], axis=0, keepdims=True)

    row = lambda cb: pl.BlockSpec((tm, SSM_WIDTH), lambda i: (i, cb))
    vec = pl.BlockSpec((1, SSM_WIDTH), lambda i: (0, 0))
    return pl.pallas_call(
        body, name="gate_norm_bwd", grid=(t // tm,),
        in_specs=[pl.BlockSpec((2, tm, SSM_WIDTH), lambda i: (0, i, 0)), row(0), row(PZ // SSM_WIDTH), row(0), vec, vec],
        out_specs=[row(0), row(0), vec, vec],
        out_shape=[jax.ShapeDtypeStruct((t, SSM_WIDTH), F32), jax.ShapeDtypeStruct((t, SSM_WIDTH), F32),
                   jax.ShapeDtypeStruct((1, SSM_WIDTH), F32), jax.ShapeDtypeStruct((1, SSM_WIDTH), F32)],
        compiler_params=_params("arbitrary"),
    )(y2, xbc, proj, dmix, dvec, nw)


def _ssd_specs(nchunk, row):
    return [pl.BlockSpec((CHUNK, SSM_WIDTH), lambda d, b, c: (row(d, b, c), 0)),
            pl.BlockSpec((CHUNK, 2 * BC_WIDTH), lambda d, b, c: (row(d, b, c), SSM_WIDTH // (2 * BC_WIDTH))),
            pl.BlockSpec((CHUNK, LANES), lambda d, b, c: (row(d, b, c), PDT // LANES)),
            pl.BlockSpec((1, LANES), lambda d, b, c: (0, 0)),
            pl.BlockSpec((1, LANES), lambda d, b, c: (0, 0)),
            pl.BlockSpec((CHUNK, SSM_WIDTH), lambda d, b, c: (row(d, b, c), 0))]


_SSD_SCRATCH = [pltpu.VMEM((N_HEADS, HEAD_DIM, N_STATE), F32),
                pltpu.VMEM((CHUNK, N_HEADS), F32), pltpu.VMEM((N_HEADS, CHUNK), F32),
                pltpu.VMEM((CHUNK, N_HEADS), F32)]


def _ssd_bwd(xbc, proj, dtb, alog, dyv, states, *, nb, seq, scatter=()):
    nchunk = seq // CHUNK
    t = nb * seq
    row = _chunk_index(nchunk, False)
    ns = len(scatter)

    def body(*refs):
        xs_ref, bc_ref, dt_ref, dtb_ref, alog_ref, dy_ref, hs_ref = refs[:7]
        dx_ref, db_ref, dc_ref, draw_ref, da_ref, dbias_ref = refs[7 + ns:13 + ns]
        h_ref, z_ref, zt_ref, dts_ref = refs[13 + 2 * ns:17 + 2 * ns]
        d, b, c = pl.program_id(0), pl.program_id(1), pl.program_id(2)
        if ns:
            step = (d * nb + b) * nchunk + c
            copies = _exchange_copies(refs[7:7 + ns], refs[13 + ns:13 + 2 * ns], *refs[17 + 2 * ns:], gather=False)
            pl.when(step == 0)(lambda: _exchange_start(copies))

        @pl.when(c == 0)
        def _():
            h_ref[...] = jnp.zeros_like(h_ref)

        @pl.when((c == 0) & (b == 0) & (d == 0))
        def _():
            da_ref[...] = jnp.zeros_like(da_ref)
            dbias_ref[...] = jnp.zeros_like(dbias_ref)

        ahead, dt, tot, dt_all = _scan_setup(d, dt_ref, dtb_ref, alog_ref, z_ref, zt_ref,
                                             lower_when_dir0=False, inclusive=False)
        mask = ahead >= 0
        dts_ref[...] = dt
        e_tot = jnp.exp(tot)
        lane = lax.broadcasted_iota(jnp.int32, (1, LANES), 1)
        ddt = jnp.zeros((CHUNK, LANES), F32)
        head_row = lax.broadcasted_iota(jnp.int32, (LANES, 1), 0)
        inner = jnp.zeros((CHUNK, LANES), F32)
        inner_t = jnp.zeros((LANES, CHUNK), F32)
        outer = jnp.zeros((CHUNK, LANES), F32)
        span = jnp.zeros((1, LANES), F32)
        for g in range(N_GROUPS):
            bg = bc_ref[:, g * N_STATE:(g + 1) * N_STATE]
            cg = bc_ref[:, BC_WIDTH + g * N_STATE:BC_WIDTH + (g + 1) * N_STATE]
            bg16 = bg.astype(BF16)
            cg16 = cg.astype(BF16)
            bc_t = _dot(bg16, cg16, NT)
            dbg = jnp.zeros((CHUNK, N_STATE), F32)
            dcg = jnp.zeros((CHUNK, N_STATE), F32)
            for r in range(HEADS_PER_GROUP):
                h = g * HEADS_PER_GROUP + r
                hs = slice(h * HEAD_DIM, (h + 1) * HEAD_DIM)
                here = lane == d * N_HEADS + h
                here_t = head_row == d * N_HEADS + h
                zc = jnp.broadcast_to(z_ref[:, h:h + 1], (CHUNK, CHUNK))
                decay = jnp.exp(jnp.where(mask, zc - zt_ref[h:h + 1, :], -jnp.inf))
                e_z = jnp.exp(zc)
                e_tz = jnp.exp(tot[:, h:h + 1] - zc)
                x_h = xs_ref[:, hs]
                dt_h = dts_ref[:, h:h + 1]
                u = (x_h * dt_h).astype(BF16)
                dy = dy_ref[:, hs].astype(BF16)
                state = h_ref[h]
                st16 = state.astype(BF16)
                fstate = hs_ref[0, 0, h]
                du = _dot((bc_t * decay).astype(BF16), dy, NN) + _dot((bg * e_z).astype(BF16), st16, NT)
                w2f = _dot(u, dy, NT) * decay
                w2 = w2f.astype(BF16)
                db_out = e_z * _dot(u, st16, NN)
                dc_out = e_tz * _dot(dy, fstate.astype(BF16), NN)
                dbg = dbg + _dot(w2, cg16, NN) + db_out
                dcg = dcg + _dot(w2, bg16, TN) + dc_out
                pairs = w2f * bc_t
                col_in = jnp.sum(pairs, axis=-1, keepdims=True)
                row_in = jnp.sum(pairs, axis=0, keepdims=True)
                row_out = jnp.sum(dc_out * cg, axis=-1, keepdims=True)
                col_out = jnp.sum(db_out * bg, axis=-1, keepdims=True)
                inner = inner + jnp.where(here, row_out - col_in, 0.0)
                inner_t = inner_t + jnp.where(here_t, row_in, 0.0)
                outer = outer + jnp.where(here, col_out, 0.0)
                span = span + jnp.where(here, e_tot[:, h:h + 1] * jnp.sum(fstate * state), 0.0)
                ddt = ddt + jnp.where(here, jnp.sum(du * x_h, axis=-1, keepdims=True), 0.0)
                dx_ref[0, :, hs] = du * dt_h
                h_ref[h] = state * e_tot[:, h:h + 1] + _dot(dy, (cg * e_tz).astype(BF16), TN)
            db_ref[0, :, g * N_STATE:(g + 1) * N_STATE] = dbg
            dc_ref[0, :, g * N_STATE:(g + 1) * N_STATE] = dcg
        tri = mask.astype(F32)
        dadt = (jnp.dot(tri, inner + inner_t.T, precision=HIGHEST, preferred_element_type=F32)
                + jnp.dot(1.0 - tri, outer, precision=HIGHEST, preferred_element_type=F32) + span)
        a = -jnp.exp(alog_ref[...])
        draw = (ddt + a * dadt) * _sigmoid(dt_ref[...] + dtb_ref[...])
        draw_ref[0] = draw
        da_ref[...] += jnp.sum(dt_all * dadt, axis=0, keepdims=True) * a
        dbias_ref[...] += jnp.sum(draw, axis=0, keepdims=True)
        if ns:
            pl.when(step == 2 * nb * nchunk - 1)(lambda: _exchange_wait(copies))

    out_row = lambda w: pl.BlockSpec((1, CHUNK, w), lambda d, b, c: (d, row(d, b, c), 0))
    vec = pl.BlockSpec((1, LANES), lambda d, b, c: (0, 0))
    any_spec = pl.BlockSpec(memory_space=pl.ANY)
    res = pl.pallas_call(
        body, name="ssd_bwd_scatter" if ns else "ssd_bwd", grid=(2, nb, nchunk),
        in_specs=_ssd_specs(nchunk, row) + [
            pl.BlockSpec((1, 1, N_HEADS, HEAD_DIM, N_STATE), lambda d, b, c: (d, row(d, b, c), 0, 0, 0))]
        + [any_spec] * ns,
        out_specs=[out_row(SSM_WIDTH), out_row(BC_WIDTH), out_row(BC_WIDTH), out_row(LANES), vec, vec] + [any_spec] * ns,
        out_shape=[jax.ShapeDtypeStruct((2, t, SSM_WIDTH), F32), jax.ShapeDtypeStruct((2, t, BC_WIDTH), F32),
                   jax.ShapeDtypeStruct((2, t, BC_WIDTH), F32),
                   jax.ShapeDtypeStruct((2, t, LANES), F32), jax.ShapeDtypeStruct((1, LANES), F32),
                   jax.ShapeDtypeStruct((1, LANES), F32)] + _exchange_out_shapes(scatter, gather=False),
        scratch_shapes=_SSD_SCRATCH + _exchange_sems(ns),
        compiler_params=_params("arbitrary", "arbitrary", "arbitrary"),
    )(xbc, xbc, proj, dtb, alog, dyv, states, *scatter)
    return tuple(res[:6]) + (list(res[6:]),)


def _layer_bwd(dx2, p, s, band_bias, dbias_in, *, nb, seq, scatter=()):
    t = nb * seq
    x, proj, xbc, x1 = s["x"], s["proj"], s["xbc"], s["x1"]
    dact = _matmul(dx2, p["w_down"], "nt", name="down_proj_dx", tm=MM_ROWS, tn=1408, tk=1024, out_dtype=BF16)
    g_w_down = _matmul(s["act"], dx2, "tn", name="down_proj_dw", tm=1408, tn=1024, tk=1024)
    dg, du, dwb_ffn = _ffn_act_bwd(s["gu"], dact, p["ffn_cw"], p["ffn_cb"], nb=nb, seq=seq)
    dh2 = _matmul(dg, p["w_up"][:, :D_FF], "nt", name="up_proj_dx_g", tm=MM_ROWS, tn=1024, tk=1408)
    dh2 = _matmul(du, p["w_up"][:, D_FF:], "nt", name="up_proj_dx_u", tm=MM_ROWS, tn=1024, tk=1408, res=dh2)
    g_w_up = jnp.concatenate([_matmul(s["h2"], dg, "tn", name="up_proj_dw_g", tm=1024, tn=1408, tk=1024),
                              _matmul(s["h2"], du, "tn", name="up_proj_dw_u", tm=1024, tn=1408, tk=1024)], axis=1)
    dx1, g_n2 = _rms_bwd(x1, dh2, p["n2"], dx2, tm=512, name="norm2_bwd")
    dmix = _matmul(dx1, p["w_out"], "nt", name="out_proj_dx", tm=MM_ROWS, tn=1024, tk=1024)
    g_w_out = jnp.concatenate([_matmul(s["y_ssm"], dx1, "tn", name="out_proj_dw_ssm", tm=1024, tn=1024, tk=1024),
                               _matmul(s["y_att"], dx1, "tn", name="out_proj_dw_att", tm=1024, tn=1024, tk=1024)], axis=0)
    dq, dkv, dbias, dsink = _attn_bwd(proj, dmix, s["y_att"], s["lse"], band_bias, p["sink"], dbias_in, nb=nb, seq=seq)
    dkv = dkv[:, CHUNK:CHUNK + seq, :].reshape(t, 2 * KV_WIDTH)
    dyv, dz, g_dvec, g_ssm_nw = _gate_norm_bwd(s["y2"], xbc, proj, dmix, p["dvec"], p["ssm_nw"], tm=256)
    dxs2, db2, dc2, draw2, g_alog, g_dtb, exchanged = _ssd_bwd(xbc, proj, p["dtb"], p["alog"], dyv, s["states"],
                                                               nb=nb, seq=seq, scatter=scatter)
    ddt_raw = draw2[0] + draw2[1]
    conv = dict(nb=nb, seq=seq)
    dxs_pre, dwb_xs = _ssm_conv_bwd(proj, dxs2, p["conv_w"], p["conv_b"], name="ssm_conv_bwd_x", width=256,
                                    proj_col=PXS // 256, conv_col=0, ncol=4, extra=dyv, scale=p["dvec"], **conv)
    db_pre, dwb_b = _ssm_conv_bwd(proj, db2, p["conv_w"], p["conv_b"], name="ssm_conv_bwd_b", width=256,
                                  proj_col=PB // 256, conv_col=SSM_WIDTH // 256, ncol=1, **conv)
    dc_pre, dwb_c = _ssm_conv_bwd(proj, dc2, p["conv_w"], p["conv_b"], name="ssm_conv_bwd_c", width=256,
                                  proj_col=PC // 256, conv_col=(SSM_WIDTH + BC_WIDTH) // 256, ncol=1, **conv)
    dwb_ssm = jnp.concatenate([dwb_xs, dwb_b, dwb_c], axis=1)
    dproj = jnp.concatenate([dz, dxs_pre, dq, db_pre, dc_pre, dkv, ddt_raw], axis=1).astype(BF16)
    dh1 = _matmul(dproj, p["w_in"], "nt", name="in_proj_dx", tm=MM_ROWS, tn=1024, tk=1408)
    g_w_in = _matmul(s["h1"], dproj, "tn", name="in_proj_dw", tm=1024, tn=1408, tk=1024)
    dx, g_n1 = _rms_bwd(x, dh1, p["n1"], dx1, tm=512, name="norm1_bwd")
    grads = dict(n1=g_n1, w_in=g_w_in, conv_w=dwb_ssm[:SSM_TAPS], conv_b=dwb_ssm[SUBLANES - 1], dtb=g_dtb, alog=g_alog,
                 dvec=g_dvec, ssm_nw=g_ssm_nw, sink=dsink, w_out=g_w_out, n2=g_n2, w_up=g_w_up,
                 ffn_cw=dwb_ffn[:FFN_TAPS], ffn_cb=dwb_ffn[SUBLANES - 1], w_down=g_w_down)
    return dx, dbias, grads, exchanged


def _band_bias(rel_bias, bucket):
    def body(rb_ref, b_ref, o_ref):
        o_ref[...] = jnp.zeros_like(o_ref)

        def per_bucket(k, carry):
            hit = b_ref[...] == k
            for h in range(N_HEADS):
                o_ref[h] = jnp.where(hit, rb_ref[k, h], o_ref[h])
            return carry

        lax.fori_loop(0, REL_BUCKETS, per_bucket, 0)

    return pl.pallas_call(
        body, name="band_bias", out_shape=jax.ShapeDtypeStruct((N_HEADS, CHUNK, KEY_SPAN), F32),
        in_specs=[pl.BlockSpec(memory_space=pltpu.SMEM), pl.BlockSpec(memory_space=pltpu.VMEM)],
        out_specs=pl.BlockSpec(memory_space=pltpu.VMEM),
    )(rel_bias, bucket)


def _rel_bias_grad(dbias, bucket):
    def body(d_ref, b_ref, o_ref):
        o_ref[...] = jnp.zeros_like(o_ref)
        lane = lax.broadcasted_iota(jnp.int32, (1, LANES), 1)

        def per_bucket(k, carry):
            hit = b_ref[...] == k
            for h in range(N_HEADS):
                part = jnp.sum(jnp.where(hit, d_ref[h], 0.0), axis=1, keepdims=True)
                o_ref[h:h + 1, :] += jnp.where(lane == k, jnp.sum(part, axis=0, keepdims=True), 0.0)
            return carry

        lax.fori_loop(0, REL_BUCKETS, per_bucket, 0)

    return pl.pallas_call(
        body, name="rel_bias_grad", out_shape=jax.ShapeDtypeStruct((N_HEADS, LANES), F32),
        compiler_params=pltpu.CompilerParams(vmem_limit_bytes=VMEM_LIMIT_BYTES),
    )(dbias, bucket)


N_PEER = N_DEV - 1


def _exchange_copies(ins, outs, send_sems, recv_sems, local_sems, *, gather):
    x, y, c = lax.axis_index("x"), lax.axis_index("y"), lax.axis_index("c")
    me = 4 * x + 2 * y + c
    peers = []
    for k in range(1, N_DEV):
        px, py, pc = x ^ ((k >> 2) & 1), y ^ ((k >> 1) & 1), c ^ (k & 1)
        peers.append(((px, py, pc), 4 * px + 2 * py + pc))
    local, sends, recvs = [], [], []
    for i in range(len(ins)):
        mine = ins[i] if gather else ins[i].at[me]
        local.append(pltpu.make_async_copy(mine, outs[i].at[me], local_sems.at[i]))
        for k, (pid, pslot) in enumerate(peers):
            src = ins[i] if gather else ins[i].at[pslot]
            sem = i * N_PEER + k
            sends.append(pltpu.make_async_remote_copy(
                src_ref=src, dst_ref=outs[i].at[me], send_sem=send_sems.at[sem], recv_sem=recv_sems.at[sem],
                device_id=pid, device_id_type=pl.DeviceIdType.MESH))
            recvs.append(pltpu.make_async_remote_copy(
                src_ref=src, dst_ref=outs[i].at[pslot], send_sem=send_sems.at[sem], recv_sem=recv_sems.at[sem],
                device_id=pid, device_id_type=pl.DeviceIdType.MESH))
    return local, sends, recvs


def _exchange_start(copies):
    local, sends, _ = copies
    for cp in local + sends:
        cp.start()


def _exchange_wait(copies):
    local, sends, recvs = copies
    for cp in recvs:
        cp.wait_recv()
    for cp in sends:
        cp.wait_send()
    for cp in local:
        cp.wait()


def _exchange_out_shapes(arrs, *, gather):
    return [jax.ShapeDtypeStruct((N_DEV,) + (a.shape if gather else a.shape[1:]), a.dtype) for a in arrs]


def _exchange_sems(n):
    if not n:
        return []
    return [pltpu.SemaphoreType.DMA((n * N_PEER,)), pltpu.SemaphoreType.DMA((n * N_PEER,)), pltpu.SemaphoreType.DMA((n,))]


def _exchange(arrs, *, gather, name):
    n = len(arrs)

    def body(*refs):
        copies = _exchange_copies(refs[:n], refs[n:2 * n], *refs[2 * n:], gather=gather)
        _exchange_start(copies)
        _exchange_wait(copies)

    any_spec = pl.BlockSpec(memory_space=pl.ANY)
    return pl.pallas_call(
        body, name=name, in_specs=[any_spec] * n, out_specs=[any_spec] * n,
        out_shape=_exchange_out_shapes(arrs, gather=gather), scratch_shapes=_exchange_sems(n),
        compiler_params=pltpu.CompilerParams(has_side_effects=True),
    )(*arrs)


def _adamw(parts, w, m, v, *, name, tr):
    r, c = w.shape
    nparts = len(parts)
    rows = r // nparts
    assert rows * nparts == r and rows % tr == 0 and all(p.shape == (N_DEV, rows, c) for p in parts)
    per = rows // tr
    c1 = 1.0 - ADAM_B1 ** ADAM_STEP
    c2 = 1.0 - ADAM_B2 ** ADAM_STEP

    def body(*refs):
        p_refs = refs[:nparts]
        w_ref, m_ref, v_ref, g_ref, d_ref, nm_ref, nv_ref = refs[nparts:]
        which = pl.program_id(0) // per
        for k, p_ref in enumerate(p_refs):
            @pl.when(which == k)
            def _(p_ref=p_ref):
                acc = p_ref[0].astype(F32)
                for j in range(1, N_DEV):
                    acc = acc + p_ref[j].astype(F32)
                g_ref[...] = acc

        g = g_ref[...]
        nm = ADAM_B1 * m_ref[...] + (1.0 - ADAM_B1) * g
        nv = ADAM_B2 * v_ref[...] + (1.0 - ADAM_B2) * (g * g)
        nm_ref[...] = nm
        nv_ref[...] = nv
        d_ref[...] = -ADAM_LR * ((nm / c1) / (jnp.sqrt(nv / c2) + ADAM_EPS) + ADAM_WD * w_ref[...])

    def part_spec(k):
        return pl.BlockSpec((N_DEV, tr, c), lambda i: (0, jnp.clip(i - k * per, 0, per - 1), 0))

    blk = pl.BlockSpec((tr, c), lambda i: (i, 0))
    return pl.pallas_call(
        body, name=name, grid=(r // tr,),
        in_specs=[part_spec(k) for k in range(nparts)] + [blk, blk, blk],
        out_specs=[blk] * 4, out_shape=[jax.ShapeDtypeStruct((r, c), F32)] * 4,
        compiler_params=_params("arbitrary"),
    )(*parts, w, m, v)


def _t5_bucket(rel):
    half = REL_BUCKETS // 2
    max_exact = half // 2
    ret = jnp.where(rel > 0, half, 0)
    n = jnp.abs(rel)
    nf = jnp.maximum(n, 1).astype(F32)
    large = max_exact + (jnp.log(nf / max_exact) / math.log(CHUNK / max_exact) * (half - max_exact)).astype(jnp.int32)
    large = jnp.minimum(large, half - 1)
    return ret + jnp.where(n < max_exact, n, large)


def _split16(w):
    hi = w.astype(BF16)
    return hi, (w - hi.astype(F32)).astype(BF16)


def _cols_to_blocks(g, depth):
    _, r, c8 = g.shape
    return g.reshape(depth, r, N_DEV, c8 // N_DEV).transpose(2, 0, 1, 3).reshape(N_DEV, depth * r, c8 // N_DEV)


def _rows_to_blocks(g, depth):
    _, r8, c = g.shape
    return g.reshape(depth, N_DEV, r8 // N_DEV, c).transpose(1, 0, 2, 3).reshape(N_DEV, depth * r8 // N_DEV, c)


def _blocks_to_cols(a, depth):
    _, dr, c = a.shape
    r = dr // depth
    return a.reshape(N_DEV, depth, r, c).transpose(1, 2, 0, 3).reshape(depth, r, N_DEV * c)


def _blocks_to_rows(a, depth):
    _, dr, c = a.shape
    r = dr // depth
    return a.reshape(N_DEV, depth, r, c).transpose(1, 0, 2, 3).reshape(depth, N_DEV * r, c)


_SMALL = ("rel_bias", "norm1_w", "conv_b", "dt_bias", "a_log", "d_skip", "ssm_norm_w", "attn_sink", "norm2_w",
          "ffn_conv_b", "final_norm_w")
_SHARDED = ("w_in", "conv_w", "w_out", "w_up", "ffn_conv_w", "w_down")
_ORDER = ("rel_bias", "norm1_w", "w_in", "conv_w", "conv_b", "dt_bias", "a_log", "d_skip", "ssm_norm_w", "attn_sink",
          "w_out", "norm2_w", "w_up", "ffn_conv_w", "ffn_conv_b", "w_down", "final_norm_w")


def _pack_small(d):
    flat = jnp.concatenate([d[k].reshape(-1).astype(F32) for k in _SMALL])
    rows = -(-flat.size // (LANES * SUBLANES)) * SUBLANES
    return jnp.pad(flat, (0, rows * LANES - flat.size)).reshape(rows, LANES)


def _unpack_small(packed, like):
    flat = packed.reshape(-1)
    out, off = {}, 0
    for k in _SMALL:
        out[k] = flat[off:off + like[k].size].reshape(like[k].shape)
        off += like[k].size
    return out


def kernel(x, rel_bias, norm1_w, w_in, conv_w, conv_b, dt_bias, a_log, d_skip, ssm_norm_w, attn_sink, w_out, norm2_w, w_up, ffn_conv_w, ffn_conv_b, w_down, final_norm_w, loss_target, m_rel_bias, m_norm1_w, m_w_in, m_conv_w, m_conv_b, m_dt_bias, m_a_log, m_d_skip, m_ssm_norm_w, m_attn_sink, m_w_out, m_norm2_w, m_w_up, m_ffn_conv_w, m_ffn_conv_b, m_w_down, m_final_norm_w, v_rel_bias, v_norm1_w, v_w_in, v_conv_w, v_conv_b, v_dt_bias, v_a_log, v_d_skip, v_ssm_norm_w, v_attn_sink, v_w_out, v_norm2_w, v_w_up, v_ffn_conv_w, v_ffn_conv_b, v_w_down, v_final_norm_w):
    w = dict(rel_bias=rel_bias, norm1_w=norm1_w, w_in=w_in, conv_w=conv_w, conv_b=conv_b, dt_bias=dt_bias, a_log=a_log,
             d_skip=d_skip, ssm_norm_w=ssm_norm_w, attn_sink=attn_sink, w_out=w_out, norm2_w=norm2_w, w_up=w_up,
             ffn_conv_w=ffn_conv_w, ffn_conv_b=ffn_conv_b, w_down=w_down, final_norm_w=final_norm_w)
    m = dict(rel_bias=m_rel_bias, norm1_w=m_norm1_w, w_in=m_w_in, conv_w=m_conv_w, conv_b=m_conv_b, dt_bias=m_dt_bias,
             a_log=m_a_log, d_skip=m_d_skip, ssm_norm_w=m_ssm_norm_w, attn_sink=m_attn_sink, w_out=m_w_out,
             norm2_w=m_norm2_w, w_up=m_w_up, ffn_conv_w=m_ffn_conv_w, ffn_conv_b=m_ffn_conv_b, w_down=m_w_down,
             final_norm_w=m_final_norm_w)
    v = dict(rel_bias=v_rel_bias, norm1_w=v_norm1_w, w_in=v_w_in, conv_w=v_conv_w, conv_b=v_conv_b, dt_bias=v_dt_bias,
             a_log=v_a_log, d_skip=v_d_skip, ssm_norm_w=v_ssm_norm_w, attn_sink=v_attn_sink, w_out=v_w_out,
             norm2_w=v_norm2_w, w_up=v_w_up, ffn_conv_w=v_ffn_conv_w, ffn_conv_b=v_ffn_conv_b, w_down=v_w_down,
             final_norm_w=v_final_norm_w)
    nb, seq, _ = x.shape
    t = nb * seq
    depth = w_in.shape[0]

    flat2 = lambda a: a.reshape(-1, a.shape[-1])
    big_shards = lambda i: [w_in[i].astype(BF16), w_out[i].astype(BF16), w_up[i].astype(BF16), w_down[i].astype(BF16)]
    cw_hi, cw_lo = _split16(flat2(conv_w))
    fw_hi, fw_lo = _split16(flat2(ffn_conv_w))
    first = _exchange(big_shards(0) + [cw_hi, cw_lo, fw_hi, fw_lo], gather=True, name="gather_weights")
    gathered, (g_cwh, g_cwl, g_fwh, g_fwl) = first[:4], first[4:]
    full_conv_w = _blocks_to_cols(g_cwh.astype(F32) + g_cwl.astype(F32), depth)
    full_ffn_cw = _blocks_to_cols(g_fwh.astype(F32) + g_fwl.astype(F32), depth)

    rel = jnp.arange(KEY_SPAN)[None, :] - CHUNK - jnp.arange(CHUNK)[:, None]
    bucket = _t5_bucket(rel)
    band_bias = _band_bias(rel_bias, bucket)

    def layer_params(i, g_in, g_out, g_up, g_down):
        return dict(n1=norm1_w[i][None], w_in=_to_proj_layout(_blocks_to_cols(g_in, 1)[0]), conv_w=full_conv_w[i],
                    conv_b=conv_b[i][None], dtb=_pad_lanes(dt_bias[i].reshape(-1)), alog=_pad_lanes(a_log[i].reshape(-1)),
                    dvec=jnp.repeat(d_skip[i], HEAD_DIM)[None], ssm_nw=ssm_norm_w[i][None], sink=_pad_lanes(attn_sink[i]),
                    w_out=_blocks_to_rows(g_out, 1)[0], n2=norm2_w[i][None], w_up=_blocks_to_cols(g_up, 1)[0],
                    ffn_cw=full_ffn_cw[i], ffn_cb=ffn_conv_b[i][None], w_down=_blocks_to_rows(g_down, 1)[0])

    h = x.reshape(t, D_MODEL)
    params, saved = [], []
    for i in range(depth):
        params.append(layer_params(i, *gathered))
        h, s, gathered = _layer_fwd(h, params[i], band_bias, nb=nb, seq=seq,
                                    gather=big_shards(i + 1) if i + 1 < depth else ())
        saved.append(s)
    dh, g_final, loss_part = _loss_head(h, loss_target.reshape(t, D_MODEL), final_norm_w[None], tm=512)
    loss = lax.psum(loss_part[0, 0], ("x", "y", "c"))

    def big_blocks(g):
        return [_cols_to_blocks(_from_proj_layout(g["w_in"])[None], 1).astype(BF16),
                _rows_to_blocks(g["w_out"][None], 1).astype(BF16), _cols_to_blocks(g["w_up"][None], 1).astype(BF16),
                _rows_to_blocks(g["w_down"][None], 1).astype(BF16)]

    dbias = jnp.zeros((N_HEADS, CHUNK, KEY_SPAN), F32)
    grads, big_parts, pending = [None] * depth, [None] * depth, ()
    for i in reversed(range(depth)):
        dh, dbias, grads[i], arrived = _layer_bwd(dh, params[i], saved[i], band_bias, dbias, nb=nb, seq=seq,
                                                  scatter=pending)
        if pending:
            big_parts[i + 1] = arrived
        pending = big_blocks(grads[i])
    grad_x = dh.reshape(nb, seq, D_MODEL)
    stack = lambda k: jnp.stack([g[k] for g in grads])
    last = _exchange(pending + [_cols_to_blocks(stack("conv_w"), depth).astype(BF16),
                                _cols_to_blocks(stack("ffn_cw"), depth).astype(BF16)], gather=False, name="scatter_grads")
    big_parts[0] = last[:4]

    out = {}
    for j, k in enumerate(("w_in", "w_out", "w_up", "w_down")):
        rows = big_parts[0][j].shape[1]
        tr = max(d for d in range(16, 129, 16) if rows % d == 0)
        res = _adamw([big_parts[i][j] for i in range(depth)], flat2(w[k]), flat2(m[k]), flat2(v[k]), name="adamw_" + k, tr=tr)
        out[k] = [a.reshape(w[k].shape) for a in res]
    for k, p8 in zip(("conv_w", "ffn_conv_w"), last[4:]):
        res = _adamw([p8], flat2(w[k]), flat2(m[k]), flat2(v[k]), name="adamw_" + k, tr=p8.shape[1])
        out[k] = [a.reshape(w[k].shape) for a in res]

    small = dict(rel_bias=_rel_bias_grad(dbias, bucket)[:, :REL_BUCKETS].T, norm1_w=stack("n1"), conv_b=stack("conv_b"),
                 dt_bias=stack("dtb")[:, 0, :2 * N_HEADS], a_log=stack("alog")[:, 0, :2 * N_HEADS],
                 d_skip=stack("dvec").reshape(depth, N_HEADS, HEAD_DIM).sum(-1), ssm_norm_w=stack("ssm_nw"),
                 attn_sink=stack("sink")[:, 0, :N_HEADS], norm2_w=stack("n2"), ffn_conv_b=stack("ffn_cb"),
                 final_norm_w=g_final)
    (small_parts,) = _exchange([_pack_small(small)], gather=True, name="gather_small_grads")
    res = _adamw([small_parts], _pack_small(w), _pack_small(m), _pack_small(v), name="adamw_small", tr=small_parts.shape[1])
    unpacked = [_unpack_small(a, w) for a in res]
    for k in _SMALL:
        out[k] = [u[k] for u in unpacked]

    return (loss, grad_x, *[out[k][0] for k in _ORDER], *[out[k][1] for k in _ORDER],
            *[out[k][2] for k in _ORDER], *[out[k][3] for k in _ORDER])
```

```python
import math

import numpy as np
import jax
import jax.numpy as jnp
from jax import lax
from jax.experimental import pallas as pl
from jax.experimental.pallas import tpu as pltpu

F32, BF16 = jnp.float32, jnp.bfloat16
HIGHEST = lax.Precision.HIGHEST

D_MODEL = 1024
HEAD_DIM = 64
N_HEADS = 16
N_GROUPS = 2
HEADS_PER_GROUP = N_HEADS // N_GROUPS
N_STATE = 128
SSM_WIDTH = 1024
BC_WIDTH = 256
CONV_CH = SSM_WIDTH + 2 * BC_WIDTH
SSM_TAPS = 7
CHUNK = 128
KV_HEADS = 4
KV_WIDTH = 256
Q_PER_KV = N_HEADS // KV_HEADS
KEY_SPAN = 3 * CHUNK
REL_BUCKETS = 32
D_FF = 2816
FFN_TAPS = 3
IN_COLS = 4128
NORM_EPS = 1e-6
N_DEV = 8

LANES = 128
SUBLANES = 8
VMEM_LIMIT_BYTES = 56 * 1024 * 1024
MM_ROWS = 1024

PZ, PXS, PQ, PB, PC, PK, PV, PDT, PROJ_W = 0, 1024, 2048, 3072, 3328, 3584, 3840, 4096, 4224
OZ, OXBC, ODT, OQ, OK_, OV = 0, 1024, 2560, 2592, 3616, 3872

ADAM_LR, ADAM_B1, ADAM_B2, ADAM_EPS, ADAM_WD, ADAM_STEP = 0.001, 0.9, 0.999, 1e-08, 0.01, 10


def _params(*sem):
    return pltpu.CompilerParams(dimension_semantics=sem, vmem_limit_bytes=VMEM_LIMIT_BYTES)


def _sigmoid(x):
    return 1.0 / (1.0 + jnp.exp(-x))


def _softplus(x):
    return jnp.maximum(x, 0.0) + jnp.log(1.0 + jnp.exp(-jnp.abs(x)))


def _dot(a, b, dims):
    return lax.dot_general(a, b, (dims, ((), ())), preferred_element_type=F32)


NN = ((1,), (0,))
NT = ((1,), (1,))
TN = ((0,), (0,))


def _matmul(a, b, mode, *, name, tm, tn, tk, res=None, out_dtype=F32, precision=None):
    if mode == "nn":
        (m, k), (k2, n) = a.shape, b.shape
        a_spec = pl.BlockSpec((tm, tk), lambda i, j, kk: (i, kk))
        b_spec = pl.BlockSpec((tk, tn), lambda i, j, kk: (kk, j))
        dims = NN
    elif mode == "nt":
        (m, k), (n, k2) = a.shape, b.shape
        a_spec = pl.BlockSpec((tm, tk), lambda i, j, kk: (i, kk))
        b_spec = pl.BlockSpec((tn, tk), lambda i, j, kk: (j, kk))
        dims = NT
    else:
        (k, m), (k2, n) = a.shape, b.shape
        a_spec = pl.BlockSpec((tk, tm), lambda i, j, kk: (kk, i))
        b_spec = pl.BlockSpec((tk, tn), lambda i, j, kk: (kk, j))
        dims = TN
    assert k == k2 and m % tm == 0 and n % tn == 0 and k % tk == 0, (name, a.shape, b.shape, tm, tn, tk)
    nk = k // tk
    has_res = res is not None

    def body(*refs):
        if has_res:
            a_ref, b_ref, r_ref, o_ref, acc = refs
        else:
            a_ref, b_ref, o_ref, acc = refs
        kk = pl.program_id(2)

        @pl.when(kk == 0)
        def _():
            acc[...] = jnp.zeros_like(acc)

        if precision is None:
            part = _dot(a_ref[...].astype(BF16), b_ref[...].astype(BF16), dims)
        else:
            part = lax.dot_general(a_ref[...], b_ref[...], (dims, ((), ())), precision=precision,
                                   preferred_element_type=F32)
        acc[...] += part

        @pl.when(kk == nk - 1)
        def _():
            r = acc[...]
            if has_res:
                r = r + r_ref[...].astype(F32)
            o_ref[...] = r.astype(out_dtype)

    in_specs = [a_spec, b_spec]
    args = [a, b]
    if has_res:
        in_specs.append(pl.BlockSpec((tm, tn), lambda i, j, kk: (i, j)))
        args.append(res)
    return pl.pallas_call(
        body, name=name, grid=(m // tm, n // tn, nk),
        in_specs=in_specs, out_specs=pl.BlockSpec((tm, tn), lambda i, j, kk: (i, j)),
        out_shape=jax.ShapeDtypeStruct((m, n), out_dtype),
        scratch_shapes=[pltpu.VMEM((tm, tn), F32)],
        compiler_params=_params("parallel", "parallel", "arbitrary"),
    )(*args)


def _rms_matmul(x, nw, w, *, name, tm, tn):
    t, d = x.shape
    n = w.shape[1]
    assert t % tm == 0 and n % tn == 0

    def body(x_ref, nw_ref, w_ref, o_ref, h_ref):
        @pl.when(pl.program_id(1) == 0)
        def _():
            xv = x_ref[...]
            r = lax.rsqrt(jnp.mean(xv * xv, axis=-1, keepdims=True) + NORM_EPS)
            h_ref[...] = (xv * r * nw_ref[...]).astype(BF16)

        o_ref[...] = _dot(h_ref[...], w_ref[...].astype(BF16), NN).astype(BF16)

    return pl.pallas_call(
        body, name=name, grid=(t // tm, n // tn),
        in_specs=[pl.BlockSpec((tm, d), lambda i, j: (i, 0)),
                  pl.BlockSpec((1, d), lambda i, j: (0, 0)),
                  pl.BlockSpec((d, tn), lambda i, j: (0, j))],
        out_specs=[pl.BlockSpec((tm, tn), lambda i, j: (i, j)),
                   pl.BlockSpec((tm, d), lambda i, j: (i, 0))],
        out_shape=[jax.ShapeDtypeStruct((t, n), BF16), jax.ShapeDtypeStruct((t, d), BF16)],
        compiler_params=_params("parallel", "arbitrary"),
    )(x, nw, w)


def _zero_ext(v):
    z = jnp.zeros((SUBLANES, v.shape[1]), v.dtype)
    return jnp.concatenate([z, v, z], axis=0)


def _shifted(v_ext, offset, seq):
    if offset == 0:
        return v_ext[SUBLANES:SUBLANES + seq]
    return pltpu.roll(v_ext, (-offset) % (seq + 2 * SUBLANES), 0)[SUBLANES:SUBLANES + seq]


def _conv_taps(v, w_ref, taps, seq):
    pad = taps // 2
    v_ext = _zero_ext(v)
    acc = None
    for k in range(taps):
        term = _shifted(v_ext, k - pad, seq) * w_ref[k:k + 1, :]
        acc = term if acc is None else acc + term
    return acc


def _ssm_conv_fwd(proj, cw, cb, *, nb, seq):
    width = 512

    def body(x_ref, w_ref, b_ref, o_ref, g_ref):
        g = _conv_taps(x_ref[...].astype(F32), w_ref, SSM_TAPS, seq) + b_ref[...]
        o_ref[...] = g * _sigmoid(g)
        g_ref[...] = g.astype(BF16)

    def col(j):
        return jnp.where(j < 2, j + PXS // width, PB // width)

    return pl.pallas_call(
        body, name="ssm_conv_fwd", grid=(nb, CONV_CH // width),
        in_specs=[pl.BlockSpec((seq, width), lambda b, j: (b, col(j))),
                  pl.BlockSpec((SSM_TAPS, width), lambda b, j: (0, j)),
                  pl.BlockSpec((1, width), lambda b, j: (0, j))],
        out_specs=[pl.BlockSpec((seq, width), lambda b, j: (b, j))] * 2,
        out_shape=[jax.ShapeDtypeStruct((nb * seq, CONV_CH), F32), jax.ShapeDtypeStruct((nb * seq, CONV_CH), BF16)],
        compiler_params=_params("parallel", "parallel"),
    )(proj, cw, cb)


def _ffn_act_fwd(gu, cw, cb, *, nb, seq):
    width = 256
    nj = D_FF // width

    def body(g_ref, u_ref, w_ref, b_ref, o_ref, s_ref):
        g = _conv_taps(g_ref[...].astype(F32), w_ref, FFN_TAPS, seq) + b_ref[...]
        o_ref[...] = (g * _sigmoid(g) * u_ref[...].astype(F32)).astype(BF16)
        s_ref[...] = g.astype(BF16)

    return pl.pallas_call(
        body, name="ffn_act_fwd", grid=(nb, nj),
        in_specs=[pl.BlockSpec((seq, width), lambda b, j: (b, j)),
                  pl.BlockSpec((seq, width), lambda b, j: (b, j + nj)),
                  pl.BlockSpec((FFN_TAPS, width), lambda b, j: (0, j)),
                  pl.BlockSpec((1, width), lambda b, j: (0, j))],
        out_specs=[pl.BlockSpec((seq, width), lambda b, j: (b, j))] * 2,
        out_shape=[jax.ShapeDtypeStruct((nb * seq, D_FF), BF16)] * 2,
        compiler_params=_params("parallel", "parallel"),
    )(gu, gu, cw, cb)


def _scan_setup(d, dt_ref, dtb_ref, alog_ref, z_ref, zt_ref, *, lower_when_dir0, inclusive):
    is0 = d == 0
    dt_all = _softplus(dt_ref[...] + dtb_ref[...])
    adt_all = dt_all * (-jnp.exp(alog_ref[...]))
    li = lax.broadcasted_iota(jnp.int32, (CHUNK, CHUNK), 0)
    si = lax.broadcasted_iota(jnp.int32, (CHUNK, CHUNK), 1)
    lower = is0 if lower_when_dir0 else jnp.logical_not(is0)
    ahead = jnp.where(lower, li - si, si - li)
    mask = ahead >= 0
    tri = mask if inclusive else ahead > 0
    z_all = jnp.dot(tri.astype(F32), adt_all, precision=HIGHEST, preferred_element_type=F32)
    zt_all = z_all.T
    z_ref[...] = jnp.where(is0, z_all[:, 0:N_HEADS], z_all[:, N_HEADS:2 * N_HEADS])
    zt_ref[...] = jnp.where(is0, zt_all[0:N_HEADS, :], zt_all[N_HEADS:2 * N_HEADS, :])
    dt = jnp.where(is0, dt_all[:, 0:N_HEADS], dt_all[:, N_HEADS:2 * N_HEADS])
    adt = jnp.where(is0, adt_all[:, 0:N_HEADS], adt_all[:, N_HEADS:2 * N_HEADS])
    tot = jnp.sum(adt, axis=0, keepdims=True)
    return ahead, dt, tot, dt_all


def _chunk_index(nchunk, forward_when_dir0):
    def idx(d, b, c):
        fwd = (d == 0) if forward_when_dir0 else (d != 0)
        return b * nchunk + jnp.where(fwd, c, nchunk - 1 - c)
    return idx


def _ssd_fwd(xbc, dtraw, dtb, alog, *, nb, seq, gather=()):
    nchunk = seq // CHUNK
    t = nb * seq
    row = _chunk_index(nchunk, True)
    ng = len(gather)

    def body(*refs):
        xs_ref, bc_ref, dt_ref, dtb_ref, alog_ref = refs[:5]
        o_ref, hs_ref = refs[5 + ng:7 + ng]
        h_ref, z_ref, zt_ref, dts_ref = refs[7 + 2 * ng:11 + 2 * ng]
        d, c = pl.program_id(0), pl.program_id(2)
        if ng:
            step = (d * nb + pl.program_id(1)) * nchunk + c
            copies = _exchange_copies(refs[5:5 + ng], refs[7 + ng:7 + 2 * ng], *refs[11 + 2 * ng:], gather=True)
            pl.when(step == 0)(lambda: _exchange_start(copies))

        @pl.when(c == 0)
        def _():
            h_ref[...] = jnp.zeros_like(h_ref)

        ahead, dt, tot, _ = _scan_setup(d, dt_ref, dtb_ref, alog_ref, z_ref, zt_ref,
                                        lower_when_dir0=True, inclusive=True)
        mask = ahead >= 0
        dts_ref[...] = dt
        e_tot = jnp.exp(tot)
        for g in range(N_GROUPS):
            bg = bc_ref[:, g * N_STATE:(g + 1) * N_STATE]
            cg = bc_ref[:, BC_WIDTH + g * N_STATE:BC_WIDTH + (g + 1) * N_STATE]
            cb = _dot(cg.astype(BF16), bg.astype(BF16), NT)
            for r in range(HEADS_PER_GROUP):
                h = g * HEADS_PER_GROUP + r
                zc = jnp.broadcast_to(z_ref[:, h:h + 1], (CHUNK, CHUNK))
                decay = jnp.exp(jnp.where(mask, zc - zt_ref[h:h + 1, :], -jnp.inf))
                u = (xs_ref[:, h * HEAD_DIM:(h + 1) * HEAD_DIM] * dts_ref[:, h:h + 1]).astype(BF16)
                state = h_ref[h]
                hs_ref[0, 0, h] = state
                y = _dot((cb * decay).astype(BF16), u, NN) + _dot((cg * jnp.exp(zc)).astype(BF16), state.astype(BF16), NT)
                bw = (bg * jnp.exp(tot[:, h:h + 1] - zc)).astype(BF16)
                h_ref[h] = state * e_tot[:, h:h + 1] + _dot(u, bw, TN)
                o_ref[0, :, h * HEAD_DIM:(h + 1) * HEAD_DIM] = y
        if ng:
            pl.when(step == 2 * nb * nchunk - 1)(lambda: _exchange_wait(copies))

    any_spec = pl.BlockSpec(memory_space=pl.ANY)
    res = pl.pallas_call(
        body, name="ssd_fwd_gather" if ng else "ssd_fwd", grid=(2, nb, nchunk),
        in_specs=[pl.BlockSpec((CHUNK, SSM_WIDTH), lambda d, b, c: (row(d, b, c), 0)),
                  pl.BlockSpec((CHUNK, 2 * BC_WIDTH), lambda d, b, c: (row(d, b, c), SSM_WIDTH // (2 * BC_WIDTH))),
                  pl.BlockSpec((CHUNK, LANES), lambda d, b, c: (row(d, b, c), 0)),
                  pl.BlockSpec((1, LANES), lambda d, b, c: (0, 0)),
                  pl.BlockSpec((1, LANES), lambda d, b, c: (0, 0))] + [any_spec] * ng,
        out_specs=[pl.BlockSpec((1, CHUNK, SSM_WIDTH), lambda d, b, c: (d, row(d, b, c), 0)),
                   pl.BlockSpec((1, 1, N_HEADS, HEAD_DIM, N_STATE), lambda d, b, c: (d, row(d, b, c), 0, 0, 0))]
        + [any_spec] * ng,
        out_shape=[jax.ShapeDtypeStruct((2, t, SSM_WIDTH), F32),
                   jax.ShapeDtypeStruct((2, nb * nchunk, N_HEADS, HEAD_DIM, N_STATE), F32)]
        + _exchange_out_shapes(gather, gather=True),
        scratch_shapes=_SSD_SCRATCH + _exchange_sems(ng),
        compiler_params=_params("arbitrary", "arbitrary", "arbitrary"),
    )(xbc, xbc, dtraw, dtb, alog, *gather)
    return res[0], res[1], list(res[2:])


def _gate_norm_fwd(y2, xbc, proj, dvec, nw, *, tm):
    t = xbc.shape[0]
    half = SSM_WIDTH // N_GROUPS

    def body(y_ref, xs_ref, z_ref, d_ref, w_ref, o_ref):
        z = z_ref[...].astype(F32)
        p = (y_ref[0] + y_ref[1] + d_ref[...] * xs_ref[...]) * (z * _sigmoid(z))
        for g in range(N_GROUPS):
            pg = p[:, g * half:(g + 1) * half]
            r = lax.rsqrt(jnp.mean(pg * pg, axis=-1, keepdims=True) + NORM_EPS)
            o_ref[:, g * half:(g + 1) * half] = (pg * r * w_ref[:, g * half:(g + 1) * half]).astype(BF16)

    return pl.pallas_call(
        body, name="gate_norm_fwd", grid=(t // tm,),
        in_specs=[pl.BlockSpec((2, tm, SSM_WIDTH), lambda i: (0, i, 0)),
                  pl.BlockSpec((tm, SSM_WIDTH), lambda i: (i, 0)),
                  pl.BlockSpec((tm, SSM_WIDTH), lambda i: (i, PZ // SSM_WIDTH)),
                  pl.BlockSpec((1, SSM_WIDTH), lambda i: (0, 0)),
                  pl.BlockSpec((1, SSM_WIDTH), lambda i: (0, 0))],
        out_specs=pl.BlockSpec((tm, SSM_WIDTH), lambda i: (i, 0)),
        out_shape=jax.ShapeDtypeStruct((t, SSM_WIDTH), BF16),
        compiler_params=_params("parallel"),
    )(y2, xbc, proj, dvec, nw)


GROUP_ROWS = Q_PER_KV * CHUNK


def _band_mask(n, nblk):
    qi = lax.broadcasted_iota(jnp.int32, (GROUP_ROWS, KEY_SPAN), 0) & (CHUNK - 1)
    kj = lax.broadcasted_iota(jnp.int32, (GROUP_ROWS, KEY_SPAN), 1)
    rel = kj - CHUNK - qi
    kpos = (n - 1) * CHUNK + kj
    return (jnp.abs(rel) <= CHUNK) & (kpos >= 0) & (kpos < nblk * CHUNK)


def _per_head_column(ref, g):
    blk = lax.broadcasted_iota(jnp.int32, (GROUP_ROWS, 1), 0) // CHUNK
    col = jnp.zeros((GROUP_ROWS, 1), F32)
    for r in range(Q_PER_KV):
        h = g * Q_PER_KV + r
        col = jnp.where(blk == r, ref[:, h:h + 1], col)
    return col


def _stack_heads(ref, g, dtype):
    return jnp.concatenate([ref[:, (g * Q_PER_KV + r) * HEAD_DIM:(g * Q_PER_KV + r + 1) * HEAD_DIM].astype(dtype)
                            for r in range(Q_PER_KV)], axis=0)


def _kv_specs(nblk):
    kvb = PK // (2 * KV_WIDTH)

    def at(off):
        def idx(b, n):
            return (b * nblk + jnp.clip(n + off, 0, nblk - 1), kvb)
        return pl.BlockSpec((CHUNK, 2 * KV_WIDTH), idx)
    return [at(-1), at(0), at(1)]


def _attn_fwd(proj, bias, sink, *, nb, seq, gather=()):
    nblk = seq // CHUNK
    t = nb * seq
    scale = HEAD_DIM ** -0.5
    ng = len(gather)

    def body(*refs):
        q_ref, kp_ref, kc_ref, kn_ref, bias_ref, sink_ref = refs[:6]
        o_ref, lse_ref = refs[6 + ng:8 + ng]
        n = pl.program_id(1)
        if ng:
            step = pl.program_id(0) * nblk + n
            copies = _exchange_copies(refs[6:6 + ng], refs[8 + ng:8 + 2 * ng], *refs[8 + 2 * ng:], gather=True)
            pl.when(step == 0)(lambda: _exchange_start(copies))
        valid = _band_mask(n, nblk)
        lses = []
        for g in range(KV_HEADS):
            ks = slice(g * HEAD_DIM, (g + 1) * HEAD_DIM)
            vs = slice(KV_WIDTH + g * HEAD_DIM, KV_WIDTH + (g + 1) * HEAD_DIM)
            kcat = jnp.concatenate([kp_ref[:, ks], kc_ref[:, ks], kn_ref[:, ks]], axis=0).astype(BF16)
            vcat = jnp.concatenate([kp_ref[:, vs], kc_ref[:, vs], kn_ref[:, vs]], axis=0).astype(BF16)
            q = _stack_heads(q_ref, g, BF16)
            bias = bias_ref[g * Q_PER_KV:(g + 1) * Q_PER_KV].reshape(GROUP_ROWS, KEY_SPAN)
            s = jnp.where(valid, _dot(q, kcat, NT) * scale + bias, -jnp.inf)
            sk = _per_head_column(sink_ref, g)
            m = jnp.maximum(jnp.max(s, axis=-1, keepdims=True), sk)
            p = jnp.exp(s - m)
            denom = jnp.sum(p, axis=-1, keepdims=True) + jnp.exp(sk - m)
            o = _dot((p * (1.0 / denom)).astype(BF16), vcat, NN).astype(BF16)
            lse = m + jnp.log(denom)
            for r in range(Q_PER_KV):
                h = g * Q_PER_KV + r
                o_ref[:, h * HEAD_DIM:(h + 1) * HEAD_DIM] = o[r * CHUNK:(r + 1) * CHUNK]
                lses.append(lse[r * CHUNK:(r + 1) * CHUNK])
        lse_ref[...] = jnp.concatenate(lses, axis=1)
        if ng:
            pl.when(step == nb * nblk - 1)(lambda: _exchange_wait(copies))

    any_spec = pl.BlockSpec(memory_space=pl.ANY)
    res = pl.pallas_call(
        body, name="attn_fwd_gather" if ng else "attn_fwd", grid=(nb, nblk),
        in_specs=[pl.BlockSpec((CHUNK, D_MODEL), lambda b, n: (b * nblk + n, PQ // D_MODEL))] + _kv_specs(nblk) + [
            pl.BlockSpec((N_HEADS, CHUNK, KEY_SPAN), lambda b, n: (0, 0, 0)),
            pl.BlockSpec((1, LANES), lambda b, n: (0, 0))] + [any_spec] * ng,
        out_specs=[pl.BlockSpec((CHUNK, D_MODEL), lambda b, n: (b * nblk + n, 0)),
                   pl.BlockSpec((CHUNK, N_HEADS), lambda b, n: (b * nblk + n, 0))] + [any_spec] * ng,
        out_shape=[jax.ShapeDtypeStruct((t, D_MODEL), BF16), jax.ShapeDtypeStruct((t, N_HEADS), F32)]
        + _exchange_out_shapes(gather, gather=True),
        scratch_shapes=_exchange_sems(ng),
        compiler_params=_params("arbitrary", "arbitrary"),
    )(proj, proj, proj, proj, bias, sink, *gather)
    return res[0], res[1], list(res[2:])


def _loss_head(x, tgt, nw, *, tm):
    t, d = x.shape

    def body(x_ref, t_ref, w_ref, dx_ref, dw_ref, l_ref):
        @pl.when(pl.program_id(0) == 0)
        def _():
            dw_ref[...] = jnp.zeros_like(dw_ref)
            l_ref[...] = jnp.zeros_like(l_ref)

        xv = x_ref[...]
        w = w_ref[...]
        r = lax.rsqrt(jnp.mean(xv * xv, axis=-1, keepdims=True) + NORM_EPS)
        xh = xv * r
        err = xh * w - t_ref[...]
        l_ref[...] += jnp.sum(err * err) * (0.5 / d)
        dy = err * (1.0 / d)
        gw = dy * w
        dx_ref[...] = r * (gw - xh * jnp.mean(gw * xh, axis=-1, keepdims=True))
        dw_ref[...] += jnp.sum(dy * xh, axis=0, keepdims=True)

    return pl.pallas_call(
        body, name="loss_head", grid=(t // tm,),
        in_specs=[pl.BlockSpec((tm, d), lambda i: (i, 0)), pl.BlockSpec((tm, d), lambda i: (i, 0)),
                  pl.BlockSpec((1, d), lambda i: (0, 0))],
        out_specs=[pl.BlockSpec((tm, d), lambda i: (i, 0)), pl.BlockSpec((1, d), lambda i: (0, 0)),
                   pl.BlockSpec((1, LANES), lambda i: (0, 0))],
        out_shape=[jax.ShapeDtypeStruct((t, d), F32), jax.ShapeDtypeStruct((1, d), F32),
                   jax.ShapeDtypeStruct((1, LANES), F32)],
        compiler_params=_params("arbitrary"),
    )(x, tgt, nw)


def _to_proj_layout(w):
    pad = jnp.zeros(w.shape[:-1] + (PROJ_W - IN_COLS,), w.dtype)
    return jnp.concatenate([w[..., OZ:OXBC], w[..., OXBC:OXBC + SSM_WIDTH], w[..., OQ:OK_],
                            w[..., OXBC + SSM_WIDTH:ODT], w[..., OK_:IN_COLS], w[..., ODT:OQ], pad], axis=-1)


def _from_proj_layout(g):
    return jnp.concatenate([g[..., PZ:PZ + 2 * SSM_WIDTH], g[..., PB:PB + 2 * BC_WIDTH], g[..., PDT:PDT + 2 * N_HEADS],
                            g[..., PQ:PQ + D_MODEL], g[..., PK:PK + 2 * KV_WIDTH]], axis=-1)


def _pad_lanes(v):
    return jnp.pad(v.reshape(1, -1), ((0, 0), (0, LANES - v.size)))


def _layer_fwd(x, p, band_bias, *, nb, seq, own_shards, next_shards=()):
    w_main, w_dt = p["w_in"][:, :PDT], p["w_in"][:, PDT:]
    proj, h1 = _rms_matmul(x, p["n1"], w_main, name="in_proj", tm=MM_ROWS, tn=1024)
    dtraw = _matmul(h1, w_dt, "nn", name="in_proj_dt", tm=MM_ROWS, tn=LANES, tk=D_MODEL)
    xbc, gconv = _ssm_conv_fwd(proj, p["conv_w"], p["conv_b"], nb=nb, seq=seq)
    y2, states, (g_out, g_up, g_down) = _ssd_fwd(xbc, dtraw, p["dtb"], p["alog"], nb=nb, seq=seq, gather=own_shards)
    p = dict(p, w_out=_blocks_to_rows(g_out, 1)[0], w_up=_blocks_to_cols(g_up, 1)[0], w_down=_blocks_to_rows(g_down, 1)[0])
    y_ssm = _gate_norm_fwd(y2, xbc, proj, p["dvec"], p["ssm_nw"], tm=256)
    y_att, lse, gathered = _attn_fwd(proj, band_bias, p["sink"], nb=nb, seq=seq, gather=next_shards)
    x1 = _matmul(y_ssm, p["w_out"][:SSM_WIDTH], "nn", name="out_proj_ssm", tm=MM_ROWS, tn=1024, tk=1024, res=x)
    x1 = _matmul(y_att, p["w_out"][SSM_WIDTH:], "nn", name="out_proj_att", tm=MM_ROWS, tn=1024, tk=1024, res=x1)
    gu, h2 = _rms_matmul(x1, p["n2"], p["w_up"], name="up_proj", tm=MM_ROWS, tn=1408)
    act, fconv = _ffn_act_fwd(gu, p["ffn_cw"], p["ffn_cb"], nb=nb, seq=seq)
    x2 = _matmul(act, p["w_down"], "nn", name="down_proj", tm=MM_ROWS, tn=1024, tk=1408, res=x1)
    saved = dict(x=x, proj=proj, dtraw=dtraw, h1=h1, xbc=xbc, gconv=gconv, y2=y2, states=states, y_ssm=y_ssm, y_att=y_att,
                 lse=lse, x1=x1, gu=gu, fconv=fconv, h2=h2, act=act)
    return x2, p, saved, gathered


def _rms_bwd(x, dh, nw, dres, *, tm, name):
    t, d = x.shape

    def body(x_ref, dh_ref, w_ref, r_ref, dx_ref, dw_ref):
        @pl.when(pl.program_id(0) == 0)
        def _():
            dw_ref[...] = jnp.zeros_like(dw_ref)

        xv = x_ref[...]
        dh_v = dh_ref[...].astype(F32)
        r = lax.rsqrt(jnp.mean(xv * xv, axis=-1, keepdims=True) + NORM_EPS)
        xh = xv * r
        gw = dh_v * w_ref[...]
        dx_ref[...] = r_ref[...] + r * (gw - xh * jnp.mean(gw * xh, axis=-1, keepdims=True))
        dw_ref[...] += jnp.sum(dh_v * xh, axis=0, keepdims=True)

    row = pl.BlockSpec((tm, d), lambda i: (i, 0))
    vec = pl.BlockSpec((1, d), lambda i: (0, 0))
    return pl.pallas_call(
        body, name=name, grid=(t // tm,), in_specs=[row, row, vec, row], out_specs=[row, vec],
        out_shape=[jax.ShapeDtypeStruct((t, d), F32), jax.ShapeDtypeStruct((1, d), F32)],
        compiler_params=_params("arbitrary"),
    )(x, dh, nw, dres)


def _dsilu(g, sg):
    return sg * (1.0 + g * (1.0 - sg))


def _conv_taps_bwd(gpre, dg, w_ref, dwb_ref, taps, seq):
    pad = taps // 2
    dg_ext, gpre_ext = _zero_ext(dg), _zero_ext(gpre)
    dpre = None
    for k in range(taps):
        term = _shifted(dg_ext, pad - k, seq) * w_ref[k:k + 1, :]
        dpre = term if dpre is None else dpre + term
        dwb_ref[k:k + 1, :] += jnp.sum(dg * _shifted(gpre_ext, k - pad, seq), axis=0, keepdims=True)
    dwb_ref[SUBLANES - 1:SUBLANES, :] += jnp.sum(dg, axis=0, keepdims=True)
    return dpre


def _ffn_act_bwd(gu, gconv, dact, cw, *, nb, seq):
    width = 256
    nj = D_FF // width

    def body(g_ref, u_ref, s_ref, da_ref, w_ref, dg_ref, du_ref, dwb_ref):
        @pl.when(pl.program_id(1) == 0)
        def _():
            dwb_ref[...] = jnp.zeros_like(dwb_ref)

        g = s_ref[...].astype(F32)
        sg = _sigmoid(g)
        da = da_ref[...].astype(F32)
        du_ref[...] = (da * g * sg).astype(BF16)
        dgc = da * u_ref[...].astype(F32) * _dsilu(g, sg)
        dg_ref[...] = _conv_taps_bwd(g_ref[...].astype(F32), dgc, w_ref, dwb_ref, FFN_TAPS, seq).astype(BF16)

    blk = lambda off: pl.BlockSpec((seq, width), lambda j, b: (b, j + off))
    return pl.pallas_call(
        body, name="ffn_act_bwd", grid=(nj, nb),
        in_specs=[blk(0), blk(nj), blk(0), blk(0), pl.BlockSpec((FFN_TAPS, width), lambda j, b: (0, j))],
        out_specs=[blk(0), blk(0), pl.BlockSpec((SUBLANES, width), lambda j, b: (0, j))],
        out_shape=[jax.ShapeDtypeStruct((nb * seq, D_FF), BF16), jax.ShapeDtypeStruct((nb * seq, D_FF), BF16),
                   jax.ShapeDtypeStruct((SUBLANES, D_FF), F32)],
        compiler_params=_params("parallel", "arbitrary"),
    )(gu, gu, gconv, dact, cw)


def _ssm_conv_bwd(proj, gconv, pair, cw, *, nb, seq, name, width, proj_col, conv_col, ncol, extra=None, scale=None):
    has_extra = extra is not None

    def body(*refs):
        if has_extra:
            x_ref, g_ref, p_ref, w_ref, e_ref, s_ref, dx_ref, dwb_ref = refs
        else:
            x_ref, g_ref, p_ref, w_ref, dx_ref, dwb_ref = refs

        @pl.when(pl.program_id(1) == 0)
        def _():
            dwb_ref[...] = jnp.zeros_like(dwb_ref)

        g = g_ref[...].astype(F32)
        da = p_ref[0] + p_ref[1]
        if has_extra:
            da = da + e_ref[...] * s_ref[...]
        dx_ref[...] = _conv_taps_bwd(x_ref[...].astype(F32), da * _dsilu(g, _sigmoid(g)), w_ref, dwb_ref, SSM_TAPS, seq)

    in_specs = [pl.BlockSpec((seq, width), lambda j, b: (b, j + proj_col)),
                pl.BlockSpec((seq, width), lambda j, b: (b, j + conv_col)),
                pl.BlockSpec((2, seq, width), lambda j, b: (0, b, j)),
                pl.BlockSpec((SSM_TAPS, width), lambda j, b: (0, j + conv_col))]
    args = [proj, gconv, pair, cw]
    if has_extra:
        in_specs += [pl.BlockSpec((seq, width), lambda j, b: (b, j)), pl.BlockSpec((1, width), lambda j, b: (0, j))]
        args += [extra, scale]
    return pl.pallas_call(
        body, name=name, grid=(ncol, nb), in_specs=in_specs,
        out_specs=[pl.BlockSpec((seq, width), lambda j, b: (b, j)), pl.BlockSpec((SUBLANES, width), lambda j, b: (0, j))],
        out_shape=[jax.ShapeDtypeStruct((nb * seq, ncol * width), F32), jax.ShapeDtypeStruct((SUBLANES, ncol * width), F32)],
        compiler_params=_params("parallel", "arbitrary"),
    )(*args)


def _attn_bwd(proj, dmix, y_att, lse, bias, sink, dbias_in, *, nb, seq):
    nblk = seq // CHUNK
    t = nb * seq
    scale = HEAD_DIM ** -0.5

    def body(q_ref, kp_ref, kc_ref, kn_ref, do_ref, o_ref, lse_ref, bias_ref, sink_ref, dbin_ref,
             dq_ref, dkv_ref, dbias_ref, dsink_ref):
        b, n = pl.program_id(0), pl.program_id(1)

        @pl.when(n == 0)
        def _():
            dkv_ref[...] = jnp.zeros_like(dkv_ref)

        @pl.when((n == 0) & (b == 0))
        def _():
            dbias_ref[...] = dbin_ref[...]
            dsink_ref[...] = jnp.zeros_like(dsink_ref)

        valid = _band_mask(n, nblk)
        lane = lax.broadcasted_iota(jnp.int32, (1, LANES), 1)
        dsink = jnp.zeros((1, LANES), F32)
        rows = pl.ds(pl.multiple_of(n * CHUNK, CHUNK), KEY_SPAN)
        for g in range(KV_HEADS):
            ks = slice(g * HEAD_DIM, (g + 1) * HEAD_DIM)
            vs = slice(KV_WIDTH + g * HEAD_DIM, KV_WIDTH + (g + 1) * HEAD_DIM)
            kcat = jnp.concatenate([kp_ref[:, ks], kc_ref[:, ks], kn_ref[:, ks]], axis=0).astype(BF16)
            vcat = jnp.concatenate([kp_ref[:, vs], kc_ref[:, vs], kn_ref[:, vs]], axis=0).astype(BF16)
            q = _stack_heads(q_ref, g, BF16)
            do = _stack_heads(do_ref, g, F32)
            lse = jnp.concatenate([lse_ref[:, g * Q_PER_KV + r:g * Q_PER_KV + r + 1] for r in range(Q_PER_KV)], axis=0)
            bias = bias_ref[g * Q_PER_KV:(g + 1) * Q_PER_KV].reshape(GROUP_ROWS, KEY_SPAN)
            s = jnp.where(valid, _dot(q, kcat, NT) * scale + bias, -jnp.inf)
            p = jnp.exp(s - lse)
            delta = jnp.sum(do * _stack_heads(o_ref, g, F32), axis=-1, keepdims=True)
            do16 = do.astype(BF16)
            ds = p * (_dot(do16, vcat, NT) - delta)
            dbias_ref[g * Q_PER_KV:(g + 1) * Q_PER_KV] += ds.reshape(Q_PER_KV, CHUNK, KEY_SPAN)
            sink_part = jnp.exp(_per_head_column(sink_ref, g) - lse) * delta
            ds16 = (ds * scale).astype(BF16)
            dq = _dot(ds16, kcat, NN)
            for r in range(Q_PER_KV):
                h = g * Q_PER_KV + r
                dq_ref[:, h * HEAD_DIM:(h + 1) * HEAD_DIM] = dq[r * CHUNK:(r + 1) * CHUNK]
                dsink = dsink - jnp.where(lane == h, jnp.sum(sink_part[r * CHUNK:(r + 1) * CHUNK], axis=0, keepdims=True), 0.0)
            dkv_ref[0, rows, ks] += _dot(ds16, q, TN)
            dkv_ref[0, rows, vs] += _dot(p.astype(BF16), do16, TN)
        dsink_ref[...] += dsink

    blk = lambda cb: pl.BlockSpec((CHUNK, D_MODEL), lambda b, n: (b * nblk + n, cb))
    whole = pl.BlockSpec((N_HEADS, CHUNK, KEY_SPAN), lambda b, n: (0, 0, 0))
    vec = pl.BlockSpec((1, LANES), lambda b, n: (0, 0))
    return pl.pallas_call(
        body, name="attn_bwd", grid=(nb, nblk),
        in_specs=[blk(PQ // D_MODEL)] + _kv_specs(nblk) + [
            blk(1), blk(0), pl.BlockSpec((CHUNK, N_HEADS), lambda b, n: (b * nblk + n, 0)), whole, vec, whole],
        out_specs=[blk(0), pl.BlockSpec((1, seq + 2 * CHUNK, 2 * KV_WIDTH), lambda b, n: (b, 0, 0)), whole, vec],
        out_shape=[jax.ShapeDtypeStruct((t, D_MODEL), F32),
                   jax.ShapeDtypeStruct((nb, seq + 2 * CHUNK, 2 * KV_WIDTH), F32),
                   jax.ShapeDtypeStruct((N_HEADS, CHUNK, KEY_SPAN), F32),
                   jax.ShapeDtypeStruct((1, LANES), F32)],
        compiler_params=_params("arbitrary", "arbitrary"),
    )(proj, proj, proj, proj, dmix, y_att, lse, bias, sink, dbias_in)


def _gate_norm_bwd(y2, xbc, proj, dmix, dvec, nw, *, tm):
    t = xbc.shape[0]
    half = SSM_WIDTH // N_GROUPS

    def body(y_ref, xs_ref, z_ref, do_ref, d_ref, w_ref, dyv_ref, dz_ref, dd_ref, dw_ref):
        @pl.when(pl.program_id(0) == 0)
        def _():
            dd_ref[...] = jnp.zeros_like(dd_ref)
            dw_ref[...] = jnp.zeros_like(dw_ref)

        z = z_ref[...].astype(F32)
        xs = xs_ref[...]
        sg = _sigmoid(z)
        gz = z * sg
        yv = y_ref[0] + y_ref[1] + d_ref[...] * xs
        p = yv * gz
        do = do_ref[...]
        for g in range(N_GROUPS):
            cs = slice(g * half, (g + 1) * half)
            pg = p[:, cs]
            r = lax.rsqrt(jnp.mean(pg * pg, axis=-1, keepdims=True) + NORM_EPS)
            ph = pg * r
            gw = do[:, cs] * w_ref[:, cs]
            dp = r * (gw - ph * jnp.mean(gw * ph, axis=-1, keepdims=True))
            dyv = dp * gz[:, cs]
            dyv_ref[:, cs] = dyv
            dz_ref[:, cs] = dp * yv[:, cs] * _dsilu(z[:, cs], sg[:, cs])
            dw_ref[:, cs] += jnp.sum(do[:, cs] * ph, axis=0, keepdims=True)
            dd_ref[:, cs] += jnp.sum(dyv * xs[:, cs], axis=0, keepdims=True)

    row = lambda cb: pl.BlockSpec((tm, SSM_WIDTH), lambda i: (i, cb))
    vec = pl.BlockSpec((1, SSM_WIDTH), lambda i: (0, 0))
    return pl.pallas_call(
        body, name="gate_norm_bwd", grid=(t // tm,),
        in_specs=[pl.BlockSpec((2, tm, SSM_WIDTH), lambda i: (0, i, 0)), row(0), row(PZ // SSM_WIDTH), row(0), vec, vec],
        out_specs=[row(0), row(0), vec, vec],
        out_shape=[jax.ShapeDtypeStruct((t, SSM_WIDTH), F32), jax.ShapeDtypeStruct((t, SSM_WIDTH), F32),
                   jax.ShapeDtypeStruct((1, SSM_WIDTH), F32), jax.ShapeDtypeStruct((1, SSM_WIDTH), F32)],
        compiler_params=_params("arbitrary"),
    )(y2, xbc, proj, dmix, dvec, nw)


def _ssd_specs(nchunk, row):
    return [pl.BlockSpec((CHUNK, SSM_WIDTH), lambda d, b, c: (row(d, b, c), 0)),
            pl.BlockSpec((CHUNK, 2 * BC_WIDTH), lambda d, b, c: (row(d, b, c), SSM_WIDTH // (2 * BC_WIDTH))),
            pl.BlockSpec((CHUNK, LANES), lambda d, b, c: (row(d, b, c), 0)),
            pl.BlockSpec((1, LANES), lambda d, b, c: (0, 0)),
            pl.BlockSpec((1, LANES), lambda d, b, c: (0, 0)),
            pl.BlockSpec((CHUNK, SSM_WIDTH), lambda d, b, c: (row(d, b, c), 0))]


_SSD_SCRATCH = [pltpu.VMEM((N_HEADS, HEAD_DIM, N_STATE), F32),
                pltpu.VMEM((CHUNK, N_HEADS), F32), pltpu.VMEM((N_HEADS, CHUNK), F32),
                pltpu.VMEM((CHUNK, N_HEADS), F32)]


def _ssd_bwd(xbc, dtraw, dtb, alog, dyv, states, *, nb, seq, scatter=()):
    nchunk = seq // CHUNK
    t = nb * seq
    row = _chunk_index(nchunk, False)
    ns = len(scatter)

    def body(*refs):
        xs_ref, bc_ref, dt_ref, dtb_ref, alog_ref, dy_ref, hs_ref = refs[:7]
        dx_ref, db_ref, dc_ref, draw_ref, da_ref, dbias_ref = refs[7 + ns:13 + ns]
        h_ref, z_ref, zt_ref, dts_ref = refs[13 + 2 * ns:17 + 2 * ns]
        d, b, c = pl.program_id(0), pl.program_id(1), pl.program_id(2)
        if ns:
            step = (d * nb + b) * nchunk + c
            copies = _exchange_copies(refs[7:7 + ns], refs[13 + ns:13 + 2 * ns], *refs[17 + 2 * ns:], gather=False)
            pl.when(step == 0)(lambda: _exchange_start(copies))

        @pl.when(c == 0)
        def _():
            h_ref[...] = jnp.zeros_like(h_ref)

        @pl.when((c == 0) & (b == 0) & (d == 0))
        def _():
            da_ref[...] = jnp.zeros_like(da_ref)
            dbias_ref[...] = jnp.zeros_like(dbias_ref)

        ahead, dt, tot, dt_all = _scan_setup(d, dt_ref, dtb_ref, alog_ref, z_ref, zt_ref,
                                             lower_when_dir0=False, inclusive=False)
        mask = ahead >= 0
        dts_ref[...] = dt
        e_tot = jnp.exp(tot)
        lane = lax.broadcasted_iota(jnp.int32, (1, LANES), 1)
        ddt = jnp.zeros((CHUNK, LANES), F32)
        head_row = lax.broadcasted_iota(jnp.int32, (LANES, 1), 0)
        inner = jnp.zeros((CHUNK, LANES), F32)
        inner_t = jnp.zeros((LANES, CHUNK), F32)
        outer = jnp.zeros((CHUNK, LANES), F32)
        span = jnp.zeros((1, LANES), F32)
        for g in range(N_GROUPS):
            bg = bc_ref[:, g * N_STATE:(g + 1) * N_STATE]
            cg = bc_ref[:, BC_WIDTH + g * N_STATE:BC_WIDTH + (g + 1) * N_STATE]
            bg16 = bg.astype(BF16)
            cg16 = cg.astype(BF16)
            bc_t = _dot(bg16, cg16, NT)
            dbg = jnp.zeros((CHUNK, N_STATE), F32)
            dcg = jnp.zeros((CHUNK, N_STATE), F32)
            for r in range(HEADS_PER_GROUP):
                h = g * HEADS_PER_GROUP + r
                hs = slice(h * HEAD_DIM, (h + 1) * HEAD_DIM)
                here = lane == d * N_HEADS + h
                here_t = head_row == d * N_HEADS + h
                zc = jnp.broadcast_to(z_ref[:, h:h + 1], (CHUNK, CHUNK))
                decay = jnp.exp(jnp.where(mask, zc - zt_ref[h:h + 1, :], -jnp.inf))
                e_z = jnp.exp(zc)
                e_tz = jnp.exp(tot[:, h:h + 1] - zc)
                x_h = xs_ref[:, hs]
                dt_h = dts_ref[:, h:h + 1]
                u = (x_h * dt_h).astype(BF16)
                dy = dy_ref[:, hs].astype(BF16)
                state = h_ref[h]
                st16 = state.astype(BF16)
                fstate = hs_ref[0, 0, h]
                du = _dot((bc_t * decay).astype(BF16), dy, NN) + _dot((bg * e_z).astype(BF16), st16, NT)
                w2f = _dot(u, dy, NT) * decay
                w2 = w2f.astype(BF16)
                db_out = e_z * _dot(u, st16, NN)
                dc_out = e_tz * _dot(dy, fstate.astype(BF16), NN)
                dbg = dbg + _dot(w2, cg16, NN) + db_out
                dcg = dcg + _dot(w2, bg16, TN) + dc_out
                pairs = w2f * bc_t
                col_in = jnp.sum(pairs, axis=-1, keepdims=True)
                row_in = jnp.sum(pairs, axis=0, keepdims=True)
                row_out = jnp.sum(dc_out * cg, axis=-1, keepdims=True)
                col_out = jnp.sum(db_out * bg, axis=-1, keepdims=True)
                inner = inner + jnp.where(here, row_out - col_in, 0.0)
                inner_t = inner_t + jnp.where(here_t, row_in, 0.0)
                outer = outer + jnp.where(here, col_out, 0.0)
                span = span + jnp.where(here, e_tot[:, h:h + 1] * jnp.sum(fstate * state), 0.0)
                ddt = ddt + jnp.where(here, jnp.sum(du * x_h, axis=-1, keepdims=True), 0.0)
                dx_ref[0, :, hs] = du * dt_h
                h_ref[h] = state * e_tot[:, h:h + 1] + _dot(dy, (cg * e_tz).astype(BF16), TN)
            db_ref[0, :, g * N_STATE:(g + 1) * N_STATE] = dbg
            dc_ref[0, :, g * N_STATE:(g + 1) * N_STATE] = dcg
        tri = mask.astype(F32)
        dadt = (jnp.dot(tri, inner + inner_t.T, precision=HIGHEST, preferred_element_type=F32)
                + jnp.dot(1.0 - tri, outer, precision=HIGHEST, preferred_element_type=F32) + span)
        a = -jnp.exp(alog_ref[...])
        draw = (ddt + a * dadt) * _sigmoid(dt_ref[...] + dtb_ref[...])
        draw_ref[0] = draw
        da_ref[...] += jnp.sum(dt_all * dadt, axis=0, keepdims=True) * a
        dbias_ref[...] += jnp.sum(draw, axis=0, keepdims=True)
        if ns:
            pl.when(step == 2 * nb * nchunk - 1)(lambda: _exchange_wait(copies))

    out_row = lambda w: pl.BlockSpec((1, CHUNK, w), lambda d, b, c: (d, row(d, b, c), 0))
    vec = pl.BlockSpec((1, LANES), lambda d, b, c: (0, 0))
    any_spec = pl.BlockSpec(memory_space=pl.ANY)
    res = pl.pallas_call(
        body, name="ssd_bwd_scatter" if ns else "ssd_bwd", grid=(2, nb, nchunk),
        in_specs=_ssd_specs(nchunk, row) + [
            pl.BlockSpec((1, 1, N_HEADS, HEAD_DIM, N_STATE), lambda d, b, c: (d, row(d, b, c), 0, 0, 0))]
        + [any_spec] * ns,
        out_specs=[out_row(SSM_WIDTH), out_row(BC_WIDTH), out_row(BC_WIDTH), out_row(LANES), vec, vec] + [any_spec] * ns,
        out_shape=[jax.ShapeDtypeStruct((2, t, SSM_WIDTH), F32), jax.ShapeDtypeStruct((2, t, BC_WIDTH), F32),
                   jax.ShapeDtypeStruct((2, t, BC_WIDTH), F32),
                   jax.ShapeDtypeStruct((2, t, LANES), F32), jax.ShapeDtypeStruct((1, LANES), F32),
                   jax.ShapeDtypeStruct((1, LANES), F32)] + _exchange_out_shapes(scatter, gather=False),
        scratch_shapes=_SSD_SCRATCH + _exchange_sems(ns),
        compiler_params=_params("arbitrary", "arbitrary", "arbitrary"),
    )(xbc, xbc, dtraw, dtb, alog, dyv, states, *scatter)
    return tuple(res[:6]) + (list(res[6:]),)


def _layer_bwd(dx2, p, s, band_bias, dbias_in, *, nb, seq, pending=()):
    t = nb * seq
    x, proj, xbc, x1 = s["x"], s["proj"], s["xbc"], s["x1"]
    dact = _matmul(dx2, p["w_down"], "nt", name="down_proj_dx", tm=MM_ROWS, tn=1408, tk=1024, out_dtype=BF16)
    g_w_down = _matmul(s["act"], dx2, "tn", name="down_proj_dw", tm=1408, tn=1024, tk=1024)
    dg, du, dwb_ffn = _ffn_act_bwd(s["gu"], s["fconv"], dact, p["ffn_cw"], nb=nb, seq=seq)
    dh2 = _matmul(dg, p["w_up"][:, :D_FF], "nt", name="up_proj_dx_g", tm=MM_ROWS, tn=1024, tk=1408)
    dh2 = _matmul(du, p["w_up"][:, D_FF:], "nt", name="up_proj_dx_u", tm=MM_ROWS, tn=1024, tk=1408, res=dh2)
    g_w_up = jnp.concatenate([_matmul(s["h2"], dg, "tn", name="up_proj_dw_g", tm=1024, tn=1408, tk=1024),
                              _matmul(s["h2"], du, "tn", name="up_proj_dw_u", tm=1024, tn=1408, tk=1024)], axis=1)
    dx1, g_n2 = _rms_bwd(x1, dh2, p["n2"], dx2, tm=512, name="norm2_bwd")
    dmix = _matmul(dx1, p["w_out"], "nt", name="out_proj_dx", tm=MM_ROWS, tn=1024, tk=1024)
    g_w_out = jnp.concatenate([_matmul(s["y_ssm"], dx1, "tn", name="out_proj_dw_ssm", tm=1024, tn=1024, tk=1024),
                               _matmul(s["y_att"], dx1, "tn", name="out_proj_dw_att", tm=1024, tn=1024, tk=1024)], axis=0)
    dq, dkv, dbias, dsink = _attn_bwd(proj, dmix, s["y_att"], s["lse"], band_bias, p["sink"], dbias_in, nb=nb, seq=seq)
    dkv = dkv[:, CHUNK:CHUNK + seq, :].reshape(t, 2 * KV_WIDTH)
    dyv, dz, g_dvec, g_ssm_nw = _gate_norm_bwd(s["y2"], xbc, proj, dmix, p["dvec"], p["ssm_nw"], tm=256)
    own = [_rows_to_blocks(g_w_down[None], 1).astype(BF16), _cols_to_blocks(g_w_up[None], 1).astype(BF16),
           _rows_to_blocks(g_w_out[None], 1).astype(BF16)]
    dxs2, db2, dc2, draw2, g_alog, g_dtb, exchanged = _ssd_bwd(xbc, s["dtraw"], p["dtb"], p["alog"], dyv, s["states"],
                                                               nb=nb, seq=seq, scatter=own + list(pending))
    ddt_raw = draw2[0] + draw2[1]
    conv = dict(nb=nb, seq=seq)
    dxs_pre, dwb_xs = _ssm_conv_bwd(proj, s["gconv"], dxs2, p["conv_w"], name="ssm_conv_bwd_x", width=256,
                                    proj_col=PXS // 256, conv_col=0, ncol=4, extra=dyv, scale=p["dvec"], **conv)
    db_pre, dwb_b = _ssm_conv_bwd(proj, s["gconv"], db2, p["conv_w"], name="ssm_conv_bwd_b", width=256,
                                  proj_col=PB // 256, conv_col=SSM_WIDTH // 256, ncol=1, **conv)
    dc_pre, dwb_c = _ssm_conv_bwd(proj, s["gconv"], dc2, p["conv_w"], name="ssm_conv_bwd_c", width=256,
                                  proj_col=PC // 256, conv_col=(SSM_WIDTH + BC_WIDTH) // 256, ncol=1, **conv)
    dwb_ssm = jnp.concatenate([dwb_xs, dwb_b, dwb_c], axis=1)
    dproj = jnp.concatenate([dz, dxs_pre, dq, db_pre, dc_pre, dkv, ddt_raw], axis=1).astype(BF16)
    dh1 = _matmul(dproj, p["w_in"], "nt", name="in_proj_dx", tm=MM_ROWS, tn=1024, tk=1408)
    g_w_in = _matmul(s["h1"], dproj, "tn", name="in_proj_dw", tm=1024, tn=1408, tk=1024)
    dx, g_n1 = _rms_bwd(x, dh1, p["n1"], dx1, tm=512, name="norm1_bwd")
    grads = dict(n1=g_n1, w_in=g_w_in, conv_w=dwb_ssm[:SSM_TAPS], conv_b=dwb_ssm[SUBLANES - 1], dtb=g_dtb, alog=g_alog,
                 dvec=g_dvec, ssm_nw=g_ssm_nw, sink=dsink, w_out=g_w_out, n2=g_n2, w_up=g_w_up,
                 ffn_cw=dwb_ffn[:FFN_TAPS], ffn_cb=dwb_ffn[SUBLANES - 1], w_down=g_w_down)
    return dx, dbias, grads, exchanged


def _band_bias(rel_bias, bucket):
    def body(rb_ref, b_ref, o_ref):
        o_ref[...] = jnp.zeros_like(o_ref)

        def per_bucket(k, carry):
            hit = b_ref[...] == k
            for h in range(N_HEADS):
                o_ref[h] = jnp.where(hit, rb_ref[k, h], o_ref[h])
            return carry

        lax.fori_loop(0, REL_BUCKETS, per_bucket, 0)

    return pl.pallas_call(
        body, name="band_bias", out_shape=jax.ShapeDtypeStruct((N_HEADS, CHUNK, KEY_SPAN), F32),
        in_specs=[pl.BlockSpec(memory_space=pltpu.SMEM), pl.BlockSpec(memory_space=pltpu.VMEM)],
        out_specs=pl.BlockSpec(memory_space=pltpu.VMEM),
    )(rel_bias, bucket)


def _rel_bias_grad(dbias, bucket):
    def body(d_ref, b_ref, o_ref):
        o_ref[...] = jnp.zeros_like(o_ref)
        lane = lax.broadcasted_iota(jnp.int32, (1, LANES), 1)

        def per_bucket(k, carry):
            hit = b_ref[...] == k
            for h in range(N_HEADS):
                part = jnp.sum(jnp.where(hit, d_ref[h], 0.0), axis=1, keepdims=True)
                o_ref[h:h + 1, :] += jnp.where(lane == k, jnp.sum(part, axis=0, keepdims=True), 0.0)
            return carry

        lax.fori_loop(0, REL_BUCKETS, per_bucket, 0)

    return pl.pallas_call(
        body, name="rel_bias_grad", out_shape=jax.ShapeDtypeStruct((N_HEADS, LANES), F32),
        compiler_params=pltpu.CompilerParams(vmem_limit_bytes=VMEM_LIMIT_BYTES),
    )(dbias, bucket)


N_PEER = N_DEV - 1


def _exchange_copies(ins, outs, send_sems, recv_sems, local_sems, *, gather):
    x, y, c = lax.axis_index("x"), lax.axis_index("y"), lax.axis_index("c")
    me = 4 * x + 2 * y + c
    peers = []
    for k in range(1, N_DEV):
        px, py, pc = x ^ ((k >> 2) & 1), y ^ ((k >> 1) & 1), c ^ (k & 1)
        peers.append(((px, py, pc), 4 * px + 2 * py + pc))
    local, sends, recvs = [], [], []
    for i in range(len(ins)):
        mine = ins[i] if gather else ins[i].at[me]
        local.append(pltpu.make_async_copy(mine, outs[i].at[me], local_sems.at[i]))
        for k, (pid, pslot) in enumerate(peers):
            src = ins[i] if gather else ins[i].at[pslot]
            sem = i * N_PEER + k
            sends.append(pltpu.make_async_remote_copy(
                src_ref=src, dst_ref=outs[i].at[me], send_sem=send_sems.at[sem], recv_sem=recv_sems.at[sem],
                device_id=pid, device_id_type=pl.DeviceIdType.MESH))
            recvs.append(pltpu.make_async_remote_copy(
                src_ref=src, dst_ref=outs[i].at[pslot], send_sem=send_sems.at[sem], recv_sem=recv_sems.at[sem],
                device_id=pid, device_id_type=pl.DeviceIdType.MESH))
    return local, sends, recvs


def _exchange_start(copies):
    local, sends, _ = copies
    for cp in local + sends:
        cp.start()


def _exchange_wait(copies):
    local, sends, recvs = copies
    for cp in recvs:
        cp.wait_recv()
    for cp in sends:
        cp.wait_send()
    for cp in local:
        cp.wait()


def _exchange_out_shapes(arrs, *, gather):
    return [jax.ShapeDtypeStruct((N_DEV,) + (a.shape if gather else a.shape[1:]), a.dtype) for a in arrs]


def _exchange_sems(n):
    if not n:
        return []
    return [pltpu.SemaphoreType.DMA((n * N_PEER,)), pltpu.SemaphoreType.DMA((n * N_PEER,)), pltpu.SemaphoreType.DMA((n,))]


def _exchange(arrs, *, gather, name):
    n = len(arrs)

    def body(*refs):
        copies = _exchange_copies(refs[:n], refs[n:2 * n], *refs[2 * n:], gather=gather)
        _exchange_start(copies)
        _exchange_wait(copies)

    any_spec = pl.BlockSpec(memory_space=pl.ANY)
    return pl.pallas_call(
        body, name=name, in_specs=[any_spec] * n, out_specs=[any_spec] * n,
        out_shape=_exchange_out_shapes(arrs, gather=gather), scratch_shapes=_exchange_sems(n),
        compiler_params=pltpu.CompilerParams(has_side_effects=True),
    )(*arrs)


def _adamw(parts, w, m, v, *, name, tr):
    r, c = w.shape
    nparts = len(parts)
    rows = r // nparts
    assert rows * nparts == r and rows % tr == 0 and all(p.shape == (N_DEV, rows, c) for p in parts)
    per = rows // tr
    c1 = 1.0 - ADAM_B1 ** ADAM_STEP
    c2 = 1.0 - ADAM_B2 ** ADAM_STEP

    def body(*refs):
        p_refs = refs[:nparts]
        w_ref, m_ref, v_ref, g_ref, d_ref, nm_ref, nv_ref = refs[nparts:]
        which = pl.program_id(0) // per
        for k, p_ref in enumerate(p_refs):
            @pl.when(which == k)
            def _(p_ref=p_ref):
                acc = p_ref[0].astype(F32)
                for j in range(1, N_DEV):
                    acc = acc + p_ref[j].astype(F32)
                g_ref[...] = acc

        g = g_ref[...]
        nm = ADAM_B1 * m_ref[...] + (1.0 - ADAM_B1) * g
        nv = ADAM_B2 * v_ref[...] + (1.0 - ADAM_B2) * (g * g)
        nm_ref[...] = nm
        nv_ref[...] = nv
        d_ref[...] = -ADAM_LR * ((nm / c1) / (jnp.sqrt(nv / c2) + ADAM_EPS) + ADAM_WD * w_ref[...])

    def part_spec(k):
        return pl.BlockSpec((N_DEV, tr, c), lambda i: (0, jnp.clip(i - k * per, 0, per - 1), 0))

    blk = pl.BlockSpec((tr, c), lambda i: (i, 0))
    return pl.pallas_call(
        body, name=name, grid=(r // tr,),
        in_specs=[part_spec(k) for k in range(nparts)] + [blk, blk, blk],
        out_specs=[blk] * 4, out_shape=[jax.ShapeDtypeStruct((r, c), F32)] * 4,
        compiler_params=_params("arbitrary"),
    )(*parts, w, m, v)


def _t5_bucket(rel):
    half = REL_BUCKETS // 2
    max_exact = half // 2
    ret = jnp.where(rel > 0, half, 0)
    n = jnp.abs(rel)
    nf = jnp.maximum(n, 1).astype(F32)
    large = max_exact + (jnp.log(nf / max_exact) / math.log(CHUNK / max_exact) * (half - max_exact)).astype(jnp.int32)
    large = jnp.minimum(large, half - 1)
    return ret + jnp.where(n < max_exact, n, large)


def _split16(w):
    hi = w.astype(BF16)
    return hi, (w - hi.astype(F32)).astype(BF16)


def _cols_to_blocks(g, depth):
    _, r, c8 = g.shape
    return g.reshape(depth, r, N_DEV, c8 // N_DEV).transpose(2, 0, 1, 3).reshape(N_DEV, depth * r, c8 // N_DEV)


def _rows_to_blocks(g, depth):
    _, r8, c = g.shape
    return g.reshape(depth, N_DEV, r8 // N_DEV, c).transpose(1, 0, 2, 3).reshape(N_DEV, depth * r8 // N_DEV, c)


def _blocks_to_cols(a, depth):
    _, dr, c = a.shape
    r = dr // depth
    return a.reshape(N_DEV, depth, r, c).transpose(1, 2, 0, 3).reshape(depth, r, N_DEV * c)


def _blocks_to_rows(a, depth):
    _, dr, c = a.shape
    r = dr // depth
    return a.reshape(N_DEV, depth, r, c).transpose(1, 0, 2, 3).reshape(depth, N_DEV * r, c)


_SMALL = ("rel_bias", "norm1_w", "conv_b", "dt_bias", "a_log", "d_skip", "ssm_norm_w", "attn_sink", "norm2_w",
          "ffn_conv_b", "final_norm_w")
_SHARDED = ("w_in", "conv_w", "w_out", "w_up", "ffn_conv_w", "w_down")
_ORDER = ("rel_bias", "norm1_w", "w_in", "conv_w", "conv_b", "dt_bias", "a_log", "d_skip", "ssm_norm_w", "attn_sink",
          "w_out", "norm2_w", "w_up", "ffn_conv_w", "ffn_conv_b", "w_down", "final_norm_w")


def _pack_small(d):
    flat = jnp.concatenate([d[k].reshape(-1).astype(F32) for k in _SMALL])
    rows = -(-flat.size // (LANES * SUBLANES)) * SUBLANES
    return jnp.pad(flat, (0, rows * LANES - flat.size)).reshape(rows, LANES)


def _unpack_small(packed, like):
    flat = packed.reshape(-1)
    out, off = {}, 0
    for k in _SMALL:
        out[k] = flat[off:off + like[k].size].reshape(like[k].shape)
        off += like[k].size
    return out


def kernel(x, rel_bias, norm1_w, w_in, conv_w, conv_b, dt_bias, a_log, d_skip, ssm_norm_w, attn_sink, w_out, norm2_w, w_up, ffn_conv_w, ffn_conv_b, w_down, final_norm_w, loss_target, m_rel_bias, m_norm1_w, m_w_in, m_conv_w, m_conv_b, m_dt_bias, m_a_log, m_d_skip, m_ssm_norm_w, m_attn_sink, m_w_out, m_norm2_w, m_w_up, m_ffn_conv_w, m_ffn_conv_b, m_w_down, m_final_norm_w, v_rel_bias, v_norm1_w, v_w_in, v_conv_w, v_conv_b, v_dt_bias, v_a_log, v_d_skip, v_ssm_norm_w, v_attn_sink, v_w_out, v_norm2_w, v_w_up, v_ffn_conv_w, v_ffn_conv_b, v_w_down, v_final_norm_w):
    w = dict(rel_bias=rel_bias, norm1_w=norm1_w, w_in=w_in, conv_w=conv_w, conv_b=conv_b, dt_bias=dt_bias, a_log=a_log,
             d_skip=d_skip, ssm_norm_w=ssm_norm_w, attn_sink=attn_sink, w_out=w_out, norm2_w=norm2_w, w_up=w_up,
             ffn_conv_w=ffn_conv_w, ffn_conv_b=ffn_conv_b, w_down=w_down, final_norm_w=final_norm_w)
    m = dict(rel_bias=m_rel_bias, norm1_w=m_norm1_w, w_in=m_w_in, conv_w=m_conv_w, conv_b=m_conv_b, dt_bias=m_dt_bias,
             a_log=m_a_log, d_skip=m_d_skip, ssm_norm_w=m_ssm_norm_w, attn_sink=m_attn_sink, w_out=m_w_out,
             norm2_w=m_norm2_w, w_up=m_w_up, ffn_conv_w=m_ffn_conv_w, ffn_conv_b=m_ffn_conv_b, w_down=m_w_down,
             final_norm_w=m_final_norm_w)
    v = dict(rel_bias=v_rel_bias, norm1_w=v_norm1_w, w_in=v_w_in, conv_w=v_conv_w, conv_b=v_conv_b, dt_bias=v_dt_bias,
             a_log=v_a_log, d_skip=v_d_skip, ssm_norm_w=v_ssm_norm_w, attn_sink=v_attn_sink, w_out=v_w_out,
             norm2_w=v_norm2_w, w_up=v_w_up, ffn_conv_w=v_ffn_conv_w, ffn_conv_b=v_ffn_conv_b, w_down=v_w_down,
             final_norm_w=v_final_norm_w)
    nb, seq, _ = x.shape
    t = nb * seq
    depth = w_in.shape[0]

    flat2 = lambda a: a.reshape(-1, a.shape[-1])
    own_shards = lambda i: [w_out[i].astype(BF16), w_up[i].astype(BF16), w_down[i].astype(BF16)]
    cw_hi, cw_lo = _split16(flat2(conv_w))
    fw_hi, fw_lo = _split16(flat2(ffn_conv_w))
    g_in, g_cwh, g_cwl, g_fwh, g_fwl = _exchange([w_in[0].astype(BF16), cw_hi, cw_lo, fw_hi, fw_lo], gather=True,
                                                 name="gather_weights")
    full_conv_w = _blocks_to_cols(g_cwh.astype(F32) + g_cwl.astype(F32), depth)
    full_ffn_cw = _blocks_to_cols(g_fwh.astype(F32) + g_fwl.astype(F32), depth)

    rel = jnp.arange(KEY_SPAN)[None, :] - CHUNK - jnp.arange(CHUNK)[:, None]
    bucket = _t5_bucket(rel)
    band_bias = _band_bias(rel_bias, bucket)

    def layer_params(i, g_in):
        return dict(n1=norm1_w[i][None], w_in=_to_proj_layout(_blocks_to_cols(g_in, 1)[0]), conv_w=full_conv_w[i],
                    conv_b=conv_b[i][None], dtb=_pad_lanes(dt_bias[i].reshape(-1)), alog=_pad_lanes(a_log[i].reshape(-1)),
                    dvec=jnp.repeat(d_skip[i], HEAD_DIM)[None], ssm_nw=ssm_norm_w[i][None], sink=_pad_lanes(attn_sink[i]),
                    n2=norm2_w[i][None], ffn_cw=full_ffn_cw[i], ffn_cb=ffn_conv_b[i][None])

    h = x.reshape(t, D_MODEL)
    params, saved = [None] * depth, [None] * depth
    for i in range(depth):
        h, params[i], saved[i], nxt = _layer_fwd(h, layer_params(i, g_in), band_bias, nb=nb, seq=seq, own_shards=own_shards(i),
                                                 next_shards=[w_in[i + 1].astype(BF16)] if i + 1 < depth else ())
        if nxt:
            (g_in,) = nxt
    dh, g_final, loss_part = _loss_head(h, loss_target.reshape(t, D_MODEL), final_norm_w[None], tm=512)
    loss = lax.psum(loss_part[0, 0], ("x", "y", "c"))

    w_in_blocks = lambda g: _cols_to_blocks(_from_proj_layout(g["w_in"])[None], 1).astype(BF16)
    dbias = jnp.zeros((N_HEADS, CHUNK, KEY_SPAN), F32)
    grads, pending = [None] * depth, ()
    parts = dict(w_in=[None] * depth, w_out=[None] * depth, w_up=[None] * depth, w_down=[None] * depth)
    for i in reversed(range(depth)):
        dh, dbias, grads[i], arrived = _layer_bwd(dh, params[i], saved[i], band_bias, dbias, nb=nb, seq=seq, pending=pending)
        parts["w_down"][i], parts["w_up"][i], parts["w_out"][i] = arrived[:3]
        if pending:
            parts["w_in"][i + 1] = arrived[3]
        pending = [w_in_blocks(grads[i])]
    grad_x = dh.reshape(nb, seq, D_MODEL)
    stack = lambda k: jnp.stack([g[k] for g in grads])
    last = _exchange(pending + [_cols_to_blocks(stack("conv_w"), depth).astype(BF16),
                                _cols_to_blocks(stack("ffn_cw"), depth).astype(BF16)], gather=False, name="scatter_grads")
    parts["w_in"][0] = last[0]

    out = {}
    for k in ("w_in", "w_out", "w_up", "w_down"):
        rows = parts[k][0].shape[1]
        tr = max(d for d in range(16, 129, 16) if rows % d == 0)
        res = _adamw(parts[k], flat2(w[k]), flat2(m[k]), flat2(v[k]), name="adamw_" + k, tr=tr)
        out[k] = [a.reshape(w[k].shape) for a in res]
    for k, p8 in zip(("conv_w", "ffn_conv_w"), last[1:]):
        res = _adamw([p8], flat2(w[k]), flat2(m[k]), flat2(v[k]), name="adamw_" + k, tr=p8.shape[1])
        out[k] = [a.reshape(w[k].shape) for a in res]

    small = dict(rel_bias=_rel_bias_grad(dbias, bucket)[:, :REL_BUCKETS].T, norm1_w=stack("n1"), conv_b=stack("conv_b"),
                 dt_bias=stack("dtb")[:, 0, :2 * N_HEADS], a_log=stack("alog")[:, 0, :2 * N_HEADS],
                 d_skip=stack("dvec").reshape(depth, N_HEADS, HEAD_DIM).sum(-1), ssm_norm_w=stack("ssm_nw"),
                 attn_sink=stack("sink")[:, 0, :N_HEADS], norm2_w=stack("n2"), ffn_conv_b=stack("ffn_cb"),
                 final_norm_w=g_final)
    (small_parts,) = _exchange([_pack_small(small)], gather=True, name="gather_small_grads")
    res = _adamw([small_parts], _pack_small(w), _pack_small(m), _pack_small(v), name="adamw_small", tr=small_parts.shape[1])
    unpacked = [_unpack_small(a, w) for a in res]
    for k in _SMALL:
        out[k] = [u[k] for u in unpacked]

    return (loss, grad_x, *[out[k][0] for k in _ORDER], *[out[k][1] for k in _ORDER],
            *[out[k][2] for k in _ORDER], *[out[k][3] for k in _ORDER])
```

```python
import math

import numpy as np
import jax
import jax.numpy as jnp
from jax import lax
from jax.experimental import pallas as pl
from jax.experimental.pallas import tpu as pltpu

F32, BF16 = jnp.float32, jnp.bfloat16
HIGHEST = lax.Precision.HIGHEST

D_MODEL = 1024
HEAD_DIM = 64
N_HEADS = 16
N_GROUPS = 2
HEADS_PER_GROUP = N_HEADS // N_GROUPS
N_STATE = 128
SSM_WIDTH = 1024
BC_WIDTH = 256
CONV_CH = SSM_WIDTH + 2 * BC_WIDTH
SSM_TAPS = 7
CHUNK = 128
KV_HEADS = 4
KV_WIDTH = 256
Q_PER_KV = N_HEADS // KV_HEADS
KEY_SPAN = 3 * CHUNK
REL_BUCKETS = 32
D_FF = 2816
FFN_TAPS = 3
IN_COLS = 4128
NORM_EPS = 1e-6
N_DEV = 8

LANES = 128
SUBLANES = 8
VMEM_LIMIT_BYTES = 56 * 1024 * 1024
MM_ROWS = 1024

PZ, PXS, PQ, PB, PC, PK, PV, PDT, PROJ_W = 0, 1024, 2048, 3072, 3328, 3584, 3840, 4096, 4224
OZ, OXBC, ODT, OQ, OK_, OV = 0, 1024, 2560, 2592, 3616, 3872

ADAM_LR, ADAM_B1, ADAM_B2, ADAM_EPS, ADAM_WD, ADAM_STEP = 0.001, 0.9, 0.999, 1e-08, 0.01, 10


def _params(*sem):
    return pltpu.CompilerParams(dimension_semantics=sem, vmem_limit_bytes=VMEM_LIMIT_BYTES)


def _sigmoid(x):
    return 1.0 / (1.0 + jnp.exp(-x))


def _softplus(x):
    return jnp.maximum(x, 0.0) + jnp.log(1.0 + jnp.exp(-jnp.abs(x)))


def _dot(a, b, dims):
    return lax.dot_general(a, b, (dims, ((), ())), preferred_element_type=F32)


NN = ((1,), (0,))
NT = ((1,), (1,))
TN = ((0,), (0,))


def _matmul(a, b, mode, *, name, tm, tn, tk, res=None, out_dtype=F32, precision=None):
    if mode == "nn":
        (m, k), (k2, n) = a.shape, b.shape
        a_spec = pl.BlockSpec((tm, tk), lambda i, j, kk: (i, kk))
        b_spec = pl.BlockSpec((tk, tn), lambda i, j, kk: (kk, j))
        dims = NN
    elif mode == "nt":
        (m, k), (n, k2) = a.shape, b.shape
        a_spec = pl.BlockSpec((tm, tk), lambda i, j, kk: (i, kk))
        b_spec = pl.BlockSpec((tn, tk), lambda i, j, kk: (j, kk))
        dims = NT
    else:
        (k, m), (k2, n) = a.shape, b.shape
        a_spec = pl.BlockSpec((tk, tm), lambda i, j, kk: (kk, i))
        b_spec = pl.BlockSpec((tk, tn), lambda i, j, kk: (kk, j))
        dims = TN
    assert k == k2 and m % tm == 0 and n % tn == 0 and k % tk == 0, (name, a.shape, b.shape, tm, tn, tk)
    nk = k // tk
    has_res = res is not None

    def body(*refs):
        if has_res:
            a_ref, b_ref, r_ref, o_ref, acc = refs
        else:
            a_ref, b_ref, o_ref, acc = refs
        kk = pl.program_id(2)

        @pl.when(kk == 0)
        def _():
            acc[...] = jnp.zeros_like(acc)

        if precision is None:
            part = _dot(a_ref[...].astype(BF16), b_ref[...].astype(BF16), dims)
        else:
            part = lax.dot_general(a_ref[...], b_ref[...], (dims, ((), ())), precision=precision,
                                   preferred_element_type=F32)
        acc[...] += part

        @pl.when(kk == nk - 1)
        def _():
            r = acc[...]
            if has_res:
                r = r + r_ref[...].astype(F32)
            o_ref[...] = r.astype(out_dtype)

    in_specs = [a_spec, b_spec]
    args = [a, b]
    if has_res:
        in_specs.append(pl.BlockSpec((tm, tn), lambda i, j, kk: (i, j)))
        args.append(res)
    return pl.pallas_call(
        body, name=name, grid=(m // tm, n // tn, nk),
        in_specs=in_specs, out_specs=pl.BlockSpec((tm, tn), lambda i, j, kk: (i, j)),
        out_shape=jax.ShapeDtypeStruct((m, n), out_dtype),
        scratch_shapes=[pltpu.VMEM((tm, tn), F32)],
        compiler_params=_params("parallel", "parallel", "arbitrary"),
    )(*args)


def _rms_matmul(x, nw, w, *, name, tm, tn):
    t, d = x.shape
    n = w.shape[1]
    assert t % tm == 0 and n % tn == 0

    def body(x_ref, nw_ref, w_ref, o_ref, h_ref):
        @pl.when(pl.program_id(1) == 0)
        def _():
            xv = x_ref[...]
            r = lax.rsqrt(jnp.mean(xv * xv, axis=-1, keepdims=True) + NORM_EPS)
            h_ref[...] = (xv * r * nw_ref[...]).astype(BF16)

        o_ref[...] = _dot(h_ref[...], w_ref[...].astype(BF16), NN).astype(BF16)

    return pl.pallas_call(
        body, name=name, grid=(t // tm, n // tn),
        in_specs=[pl.BlockSpec((tm, d), lambda i, j: (i, 0)),
                  pl.BlockSpec((1, d), lambda i, j: (0, 0)),
                  pl.BlockSpec((d, tn), lambda i, j: (0, j))],
        out_specs=[pl.BlockSpec((tm, tn), lambda i, j: (i, j)),
                   pl.BlockSpec((tm, d), lambda i, j: (i, 0))],
        out_shape=[jax.ShapeDtypeStruct((t, n), BF16), jax.ShapeDtypeStruct((t, d), BF16)],
        compiler_params=_params("parallel", "arbitrary"),
    )(x, nw, w)


def _zero_ext(v):
    z = jnp.zeros((SUBLANES, v.shape[1]), v.dtype)
    return jnp.concatenate([z, v, z], axis=0)


def _shifted(v_ext, offset, seq):
    if offset == 0:
        return v_ext[SUBLANES:SUBLANES + seq]
    return pltpu.roll(v_ext, (-offset) % (seq + 2 * SUBLANES), 0)[SUBLANES:SUBLANES + seq]


def _conv_taps(v, w_ref, taps, seq):
    pad = taps // 2
    v_ext = _zero_ext(v)
    acc = None
    for k in range(taps):
        term = _shifted(v_ext, k - pad, seq) * w_ref[k:k + 1, :]
        acc = term if acc is None else acc + term
    return acc


def _ssm_conv_fwd(proj, cw, cb, *, nb, seq):
    width = 512

    def body(x_ref, w_ref, b_ref, o_ref, g_ref):
        g = _conv_taps(x_ref[...].astype(F32), w_ref, SSM_TAPS, seq) + b_ref[...]
        o_ref[...] = g * _sigmoid(g)
        g_ref[...] = g.astype(BF16)

    def col(j):
        return jnp.where(j < 2, j + PXS // width, PB // width)

    return pl.pallas_call(
        body, name="ssm_conv_fwd", grid=(nb, CONV_CH // width),
        in_specs=[pl.BlockSpec((seq, width), lambda b, j: (b, col(j))),
                  pl.BlockSpec((SSM_TAPS, width), lambda b, j: (0, j)),
                  pl.BlockSpec((1, width), lambda b, j: (0, j))],
        out_specs=[pl.BlockSpec((seq, width), lambda b, j: (b, j))] * 2,
        out_shape=[jax.ShapeDtypeStruct((nb * seq, CONV_CH), F32), jax.ShapeDtypeStruct((nb * seq, CONV_CH), BF16)],
        compiler_params=_params("parallel", "parallel"),
    )(proj, cw, cb)


def _ffn_act_fwd(gu, cw, cb, *, nb, seq):
    width = 256
    nj = D_FF // width

    def body(g_ref, u_ref, w_ref, b_ref, o_ref, s_ref):
        g = _conv_taps(g_ref[...].astype(F32), w_ref, FFN_TAPS, seq) + b_ref[...]
        o_ref[...] = (g * _sigmoid(g) * u_ref[...].astype(F32)).astype(BF16)
        s_ref[...] = g.astype(BF16)

    return pl.pallas_call(
        body, name="ffn_act_fwd", grid=(nb, nj),
        in_specs=[pl.BlockSpec((seq, width), lambda b, j: (b, j)),
                  pl.BlockSpec((seq, width), lambda b, j: (b, j + nj)),
                  pl.BlockSpec((FFN_TAPS, width), lambda b, j: (0, j)),
                  pl.BlockSpec((1, width), lambda b, j: (0, j))],
        out_specs=[pl.BlockSpec((seq, width), lambda b, j: (b, j))] * 2,
        out_shape=[jax.ShapeDtypeStruct((nb * seq, D_FF), BF16)] * 2,
        compiler_params=_params("parallel", "parallel"),
    )(gu, gu, cw, cb)


def _scan_setup(d, dt_ref, dtb_ref, alog_ref, z_ref, zt_ref, *, lower_when_dir0, inclusive):
    is0 = d == 0
    dt_all = _softplus(dt_ref[...] + dtb_ref[...])
    adt_all = dt_all * (-jnp.exp(alog_ref[...]))
    li = lax.broadcasted_iota(jnp.int32, (CHUNK, CHUNK), 0)
    si = lax.broadcasted_iota(jnp.int32, (CHUNK, CHUNK), 1)
    lower = is0 if lower_when_dir0 else jnp.logical_not(is0)
    ahead = jnp.where(lower, li - si, si - li)
    mask = ahead >= 0
    tri = mask if inclusive else ahead > 0
    z_all = jnp.dot(tri.astype(F32), adt_all, precision=HIGHEST, preferred_element_type=F32)
    zt_all = z_all.T
    z_ref[...] = jnp.where(is0, z_all[:, 0:N_HEADS], z_all[:, N_HEADS:2 * N_HEADS])
    zt_ref[...] = jnp.where(is0, zt_all[0:N_HEADS, :], zt_all[N_HEADS:2 * N_HEADS, :])
    dt = jnp.where(is0, dt_all[:, 0:N_HEADS], dt_all[:, N_HEADS:2 * N_HEADS])
    adt = jnp.where(is0, adt_all[:, 0:N_HEADS], adt_all[:, N_HEADS:2 * N_HEADS])
    tot = jnp.sum(adt, axis=0, keepdims=True)
    return ahead, dt, tot, dt_all


def _chunk_index(nchunk, forward_when_dir0):
    def idx(d, b, c):
        fwd = (d == 0) if forward_when_dir0 else (d != 0)
        return b * nchunk + jnp.where(fwd, c, nchunk - 1 - c)
    return idx


def _ssd_fwd(xbc, dtraw, dtb, alog, *, nb, seq, gather=()):
    nchunk = seq // CHUNK
    t = nb * seq
    row = _chunk_index(nchunk, True)
    ng = len(gather)

    def body(*refs):
        xs_ref, bc_ref, dt_ref, dtb_ref, alog_ref = refs[:5]
        o_ref, hs_ref = refs[5 + ng:7 + ng]
        h_ref, z_ref, zt_ref, dts_ref = refs[7 + 2 * ng:11 + 2 * ng]
        d, c = pl.program_id(0), pl.program_id(2)
        if ng:
            step = (d * nb + pl.program_id(1)) * nchunk + c
            copies = _exchange_copies(refs[5:5 + ng], refs[7 + ng:7 + 2 * ng], *refs[11 + 2 * ng:], gather=True)
            pl.when(step == 0)(lambda: _exchange_start(copies))

        @pl.when(c == 0)
        def _():
            h_ref[...] = jnp.zeros_like(h_ref)

        ahead, dt, tot, _ = _scan_setup(d, dt_ref, dtb_ref, alog_ref, z_ref, zt_ref,
                                        lower_when_dir0=True, inclusive=True)
        mask = ahead >= 0
        dts_ref[...] = dt
        e_tot = jnp.exp(tot)
        for g in range(N_GROUPS):
            heads = range(g * HEADS_PER_GROUP, (g + 1) * HEADS_PER_GROUP)
            bg = bc_ref[:, g * N_STATE:(g + 1) * N_STATE]
            cg = bc_ref[:, BC_WIDTH + g * N_STATE:BC_WIDTH + (g + 1) * N_STATE]
            cb = _dot(cg.astype(BF16), bg.astype(BF16), NT)
            zc = {h: jnp.broadcast_to(z_ref[:, h:h + 1], (CHUNK, CHUNK)) for h in heads}
            decay = {h: jnp.exp(jnp.where(mask, zc[h] - zt_ref[h:h + 1, :], -jnp.inf)) for h in heads}
            u = {h: (xs_ref[:, h * HEAD_DIM:(h + 1) * HEAD_DIM] * dts_ref[:, h:h + 1]).astype(BF16) for h in heads}
            state = {h: h_ref[h] for h in heads}
            for h in heads:
                hs_ref[0, 0, h] = state[h]
            mix = {h: (cb * decay[h]).astype(BF16) for h in heads}
            cz = {h: (cg * jnp.exp(zc[h])).astype(BF16) for h in heads}
            bw = {h: (bg * jnp.exp(tot[:, h:h + 1] - zc[h])).astype(BF16) for h in heads}
            y = {h: _dot(mix[h], u[h], NN) + _dot(cz[h], state[h].astype(BF16), NT) for h in heads}
            new = {h: state[h] * e_tot[:, h:h + 1] + _dot(u[h], bw[h], TN) for h in heads}
            for h in heads:
                h_ref[h] = new[h]
                o_ref[0, :, h * HEAD_DIM:(h + 1) * HEAD_DIM] = y[h]
        if ng:
            pl.when(step == 2 * nb * nchunk - 1)(lambda: _exchange_wait(copies))

    any_spec = pl.BlockSpec(memory_space=pl.ANY)
    res = pl.pallas_call(
        body, name="ssd_fwd_gather" if ng else "ssd_fwd", grid=(2, nb, nchunk),
        in_specs=[pl.BlockSpec((CHUNK, SSM_WIDTH), lambda d, b, c: (row(d, b, c), 0)),
                  pl.BlockSpec((CHUNK, 2 * BC_WIDTH), lambda d, b, c: (row(d, b, c), SSM_WIDTH // (2 * BC_WIDTH))),
                  pl.BlockSpec((CHUNK, LANES), lambda d, b, c: (row(d, b, c), 0)),
                  pl.BlockSpec((1, LANES), lambda d, b, c: (0, 0)),
                  pl.BlockSpec((1, LANES), lambda d, b, c: (0, 0))] + [any_spec] * ng,
        out_specs=[pl.BlockSpec((1, CHUNK, SSM_WIDTH), lambda d, b, c: (d, row(d, b, c), 0)),
                   pl.BlockSpec((1, 1, N_HEADS, HEAD_DIM, N_STATE), lambda d, b, c: (d, row(d, b, c), 0, 0, 0))]
        + [any_spec] * ng,
        out_shape=[jax.ShapeDtypeStruct((2, t, SSM_WIDTH), F32),
                   jax.ShapeDtypeStruct((2, nb * nchunk, N_HEADS, HEAD_DIM, N_STATE), F32)]
        + _exchange_out_shapes(gather, gather=True),
        scratch_shapes=_SSD_SCRATCH + _exchange_sems(ng),
        compiler_params=_params("arbitrary", "arbitrary", "arbitrary"),
    )(xbc, xbc, dtraw, dtb, alog, *gather)
    return res[0], res[1], list(res[2:])


def _gate_norm_fwd(y2, xbc, proj, dvec, nw, *, tm):
    t = xbc.shape[0]
    half = SSM_WIDTH // N_GROUPS

    def body(y_ref, xs_ref, z_ref, d_ref, w_ref, o_ref):
        z = z_ref[...].astype(F32)
        p = (y_ref[0] + y_ref[1] + d_ref[...] * xs_ref[...]) * (z * _sigmoid(z))
        for g in range(N_GROUPS):
            pg = p[:, g * half:(g + 1) * half]
            r = lax.rsqrt(jnp.mean(pg * pg, axis=-1, keepdims=True) + NORM_EPS)
            o_ref[:, g * half:(g + 1) * half] = (pg * r * w_ref[:, g * half:(g + 1) * half]).astype(BF16)

    return pl.pallas_call(
        body, name="gate_norm_fwd", grid=(t // tm,),
        in_specs=[pl.BlockSpec((2, tm, SSM_WIDTH), lambda i: (0, i, 0)),
                  pl.BlockSpec((tm, SSM_WIDTH), lambda i: (i, 0)),
                  pl.BlockSpec((tm, SSM_WIDTH), lambda i: (i, PZ // SSM_WIDTH)),
                  pl.BlockSpec((1, SSM_WIDTH), lambda i: (0, 0)),
                  pl.BlockSpec((1, SSM_WIDTH), lambda i: (0, 0))],
        out_specs=pl.BlockSpec((tm, SSM_WIDTH), lambda i: (i, 0)),
        out_shape=jax.ShapeDtypeStruct((t, SSM_WIDTH), BF16),
        compiler_params=_params("parallel"),
    )(y2, xbc, proj, dvec, nw)


GROUP_ROWS = Q_PER_KV * CHUNK


def _keys_inside(n, nblk):
    kpos = (n - 1) * CHUNK + lax.broadcasted_iota(jnp.int32, (1, KEY_SPAN), 1)
    return (kpos >= 0) & (kpos < nblk * CHUNK)


def _per_head_column(ref, g):
    blk = lax.broadcasted_iota(jnp.int32, (GROUP_ROWS, 1), 0) // CHUNK
    col = jnp.zeros((GROUP_ROWS, 1), F32)
    for r in range(Q_PER_KV):
        h = g * Q_PER_KV + r
        col = jnp.where(blk == r, ref[:, h:h + 1], col)
    return col


def _stack_heads(ref, g, dtype):
    return jnp.concatenate([ref[:, (g * Q_PER_KV + r) * HEAD_DIM:(g * Q_PER_KV + r + 1) * HEAD_DIM].astype(dtype)
                            for r in range(Q_PER_KV)], axis=0)


def _kv_specs(nblk):
    kvb = PK // (2 * KV_WIDTH)

    def at(off):
        def idx(b, n):
            return (b * nblk + jnp.clip(n + off, 0, nblk - 1), kvb)
        return pl.BlockSpec((CHUNK, 2 * KV_WIDTH), idx)
    return [at(-1), at(0), at(1)]


def _attn_fwd(proj, bias, sink, *, nb, seq, gather=()):
    nblk = seq // CHUNK
    t = nb * seq
    scale = HEAD_DIM ** -0.5
    ng = len(gather)

    def body(*refs):
        q_ref, kp_ref, kc_ref, kn_ref, bias_ref, sink_ref = refs[:6]
        o_ref, lse_ref = refs[6 + ng:8 + ng]
        n = pl.program_id(1)
        if ng:
            step = pl.program_id(0) * nblk + n
            copies = _exchange_copies(refs[6:6 + ng], refs[8 + ng:8 + 2 * ng], *refs[8 + 2 * ng:], gather=True)
            pl.when(step == 0)(lambda: _exchange_start(copies))
        inside = _keys_inside(n, nblk)
        groups = range(KV_HEADS)

        def keys(g, off):
            cs = slice(off + g * HEAD_DIM, off + (g + 1) * HEAD_DIM)
            return jnp.concatenate([kp_ref[:, cs], kc_ref[:, cs], kn_ref[:, cs]], axis=0).astype(BF16)

        qs = [(_stack_heads(q_ref, g, F32) * scale).astype(BF16) for g in groups]
        ss = [jnp.where(inside, _dot(qs[g], keys(g, 0), NT)
                        + bias_ref[g * Q_PER_KV:(g + 1) * Q_PER_KV].reshape(GROUP_ROWS, KEY_SPAN), -jnp.inf) for g in groups]
        sks = [_per_head_column(sink_ref, g) for g in groups]
        ms = [jnp.maximum(jnp.max(ss[g], axis=-1, keepdims=True), sks[g]) for g in groups]
        ps = [jnp.exp(ss[g] - ms[g]) for g in groups]
        denoms = [jnp.sum(ps[g], axis=-1, keepdims=True) + jnp.exp(sks[g] - ms[g]) for g in groups]
        outs = [(_dot(ps[g].astype(BF16), keys(g, KV_WIDTH), NN) * (1.0 / denoms[g])).astype(BF16) for g in groups]
        lses = []
        for g in groups:
            lse = ms[g] + jnp.log(denoms[g])
            for r in range(Q_PER_KV):
                h = g * Q_PER_KV + r
                o_ref[:, h * HEAD_DIM:(h + 1) * HEAD_DIM] = outs[g][r * CHUNK:(r + 1) * CHUNK]
                lses.append(lse[r * CHUNK:(r + 1) * CHUNK])
        lse_ref[...] = jnp.concatenate(lses, axis=1)
        if ng:
            pl.when(step == nb * nblk - 1)(lambda: _exchange_wait(copies))

    any_spec = pl.BlockSpec(memory_space=pl.ANY)
    res = pl.pallas_call(
        body, name="attn_fwd_gather" if ng else "attn_fwd", grid=(nb, nblk),
        in_specs=[pl.BlockSpec((CHUNK, D_MODEL), lambda b, n: (b * nblk + n, PQ // D_MODEL))] + _kv_specs(nblk) + [
            pl.BlockSpec((N_HEADS, CHUNK, KEY_SPAN), lambda b, n: (0, 0, 0)),
            pl.BlockSpec((1, LANES), lambda b, n: (0, 0))] + [any_spec] * ng,
        out_specs=[pl.BlockSpec((CHUNK, D_MODEL), lambda b, n: (b * nblk + n, 0)),
                   pl.BlockSpec((CHUNK, N_HEADS), lambda b, n: (b * nblk + n, 0))] + [any_spec] * ng,
        out_shape=[jax.ShapeDtypeStruct((t, D_MODEL), BF16), jax.ShapeDtypeStruct((t, N_HEADS), F32)]
        + _exchange_out_shapes(gather, gather=True),
        scratch_shapes=_exchange_sems(ng),
        compiler_params=_params("arbitrary", "arbitrary"),
    )(proj, proj, proj, proj, bias, sink, *gather)
    return res[0], res[1], list(res[2:])


def _loss_head(x, tgt, nw, *, tm):
    t, d = x.shape

    def body(x_ref, t_ref, w_ref, dx_ref, dw_ref, l_ref):
        @pl.when(pl.program_id(0) == 0)
        def _():
            dw_ref[...] = jnp.zeros_like(dw_ref)
            l_ref[...] = jnp.zeros_like(l_ref)

        xv = x_ref[...]
        w = w_ref[...]
        r = lax.rsqrt(jnp.mean(xv * xv, axis=-1, keepdims=True) + NORM_EPS)
        xh = xv * r
        err = xh * w - t_ref[...]
        l_ref[...] += jnp.sum(err * err) * (0.5 / d)
        dy = err * (1.0 / d)
        gw = dy * w
        dx_ref[...] = r * (gw - xh * jnp.mean(gw * xh, axis=-1, keepdims=True))
        dw_ref[...] += jnp.sum(dy * xh, axis=0, keepdims=True)

    return pl.pallas_call(
        body, name="loss_head", grid=(t // tm,),
        in_specs=[pl.BlockSpec((tm, d), lambda i: (i, 0)), pl.BlockSpec((tm, d), lambda i: (i, 0)),
                  pl.BlockSpec((1, d), lambda i: (0, 0))],
        out_specs=[pl.BlockSpec((tm, d), lambda i: (i, 0)), pl.BlockSpec((1, d), lambda i: (0, 0)),
                   pl.BlockSpec((1, LANES), lambda i: (0, 0))],
        out_shape=[jax.ShapeDtypeStruct((t, d), F32), jax.ShapeDtypeStruct((1, d), F32),
                   jax.ShapeDtypeStruct((1, LANES), F32)],
        compiler_params=_params("arbitrary"),
    )(x, tgt, nw)


def _to_proj_layout(w):
    pad = jnp.zeros(w.shape[:-1] + (PROJ_W - IN_COLS,), w.dtype)
    return jnp.concatenate([w[..., OZ:OXBC], w[..., OXBC:OXBC + SSM_WIDTH], w[..., OQ:OK_],
                            w[..., OXBC + SSM_WIDTH:ODT], w[..., OK_:IN_COLS], w[..., ODT:OQ], pad], axis=-1)


def _from_proj_layout(g):
    return jnp.concatenate([g[..., PZ:PZ + 2 * SSM_WIDTH], g[..., PB:PB + 2 * BC_WIDTH], g[..., PDT:PDT + 2 * N_HEADS],
                            g[..., PQ:PQ + D_MODEL], g[..., PK:PK + 2 * KV_WIDTH]], axis=-1)


def _pad_lanes(v):
    return jnp.pad(v.reshape(1, -1), ((0, 0), (0, LANES - v.size)))


def _layer_fwd(x, p, band_bias, *, nb, seq, own_shards, next_shards=()):
    w_main, w_dt = p["w_in"][:, :PDT], p["w_in"][:, PDT:]
    proj, h1 = _rms_matmul(x, p["n1"], w_main, name="in_proj", tm=MM_ROWS, tn=1024)
    dtraw = _matmul(h1, w_dt, "nn", name="in_proj_dt", tm=MM_ROWS, tn=LANES, tk=D_MODEL)
    xbc, gconv = _ssm_conv_fwd(proj, p["conv_w"], p["conv_b"], nb=nb, seq=seq)
    y2, states, (g_out, g_up, g_down) = _ssd_fwd(xbc, dtraw, p["dtb"], p["alog"], nb=nb, seq=seq, gather=own_shards)
    p = dict(p, w_out=_blocks_to_rows(g_out, 1)[0], w_up=_blocks_to_cols(g_up, 1)[0], w_down=_blocks_to_rows(g_down, 1)[0])
    y_ssm = _gate_norm_fwd(y2, xbc, proj, p["dvec"], p["ssm_nw"], tm=256)
    y_att, lse, gathered = _attn_fwd(proj, band_bias, p["sink"], nb=nb, seq=seq, gather=next_shards)
    x1 = _matmul(y_ssm, p["w_out"][:SSM_WIDTH], "nn", name="out_proj_ssm", tm=MM_ROWS, tn=1024, tk=1024, res=x)
    x1 = _matmul(y_att, p["w_out"][SSM_WIDTH:], "nn", name="out_proj_att", tm=MM_ROWS, tn=1024, tk=1024, res=x1)
    gu, h2 = _rms_matmul(x1, p["n2"], p["w_up"], name="up_proj", tm=MM_ROWS, tn=1408)
    act, fconv = _ffn_act_fwd(gu, p["ffn_cw"], p["ffn_cb"], nb=nb, seq=seq)
    x2 = _matmul(act, p["w_down"], "nn", name="down_proj", tm=MM_ROWS, tn=1024, tk=1408, res=x1)
    saved = dict(x=x, proj=proj, dtraw=dtraw, h1=h1, xbc=xbc, gconv=gconv, y2=y2, states=states, y_ssm=y_ssm, y_att=y_att,
                 lse=lse, x1=x1, gu=gu, fconv=fconv, h2=h2, act=act)
    return x2, p, saved, gathered


def _rms_bwd(x, dh, nw, dres, *, tm, name):
    t, d = x.shape

    def body(x_ref, dh_ref, w_ref, r_ref, dx_ref, dw_ref):
        @pl.when(pl.program_id(0) == 0)
        def _():
            dw_ref[...] = jnp.zeros_like(dw_ref)

        xv = x_ref[...]
        dh_v = dh_ref[...].astype(F32)
        r = lax.rsqrt(jnp.mean(xv * xv, axis=-1, keepdims=True) + NORM_EPS)
        xh = xv * r
        gw = dh_v * w_ref[...]
        dx_ref[...] = r_ref[...] + r * (gw - xh * jnp.mean(gw * xh, axis=-1, keepdims=True))
        dw_ref[...] += jnp.sum(dh_v * xh, axis=0, keepdims=True)

    row = pl.BlockSpec((tm, d), lambda i: (i, 0))
    vec = pl.BlockSpec((1, d), lambda i: (0, 0))
    return pl.pallas_call(
        body, name=name, grid=(t // tm,), in_specs=[row, row, vec, row], out_specs=[row, vec],
        out_shape=[jax.ShapeDtypeStruct((t, d), F32), jax.ShapeDtypeStruct((1, d), F32)],
        compiler_params=_params("arbitrary"),
    )(x, dh, nw, dres)


def _dsilu(g, sg):
    return sg * (1.0 + g * (1.0 - sg))


def _conv_taps_bwd(gpre, dg, w_ref, dwb_ref, taps, seq):
    pad = taps // 2
    dg_ext, gpre_ext = _zero_ext(dg), _zero_ext(gpre)
    dpre = None
    for k in range(taps):
        term = _shifted(dg_ext, pad - k, seq) * w_ref[k:k + 1, :]
        dpre = term if dpre is None else dpre + term
        dwb_ref[k:k + 1, :] += jnp.sum(dg * _shifted(gpre_ext, k - pad, seq), axis=0, keepdims=True)
    dwb_ref[SUBLANES - 1:SUBLANES, :] += jnp.sum(dg, axis=0, keepdims=True)
    return dpre


def _ffn_act_bwd(gu, gconv, dact, cw, *, nb, seq):
    width = 256
    nj = D_FF // width

    def body(g_ref, u_ref, s_ref, da_ref, w_ref, dg_ref, du_ref, dwb_ref):
        @pl.when(pl.program_id(1) == 0)
        def _():
            dwb_ref[...] = jnp.zeros_like(dwb_ref)

        g = s_ref[...].astype(F32)
        sg = _sigmoid(g)
        da = da_ref[...].astype(F32)
        du_ref[...] = (da * g * sg).astype(BF16)
        dgc = da * u_ref[...].astype(F32) * _dsilu(g, sg)
        dg_ref[...] = _conv_taps_bwd(g_ref[...].astype(F32), dgc, w_ref, dwb_ref, FFN_TAPS, seq).astype(BF16)

    blk = lambda off: pl.BlockSpec((seq, width), lambda j, b: (b, j + off))
    return pl.pallas_call(
        body, name="ffn_act_bwd", grid=(nj, nb),
        in_specs=[blk(0), blk(nj), blk(0), blk(0), pl.BlockSpec((FFN_TAPS, width), lambda j, b: (0, j))],
        out_specs=[blk(0), blk(0), pl.BlockSpec((SUBLANES, width), lambda j, b: (0, j))],
        out_shape=[jax.ShapeDtypeStruct((nb * seq, D_FF), BF16), jax.ShapeDtypeStruct((nb * seq, D_FF), BF16),
                   jax.ShapeDtypeStruct((SUBLANES, D_FF), F32)],
        compiler_params=_params("parallel", "arbitrary"),
    )(gu, gu, gconv, dact, cw)


def _ssm_conv_bwd(proj, gconv, pair, cw, *, nb, seq, name, width, proj_col, conv_col, ncol, extra=None, scale=None):
    has_extra = extra is not None

    def body(*refs):
        if has_extra:
            x_ref, g_ref, p_ref, w_ref, e_ref, s_ref, dx_ref, dwb_ref = refs
        else:
            x_ref, g_ref, p_ref, w_ref, dx_ref, dwb_ref = refs

        @pl.when(pl.program_id(1) == 0)
        def _():
            dwb_ref[...] = jnp.zeros_like(dwb_ref)

        g = g_ref[...].astype(F32)
        da = p_ref[0] + p_ref[1]
        if has_extra:
            da = da + e_ref[...] * s_ref[...]
        dx_ref[...] = _conv_taps_bwd(x_ref[...].astype(F32), da * _dsilu(g, _sigmoid(g)), w_ref, dwb_ref, SSM_TAPS, seq)

    in_specs = [pl.BlockSpec((seq, width), lambda j, b: (b, j + proj_col)),
                pl.BlockSpec((seq, width), lambda j, b: (b, j + conv_col)),
                pl.BlockSpec((2, seq, width), lambda j, b: (0, b, j)),
                pl.BlockSpec((SSM_TAPS, width), lambda j, b: (0, j + conv_col))]
    args = [proj, gconv, pair, cw]
    if has_extra:
        in_specs += [pl.BlockSpec((seq, width), lambda j, b: (b, j)), pl.BlockSpec((1, width), lambda j, b: (0, j))]
        args += [extra, scale]
    return pl.pallas_call(
        body, name=name, grid=(ncol, nb), in_specs=in_specs,
        out_specs=[pl.BlockSpec((seq, width), lambda j, b: (b, j)), pl.BlockSpec((SUBLANES, width), lambda j, b: (0, j))],
        out_shape=[jax.ShapeDtypeStruct((nb * seq, ncol * width), F32), jax.ShapeDtypeStruct((SUBLANES, ncol * width), F32)],
        compiler_params=_params("parallel", "arbitrary"),
    )(*args)


def _attn_bwd(proj, dmix, y_att, lse, bias, sink, dbias_in, *, nb, seq):
    nblk = seq // CHUNK
    t = nb * seq
    scale = HEAD_DIM ** -0.5

    def body(q_ref, kp_ref, kc_ref, kn_ref, do_ref, o_ref, lse_ref, bias_ref, sink_ref, dbin_ref,
             dq_ref, dkv_ref, dbias_ref, dsink_ref):
        b, n = pl.program_id(0), pl.program_id(1)

        @pl.when(n == 0)
        def _():
            dkv_ref[...] = jnp.zeros_like(dkv_ref)

        @pl.when((n == 0) & (b == 0))
        def _():
            dbias_ref[...] = dbin_ref[...]
            dsink_ref[...] = jnp.zeros_like(dsink_ref)

        inside = _keys_inside(n, nblk)
        lane = lax.broadcasted_iota(jnp.int32, (1, LANES), 1)
        dsink = jnp.zeros((1, LANES), F32)
        rows = pl.ds(pl.multiple_of(n * CHUNK, CHUNK), KEY_SPAN)
        groups = range(KV_HEADS)

        def keys(g, off):
            cs = slice(off + g * HEAD_DIM, off + (g + 1) * HEAD_DIM)
            return jnp.concatenate([kp_ref[:, cs], kc_ref[:, cs], kn_ref[:, cs]], axis=0).astype(BF16)

        kcat = [keys(g, 0) for g in groups]
        vcat = [keys(g, KV_WIDTH) for g in groups]
        q = [(_stack_heads(q_ref, g, F32) * scale).astype(BF16) for g in groups]
        do = [_stack_heads(do_ref, g, F32) for g in groups]
        do16 = [do[g].astype(BF16) for g in groups]
        lse = [jnp.concatenate([lse_ref[:, g * Q_PER_KV + r:g * Q_PER_KV + r + 1] for r in range(Q_PER_KV)], axis=0)
               for g in groups]
        s = [jnp.where(inside, _dot(q[g], kcat[g], NT)
                       + bias_ref[g * Q_PER_KV:(g + 1) * Q_PER_KV].reshape(GROUP_ROWS, KEY_SPAN), -jnp.inf) for g in groups]
        p = [jnp.exp(s[g] - lse[g]) for g in groups]
        delta = [jnp.sum(do[g] * _stack_heads(o_ref, g, F32), axis=-1, keepdims=True) for g in groups]
        ds = [p[g] * (_dot(do16[g], vcat[g], NT) - delta[g]) for g in groups]
        ds16 = [ds[g].astype(BF16) for g in groups]
        sink_part = [jnp.exp(_per_head_column(sink_ref, g) - lse[g]) * delta[g] for g in groups]
        dq = [_dot(ds16[g], kcat[g], NN) * scale for g in groups]
        dk = [_dot(ds16[g], q[g], TN) for g in groups]
        dv = [_dot(p[g].astype(BF16), do16[g], TN) for g in groups]
        for g in groups:
            dbias_ref[g * Q_PER_KV:(g + 1) * Q_PER_KV] += ds[g].reshape(Q_PER_KV, CHUNK, KEY_SPAN)
            for r in range(Q_PER_KV):
                h = g * Q_PER_KV + r
                dq_ref[:, h * HEAD_DIM:(h + 1) * HEAD_DIM] = dq[g][r * CHUNK:(r + 1) * CHUNK]
                dsink = dsink - jnp.where(lane == h, jnp.sum(sink_part[g][r * CHUNK:(r + 1) * CHUNK], axis=0, keepdims=True), 0.0)
            dkv_ref[0, rows, g * HEAD_DIM:(g + 1) * HEAD_DIM] += dk[g]
            dkv_ref[0, rows, KV_WIDTH + g * HEAD_DIM:KV_WIDTH + (g + 1) * HEAD_DIM] += dv[g]
        dsink_ref[...] += dsink

    blk = lambda cb: pl.BlockSpec((CHUNK, D_MODEL), lambda b, n: (b * nblk + n, cb))
    whole = pl.BlockSpec((N_HEADS, CHUNK, KEY_SPAN), lambda b, n: (0, 0, 0))
    vec = pl.BlockSpec((1, LANES), lambda b, n: (0, 0))
    return pl.pallas_call(
        body, name="attn_bwd", grid=(nb, nblk),
        in_specs=[blk(PQ // D_MODEL)] + _kv_specs(nblk) + [
            blk(1), blk(0), pl.BlockSpec((CHUNK, N_HEADS), lambda b, n: (b * nblk + n, 0)), whole, vec, whole],
        out_specs=[blk(0), pl.BlockSpec((1, seq + 2 * CHUNK, 2 * KV_WIDTH), lambda b, n: (b, 0, 0)), whole, vec],
        out_shape=[jax.ShapeDtypeStruct((t, D_MODEL), F32),
                   jax.ShapeDtypeStruct((nb, seq + 2 * CHUNK, 2 * KV_WIDTH), F32),
                   jax.ShapeDtypeStruct((N_HEADS, CHUNK, KEY_SPAN), F32),
                   jax.ShapeDtypeStruct((1, LANES), F32)],
        compiler_params=_params("arbitrary", "arbitrary"),
    )(proj, proj, proj, proj, dmix, y_att, lse, bias, sink, dbias_in)


def _gate_norm_bwd(y2, xbc, proj, dmix, dvec, nw, *, tm):
    t = xbc.shape[0]
    half = SSM_WIDTH // N_GROUPS

    def body(y_ref, xs_ref, z_ref, do_ref, d_ref, w_ref, dyv_ref, dz_ref, dd_ref, dw_ref):
        @pl.when(pl.program_id(0) == 0)
        def _():
            dd_ref[...] = jnp.zeros_like(dd_ref)
            dw_ref[...] = jnp.zeros_like(dw_ref)

        z = z_ref[...].astype(F32)
        xs = xs_ref[...]
        sg = _sigmoid(z)
        gz = z * sg
        yv = y_ref[0] + y_ref[1] + d_ref[...] * xs
        p = yv * gz
        do = do_ref[...]
        for g in range(N_GROUPS):
            cs = slice(g * half, (g + 1) * half)
            pg = p[:, cs]
            r = lax.rsqrt(jnp.mean(pg * pg, axis=-1, keepdims=True) + NORM_EPS)
            ph = pg * r
            gw = do[:, cs] * w_ref[:, cs]
            dp = r * (gw - ph * jnp.mean(gw * ph, axis=-1, keepdims=True))
            dyv = dp * gz[:, cs]
            dyv_ref[:, cs] = dyv
            dz_ref[:, cs] = dp * yv[:, cs] * _dsilu(z[:, cs], sg[:, cs])
            dw_ref[:, cs] += jnp.sum(do[:, cs] * ph, axis=0, keepdims=True)
            dd_ref[:, cs] += jnp.sum(dyv * xs[:, cs], axis=0, keepdims=True)

    row = lambda cb: pl.BlockSpec((tm, SSM_WIDTH), lambda i: (i, cb))
    vec = pl.BlockSpec((1, SSM_WIDTH), lambda i: (0, 0))
    return pl.pallas_call(
        body, name="gate_norm_bwd", grid=(t // tm,),
        in_specs=[pl.BlockSpec((2, tm, SSM_WIDTH), lambda i: (0, i, 0)), row(0), row(PZ // SSM_WIDTH), row(0), vec, vec],
        out_specs=[row(0), row(0), vec, vec],
        out_shape=[jax.ShapeDtypeStruct((t, SSM_WIDTH), F32), jax.ShapeDtypeStruct((t, SSM_WIDTH), F32),
                   jax.ShapeDtypeStruct((1, SSM_WIDTH), F32), jax.ShapeDtypeStruct((1, SSM_WIDTH), F32)],
        compiler_params=_params("arbitrary"),
    )(y2, xbc, proj, dmix, dvec, nw)


def _ssd_specs(nchunk, row):
    return [pl.BlockSpec((CHUNK, SSM_WIDTH), lambda d, b, c: (row(d, b, c), 0)),
            pl.BlockSpec((CHUNK, 2 * BC_WIDTH), lambda d, b, c: (row(d, b, c), SSM_WIDTH // (2 * BC_WIDTH))),
            pl.BlockSpec((CHUNK, LANES), lambda d, b, c: (row(d, b, c), 0)),
            pl.BlockSpec((1, LANES), lambda d, b, c: (0, 0)),
            pl.BlockSpec((1, LANES), lambda d, b, c: (0, 0)),
            pl.BlockSpec((CHUNK, SSM_WIDTH), lambda d, b, c: (row(d, b, c), 0))]


_SSD_SCRATCH = [pltpu.VMEM((N_HEADS, HEAD_DIM, N_STATE), F32),
                pltpu.VMEM((CHUNK, N_HEADS), F32), pltpu.VMEM((N_HEADS, CHUNK), F32),
                pltpu.VMEM((CHUNK, N_HEADS), F32)]


def _ssd_bwd(xbc, dtraw, dtb, alog, dyv, states, *, nb, seq, scatter=()):
    nchunk = seq // CHUNK
    t = nb * seq
    row = _chunk_index(nchunk, False)
    ns = len(scatter)

    def body(*refs):
        xs_ref, bc_ref, dt_ref, dtb_ref, alog_ref, dy_ref, hs_ref = refs[:7]
        dx_ref, db_ref, dc_ref, draw_ref, da_ref, dbias_ref = refs[7 + ns:13 + ns]
        h_ref, z_ref, zt_ref, dts_ref = refs[13 + 2 * ns:17 + 2 * ns]
        d, b, c = pl.program_id(0), pl.program_id(1), pl.program_id(2)
        if ns:
            step = (d * nb + b) * nchunk + c
            copies = _exchange_copies(refs[7:7 + ns], refs[13 + ns:13 + 2 * ns], *refs[17 + 2 * ns:], gather=False)
            pl.when(step == 0)(lambda: _exchange_start(copies))

        @pl.when(c == 0)
        def _():
            h_ref[...] = jnp.zeros_like(h_ref)

        @pl.when((c == 0) & (b == 0) & (d == 0))
        def _():
            da_ref[...] = jnp.zeros_like(da_ref)
            dbias_ref[...] = jnp.zeros_like(dbias_ref)

        ahead, dt, tot, dt_all = _scan_setup(d, dt_ref, dtb_ref, alog_ref, z_ref, zt_ref,
                                             lower_when_dir0=False, inclusive=False)
        mask = ahead >= 0
        dts_ref[...] = dt
        e_tot = jnp.exp(tot)
        lane = lax.broadcasted_iota(jnp.int32, (1, LANES), 1)
        ddt = jnp.zeros((CHUNK, LANES), F32)
        head_row = lax.broadcasted_iota(jnp.int32, (LANES, 1), 0)
        inner = jnp.zeros((CHUNK, LANES), F32)
        inner_t = jnp.zeros((LANES, CHUNK), F32)
        outer = jnp.zeros((CHUNK, LANES), F32)
        span = jnp.zeros((1, LANES), F32)
        for g in range(N_GROUPS):
            bg = bc_ref[:, g * N_STATE:(g + 1) * N_STATE]
            cg = bc_ref[:, BC_WIDTH + g * N_STATE:BC_WIDTH + (g + 1) * N_STATE]
            bg16 = bg.astype(BF16)
            cg16 = cg.astype(BF16)
            bc_t = _dot(bg16, cg16, NT)
            heads = range(g * HEADS_PER_GROUP, (g + 1) * HEADS_PER_GROUP)
            cols = {h: slice(h * HEAD_DIM, (h + 1) * HEAD_DIM) for h in heads}
            zc = {h: jnp.broadcast_to(z_ref[:, h:h + 1], (CHUNK, CHUNK)) for h in heads}
            decay = {h: jnp.exp(jnp.where(mask, zc[h] - zt_ref[h:h + 1, :], -jnp.inf)) for h in heads}
            e_z = {h: jnp.exp(zc[h]) for h in heads}
            e_tz = {h: jnp.exp(tot[:, h:h + 1] - zc[h]) for h in heads}
            x_h = {h: xs_ref[:, cols[h]] for h in heads}
            dt_h = {h: dts_ref[:, h:h + 1] for h in heads}
            u = {h: (x_h[h] * dt_h[h]).astype(BF16) for h in heads}
            dy = {h: dy_ref[:, cols[h]].astype(BF16) for h in heads}
            state = {h: h_ref[h] for h in heads}
            st16 = {h: state[h].astype(BF16) for h in heads}
            fstate = {h: hs_ref[0, 0, h] for h in heads}
            mix = {h: (bc_t * decay[h]).astype(BF16) for h in heads}
            bz = {h: (bg * e_z[h]).astype(BF16) for h in heads}
            du = {h: _dot(mix[h], dy[h], NN) + _dot(bz[h], st16[h], NT) for h in heads}
            w2f = {h: _dot(u[h], dy[h], NT) * decay[h] for h in heads}
            w2 = {h: w2f[h].astype(BF16) for h in heads}
            db_out = {h: e_z[h] * _dot(u[h], st16[h], NN) for h in heads}
            dc_out = {h: e_tz[h] * _dot(dy[h], fstate[h].astype(BF16), NN) for h in heads}
            db_in = {h: _dot(w2[h], cg16, NN) for h in heads}
            dc_in = {h: _dot(w2[h], bg16, TN) for h in heads}
            pairs = {h: w2f[h] * bc_t for h in heads}
            col_in = {h: jnp.sum(pairs[h], axis=-1, keepdims=True) for h in heads}
            row_in = {h: jnp.sum(pairs[h], axis=0, keepdims=True) for h in heads}
            row_out = {h: jnp.sum(dc_out[h] * cg, axis=-1, keepdims=True) for h in heads}
            col_out = {h: jnp.sum(db_out[h] * bg, axis=-1, keepdims=True) for h in heads}
            ddt_h = {h: jnp.sum(du[h] * x_h[h], axis=-1, keepdims=True) for h in heads}
            cz = {h: (cg * e_tz[h]).astype(BF16) for h in heads}
            new = {h: state[h] * e_tot[:, h:h + 1] + _dot(dy[h], cz[h], TN) for h in heads}
            dbg = jnp.zeros((CHUNK, N_STATE), F32)
            dcg = jnp.zeros((CHUNK, N_STATE), F32)
            for h in heads:
                here = lane == d * N_HEADS + h
                dbg = dbg + db_in[h] + db_out[h]
                dcg = dcg + dc_in[h] + dc_out[h]
                inner = inner + jnp.where(here, row_out[h] - col_in[h], 0.0)
                inner_t = inner_t + jnp.where(head_row == d * N_HEADS + h, row_in[h], 0.0)
                outer = outer + jnp.where(here, col_out[h], 0.0)
                span = span + jnp.where(here, e_tot[:, h:h + 1] * jnp.sum(fstate[h] * state[h]), 0.0)
                ddt = ddt + jnp.where(here, ddt_h[h], 0.0)
                dx_ref[0, :, cols[h]] = du[h] * dt_h[h]
                h_ref[h] = new[h]
            db_ref[0, :, g * N_STATE:(g + 1) * N_STATE] = dbg
            dc_ref[0, :, g * N_STATE:(g + 1) * N_STATE] = dcg
        tri = mask.astype(F32)
        dadt = (jnp.dot(tri, inner + inner_t.T, precision=HIGHEST, preferred_element_type=F32)
                + jnp.dot(1.0 - tri, outer, precision=HIGHEST, preferred_element_type=F32) + span)
        a = -jnp.exp(alog_ref[...])
        draw = (ddt + a * dadt) * _sigmoid(dt_ref[...] + dtb_ref[...])
        draw_ref[0] = draw
        da_ref[...] += jnp.sum(dt_all * dadt, axis=0, keepdims=True) * a
        dbias_ref[...] += jnp.sum(draw, axis=0, keepdims=True)
        if ns:
            pl.when(step == 2 * nb * nchunk - 1)(lambda: _exchange_wait(copies))

    out_row = lambda w: pl.BlockSpec((1, CHUNK, w), lambda d, b, c: (d, row(d, b, c), 0))
    vec = pl.BlockSpec((1, LANES), lambda d, b, c: (0, 0))
    any_spec = pl.BlockSpec(memory_space=pl.ANY)
    res = pl.pallas_call(
        body, name="ssd_bwd_scatter" if ns else "ssd_bwd", grid=(2, nb, nchunk),
        in_specs=_ssd_specs(nchunk, row) + [
            pl.BlockSpec((1, 1, N_HEADS, HEAD_DIM, N_STATE), lambda d, b, c: (d, row(d, b, c), 0, 0, 0))]
        + [any_spec] * ns,
        out_specs=[out_row(SSM_WIDTH), out_row(BC_WIDTH), out_row(BC_WIDTH), out_row(LANES), vec, vec] + [any_spec] * ns,
        out_shape=[jax.ShapeDtypeStruct((2, t, SSM_WIDTH), F32), jax.ShapeDtypeStruct((2, t, BC_WIDTH), F32),
                   jax.ShapeDtypeStruct((2, t, BC_WIDTH), F32),
                   jax.ShapeDtypeStruct((2, t, LANES), F32), jax.ShapeDtypeStruct((1, LANES), F32),
                   jax.ShapeDtypeStruct((1, LANES), F32)] + _exchange_out_shapes(scatter, gather=False),
        scratch_shapes=_SSD_SCRATCH + _exchange_sems(ns),
        compiler_params=_params("arbitrary", "arbitrary", "arbitrary"),
    )(xbc, xbc, dtraw, dtb, alog, dyv, states, *scatter)
    return tuple(res[:6]) + (list(res[6:]),)


def _layer_bwd(dx2, p, s, band_bias, dbias_in, *, nb, seq, pending=()):
    t = nb * seq
    x, proj, xbc, x1 = s["x"], s["proj"], s["xbc"], s["x1"]
    dact = _matmul(dx2, p["w_down"], "nt", name="down_proj_dx", tm=MM_ROWS, tn=1408, tk=1024, out_dtype=BF16)
    g_w_down = _matmul(s["act"], dx2, "tn", name="down_proj_dw", tm=1408, tn=1024, tk=1024)
    dg, du, dwb_ffn = _ffn_act_bwd(s["gu"], s["fconv"], dact, p["ffn_cw"], nb=nb, seq=seq)
    dh2 = _matmul(dg, p["w_up"][:, :D_FF], "nt", name="up_proj_dx_g", tm=MM_ROWS, tn=1024, tk=1408)
    dh2 = _matmul(du, p["w_up"][:, D_FF:], "nt", name="up_proj_dx_u", tm=MM_ROWS, tn=1024, tk=1408, res=dh2)
    g_w_up = jnp.concatenate([_matmul(s["h2"], dg, "tn", name="up_proj_dw_g", tm=1024, tn=1408, tk=1024),
                              _matmul(s["h2"], du, "tn", name="up_proj_dw_u", tm=1024, tn=1408, tk=1024)], axis=1)
    dx1, g_n2 = _rms_bwd(x1, dh2, p["n2"], dx2, tm=512, name="norm2_bwd")
    dmix = _matmul(dx1, p["w_out"], "nt", name="out_proj_dx", tm=MM_ROWS, tn=1024, tk=1024)
    g_w_out = jnp.concatenate([_matmul(s["y_ssm"], dx1, "tn", name="out_proj_dw_ssm", tm=1024, tn=1024, tk=1024),
                               _matmul(s["y_att"], dx1, "tn", name="out_proj_dw_att", tm=1024, tn=1024, tk=1024)], axis=0)
    dq, dkv, dbias, dsink = _attn_bwd(proj, dmix, s["y_att"], s["lse"], band_bias, p["sink"], dbias_in, nb=nb, seq=seq)
    dkv = dkv[:, CHUNK:CHUNK + seq, :].reshape(t, 2 * KV_WIDTH)
    dyv, dz, g_dvec, g_ssm_nw = _gate_norm_bwd(s["y2"], xbc, proj, dmix, p["dvec"], p["ssm_nw"], tm=256)
    own = [_rows_to_blocks(g_w_down[None], 1).astype(BF16), _cols_to_blocks(g_w_up[None], 1).astype(BF16),
           _rows_to_blocks(g_w_out[None], 1).astype(BF16)]
    dxs2, db2, dc2, draw2, g_alog, g_dtb, exchanged = _ssd_bwd(xbc, s["dtraw"], p["dtb"], p["alog"], dyv, s["states"],
                                                               nb=nb, seq=seq, scatter=own + list(pending))
    ddt_raw = draw2[0] + draw2[1]
    conv = dict(nb=nb, seq=seq)
    dxs_pre, dwb_xs = _ssm_conv_bwd(proj, s["gconv"], dxs2, p["conv_w"], name="ssm_conv_bwd_x", width=256,
                                    proj_col=PXS // 256, conv_col=0, ncol=4, extra=dyv, scale=p["dvec"], **conv)
    db_pre, dwb_b = _ssm_conv_bwd(proj, s["gconv"], db2, p["conv_w"], name="ssm_conv_bwd_b", width=256,
                                  proj_col=PB // 256, conv_col=SSM_WIDTH // 256, ncol=1, **conv)
    dc_pre, dwb_c = _ssm_conv_bwd(proj, s["gconv"], dc2, p["conv_w"], name="ssm_conv_bwd_c", width=256,
                                  proj_col=PC // 256, conv_col=(SSM_WIDTH + BC_WIDTH) // 256, ncol=1, **conv)
    dwb_ssm = jnp.concatenate([dwb_xs, dwb_b, dwb_c], axis=1)
    dproj = jnp.concatenate([dz, dxs_pre, dq, db_pre, dc_pre, dkv, ddt_raw], axis=1).astype(BF16)
    dh1 = _matmul(dproj, p["w_in"], "nt", name="in_proj_dx", tm=MM_ROWS, tn=1024, tk=1408)
    g_w_in = _matmul(s["h1"], dproj, "tn", name="in_proj_dw", tm=1024, tn=1408, tk=1024)
    dx, g_n1 = _rms_bwd(x, dh1, p["n1"], dx1, tm=512, name="norm1_bwd")
    grads = dict(n1=g_n1, w_in=g_w_in, conv_w=dwb_ssm[:SSM_TAPS], conv_b=dwb_ssm[SUBLANES - 1], dtb=g_dtb, alog=g_alog,
                 dvec=g_dvec, ssm_nw=g_ssm_nw, sink=dsink, w_out=g_w_out, n2=g_n2, w_up=g_w_up,
                 ffn_cw=dwb_ffn[:FFN_TAPS], ffn_cb=dwb_ffn[SUBLANES - 1], w_down=g_w_down)
    return dx, dbias, grads, exchanged


def _band_bias(rel_bias, bucket):
    def body(rb_ref, b_ref, o_ref):
        o_ref[...] = jnp.zeros_like(o_ref)

        def per_bucket(k, carry):
            hit = b_ref[...] == k
            for h in range(N_HEADS):
                o_ref[h] = jnp.where(hit, rb_ref[k, h], o_ref[h])
            return carry

        lax.fori_loop(0, REL_BUCKETS, per_bucket, 0)
        qi = lax.broadcasted_iota(jnp.int32, (CHUNK, KEY_SPAN), 0)
        kj = lax.broadcasted_iota(jnp.int32, (CHUNK, KEY_SPAN), 1)
        band = jnp.abs(kj - CHUNK - qi) <= CHUNK
        for h in range(N_HEADS):
            o_ref[h] = jnp.where(band, o_ref[h], -jnp.inf)

    return pl.pallas_call(
        body, name="band_bias", out_shape=jax.ShapeDtypeStruct((N_HEADS, CHUNK, KEY_SPAN), F32),
        in_specs=[pl.BlockSpec(memory_space=pltpu.SMEM), pl.BlockSpec(memory_space=pltpu.VMEM)],
        out_specs=pl.BlockSpec(memory_space=pltpu.VMEM),
    )(rel_bias, bucket)


def _rel_bias_grad(dbias, bucket):
    def body(d_ref, b_ref, o_ref):
        o_ref[...] = jnp.zeros_like(o_ref)
        lane = lax.broadcasted_iota(jnp.int32, (1, LANES), 1)

        def per_bucket(k, carry):
            hit = b_ref[...] == k
            for h in range(N_HEADS):
                part = jnp.sum(jnp.where(hit, d_ref[h], 0.0), axis=1, keepdims=True)
                o_ref[h:h + 1, :] += jnp.where(lane == k, jnp.sum(part, axis=0, keepdims=True), 0.0)
            return carry

        lax.fori_loop(0, REL_BUCKETS, per_bucket, 0)

    return pl.pallas_call(
        body, name="rel_bias_grad", out_shape=jax.ShapeDtypeStruct((N_HEADS, LANES), F32),
        compiler_params=pltpu.CompilerParams(vmem_limit_bytes=VMEM_LIMIT_BYTES),
    )(dbias, bucket)


N_PEER = N_DEV - 1


def _exchange_copies(ins, outs, send_sems, recv_sems, local_sems, *, gather):
    x, y, c = lax.axis_index("x"), lax.axis_index("y"), lax.axis_index("c")
    me = 4 * x + 2 * y + c
    peers = []
    for k in range(1, N_DEV):
        px, py, pc = x ^ ((k >> 2) & 1), y ^ ((k >> 1) & 1), c ^ (k & 1)
        peers.append(((px, py, pc), 4 * px + 2 * py + pc))
    local, sends, recvs = [], [], []
    for i in range(len(ins)):
        mine = ins[i] if gather else ins[i].at[me]
        local.append(pltpu.make_async_copy(mine, outs[i].at[me], local_sems.at[i]))
        for k, (pid, pslot) in enumerate(peers):
            src = ins[i] if gather else ins[i].at[pslot]
            sem = i * N_PEER + k
            sends.append(pltpu.make_async_remote_copy(
                src_ref=src, dst_ref=outs[i].at[me], send_sem=send_sems.at[sem], recv_sem=recv_sems.at[sem],
                device_id=pid, device_id_type=pl.DeviceIdType.MESH))
            recvs.append(pltpu.make_async_remote_copy(
                src_ref=src, dst_ref=outs[i].at[pslot], send_sem=send_sems.at[sem], recv_sem=recv_sems.at[sem],
                device_id=pid, device_id_type=pl.DeviceIdType.MESH))
    return local, sends, recvs


def _exchange_start(copies):
    local, sends, _ = copies
    for cp in local + sends:
        cp.start()


def _exchange_wait(copies):
    local, sends, recvs = copies
    for cp in recvs:
        cp.wait_recv()
    for cp in sends:
        cp.wait_send()
    for cp in local:
        cp.wait()


def _exchange_out_shapes(arrs, *, gather):
    return [jax.ShapeDtypeStruct((N_DEV,) + (a.shape if gather else a.shape[1:]), a.dtype) for a in arrs]


def _exchange_sems(n):
    if not n:
        return []
    return [pltpu.SemaphoreType.DMA((n * N_PEER,)), pltpu.SemaphoreType.DMA((n * N_PEER,)), pltpu.SemaphoreType.DMA((n,))]


def _exchange(arrs, *, gather, name):
    n = len(arrs)

    def body(*refs):
        copies = _exchange_copies(refs[:n], refs[n:2 * n], *refs[2 * n:], gather=gather)
        _exchange_start(copies)
        _exchange_wait(copies)

    any_spec = pl.BlockSpec(memory_space=pl.ANY)
    return pl.pallas_call(
        body, name=name, in_specs=[any_spec] * n, out_specs=[any_spec] * n,
        out_shape=_exchange_out_shapes(arrs, gather=gather), scratch_shapes=_exchange_sems(n),
        compiler_params=pltpu.CompilerParams(has_side_effects=True),
    )(*arrs)


def _adamw(parts, w, m, v, *, name, tr):
    r, c = w.shape
    nparts = len(parts)
    rows = r // nparts
    assert rows * nparts == r and rows % tr == 0 and all(p.shape == (N_DEV, rows, c) for p in parts)
    per = rows // tr
    c1 = 1.0 - ADAM_B1 ** ADAM_STEP
    c2 = 1.0 - ADAM_B2 ** ADAM_STEP

    def body(*refs):
        p_refs = refs[:nparts]
        w_ref, m_ref, v_ref, g_ref, d_ref, nm_ref, nv_ref = refs[nparts:]
        which = pl.program_id(0) // per
        for k, p_ref in enumerate(p_refs):
            @pl.when(which == k)
            def _(p_ref=p_ref):
                acc = p_ref[0].astype(F32)
                for j in range(1, N_DEV):
                    acc = acc + p_ref[j].astype(F32)
                g_ref[...] = acc

        g = g_ref[...]
        nm = ADAM_B1 * m_ref[...] + (1.0 - ADAM_B1) * g
        nv = ADAM_B2 * v_ref[...] + (1.0 - ADAM_B2) * (g * g)
        nm_ref[...] = nm
        nv_ref[...] = nv
        d_ref[...] = -ADAM_LR * ((nm / c1) / (jnp.sqrt(nv / c2) + ADAM_EPS) + ADAM_WD * w_ref[...])

    def part_spec(k):
        return pl.BlockSpec((N_DEV, tr, c), lambda i: (0, jnp.clip(i - k * per, 0, per - 1), 0))

    blk = pl.BlockSpec((tr, c), lambda i: (i, 0))
    return pl.pallas_call(
        body, name=name, grid=(r // tr,),
        in_specs=[part_spec(k) for k in range(nparts)] + [blk, blk, blk],
        out_specs=[blk] * 4, out_shape=[jax.ShapeDtypeStruct((r, c), F32)] * 4,
        compiler_params=_params("arbitrary"),
    )(*parts, w, m, v)


def _t5_bucket(rel):
    half = REL_BUCKETS // 2
    max_exact = half // 2
    ret = jnp.where(rel > 0, half, 0)
    n = jnp.abs(rel)
    nf = jnp.maximum(n, 1).astype(F32)
    large = max_exact + (jnp.log(nf / max_exact) / math.log(CHUNK / max_exact) * (half - max_exact)).astype(jnp.int32)
    large = jnp.minimum(large, half - 1)
    return ret + jnp.where(n < max_exact, n, large)


def _split16(w):
    hi = w.astype(BF16)
    return hi, (w - hi.astype(F32)).astype(BF16)


def _cols_to_blocks(g, depth):
    _, r, c8 = g.shape
    return g.reshape(depth, r, N_DEV, c8 // N_DEV).transpose(2, 0, 1, 3).reshape(N_DEV, depth * r, c8 // N_DEV)


def _rows_to_blocks(g, depth):
    _, r8, c = g.shape
    return g.reshape(depth, N_DEV, r8 // N_DEV, c).transpose(1, 0, 2, 3).reshape(N_DEV, depth * r8 // N_DEV, c)


def _blocks_to_cols(a, depth):
    _, dr, c = a.shape
    r = dr // depth
    return a.reshape(N_DEV, depth, r, c).transpose(1, 2, 0, 3).reshape(depth, r, N_DEV * c)


def _blocks_to_rows(a, depth):
    _, dr, c = a.shape
    r = dr // depth
    return a.reshape(N_DEV, depth, r, c).transpose(1, 0, 2, 3).reshape(depth, N_DEV * r, c)


_SMALL = ("rel_bias", "norm1_w", "conv_b", "dt_bias", "a_log", "d_skip", "ssm_norm_w", "attn_sink", "norm2_w",
          "ffn_conv_b", "final_norm_w")
_SHARDED = ("w_in", "conv_w", "w_out", "w_up", "ffn_conv_w", "w_down")
_ORDER = ("rel_bias", "norm1_w", "w_in", "conv_w", "conv_b", "dt_bias", "a_log", "d_skip", "ssm_norm_w", "attn_sink",
          "w_out", "norm2_w", "w_up", "ffn_conv_w", "ffn_conv_b", "w_down", "final_norm_w")


def _pack_small(d):
    flat = jnp.concatenate([d[k].reshape(-1).astype(F32) for k in _SMALL])
    rows = -(-flat.size // (LANES * SUBLANES)) * SUBLANES
    return jnp.pad(flat, (0, rows * LANES - flat.size)).reshape(rows, LANES)


def _unpack_small(packed, like):
    flat = packed.reshape(-1)
    out, off = {}, 0
    for k in _SMALL:
        out[k] = flat[off:off + like[k].size].reshape(like[k].shape)
        off += like[k].size
    return out


def kernel(x, rel_bias, norm1_w, w_in, conv_w, conv_b, dt_bias, a_log, d_skip, ssm_norm_w, attn_sink, w_out, norm2_w, w_up, ffn_conv_w, ffn_conv_b, w_down, final_norm_w, loss_target, m_rel_bias, m_norm1_w, m_w_in, m_conv_w, m_conv_b, m_dt_bias, m_a_log, m_d_skip, m_ssm_norm_w, m_attn_sink, m_w_out, m_norm2_w, m_w_up, m_ffn_conv_w, m_ffn_conv_b, m_w_down, m_final_norm_w, v_rel_bias, v_norm1_w, v_w_in, v_conv_w, v_conv_b, v_dt_bias, v_a_log, v_d_skip, v_ssm_norm_w, v_attn_sink, v_w_out, v_norm2_w, v_w_up, v_ffn_conv_w, v_ffn_conv_b, v_w_down, v_final_norm_w):
    w = dict(rel_bias=rel_bias, norm1_w=norm1_w, w_in=w_in, conv_w=conv_w, conv_b=conv_b, dt_bias=dt_bias, a_log=a_log,
             d_skip=d_skip, ssm_norm_w=ssm_norm_w, attn_sink=attn_sink, w_out=w_out, norm2_w=norm2_w, w_up=w_up,
             ffn_conv_w=ffn_conv_w, ffn_conv_b=ffn_conv_b, w_down=w_down, final_norm_w=final_norm_w)
    m = dict(rel_bias=m_rel_bias, norm1_w=m_norm1_w, w_in=m_w_in, conv_w=m_conv_w, conv_b=m_conv_b, dt_bias=m_dt_bias,
             a_log=m_a_log, d_skip=m_d_skip, ssm_norm_w=m_ssm_norm_w, attn_sink=m_attn_sink, w_out=m_w_out,
             norm2_w=m_norm2_w, w_up=m_w_up, ffn_conv_w=m_ffn_conv_w, ffn_conv_b=m_ffn_conv_b, w_down=m_w_down,
             final_norm_w=m_final_norm_w)
    v = dict(rel_bias=v_rel_bias, norm1_w=v_norm1_w, w_in=v_w_in, conv_w=v_conv_w, conv_b=v_conv_b, dt_bias=v_dt_bias,
             a_log=v_a_log, d_skip=v_d_skip, ssm_norm_w=v_ssm_norm_w, attn_sink=v_attn_sink, w_out=v_w_out,
             norm2_w=v_norm2_w, w_up=v_w_up, ffn_conv_w=v_ffn_conv_w, ffn_conv_b=v_ffn_conv_b, w_down=v_w_down,
             final_norm_w=v_final_norm_w)
    nb, seq, _ = x.shape
    t = nb * seq
    depth = w_in.shape[0]

    flat2 = lambda a: a.reshape(-1, a.shape[-1])
    own_shards = lambda i: [w_out[i].astype(BF16), w_up[i].astype(BF16), w_down[i].astype(BF16)]
    cw_hi, cw_lo = _split16(flat2(conv_w))
    fw_hi, fw_lo = _split16(flat2(ffn_conv_w))
    g_in, g_cwh, g_cwl, g_fwh, g_fwl = _exchange([w_in[0].astype(BF16), cw_hi, cw_lo, fw_hi, fw_lo], gather=True,
                                                 name="gather_weights")
    full_conv_w = _blocks_to_cols(g_cwh.astype(F32) + g_cwl.astype(F32), depth)
    full_ffn_cw = _blocks_to_cols(g_fwh.astype(F32) + g_fwl.astype(F32), depth)

    rel = jnp.arange(KEY_SPAN)[None, :] - CHUNK - jnp.arange(CHUNK)[:, None]
    bucket = _t5_bucket(rel)
    band_bias = _band_bias(rel_bias, bucket)

    def layer_params(i, g_in):
        return dict(n1=norm1_w[i][None], w_in=_to_proj_layout(_blocks_to_cols(g_in, 1)[0]), conv_w=full_conv_w[i],
                    conv_b=conv_b[i][None], dtb=_pad_lanes(dt_bias[i].reshape(-1)), alog=_pad_lanes(a_log[i].reshape(-1)),
                    dvec=jnp.repeat(d_skip[i], HEAD_DIM)[None], ssm_nw=ssm_norm_w[i][None], sink=_pad_lanes(attn_sink[i]),
                    n2=norm2_w[i][None], ffn_cw=full_ffn_cw[i], ffn_cb=ffn_conv_b[i][None])

    h = x.reshape(t, D_MODEL)
    params, saved = [None] * depth, [None] * depth
    for i in range(depth):
        h, params[i], saved[i], nxt = _layer_fwd(h, layer_params(i, g_in), band_bias, nb=nb, seq=seq, own_shards=own_shards(i),
                                                 next_shards=[w_in[i + 1].astype(BF16)] if i + 1 < depth else ())
        if nxt:
            (g_in,) = nxt
    dh, g_final, loss_part = _loss_head(h, loss_target.reshape(t, D_MODEL), final_norm_w[None], tm=512)
    loss = lax.psum(loss_part[0, 0], ("x", "y", "c"))

    w_in_blocks = lambda g: _cols_to_blocks(_from_proj_layout(g["w_in"])[None], 1).astype(BF16)
    dbias = jnp.zeros((N_HEADS, CHUNK, KEY_SPAN), F32)
    grads, pending = [None] * depth, ()
    parts = dict(w_in=[None] * depth, w_out=[None] * depth, w_up=[None] * depth, w_down=[None] * depth)
    for i in reversed(range(depth)):
        dh, dbias, grads[i], arrived = _layer_bwd(dh, params[i], saved[i], band_bias, dbias, nb=nb, seq=seq, pending=pending)
        parts["w_down"][i], parts["w_up"][i], parts["w_out"][i] = arrived[:3]
        if pending:
            parts["w_in"][i + 1] = arrived[3]
        pending = [w_in_blocks(grads[i])]
    grad_x = dh.reshape(nb, seq, D_MODEL)
    stack = lambda k: jnp.stack([g[k] for g in grads])
    last = _exchange(pending + [_cols_to_blocks(stack("conv_w"), depth).astype(BF16),
                                _cols_to_blocks(stack("ffn_cw"), depth).astype(BF16)], gather=False, name="scatter_grads")
    parts["w_in"][0] = last[0]

    out = {}
    for k in ("w_in", "w_out", "w_up", "w_down"):
        rows = parts[k][0].shape[1]
        tr = max(d for d in range(16, 129, 16) if rows % d == 0)
        res = _adamw(parts[k], flat2(w[k]), flat2(m[k]), flat2(v[k]), name="adamw_" + k, tr=tr)
        out[k] = [a.reshape(w[k].shape) for a in res]
    for k, p8 in zip(("conv_w", "ffn_conv_w"), last[1:]):
        res = _adamw([p8], flat2(w[k]), flat2(m[k]), flat2(v[k]), name="adamw_" + k, tr=p8.shape[1])
        out[k] = [a.reshape(w[k].shape) for a in res]

    small = dict(rel_bias=_rel_bias_grad(dbias, bucket)[:, :REL_BUCKETS].T, norm1_w=stack("n1"), conv_b=stack("conv_b"),
                 dt_bias=stack("dtb")[:, 0, :2 * N_HEADS], a_log=stack("alog")[:, 0, :2 * N_HEADS],
                 d_skip=stack("dvec").reshape(depth, N_HEADS, HEAD_DIM).sum(-1), ssm_norm_w=stack("ssm_nw"),
                 attn_sink=stack("sink")[:, 0, :N_HEADS], norm2_w=stack("n2"), ffn_conv_b=stack("ffn_cb"),
                 final_norm_w=g_final)
    (small_parts,) = _exchange([_pack_small(small)], gather=True, name="gather_small_grads")
    res = _adamw([small_parts], _pack_small(w), _pack_small(m), _pack_small(v), name="adamw_small", tr=small_parts.shape[1])
    unpacked = [_unpack_small(a, w) for a in res]
    for k in _SMALL:
        out[k] = [u[k] for u in unpacked]

    return (loss, grad_x, *[out[k][0] for k in _ORDER], *[out[k][1] for k in _ORDER],
            *[out[k][2] for k in _ORDER], *[out[k][3] for k in _ORDER])
```

```python
import math

import numpy as np
import jax
import jax.numpy as jnp
from jax import lax
from jax.experimental import pallas as pl
from jax.experimental.pallas import tpu as pltpu

F32, BF16 = jnp.float32, jnp.bfloat16
HIGHEST = lax.Precision.HIGHEST

D_MODEL = 1024
HEAD_DIM = 64
N_HEADS = 16
N_GROUPS = 2
HEADS_PER_GROUP = N_HEADS // N_GROUPS
N_STATE = 128
SSM_WIDTH = 1024
BC_WIDTH = 256
CONV_CH = SSM_WIDTH + 2 * BC_WIDTH
SSM_TAPS = 7
CHUNK = 128
KV_HEADS = 4
KV_WIDTH = 256
Q_PER_KV = N_HEADS // KV_HEADS
KEY_SPAN = 3 * CHUNK
REL_BUCKETS = 32
D_FF = 2816
FFN_TAPS = 3
IN_COLS = 4128
NORM_EPS = 1e-6
N_DEV = 8

LANES = 128
SUBLANES = 8
VMEM_LIMIT_BYTES = 56 * 1024 * 1024
MM_ROWS = 1024

PZ, PXS, PQ, PB, PC, PK, PV, PDT, PROJ_W = 0, 1024, 2048, 3072, 3328, 3584, 3840, 4096, 4224
OZ, OXBC, ODT, OQ, OK_, OV = 0, 1024, 2560, 2592, 3616, 3872

ADAM_LR, ADAM_B1, ADAM_B2, ADAM_EPS, ADAM_WD, ADAM_STEP = 0.001, 0.9, 0.999, 1e-08, 0.01, 10


def _params(*sem):
    return pltpu.CompilerParams(dimension_semantics=sem, vmem_limit_bytes=VMEM_LIMIT_BYTES)


def _sigmoid(x):
    return 0.5 * jnp.tanh(0.5 * x) + 0.5


def _softplus(x):
    return jnp.maximum(x, 0.0) + jnp.log(1.0 + jnp.exp(-jnp.abs(x)))


def _dot(a, b, dims):
    return lax.dot_general(a, b, (dims, ((), ())), preferred_element_type=F32)


NN = ((1,), (0,))
NT = ((1,), (1,))
TN = ((0,), (0,))


def _matmul(a, b, mode, *, name, tm, tn, tk, res=None, out_dtype=F32, precision=None):
    if mode == "nn":
        (m, k), (k2, n) = a.shape, b.shape
        a_spec = pl.BlockSpec((tm, tk), lambda i, j, kk: (i, kk))
        b_spec = pl.BlockSpec((tk, tn), lambda i, j, kk: (kk, j))
        dims = NN
    elif mode == "nt":
        (m, k), (n, k2) = a.shape, b.shape
        a_spec = pl.BlockSpec((tm, tk), lambda i, j, kk: (i, kk))
        b_spec = pl.BlockSpec((tn, tk), lambda i, j, kk: (j, kk))
        dims = NT
    else:
        (k, m), (k2, n) = a.shape, b.shape
        a_spec = pl.BlockSpec((tk, tm), lambda i, j, kk: (kk, i))
        b_spec = pl.BlockSpec((tk, tn), lambda i, j, kk: (kk, j))
        dims = TN
    assert k == k2 and m % tm == 0 and n % tn == 0 and k % tk == 0, (name, a.shape, b.shape, tm, tn, tk)
    nk = k // tk
    has_res = res is not None

    def body(*refs):
        if has_res:
            a_ref, b_ref, r_ref, o_ref, acc = refs
        else:
            a_ref, b_ref, o_ref, acc = refs
        kk = pl.program_id(2)

        @pl.when(kk == 0)
        def _():
            acc[...] = jnp.zeros_like(acc)

        if precision is None:
            part = _dot(a_ref[...].astype(BF16), b_ref[...].astype(BF16), dims)
        else:
            part = lax.dot_general(a_ref[...], b_ref[...], (dims, ((), ())), precision=precision,
                                   preferred_element_type=F32)
        acc[...] += part

        @pl.when(kk == nk - 1)
        def _():
            r = acc[...]
            if has_res:
                r = r + r_ref[...].astype(F32)
            o_ref[...] = r.astype(out_dtype)

    in_specs = [a_spec, b_spec]
    args = [a, b]
    if has_res:
        in_specs.append(pl.BlockSpec((tm, tn), lambda i, j, kk: (i, j)))
        args.append(res)
    return pl.pallas_call(
        body, name=name, grid=(m // tm, n // tn, nk),
        in_specs=in_specs, out_specs=pl.BlockSpec((tm, tn), lambda i, j, kk: (i, j)),
        out_shape=jax.ShapeDtypeStruct((m, n), out_dtype),
        scratch_shapes=[pltpu.VMEM((tm, tn), F32)],
        compiler_params=_params("parallel", "parallel", "arbitrary"),
    )(*args)


def _rms_matmul(x, nw, w, *, name, tm, tn):
    t, d = x.shape
    n = w.shape[1]
    assert t % tm == 0 and n % tn == 0

    def body(x_ref, nw_ref, w_ref, o_ref, h_ref):
        @pl.when(pl.program_id(1) == 0)
        def _():
            xv = x_ref[...]
            r = lax.rsqrt(jnp.mean(xv * xv, axis=-1, keepdims=True) + NORM_EPS)
            h_ref[...] = (xv * r * nw_ref[...]).astype(BF16)

        o_ref[...] = _dot(h_ref[...], w_ref[...].astype(BF16), NN).astype(BF16)

    return pl.pallas_call(
        body, name=name, grid=(t // tm, n // tn),
        in_specs=[pl.BlockSpec((tm, d), lambda i, j: (i, 0)),
                  pl.BlockSpec((1, d), lambda i, j: (0, 0)),
                  pl.BlockSpec((d, tn), lambda i, j: (0, j))],
        out_specs=[pl.BlockSpec((tm, tn), lambda i, j: (i, j)),
                   pl.BlockSpec((tm, d), lambda i, j: (i, 0))],
        out_shape=[jax.ShapeDtypeStruct((t, n), BF16), jax.ShapeDtypeStruct((t, d), BF16)],
        compiler_params=_params("parallel", "arbitrary"),
    )(x, nw, w)


def _zero_ext(v):
    z = jnp.zeros((SUBLANES, v.shape[1]), v.dtype)
    return jnp.concatenate([z, v, z], axis=0)


def _shifted(v_ext, offset, seq):
    if offset == 0:
        return v_ext[SUBLANES:SUBLANES + seq]
    return pltpu.roll(v_ext, (-offset) % (seq + 2 * SUBLANES), 0)[SUBLANES:SUBLANES + seq]


def _conv_taps(v, w_ref, taps, seq):
    pad = taps // 2
    v_ext = _zero_ext(v)
    acc = None
    for k in range(taps):
        term = _shifted(v_ext, k - pad, seq) * w_ref[k:k + 1, :]
        acc = term if acc is None else acc + term
    return acc


def _ssm_conv_fwd(proj, cw, cb, *, nb, seq):
    width = 512

    def body(x_ref, w_ref, b_ref, o_ref, g_ref):
        g = _conv_taps(x_ref[...].astype(F32), w_ref, SSM_TAPS, seq) + b_ref[...]
        o_ref[...] = g * _sigmoid(g)
        g_ref[...] = g.astype(BF16)

    def col(j):
        return jnp.where(j < 2, j + PXS // width, PB // width)

    return pl.pallas_call(
        body, name="ssm_conv_fwd", grid=(nb, CONV_CH // width),
        in_specs=[pl.BlockSpec((seq, width), lambda b, j: (b, col(j))),
                  pl.BlockSpec((SSM_TAPS, width), lambda b, j: (0, j)),
                  pl.BlockSpec((1, width), lambda b, j: (0, j))],
        out_specs=[pl.BlockSpec((seq, width), lambda b, j: (b, j))] * 2,
        out_shape=[jax.ShapeDtypeStruct((nb * seq, CONV_CH), F32), jax.ShapeDtypeStruct((nb * seq, CONV_CH), BF16)],
        compiler_params=_params("parallel", "parallel"),
    )(proj, cw, cb)


def _ffn_act_fwd(gu, cw, cb, *, nb, seq):
    width = 256
    nj = D_FF // width

    def body(g_ref, u_ref, w_ref, b_ref, o_ref, s_ref):
        g = _conv_taps(g_ref[...].astype(F32), w_ref, FFN_TAPS, seq) + b_ref[...]
        o_ref[...] = (g * _sigmoid(g) * u_ref[...].astype(F32)).astype(BF16)
        s_ref[...] = g.astype(BF16)

    return pl.pallas_call(
        body, name="ffn_act_fwd", grid=(nb, nj),
        in_specs=[pl.BlockSpec((seq, width), lambda b, j: (b, j)),
                  pl.BlockSpec((seq, width), lambda b, j: (b, j + nj)),
                  pl.BlockSpec((FFN_TAPS, width), lambda b, j: (0, j)),
                  pl.BlockSpec((1, width), lambda b, j: (0, j))],
        out_specs=[pl.BlockSpec((seq, width), lambda b, j: (b, j))] * 2,
        out_shape=[jax.ShapeDtypeStruct((nb * seq, D_FF), BF16)] * 2,
        compiler_params=_params("parallel", "parallel"),
    )(gu, gu, cw, cb)


def _scan_setup(d, dt_ref, dtb_ref, alog_ref, z_ref, zt_ref, *, lower_when_dir0, inclusive):
    is0 = d == 0
    dt_all = _softplus(dt_ref[...] + dtb_ref[...])
    adt_all = dt_all * (-jnp.exp(alog_ref[...]))
    li = lax.broadcasted_iota(jnp.int32, (CHUNK, CHUNK), 0)
    si = lax.broadcasted_iota(jnp.int32, (CHUNK, CHUNK), 1)
    lower = is0 if lower_when_dir0 else jnp.logical_not(is0)
    ahead = jnp.where(lower, li - si, si - li)
    mask = ahead >= 0
    tri = mask if inclusive else ahead > 0
    z_all = jnp.dot(tri.astype(F32), adt_all, precision=HIGHEST, preferred_element_type=F32)
    zt_all = z_all.T
    z_ref[...] = jnp.where(is0, z_all[:, 0:N_HEADS], z_all[:, N_HEADS:2 * N_HEADS])
    zt_ref[...] = jnp.where(is0, zt_all[0:N_HEADS, :], zt_all[N_HEADS:2 * N_HEADS, :])
    dt = jnp.where(is0, dt_all[:, 0:N_HEADS], dt_all[:, N_HEADS:2 * N_HEADS])
    adt = jnp.where(is0, adt_all[:, 0:N_HEADS], adt_all[:, N_HEADS:2 * N_HEADS])
    tot = jnp.sum(adt, axis=0, keepdims=True)
    return ahead, dt, tot, dt_all


def _chunk_index(nchunk, forward_when_dir0):
    def idx(d, b, c):
        fwd = (d == 0) if forward_when_dir0 else (d != 0)
        return b * nchunk + jnp.where(fwd, c, nchunk - 1 - c)
    return idx


def _ssd_fwd(xbc, dtraw, dtb, alog, *, nb, seq, gather=()):
    nchunk = seq // CHUNK
    t = nb * seq
    row = _chunk_index(nchunk, True)
    ng = len(gather)

    def body(*refs):
        xs_ref, bc_ref, dt_ref, dtb_ref, alog_ref = refs[:5]
        o_ref, hs_ref = refs[5 + ng:7 + ng]
        h_ref, z_ref, zt_ref, dts_ref = refs[7 + 2 * ng:11 + 2 * ng]
        d, c = pl.program_id(0), pl.program_id(2)
        if ng:
            step = (d * nb + pl.program_id(1)) * nchunk + c
            copies = _exchange_copies(refs[5:5 + ng], refs[7 + ng:7 + 2 * ng], *refs[11 + 2 * ng:], gather=True)
            pl.when(step == 0)(lambda: _exchange_start(copies))

        @pl.when(c == 0)
        def _():
            h_ref[...] = jnp.zeros_like(h_ref)

        ahead, dt, tot, _ = _scan_setup(d, dt_ref, dtb_ref, alog_ref, z_ref, zt_ref,
                                        lower_when_dir0=True, inclusive=True)
        mask = ahead >= 0
        dts_ref[...] = dt
        e_tot = jnp.exp(tot)
        for g in range(N_GROUPS):
            heads = range(g * HEADS_PER_GROUP, (g + 1) * HEADS_PER_GROUP)
            bg = bc_ref[:, g * N_STATE:(g + 1) * N_STATE]
            cg = bc_ref[:, BC_WIDTH + g * N_STATE:BC_WIDTH + (g + 1) * N_STATE]
            cb = _dot(cg.astype(BF16), bg.astype(BF16), NT)
            zc = {h: jnp.broadcast_to(z_ref[:, h:h + 1], (CHUNK, CHUNK)) for h in heads}
            decay = {h: jnp.exp(jnp.where(mask, zc[h] - zt_ref[h:h + 1, :], -jnp.inf)) for h in heads}
            u = {h: (xs_ref[:, h * HEAD_DIM:(h + 1) * HEAD_DIM] * dts_ref[:, h:h + 1]).astype(BF16) for h in heads}
            state = {h: h_ref[h] for h in heads}
            for h in heads:
                hs_ref[0, 0, h] = state[h]
            mix = {h: (cb * decay[h]).astype(BF16) for h in heads}
            cz = {h: (cg * jnp.exp(zc[h])).astype(BF16) for h in heads}
            bw = {h: (bg * jnp.exp(tot[:, h:h + 1] - zc[h])).astype(BF16) for h in heads}
            y = {h: _dot(mix[h], u[h], NN) + _dot(cz[h], state[h].astype(BF16), NT) for h in heads}
            new = {h: state[h] * e_tot[:, h:h + 1] + _dot(u[h], bw[h], TN) for h in heads}
            for h in heads:
                h_ref[h] = new[h]
                o_ref[0, :, h * HEAD_DIM:(h + 1) * HEAD_DIM] = y[h]
        if ng:
            pl.when(step == 2 * nb * nchunk - 1)(lambda: _exchange_wait(copies))

    any_spec = pl.BlockSpec(memory_space=pl.ANY)
    res = pl.pallas_call(
        body, name="ssd_fwd_gather" if ng else "ssd_fwd", grid=(2, nb, nchunk),
        in_specs=[pl.BlockSpec((CHUNK, SSM_WIDTH), lambda d, b, c: (row(d, b, c), 0)),
                  pl.BlockSpec((CHUNK, 2 * BC_WIDTH), lambda d, b, c: (row(d, b, c), SSM_WIDTH // (2 * BC_WIDTH))),
                  pl.BlockSpec((CHUNK, LANES), lambda d, b, c: (row(d, b, c), 0)),
                  pl.BlockSpec((1, LANES), lambda d, b, c: (0, 0)),
                  pl.BlockSpec((1, LANES), lambda d, b, c: (0, 0))] + [any_spec] * ng,
        out_specs=[pl.BlockSpec((1, CHUNK, SSM_WIDTH), lambda d, b, c: (d, row(d, b, c), 0)),
                   pl.BlockSpec((1, 1, N_HEADS, HEAD_DIM, N_STATE), lambda d, b, c: (d, row(d, b, c), 0, 0, 0))]
        + [any_spec] * ng,
        out_shape=[jax.ShapeDtypeStruct((2, t, SSM_WIDTH), F32),
                   jax.ShapeDtypeStruct((2, nb * nchunk, N_HEADS, HEAD_DIM, N_STATE), F32)]
        + _exchange_out_shapes(gather, gather=True),
        scratch_shapes=_SSD_SCRATCH + _exchange_sems(ng),
        compiler_params=_params("arbitrary", "arbitrary", "arbitrary"),
    )(xbc, xbc, dtraw, dtb, alog, *gather)
    return res[0], res[1], list(res[2:])


def _gate_norm_fwd(y2, xbc, proj, dvec, nw, *, tm):
    t = xbc.shape[0]
    half = SSM_WIDTH // N_GROUPS

    def body(y_ref, xs_ref, z_ref, d_ref, w_ref, o_ref):
        z = z_ref[...].astype(F32)
        p = (y_ref[0] + y_ref[1] + d_ref[...] * xs_ref[...]) * (z * _sigmoid(z))
        for g in range(N_GROUPS):
            pg = p[:, g * half:(g + 1) * half]
            r = lax.rsqrt(jnp.mean(pg * pg, axis=-1, keepdims=True) + NORM_EPS)
            o_ref[:, g * half:(g + 1) * half] = (pg * r * w_ref[:, g * half:(g + 1) * half]).astype(BF16)

    return pl.pallas_call(
        body, name="gate_norm_fwd", grid=(t // tm,),
        in_specs=[pl.BlockSpec((2, tm, SSM_WIDTH), lambda i: (0, i, 0)),
                  pl.BlockSpec((tm, SSM_WIDTH), lambda i: (i, 0)),
                  pl.BlockSpec((tm, SSM_WIDTH), lambda i: (i, PZ // SSM_WIDTH)),
                  pl.BlockSpec((1, SSM_WIDTH), lambda i: (0, 0)),
                  pl.BlockSpec((1, SSM_WIDTH), lambda i: (0, 0))],
        out_specs=pl.BlockSpec((tm, SSM_WIDTH), lambda i: (i, 0)),
        out_shape=jax.ShapeDtypeStruct((t, SSM_WIDTH), BF16),
        compiler_params=_params("parallel"),
    )(y2, xbc, proj, dvec, nw)


GROUP_ROWS = Q_PER_KV * CHUNK


def _keys_inside(n, nblk):
    kpos = (n - 1) * CHUNK + lax.broadcasted_iota(jnp.int32, (1, KEY_SPAN), 1)
    return (kpos >= 0) & (kpos < nblk * CHUNK)


def _per_head_column(ref, g):
    blk = lax.broadcasted_iota(jnp.int32, (GROUP_ROWS, 1), 0) // CHUNK
    col = jnp.zeros((GROUP_ROWS, 1), F32)
    for r in range(Q_PER_KV):
        h = g * Q_PER_KV + r
        col = jnp.where(blk == r, ref[:, h:h + 1], col)
    return col


def _stack_heads(ref, g, dtype):
    return jnp.concatenate([ref[:, (g * Q_PER_KV + r) * HEAD_DIM:(g * Q_PER_KV + r + 1) * HEAD_DIM].astype(dtype)
                            for r in range(Q_PER_KV)], axis=0)


def _kv_specs(nblk):
    kvb = PK // (2 * KV_WIDTH)

    def at(off):
        def idx(b, n):
            return (b * nblk + jnp.clip(n + off, 0, nblk - 1), kvb)
        return pl.BlockSpec((CHUNK, 2 * KV_WIDTH), idx)
    return [at(-1), at(0), at(1)]


def _attn_fwd(proj, bias, sink, *, nb, seq, gather=()):
    nblk = seq // CHUNK
    t = nb * seq
    scale = HEAD_DIM ** -0.5
    ng = len(gather)

    def body(*refs):
        q_ref, kp_ref, kc_ref, kn_ref, bias_ref, sink_ref = refs[:6]
        o_ref, lse_ref = refs[6 + ng:8 + ng]
        n = pl.program_id(1)
        if ng:
            step = pl.program_id(0) * nblk + n
            copies = _exchange_copies(refs[6:6 + ng], refs[8 + ng:8 + 2 * ng], *refs[8 + 2 * ng:], gather=True)
            pl.when(step == 0)(lambda: _exchange_start(copies))
        inside = _keys_inside(n, nblk)
        groups = range(KV_HEADS)

        def keys(g, off):
            cs = slice(off + g * HEAD_DIM, off + (g + 1) * HEAD_DIM)
            return jnp.concatenate([kp_ref[:, cs], kc_ref[:, cs], kn_ref[:, cs]], axis=0).astype(BF16)

        qs = [(_stack_heads(q_ref, g, F32) * scale).astype(BF16) for g in groups]
        ss = [jnp.where(inside, _dot(qs[g], keys(g, 0), NT)
                        + bias_ref[g * Q_PER_KV:(g + 1) * Q_PER_KV].reshape(GROUP_ROWS, KEY_SPAN), -jnp.inf) for g in groups]
        sks = [_per_head_column(sink_ref, g) for g in groups]
        ms = [jnp.maximum(jnp.max(ss[g], axis=-1, keepdims=True), sks[g]) for g in groups]
        ps = [jnp.exp(ss[g] - ms[g]) for g in groups]
        denoms = [jnp.sum(ps[g], axis=-1, keepdims=True) + jnp.exp(sks[g] - ms[g]) for g in groups]
        outs = [(_dot(ps[g].astype(BF16), keys(g, KV_WIDTH), NN) * (1.0 / denoms[g])).astype(BF16) for g in groups]
        lses = []
        for g in groups:
            lse = ms[g] + jnp.log(denoms[g])
            for r in range(Q_PER_KV):
                h = g * Q_PER_KV + r
                o_ref[:, h * HEAD_DIM:(h + 1) * HEAD_DIM] = outs[g][r * CHUNK:(r + 1) * CHUNK]
                lses.append(lse[r * CHUNK:(r + 1) * CHUNK])
        lse_ref[...] = jnp.concatenate(lses, axis=1)
        if ng:
            pl.when(step == nb * nblk - 1)(lambda: _exchange_wait(copies))

    any_spec = pl.BlockSpec(memory_space=pl.ANY)
    res = pl.pallas_call(
        body, name="attn_fwd_gather" if ng else "attn_fwd", grid=(nb, nblk),
        in_specs=[pl.BlockSpec((CHUNK, D_MODEL), lambda b, n: (b * nblk + n, PQ // D_MODEL))] + _kv_specs(nblk) + [
            pl.BlockSpec((N_HEADS, CHUNK, KEY_SPAN), lambda b, n: (0, 0, 0)),
            pl.BlockSpec((1, LANES), lambda b, n: (0, 0))] + [any_spec] * ng,
        out_specs=[pl.BlockSpec((CHUNK, D_MODEL), lambda b, n: (b * nblk + n, 0)),
                   pl.BlockSpec((CHUNK, N_HEADS), lambda b, n: (b * nblk + n, 0))] + [any_spec] * ng,
        out_shape=[jax.ShapeDtypeStruct((t, D_MODEL), BF16), jax.ShapeDtypeStruct((t, N_HEADS), F32)]
        + _exchange_out_shapes(gather, gather=True),
        scratch_shapes=_exchange_sems(ng),
        compiler_params=_params("arbitrary", "arbitrary"),
    )(proj, proj, proj, proj, bias, sink, *gather)
    return res[0], res[1], list(res[2:])


def _loss_head(x, tgt, nw, *, tm):
    t, d = x.shape

    def body(x_ref, t_ref, w_ref, dx_ref, dw_ref, l_ref):
        @pl.when(pl.program_id(0) == 0)
        def _():
            dw_ref[...] = jnp.zeros_like(dw_ref)
            l_ref[...] = jnp.zeros_like(l_ref)

        xv = x_ref[...]
        w = w_ref[...]
        r = lax.rsqrt(jnp.mean(xv * xv, axis=-1, keepdims=True) + NORM_EPS)
        xh = xv * r
        err = xh * w - t_ref[...]
        l_ref[...] += jnp.sum(err * err) * (0.5 / d)
        dy = err * (1.0 / d)
        gw = dy * w
        dx_ref[...] = r * (gw - xh * jnp.mean(gw * xh, axis=-1, keepdims=True))
        dw_ref[...] += jnp.sum(dy * xh, axis=0, keepdims=True)

    return pl.pallas_call(
        body, name="loss_head", grid=(t // tm,),
        in_specs=[pl.BlockSpec((tm, d), lambda i: (i, 0)), pl.BlockSpec((tm, d), lambda i: (i, 0)),
                  pl.BlockSpec((1, d), lambda i: (0, 0))],
        out_specs=[pl.BlockSpec((tm, d), lambda i: (i, 0)), pl.BlockSpec((1, d), lambda i: (0, 0)),
                   pl.BlockSpec((1, LANES), lambda i: (0, 0))],
        out_shape=[jax.ShapeDtypeStruct((t, d), F32), jax.ShapeDtypeStruct((1, d), F32),
                   jax.ShapeDtypeStruct((1, LANES), F32)],
        compiler_params=_params("arbitrary"),
    )(x, tgt, nw)


def _to_proj_layout(w):
    pad = jnp.zeros(w.shape[:-1] + (PROJ_W - IN_COLS,), w.dtype)
    return jnp.concatenate([w[..., OZ:OXBC], w[..., OXBC:OXBC + SSM_WIDTH], w[..., OQ:OK_],
                            w[..., OXBC + SSM_WIDTH:ODT], w[..., OK_:IN_COLS], w[..., ODT:OQ], pad], axis=-1)


def _from_proj_layout(g):
    return jnp.concatenate([g[..., PZ:PZ + 2 * SSM_WIDTH], g[..., PB:PB + 2 * BC_WIDTH], g[..., PDT:PDT + 2 * N_HEADS],
                            g[..., PQ:PQ + D_MODEL], g[..., PK:PK + 2 * KV_WIDTH]], axis=-1)


def _pad_lanes(v):
    return jnp.pad(v.reshape(1, -1), ((0, 0), (0, LANES - v.size)))


def _layer_fwd(x, p, band_bias, *, nb, seq, own_shards, next_shards=()):
    w_main, w_dt = p["w_in"][:, :PDT], p["w_in"][:, PDT:]
    proj, h1 = _rms_matmul(x, p["n1"], w_main, name="in_proj", tm=MM_ROWS, tn=1024)
    dtraw = _matmul(h1, w_dt, "nn", name="in_proj_dt", tm=MM_ROWS, tn=LANES, tk=D_MODEL)
    xbc, gconv = _ssm_conv_fwd(proj, p["conv_w"], p["conv_b"], nb=nb, seq=seq)
    y2, states, (g_out, g_up, g_down) = _ssd_fwd(xbc, dtraw, p["dtb"], p["alog"], nb=nb, seq=seq, gather=own_shards)
    p = dict(p, w_out=_blocks_to_rows(g_out, 1)[0], w_up=_blocks_to_cols(g_up, 1)[0], w_down=_blocks_to_rows(g_down, 1)[0])
    y_ssm = _gate_norm_fwd(y2, xbc, proj, p["dvec"], p["ssm_nw"], tm=256)
    y_att, lse, gathered = _attn_fwd(proj, band_bias, p["sink"], nb=nb, seq=seq, gather=next_shards)
    x1 = _matmul(y_ssm, p["w_out"][:SSM_WIDTH], "nn", name="out_proj_ssm", tm=MM_ROWS, tn=1024, tk=1024, res=x)
    x1 = _matmul(y_att, p["w_out"][SSM_WIDTH:], "nn", name="out_proj_att", tm=MM_ROWS, tn=1024, tk=1024, res=x1)
    gu, h2 = _rms_matmul(x1, p["n2"], p["w_up"], name="up_proj", tm=MM_ROWS, tn=1408)
    act, fconv = _ffn_act_fwd(gu, p["ffn_cw"], p["ffn_cb"], nb=nb, seq=seq)
    x2 = _matmul(act, p["w_down"], "nn", name="down_proj", tm=MM_ROWS, tn=1024, tk=1408, res=x1)
    saved = dict(x=x, proj=proj, dtraw=dtraw, h1=h1, xbc=xbc, gconv=gconv, y2=y2, states=states, y_ssm=y_ssm, y_att=y_att,
                 lse=lse, x1=x1, gu=gu, fconv=fconv, h2=h2, act=act)
    return x2, p, saved, gathered


def _rms_bwd(x, dh, nw, dres, *, tm, name):
    t, d = x.shape

    def body(x_ref, dh_ref, w_ref, r_ref, dx_ref, dw_ref):
        @pl.when(pl.program_id(0) == 0)
        def _():
            dw_ref[...] = jnp.zeros_like(dw_ref)

        xv = x_ref[...]
        dh_v = dh_ref[...].astype(F32)
        r = lax.rsqrt(jnp.mean(xv * xv, axis=-1, keepdims=True) + NORM_EPS)
        xh = xv * r
        gw = dh_v * w_ref[...]
        dx_ref[...] = r_ref[...] + r * (gw - xh * jnp.mean(gw * xh, axis=-1, keepdims=True))
        dw_ref[...] += jnp.sum(dh_v * xh, axis=0, keepdims=True)

    row = pl.BlockSpec((tm, d), lambda i: (i, 0))
    vec = pl.BlockSpec((1, d), lambda i: (0, 0))
    return pl.pallas_call(
        body, name=name, grid=(t // tm,), in_specs=[row, row, vec, row], out_specs=[row, vec],
        out_shape=[jax.ShapeDtypeStruct((t, d), F32), jax.ShapeDtypeStruct((1, d), F32)],
        compiler_params=_params("arbitrary"),
    )(x, dh, nw, dres)


def _dsilu(g, sg):
    return sg * (1.0 + g * (1.0 - sg))


def _conv_taps_bwd(gpre, dg, w_ref, dwb_ref, taps, seq):
    pad = taps // 2
    dg_ext, gpre_ext = _zero_ext(dg), _zero_ext(gpre)
    dpre = None
    for k in range(taps):
        term = _shifted(dg_ext, pad - k, seq) * w_ref[k:k + 1, :]
        dpre = term if dpre is None else dpre + term
        dwb_ref[k:k + 1, :] += jnp.sum(dg * _shifted(gpre_ext, k - pad, seq), axis=0, keepdims=True)
    dwb_ref[SUBLANES - 1:SUBLANES, :] += jnp.sum(dg, axis=0, keepdims=True)
    return dpre


def _ffn_act_bwd(gu, gconv, dact, cw, *, nb, seq):
    width = 256
    nj = D_FF // width

    def body(g_ref, u_ref, s_ref, da_ref, w_ref, dg_ref, du_ref, dwb_ref):
        @pl.when(pl.program_id(1) == 0)
        def _():
            dwb_ref[...] = jnp.zeros_like(dwb_ref)

        g = s_ref[...].astype(F32)
        sg = _sigmoid(g)
        da = da_ref[...].astype(F32)
        du_ref[...] = (da * g * sg).astype(BF16)
        dgc = da * u_ref[...].astype(F32) * _dsilu(g, sg)
        dg_ref[...] = _conv_taps_bwd(g_ref[...].astype(F32), dgc, w_ref, dwb_ref, FFN_TAPS, seq).astype(BF16)

    blk = lambda off: pl.BlockSpec((seq, width), lambda j, b: (b, j + off))
    return pl.pallas_call(
        body, name="ffn_act_bwd", grid=(nj, nb),
        in_specs=[blk(0), blk(nj), blk(0), blk(0), pl.BlockSpec((FFN_TAPS, width), lambda j, b: (0, j))],
        out_specs=[blk(0), blk(0), pl.BlockSpec((SUBLANES, width), lambda j, b: (0, j))],
        out_shape=[jax.ShapeDtypeStruct((nb * seq, D_FF), BF16), jax.ShapeDtypeStruct((nb * seq, D_FF), BF16),
                   jax.ShapeDtypeStruct((SUBLANES, D_FF), F32)],
        compiler_params=_params("parallel", "arbitrary"),
    )(gu, gu, gconv, dact, cw)


def _ssm_conv_bwd(proj, gconv, pair, cw, *, nb, seq, name, width, proj_col, conv_col, ncol, extra=None, scale=None):
    has_extra = extra is not None

    def body(*refs):
        if has_extra:
            x_ref, g_ref, p_ref, w_ref, e_ref, s_ref, dx_ref, dwb_ref = refs
        else:
            x_ref, g_ref, p_ref, w_ref, dx_ref, dwb_ref = refs

        @pl.when(pl.program_id(1) == 0)
        def _():
            dwb_ref[...] = jnp.zeros_like(dwb_ref)

        g = g_ref[...].astype(F32)
        da = p_ref[0] + p_ref[1]
        if has_extra:
            da = da + e_ref[...] * s_ref[...]
        dx_ref[...] = _conv_taps_bwd(x_ref[...].astype(F32), da * _dsilu(g, _sigmoid(g)), w_ref, dwb_ref, SSM_TAPS, seq)

    in_specs = [pl.BlockSpec((seq, width), lambda j, b: (b, j + proj_col)),
                pl.BlockSpec((seq, width), lambda j, b: (b, j + conv_col)),
                pl.BlockSpec((2, seq, width), lambda j, b: (0, b, j)),
                pl.BlockSpec((SSM_TAPS, width), lambda j, b: (0, j + conv_col))]
    args = [proj, gconv, pair, cw]
    if has_extra:
        in_specs += [pl.BlockSpec((seq, width), lambda j, b: (b, j)), pl.BlockSpec((1, width), lambda j, b: (0, j))]
        args += [extra, scale]
    return pl.pallas_call(
        body, name=name, grid=(ncol, nb), in_specs=in_specs,
        out_specs=[pl.BlockSpec((seq, width), lambda j, b: (b, j)), pl.BlockSpec((SUBLANES, width), lambda j, b: (0, j))],
        out_shape=[jax.ShapeDtypeStruct((nb * seq, ncol * width), F32), jax.ShapeDtypeStruct((SUBLANES, ncol * width), F32)],
        compiler_params=_params("parallel", "arbitrary"),
    )(*args)


def _attn_bwd(proj, dmix, y_att, lse, bias, sink, dbias_in, *, nb, seq):
    nblk = seq // CHUNK
    t = nb * seq
    scale = HEAD_DIM ** -0.5

    def body(q_ref, kp_ref, kc_ref, kn_ref, do_ref, o_ref, lse_ref, bias_ref, sink_ref, dbin_ref,
             dq_ref, dkv_ref, dbias_ref, dsink_ref):
        b, n = pl.program_id(0), pl.program_id(1)

        @pl.when(n == 0)
        def _():
            dkv_ref[...] = jnp.zeros_like(dkv_ref)

        @pl.when((n == 0) & (b == 0))
        def _():
            dbias_ref[...] = dbin_ref[...]
            dsink_ref[...] = jnp.zeros_like(dsink_ref)

        inside = _keys_inside(n, nblk)
        lane = lax.broadcasted_iota(jnp.int32, (1, LANES), 1)
        dsink = jnp.zeros((1, LANES), F32)
        rows = pl.ds(pl.multiple_of(n * CHUNK, CHUNK), KEY_SPAN)
        groups = range(KV_HEADS)

        def keys(g, off):
            cs = slice(off + g * HEAD_DIM, off + (g + 1) * HEAD_DIM)
            return jnp.concatenate([kp_ref[:, cs], kc_ref[:, cs], kn_ref[:, cs]], axis=0).astype(BF16)

        kcat = [keys(g, 0) for g in groups]
        vcat = [keys(g, KV_WIDTH) for g in groups]
        q = [(_stack_heads(q_ref, g, F32) * scale).astype(BF16) for g in groups]
        do = [_stack_heads(do_ref, g, F32) for g in groups]
        do16 = [do[g].astype(BF16) for g in groups]
        lse = [jnp.concatenate([lse_ref[:, g * Q_PER_KV + r:g * Q_PER_KV + r + 1] for r in range(Q_PER_KV)], axis=0)
               for g in groups]
        s = [jnp.where(inside, _dot(q[g], kcat[g], NT)
                       + bias_ref[g * Q_PER_KV:(g + 1) * Q_PER_KV].reshape(GROUP_ROWS, KEY_SPAN), -jnp.inf) for g in groups]
        p = [jnp.exp(s[g] - lse[g]) for g in groups]
        delta = [jnp.sum(do[g] * _stack_heads(o_ref, g, F32), axis=-1, keepdims=True) for g in groups]
        ds = [p[g] * (_dot(do16[g], vcat[g], NT) - delta[g]) for g in groups]
        ds16 = [ds[g].astype(BF16) for g in groups]
        sink_part = [jnp.exp(_per_head_column(sink_ref, g) - lse[g]) * delta[g] for g in groups]
        dq = [_dot(ds16[g], kcat[g], NN) * scale for g in groups]
        dk = [_dot(ds16[g], q[g], TN) for g in groups]
        dv = [_dot(p[g].astype(BF16), do16[g], TN) for g in groups]
        for g in groups:
            dbias_ref[g * Q_PER_KV:(g + 1) * Q_PER_KV] += ds[g].reshape(Q_PER_KV, CHUNK, KEY_SPAN)
            for r in range(Q_PER_KV):
                h = g * Q_PER_KV + r
                dq_ref[:, h * HEAD_DIM:(h + 1) * HEAD_DIM] = dq[g][r * CHUNK:(r + 1) * CHUNK]
                dsink = dsink - jnp.where(lane == h, jnp.sum(sink_part[g][r * CHUNK:(r + 1) * CHUNK], axis=0, keepdims=True), 0.0)
            dkv_ref[0, rows, g * HEAD_DIM:(g + 1) * HEAD_DIM] += dk[g]
            dkv_ref[0, rows, KV_WIDTH + g * HEAD_DIM:KV_WIDTH + (g + 1) * HEAD_DIM] += dv[g]
        dsink_ref[...] += dsink

    blk = lambda cb: pl.BlockSpec((CHUNK, D_MODEL), lambda b, n: (b * nblk + n, cb))
    whole = pl.BlockSpec((N_HEADS, CHUNK, KEY_SPAN), lambda b, n: (0, 0, 0))
    vec = pl.BlockSpec((1, LANES), lambda b, n: (0, 0))
    return pl.pallas_call(
        body, name="attn_bwd", grid=(nb, nblk),
        in_specs=[blk(PQ // D_MODEL)] + _kv_specs(nblk) + [
            blk(1), blk(0), pl.BlockSpec((CHUNK, N_HEADS), lambda b, n: (b * nblk + n, 0)), whole, vec, whole],
        out_specs=[blk(0), pl.BlockSpec((1, seq + 2 * CHUNK, 2 * KV_WIDTH), lambda b, n: (b, 0, 0)), whole, vec],
        out_shape=[jax.ShapeDtypeStruct((t, D_MODEL), F32),
                   jax.ShapeDtypeStruct((nb, seq + 2 * CHUNK, 2 * KV_WIDTH), F32),
                   jax.ShapeDtypeStruct((N_HEADS, CHUNK, KEY_SPAN), F32),
                   jax.ShapeDtypeStruct((1, LANES), F32)],
        compiler_params=_params("arbitrary", "arbitrary"),
    )(proj, proj, proj, proj, dmix, y_att, lse, bias, sink, dbias_in)


def _gate_norm_bwd(y2, xbc, proj, dmix, dvec, nw, *, tm):
    t = xbc.shape[0]
    half = SSM_WIDTH // N_GROUPS

    def body(y_ref, xs_ref, z_ref, do_ref, d_ref, w_ref, dyv_ref, dz_ref, dd_ref, dw_ref):
        @pl.when(pl.program_id(0) == 0)
        def _():
            dd_ref[...] = jnp.zeros_like(dd_ref)
            dw_ref[...] = jnp.zeros_like(dw_ref)

        z = z_ref[...].astype(F32)
        xs = xs_ref[...]
        sg = _sigmoid(z)
        gz = z * sg
        yv = y_ref[0] + y_ref[1] + d_ref[...] * xs
        p = yv * gz
        do = do_ref[...]
        for g in range(N_GROUPS):
            cs = slice(g * half, (g + 1) * half)
            pg = p[:, cs]
            r = lax.rsqrt(jnp.mean(pg * pg, axis=-1, keepdims=True) + NORM_EPS)
            ph = pg * r
            gw = do[:, cs] * w_ref[:, cs]
            dp = r * (gw - ph * jnp.mean(gw * ph, axis=-1, keepdims=True))
            dyv = dp * gz[:, cs]
            dyv_ref[:, cs] = dyv
            dz_ref[:, cs] = dp * yv[:, cs] * _dsilu(z[:, cs], sg[:, cs])
            dw_ref[:, cs] += jnp.sum(do[:, cs] * ph, axis=0, keepdims=True)
            dd_ref[:, cs] += jnp.sum(dyv * xs[:, cs], axis=0, keepdims=True)

    row = lambda cb: pl.BlockSpec((tm, SSM_WIDTH), lambda i: (i, cb))
    vec = pl.BlockSpec((1, SSM_WIDTH), lambda i: (0, 0))
    return pl.pallas_call(
        body, name="gate_norm_bwd", grid=(t // tm,),
        in_specs=[pl.BlockSpec((2, tm, SSM_WIDTH), lambda i: (0, i, 0)), row(0), row(PZ // SSM_WIDTH), row(0), vec, vec],
        out_specs=[row(0), row(0), vec, vec],
        out_shape=[jax.ShapeDtypeStruct((t, SSM_WIDTH), F32), jax.ShapeDtypeStruct((t, SSM_WIDTH), F32),
                   jax.ShapeDtypeStruct((1, SSM_WIDTH), F32), jax.ShapeDtypeStruct((1, SSM_WIDTH), F32)],
        compiler_params=_params("arbitrary"),
    )(y2, xbc, proj, dmix, dvec, nw)


def _ssd_specs(nchunk, row):
    return [pl.BlockSpec((CHUNK, SSM_WIDTH), lambda d, b, c: (row(d, b, c), 0)),
            pl.BlockSpec((CHUNK, 2 * BC_WIDTH), lambda d, b, c: (row(d, b, c), SSM_WIDTH // (2 * BC_WIDTH))),
            pl.BlockSpec((CHUNK, LANES), lambda d, b, c: (row(d, b, c), 0)),
            pl.BlockSpec((1, LANES), lambda d, b, c: (0, 0)),
            pl.BlockSpec((1, LANES), lambda d, b, c: (0, 0)),
            pl.BlockSpec((CHUNK, SSM_WIDTH), lambda d, b, c: (row(d, b, c), 0))]


_SSD_SCRATCH = [pltpu.VMEM((N_HEADS, HEAD_DIM, N_STATE), F32),
                pltpu.VMEM((CHUNK, N_HEADS), F32), pltpu.VMEM((N_HEADS, CHUNK), F32),
                pltpu.VMEM((CHUNK, N_HEADS), F32)]


def _ssd_bwd(xbc, dtraw, dtb, alog, dyv, states, *, nb, seq, scatter=()):
    nchunk = seq // CHUNK
    t = nb * seq
    row = _chunk_index(nchunk, False)
    ns = len(scatter)

    def body(*refs):
        xs_ref, bc_ref, dt_ref, dtb_ref, alog_ref, dy_ref, hs_ref = refs[:7]
        dx_ref, db_ref, dc_ref, draw_ref, da_ref, dbias_ref = refs[7 + ns:13 + ns]
        h_ref, z_ref, zt_ref, dts_ref, acc_ref, span_ref = refs[13 + 2 * ns:19 + 2 * ns]
        d, b, c = pl.program_id(0), pl.program_id(1), pl.program_id(2)
        if ns:
            step = (d * nb + b) * nchunk + c
            copies = _exchange_copies(refs[7:7 + ns], refs[13 + ns:13 + 2 * ns], *refs[19 + 2 * ns:], gather=False)
            pl.when(step == 0)(lambda: _exchange_start(copies))

        @pl.when(c == 0)
        def _():
            h_ref[...] = jnp.zeros_like(h_ref)

        @pl.when((c == 0) & (b == 0) & (d == 0))
        def _():
            da_ref[...] = jnp.zeros_like(da_ref)
            dbias_ref[...] = jnp.zeros_like(dbias_ref)

        ahead, dt, tot, dt_all = _scan_setup(d, dt_ref, dtb_ref, alog_ref, z_ref, zt_ref,
                                             lower_when_dir0=False, inclusive=False)
        mask = ahead >= 0
        dts_ref[...] = dt
        e_tot = jnp.exp(tot)
        acc_ref[...] = jnp.zeros_like(acc_ref)
        span_ref[...] = jnp.zeros_like(span_ref)
        for g in range(N_GROUPS):
            bg = bc_ref[:, g * N_STATE:(g + 1) * N_STATE]
            cg = bc_ref[:, BC_WIDTH + g * N_STATE:BC_WIDTH + (g + 1) * N_STATE]
            bg16 = bg.astype(BF16)
            cg16 = cg.astype(BF16)
            bc_t = _dot(bg16, cg16, NT)
            heads = range(g * HEADS_PER_GROUP, (g + 1) * HEADS_PER_GROUP)
            cols = {h: slice(h * HEAD_DIM, (h + 1) * HEAD_DIM) for h in heads}
            zc = {h: jnp.broadcast_to(z_ref[:, h:h + 1], (CHUNK, CHUNK)) for h in heads}
            decay = {h: jnp.exp(jnp.where(mask, zc[h] - zt_ref[h:h + 1, :], -jnp.inf)) for h in heads}
            e_z = {h: jnp.exp(zc[h]) for h in heads}
            e_tz = {h: jnp.exp(tot[:, h:h + 1] - zc[h]) for h in heads}
            x_h = {h: xs_ref[:, cols[h]] for h in heads}
            dt_h = {h: dts_ref[:, h:h + 1] for h in heads}
            u = {h: (x_h[h] * dt_h[h]).astype(BF16) for h in heads}
            dy = {h: dy_ref[:, cols[h]].astype(BF16) for h in heads}
            state = {h: h_ref[h] for h in heads}
            st16 = {h: state[h].astype(BF16) for h in heads}
            fstate = {h: hs_ref[0, 0, h] for h in heads}
            mix = {h: (bc_t * decay[h]).astype(BF16) for h in heads}
            bz = {h: (bg * e_z[h]).astype(BF16) for h in heads}
            du = {h: _dot(mix[h], dy[h], NN) + _dot(bz[h], st16[h], NT) for h in heads}
            w2f = {h: _dot(u[h], dy[h], NT) * decay[h] for h in heads}
            w2 = {h: w2f[h].astype(BF16) for h in heads}
            db_out = {h: e_z[h] * _dot(u[h], st16[h], NN) for h in heads}
            dc_out = {h: e_tz[h] * _dot(dy[h], fstate[h].astype(BF16), NN) for h in heads}
            db_in = {h: _dot(w2[h], cg16, NN) for h in heads}
            dc_in = {h: _dot(w2[h], bg16, TN) for h in heads}
            pairs = {h: w2f[h] * bc_t for h in heads}
            col_in = {h: jnp.sum(pairs[h], axis=-1, keepdims=True) for h in heads}
            row_in = {h: jnp.sum(pairs[h], axis=0, keepdims=True) for h in heads}
            row_out = {h: jnp.sum(dc_out[h] * cg, axis=-1, keepdims=True) for h in heads}
            col_out = {h: jnp.sum(db_out[h] * bg, axis=-1, keepdims=True) for h in heads}
            ddt_h = {h: jnp.sum(du[h] * x_h[h], axis=-1, keepdims=True) for h in heads}
            cz = {h: (cg * e_tz[h]).astype(BF16) for h in heads}
            new = {h: state[h] * e_tot[:, h:h + 1] + _dot(dy[h], cz[h], TN) for h in heads}
            dbg = jnp.zeros((CHUNK, N_STATE), F32)
            dcg = jnp.zeros((CHUNK, N_STATE), F32)
            for h in heads:
                dbg = dbg + db_in[h] + db_out[h]
                dcg = dcg + dc_in[h] + dc_out[h]
                acc_ref[0, :, h:h + 1] = row_out[h] - col_in[h]
                acc_ref[1, :, h:h + 1] = col_out[h]
                acc_ref[2, :, h:h + 1] = ddt_h[h]
                acc_ref[3, h:h + 1, :] = row_in[h]
                span_ref[0:1, h:h + 1] = e_tot[:, h:h + 1] * jnp.sum(fstate[h] * state[h], keepdims=True)
                dx_ref[0, :, cols[h]] = du[h] * dt_h[h]
                h_ref[h] = new[h]
            db_ref[0, :, g * N_STATE:(g + 1) * N_STATE] = dbg
            dc_ref[0, :, g * N_STATE:(g + 1) * N_STATE] = dcg
        tri = mask.astype(F32)
        at_dir = lambda v: jnp.where(d == 0, v, pltpu.roll(v, N_HEADS, 1))
        dadt = at_dir(jnp.dot(tri, acc_ref[0] + acc_ref[3].T, precision=HIGHEST, preferred_element_type=F32)
                      + jnp.dot(1.0 - tri, acc_ref[1], precision=HIGHEST, preferred_element_type=F32) + span_ref[0:1, :])
        ddt = at_dir(acc_ref[2])
        a = -jnp.exp(alog_ref[...])
        draw = (ddt + a * dadt) * _sigmoid(dt_ref[...] + dtb_ref[...])
        draw_ref[0] = draw
        da_ref[...] += jnp.sum(dt_all * dadt, axis=0, keepdims=True) * a
        dbias_ref[...] += jnp.sum(draw, axis=0, keepdims=True)
        if ns:
            pl.when(step == 2 * nb * nchunk - 1)(lambda: _exchange_wait(copies))

    out_row = lambda w: pl.BlockSpec((1, CHUNK, w), lambda d, b, c: (d, row(d, b, c), 0))
    vec = pl.BlockSpec((1, LANES), lambda d, b, c: (0, 0))
    any_spec = pl.BlockSpec(memory_space=pl.ANY)
    res = pl.pallas_call(
        body, name="ssd_bwd_scatter" if ns else "ssd_bwd", grid=(2, nb, nchunk),
        in_specs=_ssd_specs(nchunk, row) + [
            pl.BlockSpec((1, 1, N_HEADS, HEAD_DIM, N_STATE), lambda d, b, c: (d, row(d, b, c), 0, 0, 0))]
        + [any_spec] * ns,
        out_specs=[out_row(SSM_WIDTH), out_row(BC_WIDTH), out_row(BC_WIDTH), out_row(LANES), vec, vec] + [any_spec] * ns,
        out_shape=[jax.ShapeDtypeStruct((2, t, SSM_WIDTH), F32), jax.ShapeDtypeStruct((2, t, BC_WIDTH), F32),
                   jax.ShapeDtypeStruct((2, t, BC_WIDTH), F32),
                   jax.ShapeDtypeStruct((2, t, LANES), F32), jax.ShapeDtypeStruct((1, LANES), F32),
                   jax.ShapeDtypeStruct((1, LANES), F32)] + _exchange_out_shapes(scatter, gather=False),
        scratch_shapes=_SSD_SCRATCH + [pltpu.VMEM((4, CHUNK, LANES), F32), pltpu.VMEM((SUBLANES, LANES), F32)]
        + _exchange_sems(ns),
        compiler_params=_params("arbitrary", "arbitrary", "arbitrary"),
    )(xbc, xbc, dtraw, dtb, alog, dyv, states, *scatter)
    return tuple(res[:6]) + (list(res[6:]),)


def _layer_bwd(dx2, p, s, band_bias, dbias_in, *, nb, seq, pending=()):
    t = nb * seq
    x, proj, xbc, x1 = s["x"], s["proj"], s["xbc"], s["x1"]
    dact = _matmul(dx2, p["w_down"], "nt", name="down_proj_dx", tm=MM_ROWS, tn=1408, tk=1024, out_dtype=BF16)
    g_w_down = _matmul(s["act"], dx2, "tn", name="down_proj_dw", tm=1408, tn=1024, tk=1024)
    dg, du, dwb_ffn = _ffn_act_bwd(s["gu"], s["fconv"], dact, p["ffn_cw"], nb=nb, seq=seq)
    dh2 = _matmul(dg, p["w_up"][:, :D_FF], "nt", name="up_proj_dx_g", tm=MM_ROWS, tn=1024, tk=1408)
    dh2 = _matmul(du, p["w_up"][:, D_FF:], "nt", name="up_proj_dx_u", tm=MM_ROWS, tn=1024, tk=1408, res=dh2)
    g_w_up = jnp.concatenate([_matmul(s["h2"], dg, "tn", name="up_proj_dw_g", tm=1024, tn=1408, tk=1024),
                              _matmul(s["h2"], du, "tn", name="up_proj_dw_u", tm=1024, tn=1408, tk=1024)], axis=1)
    dx1, g_n2 = _rms_bwd(x1, dh2, p["n2"], dx2, tm=512, name="norm2_bwd")
    dmix = _matmul(dx1, p["w_out"], "nt", name="out_proj_dx", tm=MM_ROWS, tn=1024, tk=1024)
    g_w_out = jnp.concatenate([_matmul(s["y_ssm"], dx1, "tn", name="out_proj_dw_ssm", tm=1024, tn=1024, tk=1024),
                               _matmul(s["y_att"], dx1, "tn", name="out_proj_dw_att", tm=1024, tn=1024, tk=1024)], axis=0)
    dq, dkv, dbias, dsink = _attn_bwd(proj, dmix, s["y_att"], s["lse"], band_bias, p["sink"], dbias_in, nb=nb, seq=seq)
    dkv = dkv[:, CHUNK:CHUNK + seq, :].reshape(t, 2 * KV_WIDTH)
    dyv, dz, g_dvec, g_ssm_nw = _gate_norm_bwd(s["y2"], xbc, proj, dmix, p["dvec"], p["ssm_nw"], tm=256)
    own = [_rows_to_blocks(g_w_down[None], 1).astype(BF16), _cols_to_blocks(g_w_up[None], 1).astype(BF16),
           _rows_to_blocks(g_w_out[None], 1).astype(BF16)]
    dxs2, db2, dc2, draw2, g_alog, g_dtb, exchanged = _ssd_bwd(xbc, s["dtraw"], p["dtb"], p["alog"], dyv, s["states"],
                                                               nb=nb, seq=seq, scatter=own + list(pending))
    ddt_raw = draw2[0] + draw2[1]
    conv = dict(nb=nb, seq=seq)
    dxs_pre, dwb_xs = _ssm_conv_bwd(proj, s["gconv"], dxs2, p["conv_w"], name="ssm_conv_bwd_x", width=256,
                                    proj_col=PXS // 256, conv_col=0, ncol=4, extra=dyv, scale=p["dvec"], **conv)
    db_pre, dwb_b = _ssm_conv_bwd(proj, s["gconv"], db2, p["conv_w"], name="ssm_conv_bwd_b", width=256,
                                  proj_col=PB // 256, conv_col=SSM_WIDTH // 256, ncol=1, **conv)
    dc_pre, dwb_c = _ssm_conv_bwd(proj, s["gconv"], dc2, p["conv_w"], name="ssm_conv_bwd_c", width=256,
                                  proj_col=PC // 256, conv_col=(SSM_WIDTH + BC_WIDTH) // 256, ncol=1, **conv)
    dwb_ssm = jnp.concatenate([dwb_xs, dwb_b, dwb_c], axis=1)
    dproj = jnp.concatenate([dz, dxs_pre, dq, db_pre, dc_pre, dkv, ddt_raw], axis=1).astype(BF16)
    dh1 = _matmul(dproj, p["w_in"], "nt", name="in_proj_dx", tm=MM_ROWS, tn=1024, tk=1408)
    g_w_in = _matmul(s["h1"], dproj, "tn", name="in_proj_dw", tm=1024, tn=1408, tk=1024)
    dx, g_n1 = _rms_bwd(x, dh1, p["n1"], dx1, tm=512, name="norm1_bwd")
    grads = dict(n1=g_n1, w_in=g_w_in, conv_w=dwb_ssm[:SSM_TAPS], conv_b=dwb_ssm[SUBLANES - 1], dtb=g_dtb, alog=g_alog,
                 dvec=g_dvec, ssm_nw=g_ssm_nw, sink=dsink, w_out=g_w_out, n2=g_n2, w_up=g_w_up,
                 ffn_cw=dwb_ffn[:FFN_TAPS], ffn_cb=dwb_ffn[SUBLANES - 1], w_down=g_w_down)
    return dx, dbias, grads, exchanged


def _band_bias(rel_bias, bucket):
    def body(rb_ref, b_ref, o_ref):
        o_ref[...] = jnp.zeros_like(o_ref)

        def per_bucket(k, carry):
            hit = b_ref[...] == k
            for h in range(N_HEADS):
                o_ref[h] = jnp.where(hit, rb_ref[k, h], o_ref[h])
            return carry

        lax.fori_loop(0, REL_BUCKETS, per_bucket, 0)
        qi = lax.broadcasted_iota(jnp.int32, (CHUNK, KEY_SPAN), 0)
        kj = lax.broadcasted_iota(jnp.int32, (CHUNK, KEY_SPAN), 1)
        band = jnp.abs(kj - CHUNK - qi) <= CHUNK
        for h in range(N_HEADS):
            o_ref[h] = jnp.where(band, o_ref[h], -jnp.inf)

    return pl.pallas_call(
        body, name="band_bias", out_shape=jax.ShapeDtypeStruct((N_HEADS, CHUNK, KEY_SPAN), F32),
        in_specs=[pl.BlockSpec(memory_space=pltpu.SMEM), pl.BlockSpec(memory_space=pltpu.VMEM)],
        out_specs=pl.BlockSpec(memory_space=pltpu.VMEM),
    )(rel_bias, bucket)


def _rel_bias_grad(dbias, bucket):
    def body(d_ref, b_ref, o_ref):
        o_ref[...] = jnp.zeros_like(o_ref)
        lane = lax.broadcasted_iota(jnp.int32, (1, LANES), 1)

        def per_bucket(k, carry):
            hit = b_ref[...] == k
            for h in range(N_HEADS):
                part = jnp.sum(jnp.where(hit, d_ref[h], 0.0), axis=1, keepdims=True)
                o_ref[h:h + 1, :] += jnp.where(lane == k, jnp.sum(part, axis=0, keepdims=True), 0.0)
            return carry

        lax.fori_loop(0, REL_BUCKETS, per_bucket, 0)

    return pl.pallas_call(
        body, name="rel_bias_grad", out_shape=jax.ShapeDtypeStruct((N_HEADS, LANES), F32),
        compiler_params=pltpu.CompilerParams(vmem_limit_bytes=VMEM_LIMIT_BYTES),
    )(dbias, bucket)


N_PEER = N_DEV - 1


def _exchange_copies(ins, outs, send_sems, recv_sems, local_sems, *, gather):
    x, y, c = lax.axis_index("x"), lax.axis_index("y"), lax.axis_index("c")
    me = 4 * x + 2 * y + c
    peers = []
    for k in range(1, N_DEV):
        px, py, pc = x ^ ((k >> 2) & 1), y ^ ((k >> 1) & 1), c ^ (k & 1)
        peers.append(((px, py, pc), 4 * px + 2 * py + pc))
    local, sends, recvs = [], [], []
    for i in range(len(ins)):
        mine = ins[i] if gather else ins[i].at[me]
        local.append(pltpu.make_async_copy(mine, outs[i].at[me], local_sems.at[i]))
        for k, (pid, pslot) in enumerate(peers):
            src = ins[i] if gather else ins[i].at[pslot]
            sem = i * N_PEER + k
            sends.append(pltpu.make_async_remote_copy(
                src_ref=src, dst_ref=outs[i].at[me], send_sem=send_sems.at[sem], recv_sem=recv_sems.at[sem],
                device_id=pid, device_id_type=pl.DeviceIdType.MESH))
            recvs.append(pltpu.make_async_remote_copy(
                src_ref=src, dst_ref=outs[i].at[pslot], send_sem=send_sems.at[sem], recv_sem=recv_sems.at[sem],
                device_id=pid, device_id_type=pl.DeviceIdType.MESH))
    return local, sends, recvs


def _exchange_start(copies):
    local, sends, _ = copies
    for cp in local + sends:
        cp.start()


def _exchange_wait(copies):
    local, sends, recvs = copies
    for cp in recvs:
        cp.wait_recv()
    for cp in sends:
        cp.wait_send()
    for cp in local:
        cp.wait()


def _exchange_out_shapes(arrs, *, gather):
    return [jax.ShapeDtypeStruct((N_DEV,) + (a.shape if gather else a.shape[1:]), a.dtype) for a in arrs]


def _exchange_sems(n):
    if not n:
        return []
    return [pltpu.SemaphoreType.DMA((n * N_PEER,)), pltpu.SemaphoreType.DMA((n * N_PEER,)), pltpu.SemaphoreType.DMA((n,))]


def _exchange(arrs, *, gather, name):
    n = len(arrs)

    def body(*refs):
        copies = _exchange_copies(refs[:n], refs[n:2 * n], *refs[2 * n:], gather=gather)
        _exchange_start(copies)
        _exchange_wait(copies)

    any_spec = pl.BlockSpec(memory_space=pl.ANY)
    return pl.pallas_call(
        body, name=name, in_specs=[any_spec] * n, out_specs=[any_spec] * n,
        out_shape=_exchange_out_shapes(arrs, gather=gather), scratch_shapes=_exchange_sems(n),
        compiler_params=pltpu.CompilerParams(has_side_effects=True),
    )(*arrs)


def _adamw(parts, w, m, v, *, name, tr):
    r, c = w.shape
    nparts = len(parts)
    rows = r // nparts
    assert rows * nparts == r and rows % tr == 0 and all(p.shape == (N_DEV, rows, c) for p in parts)
    per = rows // tr
    c1 = 1.0 - ADAM_B1 ** ADAM_STEP
    c2 = 1.0 - ADAM_B2 ** ADAM_STEP

    def body(*refs):
        p_refs = refs[:nparts]
        w_ref, m_ref, v_ref, g_ref, d_ref, nm_ref, nv_ref = refs[nparts:]
        which = pl.program_id(0) // per
        for k, p_ref in enumerate(p_refs):
            @pl.when(which == k)
            def _(p_ref=p_ref):
                acc = p_ref[0].astype(F32)
                for j in range(1, N_DEV):
                    acc = acc + p_ref[j].astype(F32)
                g_ref[...] = acc

        g = g_ref[...]
        nm = ADAM_B1 * m_ref[...] + (1.0 - ADAM_B1) * g
        nv = ADAM_B2 * v_ref[...] + (1.0 - ADAM_B2) * (g * g)
        nm_ref[...] = nm
        nv_ref[...] = nv
        d_ref[...] = -ADAM_LR * ((nm / c1) / (jnp.sqrt(nv / c2) + ADAM_EPS) + ADAM_WD * w_ref[...])

    def part_spec(k):
        return pl.BlockSpec((N_DEV, tr, c), lambda i: (0, jnp.clip(i - k * per, 0, per - 1), 0))

    blk = pl.BlockSpec((tr, c), lambda i: (i, 0))
    return pl.pallas_call(
        body, name=name, grid=(r // tr,),
        in_specs=[part_spec(k) for k in range(nparts)] + [blk, blk, blk],
        out_specs=[blk] * 4, out_shape=[jax.ShapeDtypeStruct((r, c), F32)] * 4,
        compiler_params=_params("arbitrary"),
    )(*parts, w, m, v)


def _t5_bucket(rel):
    half = REL_BUCKETS // 2
    max_exact = half // 2
    ret = jnp.where(rel > 0, half, 0)
    n = jnp.abs(rel)
    nf = jnp.maximum(n, 1).astype(F32)
    large = max_exact + (jnp.log(nf / max_exact) / math.log(CHUNK / max_exact) * (half - max_exact)).astype(jnp.int32)
    large = jnp.minimum(large, half - 1)
    return ret + jnp.where(n < max_exact, n, large)


def _split16(w):
    hi = w.astype(BF16)
    return hi, (w - hi.astype(F32)).astype(BF16)


def _cols_to_blocks(g, depth):
    _, r, c8 = g.shape
    return g.reshape(depth, r, N_DEV, c8 // N_DEV).transpose(2, 0, 1, 3).reshape(N_DEV, depth * r, c8 // N_DEV)


def _rows_to_blocks(g, depth):
    _, r8, c = g.shape
    return g.reshape(depth, N_DEV, r8 // N_DEV, c).transpose(1, 0, 2, 3).reshape(N_DEV, depth * r8 // N_DEV, c)


def _blocks_to_cols(a, depth):
    _, dr, c = a.shape
    r = dr // depth
    return a.reshape(N_DEV, depth, r, c).transpose(1, 2, 0, 3).reshape(depth, r, N_DEV * c)


def _blocks_to_rows(a, depth):
    _, dr, c = a.shape
    r = dr // depth
    return a.reshape(N_DEV, depth, r, c).transpose(1, 0, 2, 3).reshape(depth, N_DEV * r, c)


_SMALL = ("rel_bias", "norm1_w", "conv_b", "dt_bias", "a_log", "d_skip", "ssm_norm_w", "attn_sink", "norm2_w",
          "ffn_conv_b", "final_norm_w")
_SHARDED = ("w_in", "conv_w", "w_out", "w_up", "ffn_conv_w", "w_down")
_ORDER = ("rel_bias", "norm1_w", "w_in", "conv_w", "conv_b", "dt_bias", "a_log", "d_skip", "ssm_norm_w", "attn_sink",
          "w_out", "norm2_w", "w_up", "ffn_conv_w", "ffn_conv_b", "w_down", "final_norm_w")


def _pack_small(d):
    flat = jnp.concatenate([d[k].reshape(-1).astype(F32) for k in _SMALL])
    rows = -(-flat.size // (LANES * SUBLANES)) * SUBLANES
    return jnp.pad(flat, (0, rows * LANES - flat.size)).reshape(rows, LANES)


def _unpack_small(packed, like):
    flat = packed.reshape(-1)
    out, off = {}, 0
    for k in _SMALL:
        out[k] = flat[off:off + like[k].size].reshape(like[k].shape)
        off += like[k].size
    return out


def kernel(x, rel_bias, norm1_w, w_in, conv_w, conv_b, dt_bias, a_log, d_skip, ssm_norm_w, attn_sink, w_out, norm2_w, w_up, ffn_conv_w, ffn_conv_b, w_down, final_norm_w, loss_target, m_rel_bias, m_norm1_w, m_w_in, m_conv_w, m_conv_b, m_dt_bias, m_a_log, m_d_skip, m_ssm_norm_w, m_attn_sink, m_w_out, m_norm2_w, m_w_up, m_ffn_conv_w, m_ffn_conv_b, m_w_down, m_final_norm_w, v_rel_bias, v_norm1_w, v_w_in, v_conv_w, v_conv_b, v_dt_bias, v_a_log, v_d_skip, v_ssm_norm_w, v_attn_sink, v_w_out, v_norm2_w, v_w_up, v_ffn_conv_w, v_ffn_conv_b, v_w_down, v_final_norm_w):
    w = dict(rel_bias=rel_bias, norm1_w=norm1_w, w_in=w_in, conv_w=conv_w, conv_b=conv_b, dt_bias=dt_bias, a_log=a_log,
             d_skip=d_skip, ssm_norm_w=ssm_norm_w, attn_sink=attn_sink, w_out=w_out, norm2_w=norm2_w, w_up=w_up,
             ffn_conv_w=ffn_conv_w, ffn_conv_b=ffn_conv_b, w_down=w_down, final_norm_w=final_norm_w)
    m = dict(rel_bias=m_rel_bias, norm1_w=m_norm1_w, w_in=m_w_in, conv_w=m_conv_w, conv_b=m_conv_b, dt_bias=m_dt_bias,
             a_log=m_a_log, d_skip=m_d_skip, ssm_norm_w=m_ssm_norm_w, attn_sink=m_attn_sink, w_out=m_w_out,
             norm2_w=m_norm2_w, w_up=m_w_up, ffn_conv_w=m_ffn_conv_w, ffn_conv_b=m_ffn_conv_b, w_down=m_w_down,
             final_norm_w=m_final_norm_w)
    v = dict(rel_bias=v_rel_bias, norm1_w=v_norm1_w, w_in=v_w_in, conv_w=v_conv_w, conv_b=v_conv_b, dt_bias=v_dt_bias,
             a_log=v_a_log, d_skip=v_d_skip, ssm_norm_w=v_ssm_norm_w, attn_sink=v_attn_sink, w_out=v_w_out,
             norm2_w=v_norm2_w, w_up=v_w_up, ffn_conv_w=v_ffn_conv_w, ffn_conv_b=v_ffn_conv_b, w_down=v_w_down,
             final_norm_w=v_final_norm_w)
    nb, seq, _ = x.shape
    t = nb * seq
    depth = w_in.shape[0]

    flat2 = lambda a: a.reshape(-1, a.shape[-1])
    own_shards = lambda i: [w_out[i].astype(BF16), w_up[i].astype(BF16), w_down[i].astype(BF16)]
    cw_hi, cw_lo = _split16(flat2(conv_w))
    fw_hi, fw_lo = _split16(flat2(ffn_conv_w))
    g_in, g_cwh, g_cwl, g_fwh, g_fwl = _exchange([w_in[0].astype(BF16), cw_hi, cw_lo, fw_hi, fw_lo], gather=True,
                                                 name="gather_weights")
    full_conv_w = _blocks_to_cols(g_cwh.astype(F32) + g_cwl.astype(F32), depth)
    full_ffn_cw = _blocks_to_cols(g_fwh.astype(F32) + g_fwl.astype(F32), depth)

    rel = jnp.arange(KEY_SPAN)[None, :] - CHUNK - jnp.arange(CHUNK)[:, None]
    bucket = _t5_bucket(rel)
    band_bias = _band_bias(rel_bias, bucket)

    def layer_params(i, g_in):
        return dict(n1=norm1_w[i][None], w_in=_to_proj_layout(_blocks_to_cols(g_in, 1)[0]), conv_w=full_conv_w[i],
                    conv_b=conv_b[i][None], dtb=_pad_lanes(dt_bias[i].reshape(-1)), alog=_pad_lanes(a_log[i].reshape(-1)),
                    dvec=jnp.repeat(d_skip[i], HEAD_DIM)[None], ssm_nw=ssm_norm_w[i][None], sink=_pad_lanes(attn_sink[i]),
                    n2=norm2_w[i][None], ffn_cw=full_ffn_cw[i], ffn_cb=ffn_conv_b[i][None])

    h = x.reshape(t, D_MODEL)
    params, saved = [None] * depth, [None] * depth
    for i in range(depth):
        h, params[i], saved[i], nxt = _layer_fwd(h, layer_params(i, g_in), band_bias, nb=nb, seq=seq, own_shards=own_shards(i),
                                                 next_shards=[w_in[i + 1].astype(BF16)] if i + 1 < depth else ())
        if nxt:
            (g_in,) = nxt
    dh, g_final, loss_part = _loss_head(h, loss_target.reshape(t, D_MODEL), final_norm_w[None], tm=512)
    loss = lax.psum(loss_part[0, 0], ("x", "y", "c"))

    w_in_blocks = lambda g: _cols_to_blocks(_from_proj_layout(g["w_in"])[None], 1).astype(BF16)
    dbias = jnp.zeros((N_HEADS, CHUNK, KEY_SPAN), F32)
    grads, pending = [None] * depth, ()
    parts = dict(w_in=[None] * depth, w_out=[None] * depth, w_up=[None] * depth, w_down=[None] * depth)
    for i in reversed(range(depth)):
        dh, dbias, grads[i], arrived = _layer_bwd(dh, params[i], saved[i], band_bias, dbias, nb=nb, seq=seq, pending=pending)
        parts["w_down"][i], parts["w_up"][i], parts["w_out"][i] = arrived[:3]
        if pending:
            parts["w_in"][i + 1] = arrived[3]
        pending = [w_in_blocks(grads[i])]
    grad_x = dh.reshape(nb, seq, D_MODEL)
    stack = lambda k: jnp.stack([g[k] for g in grads])
    last = _exchange(pending + [_cols_to_blocks(stack("conv_w"), depth).astype(BF16),
                                _cols_to_blocks(stack("ffn_cw"), depth).astype(BF16)], gather=False, name="scatter_grads")
    parts["w_in"][0] = last[0]

    out = {}
    for k in ("w_in", "w_out", "w_up", "w_down"):
        rows = parts[k][0].shape[1]
        tr = max(d for d in range(16, 129, 16) if rows % d == 0)
        res = _adamw(parts[k], flat2(w[k]), flat2(m[k]), flat2(v[k]), name="adamw_" + k, tr=tr)
        out[k] = [a.reshape(w[k].shape) for a in res]
    for k, p8 in zip(("conv_w", "ffn_conv_w"), last[1:]):
        res = _adamw([p8], flat2(w[k]), flat2(m[k]), flat2(v[k]), name="adamw_" + k, tr=p8.shape[1])
        out[k] = [a.reshape(w[k].shape) for a in res]

    small = dict(rel_bias=_rel_bias_grad(dbias, bucket)[:, :REL_BUCKETS].T, norm1_w=stack("n1"), conv_b=stack("conv_b"),
                 dt_bias=stack("dtb")[:, 0, :2 * N_HEADS], a_log=stack("alog")[:, 0, :2 * N_HEADS],
                 d_skip=stack("dvec").reshape(depth, N_HEADS, HEAD_DIM).sum(-1), ssm_norm_w=stack("ssm_nw"),
                 attn_sink=stack("sink")[:, 0, :N_HEADS], norm2_w=stack("n2"), ffn_conv_b=stack("ffn_cb"),
                 final_norm_w=g_final)
    (small_parts,) = _exchange([_pack_small(small)], gather=True, name="gather_small_grads")
    res = _adamw([small_parts], _pack_small(w), _pack_small(m), _pack_small(v), name="adamw_small", tr=small_parts.shape[1])
    unpacked = [_unpack_small(a, w) for a in res]
    for k in _SMALL:
        out[k] = [u[k] for u in unpacked]

    return (loss, grad_x, *[out[k][0] for k in _ORDER], *[out[k][1] for k in _ORDER],
            *[out[k][2] for k in _ORDER], *[out[k][3] for k in _ORDER])
```

```python
import math

import numpy as np
import jax
import jax.numpy as jnp
from jax import lax
from jax.experimental import pallas as pl
from jax.experimental.pallas import tpu as pltpu

F32, BF16 = jnp.float32, jnp.bfloat16
HIGHEST = lax.Precision.HIGHEST

D_MODEL = 1024
HEAD_DIM = 64
N_HEADS = 16
N_GROUPS = 2
HEADS_PER_GROUP = N_HEADS // N_GROUPS
N_STATE = 128
SSM_WIDTH = 1024
BC_WIDTH = 256
CONV_CH = SSM_WIDTH + 2 * BC_WIDTH
SSM_TAPS = 7
CHUNK = 128
KV_HEADS = 4
KV_WIDTH = 256
Q_PER_KV = N_HEADS // KV_HEADS
KEY_SPAN = 3 * CHUNK
REL_BUCKETS = 32
D_FF = 2816
FFN_TAPS = 3
IN_COLS = 4128
NORM_EPS = 1e-6
N_DEV = 8

LANES = 128
SUBLANES = 8
VMEM_LIMIT_BYTES = 56 * 1024 * 1024
MM_ROWS = 1024

PZ, PXS, PQ, PB, PC, PK, PV, PDT, PROJ_W = 0, 1024, 2048, 3072, 3328, 3584, 3840, 4096, 4224
OZ, OXBC, ODT, OQ, OK_, OV = 0, 1024, 2560, 2592, 3616, 3872

ADAM_LR, ADAM_B1, ADAM_B2, ADAM_EPS, ADAM_WD, ADAM_STEP = 0.001, 0.9, 0.999, 1e-08, 0.01, 10


def _params(*sem):
    return pltpu.CompilerParams(dimension_semantics=sem, vmem_limit_bytes=VMEM_LIMIT_BYTES)


def _sigmoid(x):
    return 0.5 * jnp.tanh(0.5 * x) + 0.5


def _softplus(x):
    return jnp.maximum(x, 0.0) + jnp.log(1.0 + jnp.exp(-jnp.abs(x)))


def _dot(a, b, dims):
    return lax.dot_general(a, b, (dims, ((), ())), preferred_element_type=F32)


NN = ((1,), (0,))
NT = ((1,), (1,))
TN = ((0,), (0,))


def _matmul(a, b, mode, *, name, tm, tn, tk, res=None, out_dtype=F32, precision=None, b_k0=0):
    assert b_k0 % tk == 0
    ko = b_k0 // tk
    if mode == "nn":
        (m, k), n = a.shape, b.shape[1]
        k2 = k if b.shape[0] >= b_k0 + k else -1
        a_spec = pl.BlockSpec((tm, tk), lambda i, j, kk: (i, kk))
        b_spec = pl.BlockSpec((tk, tn), lambda i, j, kk: (kk + ko, j))
        dims = NN
    elif mode == "nt":
        (m, k), n = a.shape, b.shape[0]
        k2 = k if b.shape[1] >= b_k0 + k else -1
        a_spec = pl.BlockSpec((tm, tk), lambda i, j, kk: (i, kk))
        b_spec = pl.BlockSpec((tn, tk), lambda i, j, kk: (j, kk + ko))
        dims = NT
    else:
        (k, m), (k2, n) = a.shape, b.shape
        a_spec = pl.BlockSpec((tk, tm), lambda i, j, kk: (kk, i))
        b_spec = pl.BlockSpec((tk, tn), lambda i, j, kk: (kk, j))
        dims = TN
    assert k == k2 and m % tm == 0 and n % tn == 0 and k % tk == 0, (name, a.shape, b.shape, tm, tn, tk)
    nk = k // tk
    has_res = res is not None

    def body(*refs):
        a_ref, b_ref = refs[:2]
        r_ref = refs[2] if has_res else None
        o_ref = refs[3] if has_res else refs[2]
        acc = refs[-1] if nk > 1 else None
        kk = pl.program_id(2)
        if precision is None:
            part = _dot(a_ref[...].astype(BF16), b_ref[...].astype(BF16), dims)
        else:
            part = lax.dot_general(a_ref[...], b_ref[...], (dims, ((), ())), precision=precision,
                                   preferred_element_type=F32)

        def finish(r):
            if has_res:
                r = r + r_ref[...].astype(F32)
            o_ref[...] = r.astype(out_dtype)

        if nk == 1:
            finish(part)
        else:
            @pl.when(kk == 0)
            def _():
                acc[...] = part

            @pl.when((kk > 0) & (kk < nk - 1))
            def _():
                acc[...] += part

            @pl.when(kk == nk - 1)
            def _():
                finish(acc[...] + part)

    in_specs = [a_spec, b_spec]
    args = [a, b]
    if has_res:
        in_specs.append(pl.BlockSpec((tm, tn), lambda i, j, kk: (i, j)))
        args.append(res)
    return pl.pallas_call(
        body, name=name, grid=(m // tm, n // tn, nk),
        in_specs=in_specs, out_specs=pl.BlockSpec((tm, tn), lambda i, j, kk: (i, j)),
        out_shape=jax.ShapeDtypeStruct((m, n), out_dtype),
        scratch_shapes=[pltpu.VMEM((tm, tn), F32)] if nk > 1 else [],
        compiler_params=_params("parallel", "parallel", "arbitrary"),
    )(*args)


def _rms_matmul(x, nw, w, *, name, tm, tn, ncols=None):
    t, d = x.shape
    n = w.shape[1] if ncols is None else ncols
    assert t % tm == 0 and n % tn == 0

    def body(x_ref, nw_ref, w_ref, o_ref, h_ref):
        @pl.when(pl.program_id(1) == 0)
        def _():
            xv = x_ref[...]
            r = lax.rsqrt(jnp.mean(xv * xv, axis=-1, keepdims=True) + NORM_EPS)
            h_ref[...] = (xv * r * nw_ref[...]).astype(BF16)

        o_ref[...] = _dot(h_ref[...], w_ref[...].astype(BF16), NN).astype(BF16)

    return pl.pallas_call(
        body, name=name, grid=(t // tm, n // tn),
        in_specs=[pl.BlockSpec((tm, d), lambda i, j: (i, 0)),
                  pl.BlockSpec((1, d), lambda i, j: (0, 0)),
                  pl.BlockSpec((d, tn), lambda i, j: (0, j))],
        out_specs=[pl.BlockSpec((tm, tn), lambda i, j: (i, j)),
                   pl.BlockSpec((tm, d), lambda i, j: (i, 0))],
        out_shape=[jax.ShapeDtypeStruct((t, n), BF16), jax.ShapeDtypeStruct((t, d), BF16)],
        compiler_params=_params("parallel", "arbitrary"),
    )(x, nw, w)


def _zero_ext(v):
    z = jnp.zeros((SUBLANES, v.shape[1]), v.dtype)
    return jnp.concatenate([z, v, z], axis=0)


def _shifted(v_ext, offset, seq):
    if offset == 0:
        return v_ext[SUBLANES:SUBLANES + seq]
    return pltpu.roll(v_ext, (-offset) % (seq + 2 * SUBLANES), 0)[SUBLANES:SUBLANES + seq]


def _conv_taps(v, w_ref, taps, seq):
    pad = taps // 2
    v_ext = _zero_ext(v)
    acc = None
    for k in range(taps):
        term = _shifted(v_ext, k - pad, seq) * w_ref[k:k + 1, :]
        acc = term if acc is None else acc + term
    return acc


def _ssm_conv_fwd(proj, cw, cb, *, nb, seq):
    width = 512

    def body(x_ref, w_ref, b_ref, o_ref, g_ref):
        g = _conv_taps(x_ref[...].astype(F32), w_ref, SSM_TAPS, seq) + b_ref[...]
        o_ref[...] = g * _sigmoid(g)
        g_ref[...] = g.astype(BF16)

    def col(j):
        return jnp.where(j < 2, j + PXS // width, PB // width)

    return pl.pallas_call(
        body, name="ssm_conv_fwd", grid=(nb, CONV_CH // width),
        in_specs=[pl.BlockSpec((seq, width), lambda b, j: (b, col(j))),
                  pl.BlockSpec((SSM_TAPS, width), lambda b, j: (0, j)),
                  pl.BlockSpec((1, width), lambda b, j: (0, j))],
        out_specs=[pl.BlockSpec((seq, width), lambda b, j: (b, j))] * 2,
        out_shape=[jax.ShapeDtypeStruct((nb * seq, CONV_CH), F32), jax.ShapeDtypeStruct((nb * seq, CONV_CH), BF16)],
        compiler_params=_params("parallel", "parallel"),
    )(proj, cw, cb)


def _ffn_act_fwd(gu, cw, cb, *, nb, seq):
    width = 256
    nj = D_FF // width

    def body(g_ref, u_ref, w_ref, b_ref, o_ref, s_ref):
        g = _conv_taps(g_ref[...].astype(F32), w_ref, FFN_TAPS, seq) + b_ref[...]
        o_ref[...] = (g * _sigmoid(g) * u_ref[...].astype(F32)).astype(BF16)
        s_ref[...] = g.astype(BF16)

    return pl.pallas_call(
        body, name="ffn_act_fwd", grid=(nb, nj),
        in_specs=[pl.BlockSpec((seq, width), lambda b, j: (b, j)),
                  pl.BlockSpec((seq, width), lambda b, j: (b, j + nj)),
                  pl.BlockSpec((FFN_TAPS, width), lambda b, j: (0, j)),
                  pl.BlockSpec((1, width), lambda b, j: (0, j))],
        out_specs=[pl.BlockSpec((seq, width), lambda b, j: (b, j))] * 2,
        out_shape=[jax.ShapeDtypeStruct((nb * seq, D_FF), BF16)] * 2,
        compiler_params=_params("parallel", "parallel"),
    )(gu, gu, cw, cb)


def _scan_setup(d, dt_ref, dtb_ref, alog_ref, z_ref, zt_ref, *, lower_when_dir0, inclusive):
    is0 = d == 0
    dt_all = _softplus(dt_ref[...] + dtb_ref[...])
    adt_all = dt_all * (-jnp.exp(alog_ref[...]))
    li = lax.broadcasted_iota(jnp.int32, (CHUNK, CHUNK), 0)
    si = lax.broadcasted_iota(jnp.int32, (CHUNK, CHUNK), 1)
    lower = is0 if lower_when_dir0 else jnp.logical_not(is0)
    ahead = jnp.where(lower, li - si, si - li)
    mask = ahead >= 0
    tri = mask if inclusive else ahead > 0
    z_all = jnp.dot(tri.astype(F32), adt_all, precision=HIGHEST, preferred_element_type=F32)
    zt_all = z_all.T
    z_ref[...] = jnp.where(is0, z_all[:, 0:N_HEADS], z_all[:, N_HEADS:2 * N_HEADS])
    zt_ref[...] = jnp.where(is0, zt_all[0:N_HEADS, :], zt_all[N_HEADS:2 * N_HEADS, :])
    dt = jnp.where(is0, dt_all[:, 0:N_HEADS], dt_all[:, N_HEADS:2 * N_HEADS])
    adt = jnp.where(is0, adt_all[:, 0:N_HEADS], adt_all[:, N_HEADS:2 * N_HEADS])
    tot = jnp.sum(adt, axis=0, keepdims=True)
    return ahead, dt, tot, dt_all


def _chunk_index(nchunk, forward_when_dir0):
    def idx(d, b, c):
        fwd = (d == 0) if forward_when_dir0 else (d != 0)
        return b * nchunk + jnp.where(fwd, c, nchunk - 1 - c)
    return idx


def _ssd_fwd(xbc, dtraw, dtb, alog, *, nb, seq, gather=()):
    nchunk = seq // CHUNK
    t = nb * seq
    row = _chunk_index(nchunk, True)
    ng = len(gather)

    def body(*refs):
        xs_ref, bc_ref, dt_ref, dtb_ref, alog_ref = refs[:5]
        o_ref, hs_ref = refs[5 + ng:7 + ng]
        h_ref, z_ref, zt_ref, dts_ref = refs[7 + 2 * ng:11 + 2 * ng]
        d, c = pl.program_id(0), pl.program_id(2)
        if ng:
            step = (d * nb + pl.program_id(1)) * nchunk + c
            copies = _exchange_copies(refs[5:5 + ng], refs[7 + ng:7 + 2 * ng], *refs[11 + 2 * ng:], gather=True)
            pl.when(step == 0)(lambda: _exchange_start(copies))

        @pl.when(c == 0)
        def _():
            h_ref[...] = jnp.zeros_like(h_ref)

        ahead, dt, tot, _ = _scan_setup(d, dt_ref, dtb_ref, alog_ref, z_ref, zt_ref,
                                        lower_when_dir0=True, inclusive=True)
        mask = ahead >= 0
        dts_ref[...] = dt
        e_tot = jnp.exp(tot)
        for g in range(N_GROUPS):
            heads = range(g * HEADS_PER_GROUP, (g + 1) * HEADS_PER_GROUP)
            bg = bc_ref[:, g * N_STATE:(g + 1) * N_STATE]
            cg = bc_ref[:, BC_WIDTH + g * N_STATE:BC_WIDTH + (g + 1) * N_STATE]
            cb = _dot(cg.astype(BF16), bg.astype(BF16), NT)
            zc = {h: jnp.broadcast_to(z_ref[:, h:h + 1], (CHUNK, CHUNK)) for h in heads}
            decay = {h: jnp.exp(jnp.where(mask, zc[h] - zt_ref[h:h + 1, :], -jnp.inf)) for h in heads}
            u = {h: (xs_ref[:, h * HEAD_DIM:(h + 1) * HEAD_DIM] * dts_ref[:, h:h + 1]).astype(BF16) for h in heads}
            state = {h: h_ref[h] for h in heads}
            for h in heads:
                hs_ref[0, 0, h] = state[h]
            mix = {h: (cb * decay[h]).astype(BF16) for h in heads}
            cz = {h: (cg * jnp.exp(zc[h])).astype(BF16) for h in heads}
            bw = {h: (bg * jnp.exp(tot[:, h:h + 1] - zc[h])).astype(BF16) for h in heads}
            y = {h: _dot(mix[h], u[h], NN) + _dot(cz[h], state[h].astype(BF16), NT) for h in heads}
            new = {h: state[h] * e_tot[:, h:h + 1] + _dot(u[h], bw[h], TN) for h in heads}
            for h in heads:
                h_ref[h] = new[h]
                o_ref[0, :, h * HEAD_DIM:(h + 1) * HEAD_DIM] = y[h]
        if ng:
            pl.when(step == 2 * nb * nchunk - 1)(lambda: _exchange_wait(copies))

    any_spec = pl.BlockSpec(memory_space=pl.ANY)
    res = pl.pallas_call(
        body, name="ssd_fwd_gather" if ng else "ssd_fwd", grid=(2, nb, nchunk),
        in_specs=[pl.BlockSpec((CHUNK, SSM_WIDTH), lambda d, b, c: (row(d, b, c), 0)),
                  pl.BlockSpec((CHUNK, 2 * BC_WIDTH), lambda d, b, c: (row(d, b, c), SSM_WIDTH // (2 * BC_WIDTH))),
                  pl.BlockSpec((CHUNK, LANES), lambda d, b, c: (row(d, b, c), 0)),
                  pl.BlockSpec((1, LANES), lambda d, b, c: (0, 0)),
                  pl.BlockSpec((1, LANES), lambda d, b, c: (0, 0))] + [any_spec] * ng,
        out_specs=[pl.BlockSpec((1, CHUNK, SSM_WIDTH), lambda d, b, c: (d, row(d, b, c), 0)),
                   pl.BlockSpec((1, 1, N_HEADS, HEAD_DIM, N_STATE), lambda d, b, c: (d, row(d, b, c), 0, 0, 0))]
        + [any_spec] * ng,
        out_shape=[jax.ShapeDtypeStruct((2, t, SSM_WIDTH), F32),
                   jax.ShapeDtypeStruct((2, nb * nchunk, N_HEADS, HEAD_DIM, N_STATE), F32)]
        + _exchange_out_shapes(gather, gather=True),
        scratch_shapes=_SSD_SCRATCH + _exchange_sems(ng),
        compiler_params=_params("arbitrary", "arbitrary", "arbitrary"),
    )(xbc, xbc, dtraw, dtb, alog, *gather)
    return res[0], res[1], list(res[2:])


def _gate_norm_fwd(y2, xbc, proj, dvec, nw, *, tm):
    t = xbc.shape[0]
    half = SSM_WIDTH // N_GROUPS

    def body(y_ref, xs_ref, z_ref, d_ref, w_ref, o_ref):
        z = z_ref[...].astype(F32)
        p = (y_ref[0] + y_ref[1] + d_ref[...] * xs_ref[...]) * (z * _sigmoid(z))
        for g in range(N_GROUPS):
            pg = p[:, g * half:(g + 1) * half]
            r = lax.rsqrt(jnp.mean(pg * pg, axis=-1, keepdims=True) + NORM_EPS)
            o_ref[:, g * half:(g + 1) * half] = (pg * r * w_ref[:, g * half:(g + 1) * half]).astype(BF16)

    return pl.pallas_call(
        body, name="gate_norm_fwd", grid=(t // tm,),
        in_specs=[pl.BlockSpec((2, tm, SSM_WIDTH), lambda i: (0, i, 0)),
                  pl.BlockSpec((tm, SSM_WIDTH), lambda i: (i, 0)),
                  pl.BlockSpec((tm, SSM_WIDTH), lambda i: (i, PZ // SSM_WIDTH)),
                  pl.BlockSpec((1, SSM_WIDTH), lambda i: (0, 0)),
                  pl.BlockSpec((1, SSM_WIDTH), lambda i: (0, 0))],
        out_specs=pl.BlockSpec((tm, SSM_WIDTH), lambda i: (i, 0)),
        out_shape=jax.ShapeDtypeStruct((t, SSM_WIDTH), BF16),
        compiler_params=_params("parallel"),
    )(y2, xbc, proj, dvec, nw)


GROUP_ROWS = Q_PER_KV * CHUNK


def _keys_inside(n, nblk):
    kpos = (n - 1) * CHUNK + lax.broadcasted_iota(jnp.int32, (1, KEY_SPAN), 1)
    return (kpos >= 0) & (kpos < nblk * CHUNK)


def _per_head_column(ref, g):
    blk = lax.broadcasted_iota(jnp.int32, (GROUP_ROWS, 1), 0) // CHUNK
    col = jnp.zeros((GROUP_ROWS, 1), F32)
    for r in range(Q_PER_KV):
        h = g * Q_PER_KV + r
        col = jnp.where(blk == r, ref[:, h:h + 1], col)
    return col


def _stack_heads(ref, g, dtype):
    return jnp.concatenate([ref[:, (g * Q_PER_KV + r) * HEAD_DIM:(g * Q_PER_KV + r + 1) * HEAD_DIM].astype(dtype)
                            for r in range(Q_PER_KV)], axis=0)


def _kv_specs(nblk):
    kvb = PK // (2 * KV_WIDTH)

    def at(off):
        def idx(b, n):
            return (b * nblk + jnp.clip(n + off, 0, nblk - 1), kvb)
        return pl.BlockSpec((CHUNK, 2 * KV_WIDTH), idx)
    return [at(-1), at(0), at(1)]


def _attn_fwd(proj, bias, sink, *, nb, seq, gather=()):
    nblk = seq // CHUNK
    t = nb * seq
    scale = HEAD_DIM ** -0.5
    ng = len(gather)

    def body(*refs):
        q_ref, kp_ref, kc_ref, kn_ref, bias_ref, sink_ref = refs[:6]
        o_ref, lse_ref = refs[6 + ng:8 + ng]
        n = pl.program_id(1)
        if ng:
            step = pl.program_id(0) * nblk + n
            copies = _exchange_copies(refs[6:6 + ng], refs[8 + ng:8 + 2 * ng], *refs[8 + 2 * ng:], gather=True)
            pl.when(step == 0)(lambda: _exchange_start(copies))
        inside = _keys_inside(n, nblk)
        groups = range(KV_HEADS)

        def keys(g, off):
            cs = slice(off + g * HEAD_DIM, off + (g + 1) * HEAD_DIM)
            return jnp.concatenate([kp_ref[:, cs], kc_ref[:, cs], kn_ref[:, cs]], axis=0).astype(BF16)

        qs = [(_stack_heads(q_ref, g, F32) * scale).astype(BF16) for g in groups]
        ss = [jnp.where(inside, _dot(qs[g], keys(g, 0), NT)
                        + bias_ref[g * Q_PER_KV:(g + 1) * Q_PER_KV].reshape(GROUP_ROWS, KEY_SPAN), -jnp.inf) for g in groups]
        sks = [_per_head_column(sink_ref, g) for g in groups]
        ms = [jnp.maximum(jnp.max(ss[g], axis=-1, keepdims=True), sks[g]) for g in groups]
        ps = [jnp.exp(ss[g] - ms[g]) for g in groups]
        denoms = [jnp.sum(ps[g], axis=-1, keepdims=True) + jnp.exp(sks[g] - ms[g]) for g in groups]
        outs = [(_dot(ps[g].astype(BF16), keys(g, KV_WIDTH), NN) * (1.0 / denoms[g])).astype(BF16) for g in groups]
        lses = []
        for g in groups:
            lse = ms[g] + jnp.log(denoms[g])
            for r in range(Q_PER_KV):
                h = g * Q_PER_KV + r
                o_ref[:, h * HEAD_DIM:(h + 1) * HEAD_DIM] = outs[g][r * CHUNK:(r + 1) * CHUNK]
                lses.append(lse[r * CHUNK:(r + 1) * CHUNK])
        lse_ref[...] = jnp.concatenate(lses, axis=1)
        if ng:
            pl.when(step == nb * nblk - 1)(lambda: _exchange_wait(copies))

    any_spec = pl.BlockSpec(memory_space=pl.ANY)
    res = pl.pallas_call(
        body, name="attn_fwd_gather" if ng else "attn_fwd", grid=(nb, nblk),
        in_specs=[pl.BlockSpec((CHUNK, D_MODEL), lambda b, n: (b * nblk + n, PQ // D_MODEL))] + _kv_specs(nblk) + [
            pl.BlockSpec((N_HEADS, CHUNK, KEY_SPAN), lambda b, n: (0, 0, 0)),
            pl.BlockSpec((1, LANES), lambda b, n: (0, 0))] + [any_spec] * ng,
        out_specs=[pl.BlockSpec((CHUNK, D_MODEL), lambda b, n: (b * nblk + n, 0)),
                   pl.BlockSpec((CHUNK, N_HEADS), lambda b, n: (b * nblk + n, 0))] + [any_spec] * ng,
        out_shape=[jax.ShapeDtypeStruct((t, D_MODEL), BF16), jax.ShapeDtypeStruct((t, N_HEADS), F32)]
        + _exchange_out_shapes(gather, gather=True),
        scratch_shapes=_exchange_sems(ng),
        compiler_params=_params("arbitrary", "arbitrary"),
    )(proj, proj, proj, proj, bias, sink, *gather)
    return res[0], res[1], list(res[2:])


def _loss_head(x, tgt, nw, *, tm):
    t, d = x.shape

    def body(x_ref, t_ref, w_ref, dx_ref, dw_ref, l_ref):
        @pl.when(pl.program_id(0) == 0)
        def _():
            dw_ref[...] = jnp.zeros_like(dw_ref)
            l_ref[...] = jnp.zeros_like(l_ref)

        xv = x_ref[...]
        w = w_ref[...]
        r = lax.rsqrt(jnp.mean(xv * xv, axis=-1, keepdims=True) + NORM_EPS)
        xh = xv * r
        err = xh * w - t_ref[...]
        l_ref[...] += jnp.sum(err * err) * (0.5 / d)
        dy = err * (1.0 / d)
        gw = dy * w
        dx_ref[...] = r * (gw - xh * jnp.mean(gw * xh, axis=-1, keepdims=True))
        dw_ref[...] += jnp.sum(dy * xh, axis=0, keepdims=True)

    return pl.pallas_call(
        body, name="loss_head", grid=(t // tm,),
        in_specs=[pl.BlockSpec((tm, d), lambda i: (i, 0)), pl.BlockSpec((tm, d), lambda i: (i, 0)),
                  pl.BlockSpec((1, d), lambda i: (0, 0))],
        out_specs=[pl.BlockSpec((tm, d), lambda i: (i, 0)), pl.BlockSpec((1, d), lambda i: (0, 0)),
                   pl.BlockSpec((1, LANES), lambda i: (0, 0))],
        out_shape=[jax.ShapeDtypeStruct((t, d), F32), jax.ShapeDtypeStruct((1, d), F32),
                   jax.ShapeDtypeStruct((1, LANES), F32)],
        compiler_params=_params("arbitrary"),
    )(x, tgt, nw)


def _to_proj_layout(w):
    pad = jnp.zeros(w.shape[:-1] + (PROJ_W - IN_COLS,), w.dtype)
    return jnp.concatenate([w[..., OZ:OXBC], w[..., OXBC:OXBC + SSM_WIDTH], w[..., OQ:OK_],
                            w[..., OXBC + SSM_WIDTH:ODT], w[..., OK_:IN_COLS], w[..., ODT:OQ], pad], axis=-1)


def _from_proj_layout(g):
    return jnp.concatenate([g[..., PZ:PZ + 2 * SSM_WIDTH], g[..., PB:PB + 2 * BC_WIDTH], g[..., PDT:PDT + 2 * N_HEADS],
                            g[..., PQ:PQ + D_MODEL], g[..., PK:PK + 2 * KV_WIDTH]], axis=-1)


def _pad_lanes(v):
    return jnp.pad(v.reshape(1, -1), ((0, 0), (0, LANES - v.size)))


def _layer_fwd(x, p, band_bias, *, nb, seq, own_shards, next_shards=()):
    proj, h1 = _rms_matmul(x, p["n1"], p["w_in"], name="in_proj", tm=MM_ROWS, tn=1024, ncols=PDT)
    dtraw = _matmul(h1, p["w_in"][:, PDT:], "nn", name="in_proj_dt", tm=MM_ROWS, tn=LANES, tk=D_MODEL)
    xbc, gconv = _ssm_conv_fwd(proj, p["conv_w"], p["conv_b"], nb=nb, seq=seq)
    y2, states, (g_out, g_up, g_down) = _ssd_fwd(xbc, dtraw, p["dtb"], p["alog"], nb=nb, seq=seq, gather=own_shards)
    p = dict(p, w_out=_blocks_to_rows(g_out, 1)[0], w_up=_blocks_to_cols(g_up, 1)[0], w_down=_blocks_to_rows(g_down, 1)[0])
    y_ssm = _gate_norm_fwd(y2, xbc, proj, p["dvec"], p["ssm_nw"], tm=256)
    y_att, lse, gathered = _attn_fwd(proj, band_bias, p["sink"], nb=nb, seq=seq, gather=next_shards)
    x1 = _matmul(y_ssm, p["w_out"], "nn", name="out_proj_ssm", tm=MM_ROWS, tn=1024, tk=1024, res=x)
    x1 = _matmul(y_att, p["w_out"], "nn", name="out_proj_att", tm=MM_ROWS, tn=1024, tk=1024, res=x1, b_k0=SSM_WIDTH)
    gu, h2 = _rms_matmul(x1, p["n2"], p["w_up"], name="up_proj", tm=MM_ROWS, tn=1408)
    act, fconv = _ffn_act_fwd(gu, p["ffn_cw"], p["ffn_cb"], nb=nb, seq=seq)
    x2 = _matmul(act, p["w_down"], "nn", name="down_proj", tm=MM_ROWS, tn=1024, tk=1408, res=x1)
    saved = dict(x=x, proj=proj, dtraw=dtraw, h1=h1, xbc=xbc, gconv=gconv, y2=y2, states=states, y_ssm=y_ssm, y_att=y_att,
                 lse=lse, x1=x1, gu=gu, fconv=fconv, h2=h2, act=act)
    return x2, p, saved, gathered


def _rms_bwd(x, dh, nw, dres, *, tm, name):
    t, d = x.shape

    def body(x_ref, dh_ref, w_ref, r_ref, dx_ref, dw_ref):
        @pl.when(pl.program_id(0) == 0)
        def _():
            dw_ref[...] = jnp.zeros_like(dw_ref)

        xv = x_ref[...]
        dh_v = dh_ref[...].astype(F32)
        r = lax.rsqrt(jnp.mean(xv * xv, axis=-1, keepdims=True) + NORM_EPS)
        xh = xv * r
        gw = dh_v * w_ref[...]
        dx_ref[...] = r_ref[...] + r * (gw - xh * jnp.mean(gw * xh, axis=-1, keepdims=True))
        dw_ref[...] += jnp.sum(dh_v * xh, axis=0, keepdims=True)

    row = pl.BlockSpec((tm, d), lambda i: (i, 0))
    vec = pl.BlockSpec((1, d), lambda i: (0, 0))
    return pl.pallas_call(
        body, name=name, grid=(t // tm,), in_specs=[row, row, vec, row], out_specs=[row, vec],
        out_shape=[jax.ShapeDtypeStruct((t, d), F32), jax.ShapeDtypeStruct((1, d), F32)],
        compiler_params=_params("arbitrary"),
    )(x, dh, nw, dres)


def _dsilu(g, sg):
    return sg * (1.0 + g * (1.0 - sg))


def _conv_taps_bwd(gpre, dg, w_ref, dwb_ref, taps, seq):
    pad = taps // 2
    dg_ext, gpre_ext = _zero_ext(dg), _zero_ext(gpre)
    dpre = None
    for k in range(taps):
        term = _shifted(dg_ext, pad - k, seq) * w_ref[k:k + 1, :]
        dpre = term if dpre is None else dpre + term
        dwb_ref[k:k + 1, :] += jnp.sum(dg * _shifted(gpre_ext, k - pad, seq), axis=0, keepdims=True)
    dwb_ref[SUBLANES - 1:SUBLANES, :] += jnp.sum(dg, axis=0, keepdims=True)
    return dpre


def _ffn_act_bwd(gu, gconv, dact, cw, *, nb, seq):
    width = 256
    nj = D_FF // width

    def body(g_ref, u_ref, s_ref, da_ref, w_ref, dg_ref, du_ref, dwb_ref):
        @pl.when(pl.program_id(1) == 0)
        def _():
            dwb_ref[...] = jnp.zeros_like(dwb_ref)

        g = s_ref[...].astype(F32)
        sg = _sigmoid(g)
        da = da_ref[...].astype(F32)
        du_ref[...] = (da * g * sg).astype(BF16)
        dgc = da * u_ref[...].astype(F32) * _dsilu(g, sg)
        dg_ref[...] = _conv_taps_bwd(g_ref[...].astype(F32), dgc, w_ref, dwb_ref, FFN_TAPS, seq).astype(BF16)

    blk = lambda off: pl.BlockSpec((seq, width), lambda j, b: (b, j + off))
    return pl.pallas_call(
        body, name="ffn_act_bwd", grid=(nj, nb),
        in_specs=[blk(0), blk(nj), blk(0), blk(0), pl.BlockSpec((FFN_TAPS, width), lambda j, b: (0, j))],
        out_specs=[blk(0), blk(0), pl.BlockSpec((SUBLANES, width), lambda j, b: (0, j))],
        out_shape=[jax.ShapeDtypeStruct((nb * seq, D_FF), BF16), jax.ShapeDtypeStruct((nb * seq, D_FF), BF16),
                   jax.ShapeDtypeStruct((SUBLANES, D_FF), F32)],
        compiler_params=_params("parallel", "arbitrary"),
    )(gu, gu, gconv, dact, cw)


def _ssm_conv_bwd(proj, gconv, pair, cw, *, nb, seq, name, width, proj_col, conv_col, ncol, extra=None, scale=None):
    has_extra = extra is not None

    def body(*refs):
        if has_extra:
            x_ref, g_ref, p_ref, w_ref, e_ref, s_ref, dx_ref, dwb_ref = refs
        else:
            x_ref, g_ref, p_ref, w_ref, dx_ref, dwb_ref = refs

        @pl.when(pl.program_id(1) == 0)
        def _():
            dwb_ref[...] = jnp.zeros_like(dwb_ref)

        g = g_ref[...].astype(F32)
        da = p_ref[0] + p_ref[1]
        if has_extra:
            da = da + e_ref[...] * s_ref[...]
        dx_ref[...] = _conv_taps_bwd(x_ref[...].astype(F32), da * _dsilu(g, _sigmoid(g)), w_ref, dwb_ref, SSM_TAPS, seq)

    in_specs = [pl.BlockSpec((seq, width), lambda j, b: (b, j + proj_col)),
                pl.BlockSpec((seq, width), lambda j, b: (b, j + conv_col)),
                pl.BlockSpec((2, seq, width), lambda j, b: (0, b, j)),
                pl.BlockSpec((SSM_TAPS, width), lambda j, b: (0, j + conv_col))]
    args = [proj, gconv, pair, cw]
    if has_extra:
        in_specs += [pl.BlockSpec((seq, width), lambda j, b: (b, j)), pl.BlockSpec((1, width), lambda j, b: (0, j))]
        args += [extra, scale]
    return pl.pallas_call(
        body, name=name, grid=(ncol, nb), in_specs=in_specs,
        out_specs=[pl.BlockSpec((seq, width), lambda j, b: (b, j)), pl.BlockSpec((SUBLANES, width), lambda j, b: (0, j))],
        out_shape=[jax.ShapeDtypeStruct((nb * seq, ncol * width), F32), jax.ShapeDtypeStruct((SUBLANES, ncol * width), F32)],
        compiler_params=_params("parallel", "arbitrary"),
    )(*args)


def _attn_bwd(proj, dmix, y_att, lse, bias, sink, dbias_in, *, nb, seq):
    nblk = seq // CHUNK
    t = nb * seq
    scale = HEAD_DIM ** -0.5

    def body(q_ref, kp_ref, kc_ref, kn_ref, do_ref, o_ref, lse_ref, bias_ref, sink_ref, dbin_ref,
             dq_ref, dkv_ref, dbias_ref, dsink_ref):
        b, n = pl.program_id(0), pl.program_id(1)

        @pl.when(n == 0)
        def _():
            dkv_ref[...] = jnp.zeros_like(dkv_ref)

        @pl.when((n == 0) & (b == 0))
        def _():
            dbias_ref[...] = dbin_ref[...]
            dsink_ref[...] = jnp.zeros_like(dsink_ref)

        inside = _keys_inside(n, nblk)
        lane = lax.broadcasted_iota(jnp.int32, (1, LANES), 1)
        dsink = jnp.zeros((1, LANES), F32)
        rows = pl.ds(pl.multiple_of(n * CHUNK, CHUNK), KEY_SPAN)
        groups = range(KV_HEADS)

        def keys(g, off):
            cs = slice(off + g * HEAD_DIM, off + (g + 1) * HEAD_DIM)
            return jnp.concatenate([kp_ref[:, cs], kc_ref[:, cs], kn_ref[:, cs]], axis=0).astype(BF16)

        kcat = [keys(g, 0) for g in groups]
        vcat = [keys(g, KV_WIDTH) for g in groups]
        q = [(_stack_heads(q_ref, g, F32) * scale).astype(BF16) for g in groups]
        do = [_stack_heads(do_ref, g, F32) for g in groups]
        do16 = [do[g].astype(BF16) for g in groups]
        lse = [jnp.concatenate([lse_ref[:, g * Q_PER_KV + r:g * Q_PER_KV + r + 1] for r in range(Q_PER_KV)], axis=0)
               for g in groups]
        s = [jnp.where(inside, _dot(q[g], kcat[g], NT)
                       + bias_ref[g * Q_PER_KV:(g + 1) * Q_PER_KV].reshape(GROUP_ROWS, KEY_SPAN), -jnp.inf) for g in groups]
        p = [jnp.exp(s[g] - lse[g]) for g in groups]
        delta = [jnp.sum(do[g] * _stack_heads(o_ref, g, F32), axis=-1, keepdims=True) for g in groups]
        ds = [p[g] * (_dot(do16[g], vcat[g], NT) - delta[g]) for g in groups]
        ds16 = [ds[g].astype(BF16) for g in groups]
        sink_part = [jnp.exp(_per_head_column(sink_ref, g) - lse[g]) * delta[g] for g in groups]
        dq = [_dot(ds16[g], kcat[g], NN) * scale for g in groups]
        dk = [_dot(ds16[g], q[g], TN) for g in groups]
        dv = [_dot(p[g].astype(BF16), do16[g], TN) for g in groups]
        for g in groups:
            dbias_ref[g * Q_PER_KV:(g + 1) * Q_PER_KV] += ds[g].reshape(Q_PER_KV, CHUNK, KEY_SPAN)
            for r in range(Q_PER_KV):
                h = g * Q_PER_KV + r
                dq_ref[:, h * HEAD_DIM:(h + 1) * HEAD_DIM] = dq[g][r * CHUNK:(r + 1) * CHUNK]
                dsink = dsink - jnp.where(lane == h, jnp.sum(sink_part[g][r * CHUNK:(r + 1) * CHUNK], axis=0, keepdims=True), 0.0)
            dkv_ref[0, rows, g * HEAD_DIM:(g + 1) * HEAD_DIM] += dk[g]
            dkv_ref[0, rows, KV_WIDTH + g * HEAD_DIM:KV_WIDTH + (g + 1) * HEAD_DIM] += dv[g]
        dsink_ref[...] += dsink

    blk = lambda cb: pl.BlockSpec((CHUNK, D_MODEL), lambda b, n: (b * nblk + n, cb))
    whole = pl.BlockSpec((N_HEADS, CHUNK, KEY_SPAN), lambda b, n: (0, 0, 0))
    vec = pl.BlockSpec((1, LANES), lambda b, n: (0, 0))
    return pl.pallas_call(
        body, name="attn_bwd", grid=(nb, nblk),
        in_specs=[blk(PQ // D_MODEL)] + _kv_specs(nblk) + [
            blk(1), blk(0), pl.BlockSpec((CHUNK, N_HEADS), lambda b, n: (b * nblk + n, 0)), whole, vec, whole],
        out_specs=[blk(0), pl.BlockSpec((1, seq + 2 * CHUNK, 2 * KV_WIDTH), lambda b, n: (b, 0, 0)), whole, vec],
        out_shape=[jax.ShapeDtypeStruct((t, D_MODEL), F32),
                   jax.ShapeDtypeStruct((nb, seq + 2 * CHUNK, 2 * KV_WIDTH), F32),
                   jax.ShapeDtypeStruct((N_HEADS, CHUNK, KEY_SPAN), F32),
                   jax.ShapeDtypeStruct((1, LANES), F32)],
        compiler_params=_params("arbitrary", "arbitrary"),
    )(proj, proj, proj, proj, dmix, y_att, lse, bias, sink, dbias_in)


def _gate_norm_bwd(y2, xbc, proj, dmix, dvec, nw, *, tm):
    t = xbc.shape[0]
    half = SSM_WIDTH // N_GROUPS

    def body(y_ref, xs_ref, z_ref, do_ref, d_ref, w_ref, dyv_ref, dz_ref, dd_ref, dw_ref):
        @pl.when(pl.program_id(0) == 0)
        def _():
            dd_ref[...] = jnp.zeros_like(dd_ref)
            dw_ref[...] = jnp.zeros_like(dw_ref)

        z = z_ref[...].astype(F32)
        xs = xs_ref[...]
        sg = _sigmoid(z)
        gz = z * sg
        yv = y_ref[0] + y_ref[1] + d_ref[...] * xs
        p = yv * gz
        do = do_ref[...]
        for g in range(N_GROUPS):
            cs = slice(g * half, (g + 1) * half)
            pg = p[:, cs]
            r = lax.rsqrt(jnp.mean(pg * pg, axis=-1, keepdims=True) + NORM_EPS)
            ph = pg * r
            gw = do[:, cs] * w_ref[:, cs]
            dp = r * (gw - ph * jnp.mean(gw * ph, axis=-1, keepdims=True))
            dyv = dp * gz[:, cs]
            dyv_ref[:, cs] = dyv
            dz_ref[:, cs] = dp * yv[:, cs] * _dsilu(z[:, cs], sg[:, cs])
            dw_ref[:, cs] += jnp.sum(do[:, cs] * ph, axis=0, keepdims=True)
            dd_ref[:, cs] += jnp.sum(dyv * xs[:, cs], axis=0, keepdims=True)

    row = lambda cb: pl.BlockSpec((tm, SSM_WIDTH), lambda i: (i, cb))
    vec = pl.BlockSpec((1, SSM_WIDTH), lambda i: (0, 0))
    return pl.pallas_call(
        body, name="gate_norm_bwd", grid=(t // tm,),
        in_specs=[pl.BlockSpec((2, tm, SSM_WIDTH), lambda i: (0, i, 0)), row(0), row(PZ // SSM_WIDTH), row(0), vec, vec],
        out_specs=[row(0), row(0), vec, vec],
        out_shape=[jax.ShapeDtypeStruct((t, SSM_WIDTH), F32), jax.ShapeDtypeStruct((t, SSM_WIDTH), F32),
                   jax.ShapeDtypeStruct((1, SSM_WIDTH), F32), jax.ShapeDtypeStruct((1, SSM_WIDTH), F32)],
        compiler_params=_params("arbitrary"),
    )(y2, xbc, proj, dmix, dvec, nw)


def _ssd_specs(nchunk, row):
    return [pl.BlockSpec((CHUNK, SSM_WIDTH), lambda d, b, c: (row(d, b, c), 0)),
            pl.BlockSpec((CHUNK, 2 * BC_WIDTH), lambda d, b, c: (row(d, b, c), SSM_WIDTH // (2 * BC_WIDTH))),
            pl.BlockSpec((CHUNK, LANES), lambda d, b, c: (row(d, b, c), 0)),
            pl.BlockSpec((1, LANES), lambda d, b, c: (0, 0)),
            pl.BlockSpec((1, LANES), lambda d, b, c: (0, 0)),
            pl.BlockSpec((CHUNK, SSM_WIDTH), lambda d, b, c: (row(d, b, c), 0))]


_SSD_SCRATCH = [pltpu.VMEM((N_HEADS, HEAD_DIM, N_STATE), F32),
                pltpu.VMEM((CHUNK, N_HEADS), F32), pltpu.VMEM((N_HEADS, CHUNK), F32),
                pltpu.VMEM((CHUNK, N_HEADS), F32)]


def _ssd_bwd(xbc, dtraw, dtb, alog, dyv, states, *, nb, seq, scatter=()):
    nchunk = seq // CHUNK
    t = nb * seq
    row = _chunk_index(nchunk, False)
    ns = len(scatter)

    def body(*refs):
        xs_ref, bc_ref, dt_ref, dtb_ref, alog_ref, dy_ref, hs_ref = refs[:7]
        dx_ref, db_ref, dc_ref, draw_ref, da_ref, dbias_ref = refs[7 + ns:13 + ns]
        h_ref, z_ref, zt_ref, dts_ref, acc_ref, span_ref = refs[13 + 2 * ns:19 + 2 * ns]
        d, b, c = pl.program_id(0), pl.program_id(1), pl.program_id(2)
        if ns:
            step = (d * nb + b) * nchunk + c
            copies = _exchange_copies(refs[7:7 + ns], refs[13 + ns:13 + 2 * ns], *refs[19 + 2 * ns:], gather=False)
            pl.when(step == 0)(lambda: _exchange_start(copies))

        @pl.when(c == 0)
        def _():
            h_ref[...] = jnp.zeros_like(h_ref)

        @pl.when((c == 0) & (b == 0) & (d == 0))
        def _():
            da_ref[...] = jnp.zeros_like(da_ref)
            dbias_ref[...] = jnp.zeros_like(dbias_ref)

        ahead, dt, tot, dt_all = _scan_setup(d, dt_ref, dtb_ref, alog_ref, z_ref, zt_ref,
                                             lower_when_dir0=False, inclusive=False)
        mask = ahead >= 0
        dts_ref[...] = dt
        e_tot = jnp.exp(tot)
        acc_ref[...] = jnp.zeros_like(acc_ref)
        span_ref[...] = jnp.zeros_like(span_ref)
        for g in range(N_GROUPS):
            bg = bc_ref[:, g * N_STATE:(g + 1) * N_STATE]
            cg = bc_ref[:, BC_WIDTH + g * N_STATE:BC_WIDTH + (g + 1) * N_STATE]
            bg16 = bg.astype(BF16)
            cg16 = cg.astype(BF16)
            bc_t = _dot(bg16, cg16, NT)
            heads = range(g * HEADS_PER_GROUP, (g + 1) * HEADS_PER_GROUP)
            cols = {h: slice(h * HEAD_DIM, (h + 1) * HEAD_DIM) for h in heads}
            zc = {h: jnp.broadcast_to(z_ref[:, h:h + 1], (CHUNK, CHUNK)) for h in heads}
            decay = {h: jnp.exp(jnp.where(mask, zc[h] - zt_ref[h:h + 1, :], -jnp.inf)) for h in heads}
            e_z = {h: jnp.exp(zc[h]) for h in heads}
            e_tz = {h: jnp.exp(tot[:, h:h + 1] - zc[h]) for h in heads}
            x_h = {h: xs_ref[:, cols[h]] for h in heads}
            dt_h = {h: dts_ref[:, h:h + 1] for h in heads}
            u = {h: (x_h[h] * dt_h[h]).astype(BF16) for h in heads}
            dy = {h: dy_ref[:, cols[h]].astype(BF16) for h in heads}
            state = {h: h_ref[h] for h in heads}
            st16 = {h: state[h].astype(BF16) for h in heads}
            fstate = {h: hs_ref[0, 0, h] for h in heads}
            mix = {h: (bc_t * decay[h]).astype(BF16) for h in heads}
            bz = {h: (bg * e_z[h]).astype(BF16) for h in heads}
            du = {h: _dot(mix[h], dy[h], NN) + _dot(bz[h], st16[h], NT) for h in heads}
            w2f = {h: _dot(u[h], dy[h], NT) * decay[h] for h in heads}
            w2 = {h: w2f[h].astype(BF16) for h in heads}
            db_out = {h: e_z[h] * _dot(u[h], st16[h], NN) for h in heads}
            dc_out = {h: e_tz[h] * _dot(dy[h], fstate[h].astype(BF16), NN) for h in heads}
            db_in = {h: _dot(w2[h], cg16, NN) for h in heads}
            dc_in = {h: _dot(w2[h], bg16, TN) for h in heads}
            pairs = {h: w2f[h] * bc_t for h in heads}
            col_in = {h: jnp.sum(pairs[h], axis=-1, keepdims=True) for h in heads}
            row_in = {h: jnp.sum(pairs[h], axis=0, keepdims=True) for h in heads}
            row_out = {h: jnp.sum(dc_out[h] * cg, axis=-1, keepdims=True) for h in heads}
            col_out = {h: jnp.sum(db_out[h] * bg, axis=-1, keepdims=True) for h in heads}
            ddt_h = {h: jnp.sum(du[h] * x_h[h], axis=-1, keepdims=True) for h in heads}
            cz = {h: (cg * e_tz[h]).astype(BF16) for h in heads}
            new = {h: state[h] * e_tot[:, h:h + 1] + _dot(dy[h], cz[h], TN) for h in heads}
            dbg = jnp.zeros((CHUNK, N_STATE), F32)
            dcg = jnp.zeros((CHUNK, N_STATE), F32)
            for h in heads:
                dbg = dbg + db_in[h] + db_out[h]
                dcg = dcg + dc_in[h] + dc_out[h]
                acc_ref[0, :, h:h + 1] = row_out[h] - col_in[h]
                acc_ref[1, :, h:h + 1] = col_out[h]
                acc_ref[2, :, h:h + 1] = ddt_h[h]
                acc_ref[3, h:h + 1, :] = row_in[h]
                span_ref[0:1, h:h + 1] = e_tot[:, h:h + 1] * jnp.sum(fstate[h] * state[h], keepdims=True)
                dx_ref[0, :, cols[h]] = du[h] * dt_h[h]
                h_ref[h] = new[h]
            db_ref[0, :, g * N_STATE:(g + 1) * N_STATE] = dbg
            dc_ref[0, :, g * N_STATE:(g + 1) * N_STATE] = dcg
        tri = mask.astype(F32)
        at_dir = lambda v: jnp.where(d == 0, v, pltpu.roll(v, N_HEADS, 1))
        dadt = at_dir(jnp.dot(tri, acc_ref[0] + acc_ref[3].T, precision=HIGHEST, preferred_element_type=F32)
                      + jnp.dot(1.0 - tri, acc_ref[1], precision=HIGHEST, preferred_element_type=F32) + span_ref[0:1, :])
        ddt = at_dir(acc_ref[2])
        a = -jnp.exp(alog_ref[...])
        draw = (ddt + a * dadt) * _sigmoid(dt_ref[...] + dtb_ref[...])
        draw_ref[0] = draw
        da_ref[...] += jnp.sum(dt_all * dadt, axis=0, keepdims=True) * a
        dbias_ref[...] += jnp.sum(draw, axis=0, keepdims=True)
        if ns:
            pl.when(step == 2 * nb * nchunk - 1)(lambda: _exchange_wait(copies))

    out_row = lambda w: pl.BlockSpec((1, CHUNK, w), lambda d, b, c: (d, row(d, b, c), 0))
    vec = pl.BlockSpec((1, LANES), lambda d, b, c: (0, 0))
    any_spec = pl.BlockSpec(memory_space=pl.ANY)
    res = pl.pallas_call(
        body, name="ssd_bwd_scatter" if ns else "ssd_bwd", grid=(2, nb, nchunk),
        in_specs=_ssd_specs(nchunk, row) + [
            pl.BlockSpec((1, 1, N_HEADS, HEAD_DIM, N_STATE), lambda d, b, c: (d, row(d, b, c), 0, 0, 0))]
        + [any_spec] * ns,
        out_specs=[out_row(SSM_WIDTH), out_row(BC_WIDTH), out_row(BC_WIDTH), out_row(LANES), vec, vec] + [any_spec] * ns,
        out_shape=[jax.ShapeDtypeStruct((2, t, SSM_WIDTH), F32), jax.ShapeDtypeStruct((2, t, BC_WIDTH), F32),
                   jax.ShapeDtypeStruct((2, t, BC_WIDTH), F32),
                   jax.ShapeDtypeStruct((2, t, LANES), F32), jax.ShapeDtypeStruct((1, LANES), F32),
                   jax.ShapeDtypeStruct((1, LANES), F32)] + _exchange_out_shapes(scatter, gather=False),
        scratch_shapes=_SSD_SCRATCH + [pltpu.VMEM((4, CHUNK, LANES), F32), pltpu.VMEM((SUBLANES, LANES), F32)]
        + _exchange_sems(ns),
        compiler_params=_params("arbitrary", "arbitrary", "arbitrary"),
    )(xbc, xbc, dtraw, dtb, alog, dyv, states, *scatter)
    return tuple(res[:6]) + (list(res[6:]),)


def _layer_bwd(dx2, p, s, band_bias, dbias_in, *, nb, seq, pending=()):
    t = nb * seq
    x, proj, xbc, x1 = s["x"], s["proj"], s["xbc"], s["x1"]
    dact = _matmul(dx2, p["w_down"], "nt", name="down_proj_dx", tm=MM_ROWS, tn=1408, tk=1024, out_dtype=BF16)
    g_w_down = _matmul(s["act"], dx2, "tn", name="down_proj_dw", tm=1408, tn=1024, tk=1024)
    dg, du, dwb_ffn = _ffn_act_bwd(s["gu"], s["fconv"], dact, p["ffn_cw"], nb=nb, seq=seq)
    dh2 = _matmul(dg, p["w_up"], "nt", name="up_proj_dx_g", tm=MM_ROWS, tn=1024, tk=1408)
    dh2 = _matmul(du, p["w_up"], "nt", name="up_proj_dx_u", tm=MM_ROWS, tn=1024, tk=1408, res=dh2, b_k0=D_FF)
    g_w_up = jnp.concatenate([_matmul(s["h2"], dg, "tn", name="up_proj_dw_g", tm=1024, tn=1408, tk=1024),
                              _matmul(s["h2"], du, "tn", name="up_proj_dw_u", tm=1024, tn=1408, tk=1024)], axis=1)
    dx1, g_n2 = _rms_bwd(x1, dh2, p["n2"], dx2, tm=512, name="norm2_bwd")
    dmix = _matmul(dx1, p["w_out"], "nt", name="out_proj_dx", tm=MM_ROWS, tn=1024, tk=1024)
    g_w_out = jnp.concatenate([_matmul(s["y_ssm"], dx1, "tn", name="out_proj_dw_ssm", tm=1024, tn=1024, tk=1024),
                               _matmul(s["y_att"], dx1, "tn", name="out_proj_dw_att", tm=1024, tn=1024, tk=1024)], axis=0)
    dq, dkv, dbias, dsink = _attn_bwd(proj, dmix, s["y_att"], s["lse"], band_bias, p["sink"], dbias_in, nb=nb, seq=seq)
    dkv = dkv[:, CHUNK:CHUNK + seq, :].reshape(t, 2 * KV_WIDTH)
    dyv, dz, g_dvec, g_ssm_nw = _gate_norm_bwd(s["y2"], xbc, proj, dmix, p["dvec"], p["ssm_nw"], tm=256)
    own = [_rows_to_blocks(g_w_down[None], 1).astype(BF16), _cols_to_blocks(g_w_up[None], 1).astype(BF16),
           _rows_to_blocks(g_w_out[None], 1).astype(BF16)]
    dxs2, db2, dc2, draw2, g_alog, g_dtb, exchanged = _ssd_bwd(xbc, s["dtraw"], p["dtb"], p["alog"], dyv, s["states"],
                                                               nb=nb, seq=seq, scatter=own + list(pending))
    ddt_raw = draw2[0] + draw2[1]
    conv = dict(nb=nb, seq=seq)
    dxs_pre, dwb_xs = _ssm_conv_bwd(proj, s["gconv"], dxs2, p["conv_w"], name="ssm_conv_bwd_x", width=256,
                                    proj_col=PXS // 256, conv_col=0, ncol=4, extra=dyv, scale=p["dvec"], **conv)
    db_pre, dwb_b = _ssm_conv_bwd(proj, s["gconv"], db2, p["conv_w"], name="ssm_conv_bwd_b", width=256,
                                  proj_col=PB // 256, conv_col=SSM_WIDTH // 256, ncol=1, **conv)
    dc_pre, dwb_c = _ssm_conv_bwd(proj, s["gconv"], dc2, p["conv_w"], name="ssm_conv_bwd_c", width=256,
                                  proj_col=PC // 256, conv_col=(SSM_WIDTH + BC_WIDTH) // 256, ncol=1, **conv)
    dwb_ssm = jnp.concatenate([dwb_xs, dwb_b, dwb_c], axis=1)
    dproj = jnp.concatenate([dz, dxs_pre, dq, db_pre, dc_pre, dkv, ddt_raw], axis=1).astype(BF16)
    dh1 = _matmul(dproj, p["w_in"], "nt", name="in_proj_dx", tm=MM_ROWS, tn=1024, tk=1408)
    g_w_in = _matmul(s["h1"], dproj, "tn", name="in_proj_dw", tm=1024, tn=1408, tk=1024)
    dx, g_n1 = _rms_bwd(x, dh1, p["n1"], dx1, tm=512, name="norm1_bwd")
    grads = dict(n1=g_n1, w_in=g_w_in, conv_w=dwb_ssm[:SSM_TAPS], conv_b=dwb_ssm[SUBLANES - 1], dtb=g_dtb, alog=g_alog,
                 dvec=g_dvec, ssm_nw=g_ssm_nw, sink=dsink, w_out=g_w_out, n2=g_n2, w_up=g_w_up,
                 ffn_cw=dwb_ffn[:FFN_TAPS], ffn_cb=dwb_ffn[SUBLANES - 1], w_down=g_w_down)
    return dx, dbias, grads, exchanged


def _band_bias(rel_bias, bucket):
    def body(rb_ref, b_ref, o_ref):
        o_ref[...] = jnp.zeros_like(o_ref)

        def per_bucket(k, carry):
            hit = b_ref[...] == k
            for h in range(N_HEADS):
                o_ref[h] = jnp.where(hit, rb_ref[k, h], o_ref[h])
            return carry

        lax.fori_loop(0, REL_BUCKETS, per_bucket, 0)
        qi = lax.broadcasted_iota(jnp.int32, (CHUNK, KEY_SPAN), 0)
        kj = lax.broadcasted_iota(jnp.int32, (CHUNK, KEY_SPAN), 1)
        band = jnp.abs(kj - CHUNK - qi) <= CHUNK
        for h in range(N_HEADS):
            o_ref[h] = jnp.where(band, o_ref[h], -jnp.inf)

    return pl.pallas_call(
        body, name="band_bias", out_shape=jax.ShapeDtypeStruct((N_HEADS, CHUNK, KEY_SPAN), F32),
        in_specs=[pl.BlockSpec(memory_space=pltpu.SMEM), pl.BlockSpec(memory_space=pltpu.VMEM)],
        out_specs=pl.BlockSpec(memory_space=pltpu.VMEM),
    )(rel_bias, bucket)


def _rel_bias_grad(dbias, bucket):
    def body(d_ref, b_ref, o_ref):
        o_ref[...] = jnp.zeros_like(o_ref)
        lane = lax.broadcasted_iota(jnp.int32, (1, LANES), 1)

        def per_bucket(k, carry):
            hit = b_ref[...] == k
            for h in range(N_HEADS):
                part = jnp.sum(jnp.where(hit, d_ref[h], 0.0), axis=1, keepdims=True)
                o_ref[h:h + 1, :] += jnp.where(lane == k, jnp.sum(part, axis=0, keepdims=True), 0.0)
            return carry

        lax.fori_loop(0, REL_BUCKETS, per_bucket, 0)

    return pl.pallas_call(
        body, name="rel_bias_grad", out_shape=jax.ShapeDtypeStruct((N_HEADS, LANES), F32),
        compiler_params=pltpu.CompilerParams(vmem_limit_bytes=VMEM_LIMIT_BYTES),
    )(dbias, bucket)


N_PEER = N_DEV - 1


def _exchange_copies(ins, outs, send_sems, recv_sems, local_sems, *, gather):
    x, y, c = lax.axis_index("x"), lax.axis_index("y"), lax.axis_index("c")
    me = 4 * x + 2 * y + c
    peers = []
    for k in range(1, N_DEV):
        px, py, pc = x ^ ((k >> 2) & 1), y ^ ((k >> 1) & 1), c ^ (k & 1)
        peers.append(((px, py, pc), 4 * px + 2 * py + pc))
    local, sends, recvs = [], [], []
    for i in range(len(ins)):
        mine = ins[i] if gather else ins[i].at[me]
        local.append(pltpu.make_async_copy(mine, outs[i].at[me], local_sems.at[i]))
        for k, (pid, pslot) in enumerate(peers):
            src = ins[i] if gather else ins[i].at[pslot]
            sem = i * N_PEER + k
            sends.append(pltpu.make_async_remote_copy(
                src_ref=src, dst_ref=outs[i].at[me], send_sem=send_sems.at[sem], recv_sem=recv_sems.at[sem],
                device_id=pid, device_id_type=pl.DeviceIdType.MESH))
            recvs.append(pltpu.make_async_remote_copy(
                src_ref=src, dst_ref=outs[i].at[pslot], send_sem=send_sems.at[sem], recv_sem=recv_sems.at[sem],
                device_id=pid, device_id_type=pl.DeviceIdType.MESH))
    return local, sends, recvs


def _exchange_start(copies):
    local, sends, _ = copies
    for cp in local + sends:
        cp.start()


def _exchange_wait(copies):
    local, sends, recvs = copies
    for cp in recvs:
        cp.wait_recv()
    for cp in sends:
        cp.wait_send()
    for cp in local:
        cp.wait()


def _exchange_out_shapes(arrs, *, gather):
    return [jax.ShapeDtypeStruct((N_DEV,) + (a.shape if gather else a.shape[1:]), a.dtype) for a in arrs]


def _exchange_sems(n):
    if not n:
        return []
    return [pltpu.SemaphoreType.DMA((n * N_PEER,)), pltpu.SemaphoreType.DMA((n * N_PEER,)), pltpu.SemaphoreType.DMA((n,))]


def _exchange(arrs, *, gather, name):
    n = len(arrs)

    def body(*refs):
        copies = _exchange_copies(refs[:n], refs[n:2 * n], *refs[2 * n:], gather=gather)
        _exchange_start(copies)
        _exchange_wait(copies)

    any_spec = pl.BlockSpec(memory_space=pl.ANY)
    return pl.pallas_call(
        body, name=name, in_specs=[any_spec] * n, out_specs=[any_spec] * n,
        out_shape=_exchange_out_shapes(arrs, gather=gather), scratch_shapes=_exchange_sems(n),
        compiler_params=pltpu.CompilerParams(has_side_effects=True),
    )(*arrs)


def _adamw(parts, w, m, v, *, name, tr):
    r, c = w.shape
    nparts = len(parts)
    rows = r // nparts
    assert rows * nparts == r and rows % tr == 0 and all(p.shape == (N_DEV, rows, c) for p in parts)
    per = rows // tr
    c1 = 1.0 - ADAM_B1 ** ADAM_STEP
    c2 = 1.0 - ADAM_B2 ** ADAM_STEP

    def body(*refs):
        p_refs = refs[:nparts]
        w_ref, m_ref, v_ref, g_ref, d_ref, nm_ref, nv_ref = refs[nparts:]
        which = pl.program_id(0) // per
        for k, p_ref in enumerate(p_refs):
            @pl.when(which == k)
            def _(p_ref=p_ref):
                acc = p_ref[0].astype(F32)
                for j in range(1, N_DEV):
                    acc = acc + p_ref[j].astype(F32)
                g_ref[...] = acc

        g = g_ref[...]
        nm = ADAM_B1 * m_ref[...] + (1.0 - ADAM_B1) * g
        nv = ADAM_B2 * v_ref[...] + (1.0 - ADAM_B2) * (g * g)
        nm_ref[...] = nm
        nv_ref[...] = nv
        d_ref[...] = -ADAM_LR * ((nm / c1) / (jnp.sqrt(nv / c2) + ADAM_EPS) + ADAM_WD * w_ref[...])

    def part_spec(k):
        return pl.BlockSpec((N_DEV, tr, c), lambda i: (0, jnp.clip(i - k * per, 0, per - 1), 0))

    blk = pl.BlockSpec((tr, c), lambda i: (i, 0))
    return pl.pallas_call(
        body, name=name, grid=(r // tr,),
        in_specs=[part_spec(k) for k in range(nparts)] + [blk, blk, blk],
        out_specs=[blk] * 4, out_shape=[jax.ShapeDtypeStruct((r, c), F32)] * 4,
        compiler_params=_params("arbitrary"),
    )(*parts, w, m, v)


def _t5_bucket(rel):
    half = REL_BUCKETS // 2
    max_exact = half // 2
    ret = jnp.where(rel > 0, half, 0)
    n = jnp.abs(rel)
    nf = jnp.maximum(n, 1).astype(F32)
    large = max_exact + (jnp.log(nf / max_exact) / math.log(CHUNK / max_exact) * (half - max_exact)).astype(jnp.int32)
    large = jnp.minimum(large, half - 1)
    return ret + jnp.where(n < max_exact, n, large)


def _split16(w):
    hi = w.astype(BF16)
    return hi, (w - hi.astype(F32)).astype(BF16)


def _cols_to_blocks(g, depth):
    _, r, c8 = g.shape
    return g.reshape(depth, r, N_DEV, c8 // N_DEV).transpose(2, 0, 1, 3).reshape(N_DEV, depth * r, c8 // N_DEV)


def _rows_to_blocks(g, depth):
    _, r8, c = g.shape
    return g.reshape(depth, N_DEV, r8 // N_DEV, c).transpose(1, 0, 2, 3).reshape(N_DEV, depth * r8 // N_DEV, c)


def _blocks_to_cols(a, depth):
    _, dr, c = a.shape
    r = dr // depth
    return a.reshape(N_DEV, depth, r, c).transpose(1, 2, 0, 3).reshape(depth, r, N_DEV * c)


def _blocks_to_rows(a, depth):
    _, dr, c = a.shape
    r = dr // depth
    return a.reshape(N_DEV, depth, r, c).transpose(1, 0, 2, 3).reshape(depth, N_DEV * r, c)


_SMALL = ("rel_bias", "norm1_w", "conv_b", "dt_bias", "a_log", "d_skip", "ssm_norm_w", "attn_sink", "norm2_w",
          "ffn_conv_b", "final_norm_w")
_SHARDED = ("w_in", "conv_w", "w_out", "w_up", "ffn_conv_w", "w_down")
_ORDER = ("rel_bias", "norm1_w", "w_in", "conv_w", "conv_b", "dt_bias", "a_log", "d_skip", "ssm_norm_w", "attn_sink",
          "w_out", "norm2_w", "w_up", "ffn_conv_w", "ffn_conv_b", "w_down", "final_norm_w")


def _pack_small(d):
    flat = jnp.concatenate([d[k].reshape(-1).astype(F32) for k in _SMALL])
    rows = -(-flat.size // (LANES * SUBLANES)) * SUBLANES
    return jnp.pad(flat, (0, rows * LANES - flat.size)).reshape(rows, LANES)


def _unpack_small(packed, like):
    flat = packed.reshape(-1)
    out, off = {}, 0
    for k in _SMALL:
        out[k] = flat[off:off + like[k].size].reshape(like[k].shape)
        off += like[k].size
    return out


def kernel(x, rel_bias, norm1_w, w_in, conv_w, conv_b, dt_bias, a_log, d_skip, ssm_norm_w, attn_sink, w_out, norm2_w, w_up, ffn_conv_w, ffn_conv_b, w_down, final_norm_w, loss_target, m_rel_bias, m_norm1_w, m_w_in, m_conv_w, m_conv_b, m_dt_bias, m_a_log, m_d_skip, m_ssm_norm_w, m_attn_sink, m_w_out, m_norm2_w, m_w_up, m_ffn_conv_w, m_ffn_conv_b, m_w_down, m_final_norm_w, v_rel_bias, v_norm1_w, v_w_in, v_conv_w, v_conv_b, v_dt_bias, v_a_log, v_d_skip, v_ssm_norm_w, v_attn_sink, v_w_out, v_norm2_w, v_w_up, v_ffn_conv_w, v_ffn_conv_b, v_w_down, v_final_norm_w):
    w = dict(rel_bias=rel_bias, norm1_w=norm1_w, w_in=w_in, conv_w=conv_w, conv_b=conv_b, dt_bias=dt_bias, a_log=a_log,
             d_skip=d_skip, ssm_norm_w=ssm_norm_w, attn_sink=attn_sink, w_out=w_out, norm2_w=norm2_w, w_up=w_up,
             ffn_conv_w=ffn_conv_w, ffn_conv_b=ffn_conv_b, w_down=w_down, final_norm_w=final_norm_w)
    m = dict(rel_bias=m_rel_bias, norm1_w=m_norm1_w, w_in=m_w_in, conv_w=m_conv_w, conv_b=m_conv_b, dt_bias=m_dt_bias,
             a_log=m_a_log, d_skip=m_d_skip, ssm_norm_w=m_ssm_norm_w, attn_sink=m_attn_sink, w_out=m_w_out,
             norm2_w=m_norm2_w, w_up=m_w_up, ffn_conv_w=m_ffn_conv_w, ffn_conv_b=m_ffn_conv_b, w_down=m_w_down,
             final_norm_w=m_final_norm_w)
    v = dict(rel_bias=v_rel_bias, norm1_w=v_norm1_w, w_in=v_w_in, conv_w=v_conv_w, conv_b=v_conv_b, dt_bias=v_dt_bias,
             a_log=v_a_log, d_skip=v_d_skip, ssm_norm_w=v_ssm_norm_w, attn_sink=v_attn_sink, w_out=v_w_out,
             norm2_w=v_norm2_w, w_up=v_w_up, ffn_conv_w=v_ffn_conv_w, ffn_conv_b=v_ffn_conv_b, w_down=v_w_down,
             final_norm_w=v_final_norm_w)
    nb, seq, _ = x.shape
    t = nb * seq
    depth = w_in.shape[0]

    flat2 = lambda a: a.reshape(-1, a.shape[-1])
    own_shards = lambda i: [w_out[i].astype(BF16), w_up[i].astype(BF16), w_down[i].astype(BF16)]
    cw_hi, cw_lo = _split16(flat2(conv_w))
    fw_hi, fw_lo = _split16(flat2(ffn_conv_w))
    g_in, g_cwh, g_cwl, g_fwh, g_fwl = _exchange([w_in[0].astype(BF16), cw_hi, cw_lo, fw_hi, fw_lo], gather=True,
                                                 name="gather_weights")
    full_conv_w = _blocks_to_cols(g_cwh.astype(F32) + g_cwl.astype(F32), depth)
    full_ffn_cw = _blocks_to_cols(g_fwh.astype(F32) + g_fwl.astype(F32), depth)

    rel = jnp.arange(KEY_SPAN)[None, :] - CHUNK - jnp.arange(CHUNK)[:, None]
    bucket = _t5_bucket(rel)
    band_bias = _band_bias(rel_bias, bucket)

    def layer_params(i, g_in):
        return dict(n1=norm1_w[i][None], w_in=_to_proj_layout(_blocks_to_cols(g_in, 1)[0]), conv_w=full_conv_w[i],
                    conv_b=conv_b[i][None], dtb=_pad_lanes(dt_bias[i].reshape(-1)), alog=_pad_lanes(a_log[i].reshape(-1)),
                    dvec=jnp.repeat(d_skip[i], HEAD_DIM)[None], ssm_nw=ssm_norm_w[i][None], sink=_pad_lanes(attn_sink[i]),
                    n2=norm2_w[i][None], ffn_cw=full_ffn_cw[i], ffn_cb=ffn_conv_b[i][None])

    h = x.reshape(t, D_MODEL)
    params, saved = [None] * depth, [None] * depth
    for i in range(depth):
        h, params[i], saved[i], nxt = _layer_fwd(h, layer_params(i, g_in), band_bias, nb=nb, seq=seq, own_shards=own_shards(i),
                                                 next_shards=[w_in[i + 1].astype(BF16)] if i + 1 < depth else ())
        if nxt:
            (g_in,) = nxt
    dh, g_final, loss_part = _loss_head(h, loss_target.reshape(t, D_MODEL), final_norm_w[None], tm=512)
    loss = lax.psum(loss_part[0, 0], ("x", "y", "c"))

    w_in_blocks = lambda g: _cols_to_blocks(_from_proj_layout(g["w_in"])[None], 1).astype(BF16)
    dbias = jnp.zeros((N_HEADS, CHUNK, KEY_SPAN), F32)
    grads, pending = [None] * depth, ()
    parts = dict(w_in=[None] * depth, w_out=[None] * depth, w_up=[None] * depth, w_down=[None] * depth)
    for i in reversed(range(depth)):
        dh, dbias, grads[i], arrived = _layer_bwd(dh, params[i], saved[i], band_bias, dbias, nb=nb, seq=seq, pending=pending)
        parts["w_down"][i], parts["w_up"][i], parts["w_out"][i] = arrived[:3]
        if pending:
            parts["w_in"][i + 1] = arrived[3]
        pending = [w_in_blocks(grads[i])]
    grad_x = dh.reshape(nb, seq, D_MODEL)
    stack = lambda k: jnp.stack([g[k] for g in grads])
    last = _exchange(pending + [_cols_to_blocks(stack("conv_w"), depth).astype(BF16),
                                _cols_to_blocks(stack("ffn_cw"), depth).astype(BF16)], gather=False, name="scatter_grads")
    parts["w_in"][0] = last[0]

    out = {}
    for k in ("w_in", "w_out", "w_up", "w_down"):
        rows = parts[k][0].shape[1]
        tr = max(d for d in range(16, 129, 16) if rows % d == 0)
        res = _adamw(parts[k], flat2(w[k]), flat2(m[k]), flat2(v[k]), name="adamw_" + k, tr=tr)
        out[k] = [a.reshape(w[k].shape) for a in res]
    for k, p8 in zip(("conv_w", "ffn_conv_w"), last[1:]):
        res = _adamw([p8], flat2(w[k]), flat2(m[k]), flat2(v[k]), name="adamw_" + k, tr=p8.shape[1])
        out[k] = [a.reshape(w[k].shape) for a in res]

    small = dict(rel_bias=_rel_bias_grad(dbias, bucket)[:, :REL_BUCKETS].T, norm1_w=stack("n1"), conv_b=stack("conv_b"),
                 dt_bias=stack("dtb")[:, 0, :2 * N_HEADS], a_log=stack("alog")[:, 0, :2 * N_HEADS],
                 d_skip=stack("dvec").reshape(depth, N_HEADS, HEAD_DIM).sum(-1), ssm_norm_w=stack("ssm_nw"),
                 attn_sink=stack("sink")[:, 0, :N_HEADS], norm2_w=stack("n2"), ffn_conv_b=stack("ffn_cb"),
                 final_norm_w=g_final)
    (small_parts,) = _exchange([_pack_small(small)], gather=True, name="gather_small_grads")
    res = _adamw([small_parts], _pack_small(w), _pack_small(m), _pack_small(v), name="adamw_small", tr=small_parts.shape[1])
    unpacked = [_unpack_small(a, w) for a in res]
    for k in _SMALL:
        out[k] = [u[k] for u in unpacked]

    return (loss, grad_x, *[out[k][0] for k in _ORDER], *[out[k][1] for k in _ORDER],
            *[out[k][2] for k in _ORDER], *[out[k][3] for k in _ORDER])
```

```python
import math

import numpy as np
import jax
import jax.numpy as jnp
from jax import lax
from jax.experimental import pallas as pl
from jax.experimental.pallas import tpu as pltpu

F32, BF16 = jnp.float32, jnp.bfloat16
HIGHEST = lax.Precision.HIGHEST

D_MODEL = 1024
HEAD_DIM = 64
N_HEADS = 16
N_GROUPS = 2
HEADS_PER_GROUP = N_HEADS // N_GROUPS
N_STATE = 128
SSM_WIDTH = 1024
BC_WIDTH = 256
CONV_CH = SSM_WIDTH + 2 * BC_WIDTH
SSM_TAPS = 7
CHUNK = 128
KV_HEADS = 4
KV_WIDTH = 256
Q_PER_KV = N_HEADS // KV_HEADS
KEY_SPAN = 3 * CHUNK
REL_BUCKETS = 32
D_FF = 2816
FFN_TAPS = 3
IN_COLS = 4128
NORM_EPS = 1e-6
N_DEV = 8

LANES = 128
SUBLANES = 8
VMEM_LIMIT_BYTES = 56 * 1024 * 1024
MM_ROWS = 1024

PZ, PXS, PQ, PB, PC, PK, PV, PDT, PROJ_W = 0, 1024, 2048, 3072, 3328, 3584, 3840, 4096, 4224
OZ, OXBC, ODT, OQ, OK_, OV = 0, 1024, 2560, 2592, 3616, 3872

ADAM_LR, ADAM_B1, ADAM_B2, ADAM_EPS, ADAM_WD, ADAM_STEP = 0.001, 0.9, 0.999, 1e-08, 0.01, 10


def _params(*sem):
    return pltpu.CompilerParams(dimension_semantics=sem, vmem_limit_bytes=VMEM_LIMIT_BYTES)


def _sigmoid(x):
    return 0.5 * jnp.tanh(0.5 * x) + 0.5


def _softplus(x):
    return jnp.maximum(x, 0.0) + jnp.log(1.0 + jnp.exp(-jnp.abs(x)))


def _dot(a, b, dims):
    return lax.dot_general(a, b, (dims, ((), ())), preferred_element_type=F32)


NN = ((1,), (0,))
NT = ((1,), (1,))
TN = ((0,), (0,))


def _matmul(a, b, mode, *, name, tm, tn, tk, res=None, out_dtype=F32, precision=None, b_k0=0):
    assert b_k0 % tk == 0
    ko = b_k0 // tk
    if mode == "nn":
        (m, k), n = a.shape, b.shape[1]
        k2 = k if b.shape[0] >= b_k0 + k else -1
        a_spec = pl.BlockSpec((tm, tk), lambda i, j, kk: (i, kk))
        b_spec = pl.BlockSpec((tk, tn), lambda i, j, kk: (kk + ko, j))
        dims = NN
    elif mode == "nt":
        (m, k), n = a.shape, b.shape[0]
        k2 = k if b.shape[1] >= b_k0 + k else -1
        a_spec = pl.BlockSpec((tm, tk), lambda i, j, kk: (i, kk))
        b_spec = pl.BlockSpec((tn, tk), lambda i, j, kk: (j, kk + ko))
        dims = NT
    else:
        (k, m), (k2, n) = a.shape, b.shape
        a_spec = pl.BlockSpec((tk, tm), lambda i, j, kk: (kk, i))
        b_spec = pl.BlockSpec((tk, tn), lambda i, j, kk: (kk, j))
        dims = TN
    assert k == k2 and m % tm == 0 and n % tn == 0 and k % tk == 0, (name, a.shape, b.shape, tm, tn, tk)
    nk = k // tk
    has_res = res is not None

    def body(*refs):
        a_ref, b_ref = refs[:2]
        r_ref = refs[2] if has_res else None
        o_ref = refs[3] if has_res else refs[2]
        acc = refs[-1] if nk > 1 else None
        kk = pl.program_id(2)
        if precision is None:
            part = _dot(a_ref[...].astype(BF16), b_ref[...].astype(BF16), dims)
        else:
            part = lax.dot_general(a_ref[...], b_ref[...], (dims, ((), ())), precision=precision,
                                   preferred_element_type=F32)

        def finish(r):
            if has_res:
                r = r + r_ref[...].astype(F32)
            o_ref[...] = r.astype(out_dtype)

        if nk == 1:
            finish(part)
        else:
            @pl.when(kk == 0)
            def _():
                acc[...] = jnp.zeros_like(acc)

            acc[...] += part

            @pl.when(kk == nk - 1)
            def _():
                finish(acc[...])

    in_specs = [a_spec, b_spec]
    args = [a, b]
    if has_res:
        in_specs.append(pl.BlockSpec((tm, tn), lambda i, j, kk: (i, j)))
        args.append(res)
    return pl.pallas_call(
        body, name=name, grid=(m // tm, n // tn, nk),
        in_specs=in_specs, out_specs=pl.BlockSpec((tm, tn), lambda i, j, kk: (i, j)),
        out_shape=jax.ShapeDtypeStruct((m, n), out_dtype),
        scratch_shapes=[pltpu.VMEM((tm, tn), F32)] if nk > 1 else [],
        compiler_params=_params("parallel", "parallel", "arbitrary"),
    )(*args)


def _rms_matmul(x, nw, w, *, name, tm, tn, ncols=None):
    t, d = x.shape
    n = w.shape[1] if ncols is None else ncols
    assert t % tm == 0 and n % tn == 0

    def body(x_ref, nw_ref, w_ref, o_ref, h_ref):
        @pl.when(pl.program_id(1) == 0)
        def _():
            xv = x_ref[...]
            r = lax.rsqrt(jnp.mean(xv * xv, axis=-1, keepdims=True) + NORM_EPS)
            h_ref[...] = (xv * r * nw_ref[...]).astype(BF16)

        o_ref[...] = _dot(h_ref[...], w_ref[...].astype(BF16), NN).astype(BF16)

    return pl.pallas_call(
        body, name=name, grid=(t // tm, n // tn),
        in_specs=[pl.BlockSpec((tm, d), lambda i, j: (i, 0)),
                  pl.BlockSpec((1, d), lambda i, j: (0, 0)),
                  pl.BlockSpec((d, tn), lambda i, j: (0, j))],
        out_specs=[pl.BlockSpec((tm, tn), lambda i, j: (i, j)),
                   pl.BlockSpec((tm, d), lambda i, j: (i, 0))],
        out_shape=[jax.ShapeDtypeStruct((t, n), BF16), jax.ShapeDtypeStruct((t, d), BF16)],
        compiler_params=_params("parallel", "arbitrary"),
    )(x, nw, w)


def _zero_ext(v):
    z = jnp.zeros((SUBLANES, v.shape[1]), v.dtype)
    return jnp.concatenate([z, v, z], axis=0)


def _shifted(v_ext, offset, seq):
    if offset == 0:
        return v_ext[SUBLANES:SUBLANES + seq]
    return pltpu.roll(v_ext, (-offset) % (seq + 2 * SUBLANES), 0)[SUBLANES:SUBLANES + seq]


def _conv_taps(v, w_ref, taps, seq):
    pad = taps // 2
    v_ext = _zero_ext(v)
    acc = None
    for k in range(taps):
        term = _shifted(v_ext, k - pad, seq) * w_ref[k:k + 1, :]
        acc = term if acc is None else acc + term
    return acc


def _ssm_conv_fwd(proj, cw, cb, *, nb, seq):
    width = 512

    def body(x_ref, w_ref, b_ref, o_ref, g_ref):
        g = _conv_taps(x_ref[...].astype(F32), w_ref, SSM_TAPS, seq) + b_ref[...]
        o_ref[...] = g * _sigmoid(g)
        g_ref[...] = g.astype(BF16)

    def col(j):
        return jnp.where(j < 2, j + PXS // width, PB // width)

    return pl.pallas_call(
        body, name="ssm_conv_fwd", grid=(nb, CONV_CH // width),
        in_specs=[pl.BlockSpec((seq, width), lambda b, j: (b, col(j))),
                  pl.BlockSpec((SSM_TAPS, width), lambda b, j: (0, j)),
                  pl.BlockSpec((1, width), lambda b, j: (0, j))],
        out_specs=[pl.BlockSpec((seq, width), lambda b, j: (b, j))] * 2,
        out_shape=[jax.ShapeDtypeStruct((nb * seq, CONV_CH), F32), jax.ShapeDtypeStruct((nb * seq, CONV_CH), BF16)],
        compiler_params=_params("parallel", "parallel"),
    )(proj, cw, cb)


def _ffn_act_fwd(gu, cw, cb, *, nb, seq):
    width = 256
    nj = D_FF // width

    def body(g_ref, u_ref, w_ref, b_ref, o_ref, s_ref):
        g = _conv_taps(g_ref[...].astype(F32), w_ref, FFN_TAPS, seq) + b_ref[...]
        o_ref[...] = (g * _sigmoid(g) * u_ref[...].astype(F32)).astype(BF16)
        s_ref[...] = g.astype(BF16)

    return pl.pallas_call(
        body, name="ffn_act_fwd", grid=(nb, nj),
        in_specs=[pl.BlockSpec((seq, width), lambda b, j: (b, j)),
                  pl.BlockSpec((seq, width), lambda b, j: (b, j + nj)),
                  pl.BlockSpec((FFN_TAPS, width), lambda b, j: (0, j)),
                  pl.BlockSpec((1, width), lambda b, j: (0, j))],
        out_specs=[pl.BlockSpec((seq, width), lambda b, j: (b, j))] * 2,
        out_shape=[jax.ShapeDtypeStruct((nb * seq, D_FF), BF16)] * 2,
        compiler_params=_params("parallel", "parallel"),
    )(gu, gu, cw, cb)


def _scan_setup(d, dt_ref, dtb_ref, alog_ref, z_ref, zt_ref, *, lower_when_dir0, inclusive):
    is0 = d == 0
    dt_all = _softplus(dt_ref[...] + dtb_ref[...])
    adt_all = dt_all * (-jnp.exp(alog_ref[...]))
    li = lax.broadcasted_iota(jnp.int32, (CHUNK, CHUNK), 0)
    si = lax.broadcasted_iota(jnp.int32, (CHUNK, CHUNK), 1)
    lower = is0 if lower_when_dir0 else jnp.logical_not(is0)
    ahead = jnp.where(lower, li - si, si - li)
    mask = ahead >= 0
    tri = mask if inclusive else ahead > 0
    z_all = jnp.dot(tri.astype(F32), adt_all, precision=HIGHEST, preferred_element_type=F32)
    zt_all = z_all.T
    z_ref[...] = jnp.where(is0, z_all[:, 0:N_HEADS], z_all[:, N_HEADS:2 * N_HEADS])
    zt_ref[...] = jnp.where(is0, zt_all[0:N_HEADS, :], zt_all[N_HEADS:2 * N_HEADS, :])
    dt = jnp.where(is0, dt_all[:, 0:N_HEADS], dt_all[:, N_HEADS:2 * N_HEADS])
    adt = jnp.where(is0, adt_all[:, 0:N_HEADS], adt_all[:, N_HEADS:2 * N_HEADS])
    tot = jnp.sum(adt, axis=0, keepdims=True)
    return ahead, dt, tot, dt_all


def _chunk_index(nchunk, forward_when_dir0):
    def idx(d, b, c):
        fwd = (d == 0) if forward_when_dir0 else (d != 0)
        return b * nchunk + jnp.where(fwd, c, nchunk - 1 - c)
    return idx


def _ssd_fwd(xbc, dtraw, dtb, alog, *, nb, seq, gather=()):
    nchunk = seq // CHUNK
    t = nb * seq
    row = _chunk_index(nchunk, True)
    ng = len(gather)

    def body(*refs):
        xs_ref, bc_ref, dt_ref, dtb_ref, alog_ref = refs[:5]
        o_ref, hs_ref = refs[5 + ng:7 + ng]
        h_ref, z_ref, zt_ref, dts_ref = refs[7 + 2 * ng:11 + 2 * ng]
        d, c = pl.program_id(0), pl.program_id(2)
        if ng:
            step = (d * nb + pl.program_id(1)) * nchunk + c
            copies = _exchange_copies(refs[5:5 + ng], refs[7 + ng:7 + 2 * ng], *refs[11 + 2 * ng:], gather=True)
            pl.when(step == 0)(lambda: _exchange_start(copies))

        @pl.when(c == 0)
        def _():
            h_ref[...] = jnp.zeros_like(h_ref)

        ahead, dt, tot, _ = _scan_setup(d, dt_ref, dtb_ref, alog_ref, z_ref, zt_ref,
                                        lower_when_dir0=True, inclusive=True)
        mask = ahead >= 0
        dts_ref[...] = dt
        e_tot = jnp.exp(tot)
        for g in range(N_GROUPS):
            heads = range(g * HEADS_PER_GROUP, (g + 1) * HEADS_PER_GROUP)
            bg = bc_ref[:, g * N_STATE:(g + 1) * N_STATE]
            cg = bc_ref[:, BC_WIDTH + g * N_STATE:BC_WIDTH + (g + 1) * N_STATE]
            cb = _dot(cg.astype(BF16), bg.astype(BF16), NT)
            zc = {h: jnp.broadcast_to(z_ref[:, h:h + 1], (CHUNK, CHUNK)) for h in heads}
            decay = {h: jnp.exp(jnp.where(mask, zc[h] - zt_ref[h:h + 1, :], -jnp.inf)) for h in heads}
            u = {h: (xs_ref[:, h * HEAD_DIM:(h + 1) * HEAD_DIM] * dts_ref[:, h:h + 1]).astype(BF16) for h in heads}
            state = {h: h_ref[h] for h in heads}
            for h in heads:
                hs_ref[0, 0, h] = state[h]
            mix = {h: (cb * decay[h]).astype(BF16) for h in heads}
            cz = {h: (cg * jnp.exp(zc[h])).astype(BF16) for h in heads}
            bw = {h: (bg * jnp.exp(tot[:, h:h + 1] - zc[h])).astype(BF16) for h in heads}
            y = {h: _dot(mix[h], u[h], NN) + _dot(cz[h], state[h].astype(BF16), NT) for h in heads}
            new = {h: state[h] * e_tot[:, h:h + 1] + _dot(u[h], bw[h], TN) for h in heads}
            for h in heads:
                h_ref[h] = new[h]
                o_ref[0, :, h * HEAD_DIM:(h + 1) * HEAD_DIM] = y[h].astype(BF16)
        if ng:
            pl.when(step == 2 * nb * nchunk - 1)(lambda: _exchange_wait(copies))

    any_spec = pl.BlockSpec(memory_space=pl.ANY)
    res = pl.pallas_call(
        body, name="ssd_fwd_gather" if ng else "ssd_fwd", grid=(2, nb, nchunk),
        in_specs=[pl.BlockSpec((CHUNK, SSM_WIDTH), lambda d, b, c: (row(d, b, c), 0)),
                  pl.BlockSpec((CHUNK, 2 * BC_WIDTH), lambda d, b, c: (row(d, b, c), SSM_WIDTH // (2 * BC_WIDTH))),
                  pl.BlockSpec((CHUNK, LANES), lambda d, b, c: (row(d, b, c), 0)),
                  pl.BlockSpec((1, LANES), lambda d, b, c: (0, 0)),
                  pl.BlockSpec((1, LANES), lambda d, b, c: (0, 0))] + [any_spec] * ng,
        out_specs=[pl.BlockSpec((1, CHUNK, SSM_WIDTH), lambda d, b, c: (d, row(d, b, c), 0)),
                   pl.BlockSpec((1, 1, N_HEADS, HEAD_DIM, N_STATE), lambda d, b, c: (d, row(d, b, c), 0, 0, 0))]
        + [any_spec] * ng,
        out_shape=[jax.ShapeDtypeStruct((2, t, SSM_WIDTH), BF16),
                   jax.ShapeDtypeStruct((2, nb * nchunk, N_HEADS, HEAD_DIM, N_STATE), F32)]
        + _exchange_out_shapes(gather, gather=True),
        scratch_shapes=_SSD_SCRATCH + _exchange_sems(ng),
        compiler_params=_params("arbitrary", "arbitrary", "arbitrary"),
    )(xbc, xbc, dtraw, dtb, alog, *gather)
    return res[0], res[1], list(res[2:])


def _gate_norm_fwd(y2, xbc, proj, dvec, nw, *, tm):
    t = xbc.shape[0]
    half = SSM_WIDTH // N_GROUPS

    def body(y_ref, xs_ref, z_ref, d_ref, w_ref, o_ref):
        z = z_ref[...].astype(F32)
        p = (y_ref[0].astype(F32) + y_ref[1].astype(F32) + d_ref[...] * xs_ref[...]) * (z * _sigmoid(z))
        for g in range(N_GROUPS):
            pg = p[:, g * half:(g + 1) * half]
            r = lax.rsqrt(jnp.mean(pg * pg, axis=-1, keepdims=True) + NORM_EPS)
            o_ref[:, g * half:(g + 1) * half] = (pg * r * w_ref[:, g * half:(g + 1) * half]).astype(BF16)

    return pl.pallas_call(
        body, name="gate_norm_fwd", grid=(t // tm,),
        in_specs=[pl.BlockSpec((2, tm, SSM_WIDTH), lambda i: (0, i, 0)),
                  pl.BlockSpec((tm, SSM_WIDTH), lambda i: (i, 0)),
                  pl.BlockSpec((tm, SSM_WIDTH), lambda i: (i, PZ // SSM_WIDTH)),
                  pl.BlockSpec((1, SSM_WIDTH), lambda i: (0, 0)),
                  pl.BlockSpec((1, SSM_WIDTH), lambda i: (0, 0))],
        out_specs=pl.BlockSpec((tm, SSM_WIDTH), lambda i: (i, 0)),
        out_shape=jax.ShapeDtypeStruct((t, SSM_WIDTH), BF16),
        compiler_params=_params("parallel"),
    )(y2, xbc, proj, dvec, nw)


GROUP_ROWS = Q_PER_KV * CHUNK


def _keys_inside(n, nblk):
    kpos = (n - 1) * CHUNK + lax.broadcasted_iota(jnp.int32, (1, KEY_SPAN), 1)
    return (kpos >= 0) & (kpos < nblk * CHUNK)


def _per_head_column(ref, g):
    blk = lax.broadcasted_iota(jnp.int32, (GROUP_ROWS, 1), 0) // CHUNK
    col = jnp.zeros((GROUP_ROWS, 1), F32)
    for r in range(Q_PER_KV):
        h = g * Q_PER_KV + r
        col = jnp.where(blk == r, ref[:, h:h + 1], col)
    return col


def _stack_heads(ref, g, dtype):
    return jnp.concatenate([ref[:, (g * Q_PER_KV + r) * HEAD_DIM:(g * Q_PER_KV + r + 1) * HEAD_DIM].astype(dtype)
                            for r in range(Q_PER_KV)], axis=0)


def _kv_specs(nblk):
    kvb = PK // (2 * KV_WIDTH)

    def at(off):
        def idx(b, n):
            return (b * nblk + jnp.clip(n + off, 0, nblk - 1), kvb)
        return pl.BlockSpec((CHUNK, 2 * KV_WIDTH), idx)
    return [at(-1), at(0), at(1)]


def _attn_fwd(proj, bias, sink, *, nb, seq, gather=()):
    nblk = seq // CHUNK
    t = nb * seq
    scale = HEAD_DIM ** -0.5
    ng = len(gather)

    def body(*refs):
        q_ref, kp_ref, kc_ref, kn_ref, bias_ref, sink_ref = refs[:6]
        o_ref, lse_ref = refs[6 + ng:8 + ng]
        n = pl.program_id(1)
        if ng:
            step = pl.program_id(0) * nblk + n
            copies = _exchange_copies(refs[6:6 + ng], refs[8 + ng:8 + 2 * ng], *refs[8 + 2 * ng:], gather=True)
            pl.when(step == 0)(lambda: _exchange_start(copies))
        inside = _keys_inside(n, nblk)
        groups = range(KV_HEADS)

        def keys(g, off):
            cs = slice(off + g * HEAD_DIM, off + (g + 1) * HEAD_DIM)
            return jnp.concatenate([kp_ref[:, cs], kc_ref[:, cs], kn_ref[:, cs]], axis=0).astype(BF16)

        qs = [(_stack_heads(q_ref, g, F32) * scale).astype(BF16) for g in groups]
        ss = [jnp.where(inside, _dot(qs[g], keys(g, 0), NT)
                        + bias_ref[g * Q_PER_KV:(g + 1) * Q_PER_KV].reshape(GROUP_ROWS, KEY_SPAN), -jnp.inf) for g in groups]
        sks = [_per_head_column(sink_ref, g) for g in groups]
        ms = [jnp.maximum(jnp.max(ss[g], axis=-1, keepdims=True), sks[g]) for g in groups]
        ps = [jnp.exp(ss[g] - ms[g]) for g in groups]
        denoms = [jnp.sum(ps[g], axis=-1, keepdims=True) + jnp.exp(sks[g] - ms[g]) for g in groups]
        outs = [(_dot(ps[g].astype(BF16), keys(g, KV_WIDTH), NN) * (1.0 / denoms[g])).astype(BF16) for g in groups]
        lses = []
        for g in groups:
            lse = ms[g] + jnp.log(denoms[g])
            for r in range(Q_PER_KV):
                h = g * Q_PER_KV + r
                o_ref[:, h * HEAD_DIM:(h + 1) * HEAD_DIM] = outs[g][r * CHUNK:(r + 1) * CHUNK]
                lses.append(lse[r * CHUNK:(r + 1) * CHUNK])
        lse_ref[...] = jnp.concatenate(lses, axis=1)
        if ng:
            pl.when(step == nb * nblk - 1)(lambda: _exchange_wait(copies))

    any_spec = pl.BlockSpec(memory_space=pl.ANY)
    res = pl.pallas_call(
        body, name="attn_fwd_gather" if ng else "attn_fwd", grid=(nb, nblk),
        in_specs=[pl.BlockSpec((CHUNK, D_MODEL), lambda b, n: (b * nblk + n, PQ // D_MODEL))] + _kv_specs(nblk) + [
            pl.BlockSpec((N_HEADS, CHUNK, KEY_SPAN), lambda b, n: (0, 0, 0)),
            pl.BlockSpec((1, LANES), lambda b, n: (0, 0))] + [any_spec] * ng,
        out_specs=[pl.BlockSpec((CHUNK, D_MODEL), lambda b, n: (b * nblk + n, 0)),
                   pl.BlockSpec((CHUNK, N_HEADS), lambda b, n: (b * nblk + n, 0))] + [any_spec] * ng,
        out_shape=[jax.ShapeDtypeStruct((t, D_MODEL), BF16), jax.ShapeDtypeStruct((t, N_HEADS), F32)]
        + _exchange_out_shapes(gather, gather=True),
        scratch_shapes=_exchange_sems(ng),
        compiler_params=_params("arbitrary", "arbitrary"),
    )(proj, proj, proj, proj, bias, sink, *gather)
    return res[0], res[1], list(res[2:])


def _loss_head(x, tgt, nw, *, tm):
    t, d = x.shape

    def body(x_ref, t_ref, w_ref, dx_ref, dw_ref, l_ref):
        @pl.when(pl.program_id(0) == 0)
        def _():
            dw_ref[...] = jnp.zeros_like(dw_ref)
            l_ref[...] = jnp.zeros_like(l_ref)

        xv = x_ref[...]
        w = w_ref[...]
        r = lax.rsqrt(jnp.mean(xv * xv, axis=-1, keepdims=True) + NORM_EPS)
        xh = xv * r
        err = xh * w - t_ref[...]
        l_ref[...] += jnp.sum(err * err) * (0.5 / d)
        dy = err * (1.0 / d)
        gw = dy * w
        dx_ref[...] = r * (gw - xh * jnp.mean(gw * xh, axis=-1, keepdims=True))
        dw_ref[...] += jnp.sum(dy * xh, axis=0, keepdims=True)

    return pl.pallas_call(
        body, name="loss_head", grid=(t // tm,),
        in_specs=[pl.BlockSpec((tm, d), lambda i: (i, 0)), pl.BlockSpec((tm, d), lambda i: (i, 0)),
                  pl.BlockSpec((1, d), lambda i: (0, 0))],
        out_specs=[pl.BlockSpec((tm, d), lambda i: (i, 0)), pl.BlockSpec((1, d), lambda i: (0, 0)),
                   pl.BlockSpec((1, LANES), lambda i: (0, 0))],
        out_shape=[jax.ShapeDtypeStruct((t, d), F32), jax.ShapeDtypeStruct((1, d), F32),
                   jax.ShapeDtypeStruct((1, LANES), F32)],
        compiler_params=_params("arbitrary"),
    )(x, tgt, nw)


def _to_proj_layout(w):
    pad = jnp.zeros(w.shape[:-1] + (PROJ_W - IN_COLS,), w.dtype)
    return jnp.concatenate([w[..., OZ:OXBC], w[..., OXBC:OXBC + SSM_WIDTH], w[..., OQ:OK_],
                            w[..., OXBC + SSM_WIDTH:ODT], w[..., OK_:IN_COLS], w[..., ODT:OQ], pad], axis=-1)


def _from_proj_layout(g):
    return jnp.concatenate([g[..., PZ:PZ + 2 * SSM_WIDTH], g[..., PB:PB + 2 * BC_WIDTH], g[..., PDT:PDT + 2 * N_HEADS],
                            g[..., PQ:PQ + D_MODEL], g[..., PK:PK + 2 * KV_WIDTH]], axis=-1)


def _pad_lanes(v):
    return jnp.pad(v.reshape(1, -1), ((0, 0), (0, LANES - v.size)))


def _layer_fwd(x, p, band_bias, *, nb, seq, own_shards, next_shards=()):
    proj, h1 = _rms_matmul(x, p["n1"], p["w_in"], name="in_proj", tm=MM_ROWS, tn=1024, ncols=PDT)
    dtraw = _matmul(h1, p["w_in"][:, PDT:], "nn", name="in_proj_dt", tm=MM_ROWS, tn=LANES, tk=D_MODEL)
    xbc, gconv = _ssm_conv_fwd(proj, p["conv_w"], p["conv_b"], nb=nb, seq=seq)
    y2, states, (g_out, g_up, g_down) = _ssd_fwd(xbc, dtraw, p["dtb"], p["alog"], nb=nb, seq=seq, gather=own_shards)
    p = dict(p, w_out=_blocks_to_rows(g_out, 1)[0], w_up=_blocks_to_cols(g_up, 1)[0], w_down=_blocks_to_rows(g_down, 1)[0])
    y_ssm = _gate_norm_fwd(y2, xbc, proj, p["dvec"], p["ssm_nw"], tm=256)
    y_att, lse, gathered = _attn_fwd(proj, band_bias, p["sink"], nb=nb, seq=seq, gather=next_shards)
    x1 = _matmul(y_ssm, p["w_out"], "nn", name="out_proj_ssm", tm=MM_ROWS, tn=1024, tk=1024, res=x)
    x1 = _matmul(y_att, p["w_out"], "nn", name="out_proj_att", tm=MM_ROWS, tn=1024, tk=1024, res=x1, b_k0=SSM_WIDTH)
    gu, h2 = _rms_matmul(x1, p["n2"], p["w_up"], name="up_proj", tm=MM_ROWS, tn=1408)
    act, fconv = _ffn_act_fwd(gu, p["ffn_cw"], p["ffn_cb"], nb=nb, seq=seq)
    x2 = _matmul(act, p["w_down"], "nn", name="down_proj", tm=MM_ROWS, tn=1024, tk=1408, res=x1)
    saved = dict(x=x, proj=proj, dtraw=dtraw, h1=h1, xbc=xbc, gconv=gconv, y2=y2, states=states, y_ssm=y_ssm, y_att=y_att,
                 lse=lse, x1=x1, gu=gu, fconv=fconv, h2=h2, act=act)
    return x2, p, saved, gathered


def _rms_bwd(x, dh, nw, dres, *, tm, name):
    t, d = x.shape

    def body(x_ref, dh_ref, w_ref, r_ref, dx_ref, dw_ref):
        @pl.when(pl.program_id(0) == 0)
        def _():
            dw_ref[...] = jnp.zeros_like(dw_ref)

        xv = x_ref[...]
        dh_v = dh_ref[...].astype(F32)
        r = lax.rsqrt(jnp.mean(xv * xv, axis=-1, keepdims=True) + NORM_EPS)
        xh = xv * r
        gw = dh_v * w_ref[...]
        dx_ref[...] = r_ref[...] + r * (gw - xh * jnp.mean(gw * xh, axis=-1, keepdims=True))
        dw_ref[...] += jnp.sum(dh_v * xh, axis=0, keepdims=True)

    row = pl.BlockSpec((tm, d), lambda i: (i, 0))
    vec = pl.BlockSpec((1, d), lambda i: (0, 0))
    return pl.pallas_call(
        body, name=name, grid=(t // tm,), in_specs=[row, row, vec, row], out_specs=[row, vec],
        out_shape=[jax.ShapeDtypeStruct((t, d), F32), jax.ShapeDtypeStruct((1, d), F32)],
        compiler_params=_params("arbitrary"),
    )(x, dh, nw, dres)


def _dsilu(g, sg):
    return sg * (1.0 + g * (1.0 - sg))


def _conv_taps_bwd(gpre, dg, w_ref, dwb_ref, taps, seq):
    pad = taps // 2
    dg_ext, gpre_ext = _zero_ext(dg), _zero_ext(gpre)
    dpre = None
    for k in range(taps):
        term = _shifted(dg_ext, pad - k, seq) * w_ref[k:k + 1, :]
        dpre = term if dpre is None else dpre + term
        dwb_ref[k:k + 1, :] += jnp.sum(dg * _shifted(gpre_ext, k - pad, seq), axis=0, keepdims=True)
    dwb_ref[SUBLANES - 1:SUBLANES, :] += jnp.sum(dg, axis=0, keepdims=True)
    return dpre


def _ffn_act_bwd(gu, gconv, dact, cw, *, nb, seq):
    width = 256
    nj = D_FF // width

    def body(g_ref, u_ref, s_ref, da_ref, w_ref, dg_ref, du_ref, dwb_ref):
        @pl.when(pl.program_id(1) == 0)
        def _():
            dwb_ref[...] = jnp.zeros_like(dwb_ref)

        g = s_ref[...].astype(F32)
        sg = _sigmoid(g)
        da = da_ref[...].astype(F32)
        du_ref[...] = (da * g * sg).astype(BF16)
        dgc = da * u_ref[...].astype(F32) * _dsilu(g, sg)
        dg_ref[...] = _conv_taps_bwd(g_ref[...].astype(F32), dgc, w_ref, dwb_ref, FFN_TAPS, seq).astype(BF16)

    blk = lambda off: pl.BlockSpec((seq, width), lambda j, b: (b, j + off))
    return pl.pallas_call(
        body, name="ffn_act_bwd", grid=(nj, nb),
        in_specs=[blk(0), blk(nj), blk(0), blk(0), pl.BlockSpec((FFN_TAPS, width), lambda j, b: (0, j))],
        out_specs=[blk(0), blk(0), pl.BlockSpec((SUBLANES, width), lambda j, b: (0, j))],
        out_shape=[jax.ShapeDtypeStruct((nb * seq, D_FF), BF16), jax.ShapeDtypeStruct((nb * seq, D_FF), BF16),
                   jax.ShapeDtypeStruct((SUBLANES, D_FF), F32)],
        compiler_params=_params("parallel", "arbitrary"),
    )(gu, gu, gconv, dact, cw)


def _ssm_conv_bwd(proj, gconv, pair, cw, *, nb, seq, name, width, proj_col, conv_col, ncol, extra=None, scale=None):
    has_extra = extra is not None

    def body(*refs):
        if has_extra:
            x_ref, g_ref, p_ref, w_ref, e_ref, s_ref, dx_ref, dwb_ref = refs
        else:
            x_ref, g_ref, p_ref, w_ref, dx_ref, dwb_ref = refs

        @pl.when(pl.program_id(1) == 0)
        def _():
            dwb_ref[...] = jnp.zeros_like(dwb_ref)

        g = g_ref[...].astype(F32)
        da = p_ref[0] + p_ref[1]
        if has_extra:
            da = da + e_ref[...] * s_ref[...]
        dx_ref[...] = _conv_taps_bwd(x_ref[...].astype(F32), da * _dsilu(g, _sigmoid(g)), w_ref, dwb_ref, SSM_TAPS,
                                     seq).astype(BF16)

    in_specs = [pl.BlockSpec((seq, width), lambda j, b: (b, j + proj_col)),
                pl.BlockSpec((seq, width), lambda j, b: (b, j + conv_col)),
                pl.BlockSpec((2, seq, width), lambda j, b: (0, b, j)),
                pl.BlockSpec((SSM_TAPS, width), lambda j, b: (0, j + conv_col))]
    args = [proj, gconv, pair, cw]
    if has_extra:
        in_specs += [pl.BlockSpec((seq, width), lambda j, b: (b, j)), pl.BlockSpec((1, width), lambda j, b: (0, j))]
        args += [extra, scale]
    return pl.pallas_call(
        body, name=name, grid=(ncol, nb), in_specs=in_specs,
        out_specs=[pl.BlockSpec((seq, width), lambda j, b: (b, j)), pl.BlockSpec((SUBLANES, width), lambda j, b: (0, j))],
        out_shape=[jax.ShapeDtypeStruct((nb * seq, ncol * width), BF16), jax.ShapeDtypeStruct((SUBLANES, ncol * width), F32)],
        compiler_params=_params("parallel", "arbitrary"),
    )(*args)


def _attn_bwd(proj, dmix, y_att, lse, bias, sink, dbias_in, *, nb, seq):
    nblk = seq // CHUNK
    t = nb * seq
    scale = HEAD_DIM ** -0.5

    def body(q_ref, kp_ref, kc_ref, kn_ref, do_ref, o_ref, lse_ref, bias_ref, sink_ref, dbin_ref,
             dq_ref, dkv_ref, dbias_ref, dsink_ref):
        b, n = pl.program_id(0), pl.program_id(1)

        @pl.when(n == 0)
        def _():
            dkv_ref[...] = jnp.zeros_like(dkv_ref)

        @pl.when((n == 0) & (b == 0))
        def _():
            dbias_ref[...] = dbin_ref[...]
            dsink_ref[...] = jnp.zeros_like(dsink_ref)

        inside = _keys_inside(n, nblk)
        lane = lax.broadcasted_iota(jnp.int32, (1, LANES), 1)
        dsink = jnp.zeros((1, LANES), F32)
        rows = pl.ds(pl.multiple_of(n * CHUNK, CHUNK), KEY_SPAN)
        groups = range(KV_HEADS)

        def keys(g, off):
            cs = slice(off + g * HEAD_DIM, off + (g + 1) * HEAD_DIM)
            return jnp.concatenate([kp_ref[:, cs], kc_ref[:, cs], kn_ref[:, cs]], axis=0).astype(BF16)

        kcat = [keys(g, 0) for g in groups]
        vcat = [keys(g, KV_WIDTH) for g in groups]
        q = [(_stack_heads(q_ref, g, F32) * scale).astype(BF16) for g in groups]
        do = [_stack_heads(do_ref, g, F32) for g in groups]
        do16 = [do[g].astype(BF16) for g in groups]
        lse = [jnp.concatenate([lse_ref[:, g * Q_PER_KV + r:g * Q_PER_KV + r + 1] for r in range(Q_PER_KV)], axis=0)
               for g in groups]
        s = [jnp.where(inside, _dot(q[g], kcat[g], NT)
                       + bias_ref[g * Q_PER_KV:(g + 1) * Q_PER_KV].reshape(GROUP_ROWS, KEY_SPAN), -jnp.inf) for g in groups]
        p = [jnp.exp(s[g] - lse[g]) for g in groups]
        delta = [jnp.sum(do[g] * _stack_heads(o_ref, g, F32), axis=-1, keepdims=True) for g in groups]
        ds = [p[g] * (_dot(do16[g], vcat[g], NT) - delta[g]) for g in groups]
        ds16 = [ds[g].astype(BF16) for g in groups]
        sink_part = [jnp.exp(_per_head_column(sink_ref, g) - lse[g]) * delta[g] for g in groups]
        dq = [_dot(ds16[g], kcat[g], NN) * scale for g in groups]
        dk = [_dot(ds16[g], q[g], TN) for g in groups]
        dv = [_dot(p[g].astype(BF16), do16[g], TN) for g in groups]
        for g in groups:
            dbias_ref[g * Q_PER_KV:(g + 1) * Q_PER_KV] += ds[g].reshape(Q_PER_KV, CHUNK, KEY_SPAN)
            for r in range(Q_PER_KV):
                h = g * Q_PER_KV + r
                dq_ref[:, h * HEAD_DIM:(h + 1) * HEAD_DIM] = dq[g][r * CHUNK:(r + 1) * CHUNK].astype(BF16)
                dsink = dsink - jnp.where(lane == h, jnp.sum(sink_part[g][r * CHUNK:(r + 1) * CHUNK], axis=0, keepdims=True), 0.0)
            dkv_ref[0, rows, g * HEAD_DIM:(g + 1) * HEAD_DIM] += dk[g]
            dkv_ref[0, rows, KV_WIDTH + g * HEAD_DIM:KV_WIDTH + (g + 1) * HEAD_DIM] += dv[g]
        dsink_ref[...] += dsink

    blk = lambda cb: pl.BlockSpec((CHUNK, D_MODEL), lambda b, n: (b * nblk + n, cb))
    whole = pl.BlockSpec((N_HEADS, CHUNK, KEY_SPAN), lambda b, n: (0, 0, 0))
    vec = pl.BlockSpec((1, LANES), lambda b, n: (0, 0))
    return pl.pallas_call(
        body, name="attn_bwd", grid=(nb, nblk),
        in_specs=[blk(PQ // D_MODEL)] + _kv_specs(nblk) + [
            blk(1), blk(0), pl.BlockSpec((CHUNK, N_HEADS), lambda b, n: (b * nblk + n, 0)), whole, vec, whole],
        out_specs=[blk(0), pl.BlockSpec((1, seq + 2 * CHUNK, 2 * KV_WIDTH), lambda b, n: (b, 0, 0)), whole, vec],
        out_shape=[jax.ShapeDtypeStruct((t, D_MODEL), BF16),
                   jax.ShapeDtypeStruct((nb, seq + 2 * CHUNK, 2 * KV_WIDTH), F32),
                   jax.ShapeDtypeStruct((N_HEADS, CHUNK, KEY_SPAN), F32),
                   jax.ShapeDtypeStruct((1, LANES), F32)],
        compiler_params=_params("arbitrary", "arbitrary"),
    )(proj, proj, proj, proj, dmix, y_att, lse, bias, sink, dbias_in)


def _gate_norm_bwd(y2, xbc, proj, dmix, dvec, nw, *, tm):
    t = xbc.shape[0]
    half = SSM_WIDTH // N_GROUPS

    def body(y_ref, xs_ref, z_ref, do_ref, d_ref, w_ref, dyv_ref, dz_ref, dd_ref, dw_ref):
        @pl.when(pl.program_id(0) == 0)
        def _():
            dd_ref[...] = jnp.zeros_like(dd_ref)
            dw_ref[...] = jnp.zeros_like(dw_ref)

        z = z_ref[...].astype(F32)
        xs = xs_ref[...]
        sg = _sigmoid(z)
        gz = z * sg
        yv = y_ref[0].astype(F32) + y_ref[1].astype(F32) + d_ref[...] * xs
        p = yv * gz
        do = do_ref[...].astype(F32)
        for g in range(N_GROUPS):
            cs = slice(g * half, (g + 1) * half)
            pg = p[:, cs]
            r = lax.rsqrt(jnp.mean(pg * pg, axis=-1, keepdims=True) + NORM_EPS)
            ph = pg * r
            gw = do[:, cs] * w_ref[:, cs]
            dp = r * (gw - ph * jnp.mean(gw * ph, axis=-1, keepdims=True))
            dyv = dp * gz[:, cs]
            dyv_ref[:, cs] = dyv
            dz_ref[:, cs] = (dp * yv[:, cs] * _dsilu(z[:, cs], sg[:, cs])).astype(BF16)
            dw_ref[:, cs] += jnp.sum(do[:, cs] * ph, axis=0, keepdims=True)
            dd_ref[:, cs] += jnp.sum(dyv * xs[:, cs], axis=0, keepdims=True)

    row = lambda cb: pl.BlockSpec((tm, SSM_WIDTH), lambda i: (i, cb))
    vec = pl.BlockSpec((1, SSM_WIDTH), lambda i: (0, 0))
    return pl.pallas_call(
        body, name="gate_norm_bwd", grid=(t // tm,),
        in_specs=[pl.BlockSpec((2, tm, SSM_WIDTH), lambda i: (0, i, 0)), row(0), row(PZ // SSM_WIDTH), row(0), vec, vec],
        out_specs=[row(0), row(0), vec, vec],
        out_shape=[jax.ShapeDtypeStruct((t, SSM_WIDTH), F32), jax.ShapeDtypeStruct((t, SSM_WIDTH), BF16),
                   jax.ShapeDtypeStruct((1, SSM_WIDTH), F32), jax.ShapeDtypeStruct((1, SSM_WIDTH), F32)],
        compiler_params=_params("arbitrary"),
    )(y2, xbc, proj, dmix, dvec, nw)


def _ssd_specs(nchunk, row):
    return [pl.BlockSpec((CHUNK, SSM_WIDTH), lambda d, b, c: (row(d, b, c), 0)),
            pl.BlockSpec((CHUNK, 2 * BC_WIDTH), lambda d, b, c: (row(d, b, c), SSM_WIDTH // (2 * BC_WIDTH))),
            pl.BlockSpec((CHUNK, LANES), lambda d, b, c: (row(d, b, c), 0)),
            pl.BlockSpec((1, LANES), lambda d, b, c: (0, 0)),
            pl.BlockSpec((1, LANES), lambda d, b, c: (0, 0)),
            pl.BlockSpec((CHUNK, SSM_WIDTH), lambda d, b, c: (row(d, b, c), 0))]


_SSD_SCRATCH = [pltpu.VMEM((N_HEADS, HEAD_DIM, N_STATE), F32),
                pltpu.VMEM((CHUNK, N_HEADS), F32), pltpu.VMEM((N_HEADS, CHUNK), F32),
                pltpu.VMEM((CHUNK, N_HEADS), F32)]


def _ssd_bwd(xbc, dtraw, dtb, alog, dyv, states, *, nb, seq, scatter=()):
    nchunk = seq // CHUNK
    t = nb * seq
    row = _chunk_index(nchunk, False)
    ns = len(scatter)

    def body(*refs):
        xs_ref, bc_ref, dt_ref, dtb_ref, alog_ref, dy_ref, hs_ref = refs[:7]
        dx_ref, db_ref, dc_ref, draw_ref, da_ref, dbias_ref = refs[7 + ns:13 + ns]
        h_ref, z_ref, zt_ref, dts_ref, acc_ref, span_ref = refs[13 + 2 * ns:19 + 2 * ns]
        d, b, c = pl.program_id(0), pl.program_id(1), pl.program_id(2)
        if ns:
            step = (d * nb + b) * nchunk + c
            copies = _exchange_copies(refs[7:7 + ns], refs[13 + ns:13 + 2 * ns], *refs[19 + 2 * ns:], gather=False)
            pl.when(step == 0)(lambda: _exchange_start(copies))

        @pl.when(c == 0)
        def _():
            h_ref[...] = jnp.zeros_like(h_ref)

        @pl.when((c == 0) & (b == 0) & (d == 0))
        def _():
            da_ref[...] = jnp.zeros_like(da_ref)
            dbias_ref[...] = jnp.zeros_like(dbias_ref)

        ahead, dt, tot, dt_all = _scan_setup(d, dt_ref, dtb_ref, alog_ref, z_ref, zt_ref,
                                             lower_when_dir0=False, inclusive=False)
        mask = ahead >= 0
        dts_ref[...] = dt
        e_tot = jnp.exp(tot)
        acc_ref[...] = jnp.zeros_like(acc_ref)
        span_ref[...] = jnp.zeros_like(span_ref)
        for g in range(N_GROUPS):
            bg = bc_ref[:, g * N_STATE:(g + 1) * N_STATE]
            cg = bc_ref[:, BC_WIDTH + g * N_STATE:BC_WIDTH + (g + 1) * N_STATE]
            bg16 = bg.astype(BF16)
            cg16 = cg.astype(BF16)
            bc_t = _dot(bg16, cg16, NT)
            heads = range(g * HEADS_PER_GROUP, (g + 1) * HEADS_PER_GROUP)
            cols = {h: slice(h * HEAD_DIM, (h + 1) * HEAD_DIM) for h in heads}
            zc = {h: jnp.broadcast_to(z_ref[:, h:h + 1], (CHUNK, CHUNK)) for h in heads}
            decay = {h: jnp.exp(jnp.where(mask, zc[h] - zt_ref[h:h + 1, :], -jnp.inf)) for h in heads}
            e_z = {h: jnp.exp(zc[h]) for h in heads}
            e_tz = {h: jnp.exp(tot[:, h:h + 1] - zc[h]) for h in heads}
            x_h = {h: xs_ref[:, cols[h]] for h in heads}
            dt_h = {h: dts_ref[:, h:h + 1] for h in heads}
            u = {h: (x_h[h] * dt_h[h]).astype(BF16) for h in heads}
            dy = {h: dy_ref[:, cols[h]].astype(BF16) for h in heads}
            state = {h: h_ref[h] for h in heads}
            st16 = {h: state[h].astype(BF16) for h in heads}
            fstate = {h: hs_ref[0, 0, h] for h in heads}
            mix = {h: (bc_t * decay[h]).astype(BF16) for h in heads}
            bz = {h: (bg * e_z[h]).astype(BF16) for h in heads}
            du = {h: _dot(mix[h], dy[h], NN) + _dot(bz[h], st16[h], NT) for h in heads}
            w2f = {h: _dot(u[h], dy[h], NT) * decay[h] for h in heads}
            db_out = {h: e_z[h] * _dot(u[h], st16[h], NN) for h in heads}
            dc_out = {h: e_tz[h] * _dot(dy[h], fstate[h].astype(BF16), NN) for h in heads}
            w2_sum = w2f[heads[0]]
            for h in heads[1:]:
                w2_sum = w2_sum + w2f[h]
            w2_sum = w2_sum.astype(BF16)
            pairs = {h: w2f[h] * bc_t for h in heads}
            col_in = {h: jnp.sum(pairs[h], axis=-1, keepdims=True) for h in heads}
            row_in = {h: jnp.sum(pairs[h], axis=0, keepdims=True) for h in heads}
            row_out = {h: jnp.sum(dc_out[h] * cg, axis=-1, keepdims=True) for h in heads}
            col_out = {h: jnp.sum(db_out[h] * bg, axis=-1, keepdims=True) for h in heads}
            ddt_h = {h: jnp.sum(du[h] * x_h[h], axis=-1, keepdims=True) for h in heads}
            cz = {h: (cg * e_tz[h]).astype(BF16) for h in heads}
            new = {h: state[h] * e_tot[:, h:h + 1] + _dot(dy[h], cz[h], TN) for h in heads}
            dbg = _dot(w2_sum, cg16, NN)
            dcg = _dot(w2_sum, bg16, TN)
            for h in heads:
                dbg = dbg + db_out[h]
                dcg = dcg + dc_out[h]
                acc_ref[0, :, h:h + 1] = row_out[h] - col_in[h]
                acc_ref[1, :, h:h + 1] = col_out[h]
                acc_ref[2, :, h:h + 1] = ddt_h[h]
                acc_ref[3, h:h + 1, :] = row_in[h]
                span_ref[0:1, h:h + 1] = e_tot[:, h:h + 1] * jnp.sum(fstate[h] * state[h], keepdims=True)
                dx_ref[0, :, cols[h]] = du[h] * dt_h[h]
                h_ref[h] = new[h]
            db_ref[0, :, g * N_STATE:(g + 1) * N_STATE] = dbg
            dc_ref[0, :, g * N_STATE:(g + 1) * N_STATE] = dcg
        tri = mask.astype(F32)
        at_dir = lambda v: jnp.where(d == 0, v, pltpu.roll(v, N_HEADS, 1))
        dadt = at_dir(jnp.dot(tri, acc_ref[0] + acc_ref[3].T, precision=HIGHEST, preferred_element_type=F32)
                      + jnp.dot(1.0 - tri, acc_ref[1], precision=HIGHEST, preferred_element_type=F32) + span_ref[0:1, :])
        ddt = at_dir(acc_ref[2])
        a = -jnp.exp(alog_ref[...])
        draw = (ddt + a * dadt) * _sigmoid(dt_ref[...] + dtb_ref[...])
        draw_ref[0] = draw
        da_ref[...] += jnp.sum(dt_all * dadt, axis=0, keepdims=True) * a
        dbias_ref[...] += jnp.sum(draw, axis=0, keepdims=True)
        if ns:
            pl.when(step == 2 * nb * nchunk - 1)(lambda: _exchange_wait(copies))

    out_row = lambda w: pl.BlockSpec((1, CHUNK, w), lambda d, b, c: (d, row(d, b, c), 0))
    vec = pl.BlockSpec((1, LANES), lambda d, b, c: (0, 0))
    any_spec = pl.BlockSpec(memory_space=pl.ANY)
    res = pl.pallas_call(
        body, name="ssd_bwd_scatter" if ns else "ssd_bwd", grid=(2, nb, nchunk),
        in_specs=_ssd_specs(nchunk, row) + [
            pl.BlockSpec((1, 1, N_HEADS, HEAD_DIM, N_STATE), lambda d, b, c: (d, row(d, b, c), 0, 0, 0))]
        + [any_spec] * ns,
        out_specs=[out_row(SSM_WIDTH), out_row(BC_WIDTH), out_row(BC_WIDTH), out_row(LANES), vec, vec] + [any_spec] * ns,
        out_shape=[jax.ShapeDtypeStruct((2, t, SSM_WIDTH), F32), jax.ShapeDtypeStruct((2, t, BC_WIDTH), F32),
                   jax.ShapeDtypeStruct((2, t, BC_WIDTH), F32),
                   jax.ShapeDtypeStruct((2, t, LANES), F32), jax.ShapeDtypeStruct((1, LANES), F32),
                   jax.ShapeDtypeStruct((1, LANES), F32)] + _exchange_out_shapes(scatter, gather=False),
        scratch_shapes=_SSD_SCRATCH + [pltpu.VMEM((4, CHUNK, LANES), F32), pltpu.VMEM((SUBLANES, LANES), F32)]
        + _exchange_sems(ns),
        compiler_params=_params("arbitrary", "arbitrary", "arbitrary"),
    )(xbc, xbc, dtraw, dtb, alog, dyv, states, *scatter)
    return tuple(res[:6]) + (list(res[6:]),)


def _layer_bwd(dx2, p, s, band_bias, dbias_in, *, nb, seq, pending=()):
    t = nb * seq
    x, proj, xbc, x1 = s["x"], s["proj"], s["xbc"], s["x1"]
    dact = _matmul(dx2, p["w_down"], "nt", name="down_proj_dx", tm=MM_ROWS, tn=1408, tk=1024, out_dtype=BF16)
    g_w_down = _matmul(s["act"], dx2, "tn", name="down_proj_dw", tm=1408, tn=1024, tk=1024)
    dg, du, dwb_ffn = _ffn_act_bwd(s["gu"], s["fconv"], dact, p["ffn_cw"], nb=nb, seq=seq)
    dh2 = _matmul(dg, p["w_up"], "nt", name="up_proj_dx_g", tm=MM_ROWS, tn=1024, tk=1408)
    dh2 = _matmul(du, p["w_up"], "nt", name="up_proj_dx_u", tm=MM_ROWS, tn=1024, tk=1408, res=dh2, b_k0=D_FF)
    g_w_up = jnp.concatenate([_matmul(s["h2"], dg, "tn", name="up_proj_dw_g", tm=1024, tn=1408, tk=1024),
                              _matmul(s["h2"], du, "tn", name="up_proj_dw_u", tm=1024, tn=1408, tk=1024)], axis=1)
    dx1, g_n2 = _rms_bwd(x1, dh2, p["n2"], dx2, tm=512, name="norm2_bwd")
    dmix = _matmul(dx1, p["w_out"], "nt", name="out_proj_dx", tm=MM_ROWS, tn=1024, tk=1024, out_dtype=BF16)
    g_w_out = jnp.concatenate([_matmul(s["y_ssm"], dx1, "tn", name="out_proj_dw_ssm", tm=1024, tn=1024, tk=1024),
                               _matmul(s["y_att"], dx1, "tn", name="out_proj_dw_att", tm=1024, tn=1024, tk=1024)], axis=0)
    dq, dkv, dbias, dsink = _attn_bwd(proj, dmix, s["y_att"], s["lse"], band_bias, p["sink"], dbias_in, nb=nb, seq=seq)
    dkv = dkv[:, CHUNK:CHUNK + seq, :].reshape(t, 2 * KV_WIDTH)
    dyv, dz, g_dvec, g_ssm_nw = _gate_norm_bwd(s["y2"], xbc, proj, dmix, p["dvec"], p["ssm_nw"], tm=256)
    own = [_rows_to_blocks(g_w_down[None], 1).astype(BF16), _cols_to_blocks(g_w_up[None], 1).astype(BF16),
           _rows_to_blocks(g_w_out[None], 1).astype(BF16)]
    dxs2, db2, dc2, draw2, g_alog, g_dtb, exchanged = _ssd_bwd(xbc, s["dtraw"], p["dtb"], p["alog"], dyv, s["states"],
                                                               nb=nb, seq=seq, scatter=own + list(pending))
    ddt_raw = draw2[0] + draw2[1]
    conv = dict(nb=nb, seq=seq)
    dxs_pre, dwb_xs = _ssm_conv_bwd(proj, s["gconv"], dxs2, p["conv_w"], name="ssm_conv_bwd_x", width=256,
                                    proj_col=PXS // 256, conv_col=0, ncol=4, extra=dyv, scale=p["dvec"], **conv)
    db_pre, dwb_b = _ssm_conv_bwd(proj, s["gconv"], db2, p["conv_w"], name="ssm_conv_bwd_b", width=256,
                                  proj_col=PB // 256, conv_col=SSM_WIDTH // 256, ncol=1, **conv)
    dc_pre, dwb_c = _ssm_conv_bwd(proj, s["gconv"], dc2, p["conv_w"], name="ssm_conv_bwd_c", width=256,
                                  proj_col=PC // 256, conv_col=(SSM_WIDTH + BC_WIDTH) // 256, ncol=1, **conv)
    dwb_ssm = jnp.concatenate([dwb_xs, dwb_b, dwb_c], axis=1)
    dproj = jnp.concatenate([dz, dxs_pre, dq, db_pre, dc_pre, dkv.astype(BF16), ddt_raw.astype(BF16)], axis=1)
    dh1 = _matmul(dproj, p["w_in"], "nt", name="in_proj_dx", tm=MM_ROWS, tn=1024, tk=1408)
    g_w_in = _matmul(s["h1"], dproj, "tn", name="in_proj_dw", tm=1024, tn=1408, tk=1024)
    dx, g_n1 = _rms_bwd(x, dh1, p["n1"], dx1, tm=512, name="norm1_bwd")
    grads = dict(n1=g_n1, w_in=g_w_in, conv_w=dwb_ssm[:SSM_TAPS], conv_b=dwb_ssm[SUBLANES - 1], dtb=g_dtb, alog=g_alog,
                 dvec=g_dvec, ssm_nw=g_ssm_nw, sink=dsink, w_out=g_w_out, n2=g_n2, w_up=g_w_up,
                 ffn_cw=dwb_ffn[:FFN_TAPS], ffn_cb=dwb_ffn[SUBLANES - 1], w_down=g_w_down)
    return dx, dbias, grads, exchanged


def _band_bias(rel_bias, bucket):
    def body(rb_ref, b_ref, o_ref):
        o_ref[...] = jnp.zeros_like(o_ref)

        def per_bucket(k, carry):
            hit = b_ref[...] == k
            for h in range(N_HEADS):
                o_ref[h] = jnp.where(hit, rb_ref[k, h], o_ref[h])
            return carry

        lax.fori_loop(0, REL_BUCKETS, per_bucket, 0)
        qi = lax.broadcasted_iota(jnp.int32, (CHUNK, KEY_SPAN), 0)
        kj = lax.broadcasted_iota(jnp.int32, (CHUNK, KEY_SPAN), 1)
        band = jnp.abs(kj - CHUNK - qi) <= CHUNK
        for h in range(N_HEADS):
            o_ref[h] = jnp.where(band, o_ref[h], -jnp.inf)

    return pl.pallas_call(
        body, name="band_bias", out_shape=jax.ShapeDtypeStruct((N_HEADS, CHUNK, KEY_SPAN), F32),
        in_specs=[pl.BlockSpec(memory_space=pltpu.SMEM), pl.BlockSpec(memory_space=pltpu.VMEM)],
        out_specs=pl.BlockSpec(memory_space=pltpu.VMEM),
    )(rel_bias, bucket)


def _rel_bias_grad(dbias, bucket):
    def body(d_ref, b_ref, o_ref):
        o_ref[...] = jnp.zeros_like(o_ref)
        lane = lax.broadcasted_iota(jnp.int32, (1, LANES), 1)

        def per_bucket(k, carry):
            hit = b_ref[...] == k
            for h in range(N_HEADS):
                part = jnp.sum(jnp.where(hit, d_ref[h], 0.0), axis=1, keepdims=True)
                o_ref[h:h + 1, :] += jnp.where(lane == k, jnp.sum(part, axis=0, keepdims=True), 0.0)
            return carry

        lax.fori_loop(0, REL_BUCKETS, per_bucket, 0)

    return pl.pallas_call(
        body, name="rel_bias_grad", out_shape=jax.ShapeDtypeStruct((N_HEADS, LANES), F32),
        compiler_params=pltpu.CompilerParams(vmem_limit_bytes=VMEM_LIMIT_BYTES),
    )(dbias, bucket)


N_PEER = N_DEV - 1


def _exchange_copies(ins, outs, send_sems, recv_sems, local_sems, *, gather):
    x, y, c = lax.axis_index("x"), lax.axis_index("y"), lax.axis_index("c")
    me = 4 * x + 2 * y + c
    peers = []
    for k in range(1, N_DEV):
        px, py, pc = x ^ ((k >> 2) & 1), y ^ ((k >> 1) & 1), c ^ (k & 1)
        peers.append(((px, py, pc), 4 * px + 2 * py + pc))
    local, sends, recvs = [], [], []
    for i in range(len(ins)):
        mine = ins[i] if gather else ins[i].at[me]
        local.append(pltpu.make_async_copy(mine, outs[i].at[me], local_sems.at[i]))
        for k, (pid, pslot) in enumerate(peers):
            src = ins[i] if gather else ins[i].at[pslot]
            sem = i * N_PEER + k
            sends.append(pltpu.make_async_remote_copy(
                src_ref=src, dst_ref=outs[i].at[me], send_sem=send_sems.at[sem], recv_sem=recv_sems.at[sem],
                device_id=pid, device_id_type=pl.DeviceIdType.MESH))
            recvs.append(pltpu.make_async_remote_copy(
                src_ref=src, dst_ref=outs[i].at[pslot], send_sem=send_sems.at[sem], recv_sem=recv_sems.at[sem],
                device_id=pid, device_id_type=pl.DeviceIdType.MESH))
    return local, sends, recvs


def _exchange_start(copies):
    local, sends, _ = copies
    for cp in local + sends:
        cp.start()


def _exchange_wait(copies):
    local, sends, recvs = copies
    for cp in recvs:
        cp.wait_recv()
    for cp in sends:
        cp.wait_send()
    for cp in local:
        cp.wait()


def _exchange_out_shapes(arrs, *, gather):
    return [jax.ShapeDtypeStruct((N_DEV,) + (a.shape if gather else a.shape[1:]), a.dtype) for a in arrs]


def _exchange_sems(n):
    if not n:
        return []
    return [pltpu.SemaphoreType.DMA((n * N_PEER,)), pltpu.SemaphoreType.DMA((n * N_PEER,)), pltpu.SemaphoreType.DMA((n,))]


def _exchange(arrs, *, gather, name):
    n = len(arrs)

    def body(*refs):
        copies = _exchange_copies(refs[:n], refs[n:2 * n], *refs[2 * n:], gather=gather)
        _exchange_start(copies)
        _exchange_wait(copies)

    any_spec = pl.BlockSpec(memory_space=pl.ANY)
    return pl.pallas_call(
        body, name=name, in_specs=[any_spec] * n, out_specs=[any_spec] * n,
        out_shape=_exchange_out_shapes(arrs, gather=gather), scratch_shapes=_exchange_sems(n),
        compiler_params=pltpu.CompilerParams(has_side_effects=True),
    )(*arrs)


def _adamw(parts, w, m, v, *, name, tr):
    r, c = w.shape
    nparts = len(parts)
    rows = r // nparts
    assert rows * nparts == r and rows % tr == 0 and all(p.shape == (N_DEV, rows, c) for p in parts)
    per = rows // tr
    c1 = 1.0 - ADAM_B1 ** ADAM_STEP
    c2 = 1.0 - ADAM_B2 ** ADAM_STEP

    def body(*refs):
        p_refs = refs[:nparts]
        w_ref, m_ref, v_ref, g_ref, d_ref, nm_ref, nv_ref = refs[nparts:]
        which = pl.program_id(0) // per
        for k, p_ref in enumerate(p_refs):
            @pl.when(which == k)
            def _(p_ref=p_ref):
                acc = p_ref[0].astype(F32)
                for j in range(1, N_DEV):
                    acc = acc + p_ref[j].astype(F32)
                g_ref[...] = acc

        g = g_ref[...]
        nm = ADAM_B1 * m_ref[...] + (1.0 - ADAM_B1) * g
        nv = ADAM_B2 * v_ref[...] + (1.0 - ADAM_B2) * (g * g)
        nm_ref[...] = nm
        nv_ref[...] = nv
        d_ref[...] = -ADAM_LR * ((nm / c1) / (jnp.sqrt(nv / c2) + ADAM_EPS) + ADAM_WD * w_ref[...])

    def part_spec(k):
        return pl.BlockSpec((N_DEV, tr, c), lambda i: (0, jnp.clip(i - k * per, 0, per - 1), 0))

    blk = pl.BlockSpec((tr, c), lambda i: (i, 0))
    return pl.pallas_call(
        body, name=name, grid=(r // tr,),
        in_specs=[part_spec(k) for k in range(nparts)] + [blk, blk, blk],
        out_specs=[blk] * 4, out_shape=[jax.ShapeDtypeStruct((r, c), F32)] * 4,
        compiler_params=_params("arbitrary"),
    )(*parts, w, m, v)


def _t5_bucket(rel):
    half = REL_BUCKETS // 2
    max_exact = half // 2
    ret = jnp.where(rel > 0, half, 0)
    n = jnp.abs(rel)
    nf = jnp.maximum(n, 1).astype(F32)
    large = max_exact + (jnp.log(nf / max_exact) / math.log(CHUNK / max_exact) * (half - max_exact)).astype(jnp.int32)
    large = jnp.minimum(large, half - 1)
    return ret + jnp.where(n < max_exact, n, large)


def _split16(w):
    hi = w.astype(BF16)
    return hi, (w - hi.astype(F32)).astype(BF16)


def _cols_to_blocks(g, depth):
    _, r, c8 = g.shape
    return g.reshape(depth, r, N_DEV, c8 // N_DEV).transpose(2, 0, 1, 3).reshape(N_DEV, depth * r, c8 // N_DEV)


def _rows_to_blocks(g, depth):
    _, r8, c = g.shape
    return g.reshape(depth, N_DEV, r8 // N_DEV, c).transpose(1, 0, 2, 3).reshape(N_DEV, depth * r8 // N_DEV, c)


def _blocks_to_cols(a, depth):
    _, dr, c = a.shape
    r = dr // depth
    return a.reshape(N_DEV, depth, r, c).transpose(1, 2, 0, 3).reshape(depth, r, N_DEV * c)


def _blocks_to_rows(a, depth):
    _, dr, c = a.shape
    r = dr // depth
    return a.reshape(N_DEV, depth, r, c).transpose(1, 0, 2, 3).reshape(depth, N_DEV * r, c)


_SMALL = ("rel_bias", "norm1_w", "conv_b", "dt_bias", "a_log", "d_skip", "ssm_norm_w", "attn_sink", "norm2_w",
          "ffn_conv_b", "final_norm_w")
_SHARDED = ("w_in", "conv_w", "w_out", "w_up", "ffn_conv_w", "w_down")
_ORDER = ("rel_bias", "norm1_w", "w_in", "conv_w", "conv_b", "dt_bias", "a_log", "d_skip", "ssm_norm_w", "attn_sink",
          "w_out", "norm2_w", "w_up", "ffn_conv_w", "ffn_conv_b", "w_down", "final_norm_w")


def _pack_small(d):
    flat = jnp.concatenate([d[k].reshape(-1).astype(F32) for k in _SMALL])
    rows = -(-flat.size // (LANES * SUBLANES)) * SUBLANES
    return jnp.pad(flat, (0, rows * LANES - flat.size)).reshape(rows, LANES)


def _unpack_small(packed, like):
    flat = packed.reshape(-1)
    out, off = {}, 0
    for k in _SMALL:
        out[k] = flat[off:off + like[k].size].reshape(like[k].shape)
        off += like[k].size
    return out


def kernel(x, rel_bias, norm1_w, w_in, conv_w, conv_b, dt_bias, a_log, d_skip, ssm_norm_w, attn_sink, w_out, norm2_w, w_up, ffn_conv_w, ffn_conv_b, w_down, final_norm_w, loss_target, m_rel_bias, m_norm1_w, m_w_in, m_conv_w, m_conv_b, m_dt_bias, m_a_log, m_d_skip, m_ssm_norm_w, m_attn_sink, m_w_out, m_norm2_w, m_w_up, m_ffn_conv_w, m_ffn_conv_b, m_w_down, m_final_norm_w, v_rel_bias, v_norm1_w, v_w_in, v_conv_w, v_conv_b, v_dt_bias, v_a_log, v_d_skip, v_ssm_norm_w, v_attn_sink, v_w_out, v_norm2_w, v_w_up, v_ffn_conv_w, v_ffn_conv_b, v_w_down, v_final_norm_w):
    w = dict(rel_bias=rel_bias, norm1_w=norm1_w, w_in=w_in, conv_w=conv_w, conv_b=conv_b, dt_bias=dt_bias, a_log=a_log,
             d_skip=d_skip, ssm_norm_w=ssm_norm_w, attn_sink=attn_sink, w_out=w_out, norm2_w=norm2_w, w_up=w_up,
             ffn_conv_w=ffn_conv_w, ffn_conv_b=ffn_conv_b, w_down=w_down, final_norm_w=final_norm_w)
    m = dict(rel_bias=m_rel_bias, norm1_w=m_norm1_w, w_in=m_w_in, conv_w=m_conv_w, conv_b=m_conv_b, dt_bias=m_dt_bias,
             a_log=m_a_log, d_skip=m_d_skip, ssm_norm_w=m_ssm_norm_w, attn_sink=m_attn_sink, w_out=m_w_out,
             norm2_w=m_norm2_w, w_up=m_w_up, ffn_conv_w=m_ffn_conv_w, ffn_conv_b=m_ffn_conv_b, w_down=m_w_down,
             final_norm_w=m_final_norm_w)
    v = dict(rel_bias=v_rel_bias, norm1_w=v_norm1_w, w_in=v_w_in, conv_w=v_conv_w, conv_b=v_conv_b, dt_bias=v_dt_bias,
             a_log=v_a_log, d_skip=v_d_skip, ssm_norm_w=v_ssm_norm_w, attn_sink=v_attn_sink, w_out=v_w_out,
             norm2_w=v_norm2_w, w_up=v_w_up, ffn_conv_w=v_ffn_conv_w, ffn_conv_b=v_ffn_conv_b, w_down=v_w_down,
             final_norm_w=v_final_norm_w)
    nb, seq, _ = x.shape
    t = nb * seq
    depth = w_in.shape[0]

    flat2 = lambda a: a.reshape(-1, a.shape[-1])
    own_shards = lambda i: [w_out[i].astype(BF16), w_up[i].astype(BF16), w_down[i].astype(BF16)]
    cw_hi, cw_lo = _split16(flat2(conv_w))
    fw_hi, fw_lo = _split16(flat2(ffn_conv_w))
    g_in, g_cwh, g_cwl, g_fwh, g_fwl = _exchange([w_in[0].astype(BF16), cw_hi, cw_lo, fw_hi, fw_lo], gather=True,
                                                 name="gather_weights")
    full_conv_w = _blocks_to_cols(g_cwh.astype(F32) + g_cwl.astype(F32), depth)
    full_ffn_cw = _blocks_to_cols(g_fwh.astype(F32) + g_fwl.astype(F32), depth)

    rel = jnp.arange(KEY_SPAN)[None, :] - CHUNK - jnp.arange(CHUNK)[:, None]
    bucket = _t5_bucket(rel)
    band_bias = _band_bias(rel_bias, bucket)

    def layer_params(i, g_in):
        return dict(n1=norm1_w[i][None], w_in=_to_proj_layout(_blocks_to_cols(g_in, 1)[0]), conv_w=full_conv_w[i],
                    conv_b=conv_b[i][None], dtb=_pad_lanes(dt_bias[i].reshape(-1)), alog=_pad_lanes(a_log[i].reshape(-1)),
                    dvec=jnp.repeat(d_skip[i], HEAD_DIM)[None], ssm_nw=ssm_norm_w[i][None], sink=_pad_lanes(attn_sink[i]),
                    n2=norm2_w[i][None], ffn_cw=full_ffn_cw[i], ffn_cb=ffn_conv_b[i][None])

    h = x.reshape(t, D_MODEL)
    params, saved = [None] * depth, [None] * depth
    for i in range(depth):
        h, params[i], saved[i], nxt = _layer_fwd(h, layer_params(i, g_in), band_bias, nb=nb, seq=seq, own_shards=own_shards(i),
                                                 next_shards=[w_in[i + 1].astype(BF16)] if i + 1 < depth else ())
        if nxt:
            (g_in,) = nxt
    dh, g_final, loss_part = _loss_head(h, loss_target.reshape(t, D_MODEL), final_norm_w[None], tm=512)
    loss = lax.psum(loss_part[0, 0], ("x", "y", "c"))

    w_in_blocks = lambda g: _cols_to_blocks(_from_proj_layout(g["w_in"])[None], 1).astype(BF16)
    dbias = jnp.zeros((N_HEADS, CHUNK, KEY_SPAN), F32)
    grads, pending = [None] * depth, ()
    parts = dict(w_in=[None] * depth, w_out=[None] * depth, w_up=[None] * depth, w_down=[None] * depth)
    for i in reversed(range(depth)):
        dh, dbias, grads[i], arrived = _layer_bwd(dh, params[i], saved[i], band_bias, dbias, nb=nb, seq=seq, pending=pending)
        parts["w_down"][i], parts["w_up"][i], parts["w_out"][i] = arrived[:3]
        if pending:
            parts["w_in"][i + 1] = arrived[3]
        pending = [w_in_blocks(grads[i])]
    grad_x = dh.reshape(nb, seq, D_MODEL)
    stack = lambda k: jnp.stack([g[k] for g in grads])
    last = _exchange(pending + [_cols_to_blocks(stack("conv_w"), depth).astype(BF16),
                                _cols_to_blocks(stack("ffn_cw"), depth).astype(BF16)], gather=False, name="scatter_grads")
    parts["w_in"][0] = last[0]

    out = {}
    for k in ("w_in", "w_out", "w_up", "w_down"):
        rows = parts[k][0].shape[1]
        tr = max(d for d in range(16, 129, 16) if rows % d == 0)
        res = _adamw(parts[k], flat2(w[k]), flat2(m[k]), flat2(v[k]), name="adamw_" + k, tr=tr)
        out[k] = [a.reshape(w[k].shape) for a in res]
    for k, p8 in zip(("conv_w", "ffn_conv_w"), last[1:]):
        res = _adamw([p8], flat2(w[k]), flat2(m[k]), flat2(v[k]), name="adamw_" + k, tr=p8.shape[1])
        out[k] = [a.reshape(w[k].shape) for a in res]

    small = dict(rel_bias=_rel_bias_grad(dbias, bucket)[:, :REL_BUCKETS].T, norm1_w=stack("n1"), conv_b=stack("conv_b"),
                 dt_bias=stack("dtb")[:, 0, :2 * N_HEADS], a_log=stack("alog")[:, 0, :2 * N_HEADS],
                 d_skip=stack("dvec").reshape(depth, N_HEADS, HEAD_DIM).sum(-1), ssm_norm_w=stack("ssm_nw"),
                 attn_sink=stack("sink")[:, 0, :N_HEADS], norm2_w=stack("n2"), ffn_conv_b=stack("ffn_cb"),
                 final_norm_w=g_final)
    (small_parts,) = _exchange([_pack_small(small)], gather=True, name="gather_small_grads")
    res = _adamw([small_parts], _pack_small(w), _pack_small(m), _pack_small(v), name="adamw_small", tr=small_parts.shape[1])
    unpacked = [_unpack_small(a, w) for a in res]
    for k in _SMALL:
        out[k] = [u[k] for u in unpacked]

    return (loss, grad_x, *[out[k][0] for k in _ORDER], *[out[k][1] for k in _ORDER],
            *[out[k][2] for k in _ORDER], *[out[k][3] for k in _ORDER])
```

```python
import math

import numpy as np
import jax
import jax.numpy as jnp
from jax import lax
from jax.experimental import pallas as pl
from jax.experimental.pallas import tpu as pltpu

F32, BF16 = jnp.float32, jnp.bfloat16
HIGHEST = lax.Precision.HIGHEST

D_MODEL = 1024
HEAD_DIM = 64
N_HEADS = 16
N_GROUPS = 2
HEADS_PER_GROUP = N_HEADS // N_GROUPS
N_STATE = 128
SSM_WIDTH = 1024
BC_WIDTH = 256
CONV_CH = SSM_WIDTH + 2 * BC_WIDTH
SSM_TAPS = 7
CHUNK = 128
KV_HEADS = 4
KV_WIDTH = 256
Q_PER_KV = N_HEADS // KV_HEADS
KEY_SPAN = 3 * CHUNK
REL_BUCKETS = 32
D_FF = 2816
FFN_TAPS = 3
IN_COLS = 4128
NORM_EPS = 1e-6
N_DEV = 8

LANES = 128
SUBLANES = 8
VMEM_LIMIT_BYTES = 56 * 1024 * 1024
MM_ROWS = 1024

PZ, PXS, PQ, PB, PC, PK, PV, PDT, PROJ_W = 0, 1024, 2048, 3072, 3328, 3584, 3840, 4096, 4224
OZ, OXBC, ODT, OQ, OK_, OV = 0, 1024, 2560, 2592, 3616, 3872

ADAM_LR, ADAM_B1, ADAM_B2, ADAM_EPS, ADAM_WD, ADAM_STEP = 0.001, 0.9, 0.999, 1e-08, 0.01, 10


def _params(*sem):
    return pltpu.CompilerParams(dimension_semantics=sem, vmem_limit_bytes=VMEM_LIMIT_BYTES)


def _sigmoid(x):
    return 0.5 * jnp.tanh(0.5 * x) + 0.5


def _softplus(x):
    return jnp.maximum(x, 0.0) + jnp.log(1.0 + jnp.exp(-jnp.abs(x)))


def _dot(a, b, dims):
    return lax.dot_general(a, b, (dims, ((), ())), preferred_element_type=F32)


NN = ((1,), (0,))
NT = ((1,), (1,))
TN = ((0,), (0,))


def _matmul(a, b, mode, *, name, tm, tn, tk, res=None, out_dtype=F32, precision=None, b_k0=0):
    assert b_k0 % tk == 0
    ko = b_k0 // tk
    if mode == "nn":
        (m, k), n = a.shape, b.shape[1]
        k2 = k if b.shape[0] >= b_k0 + k else -1
        a_spec = pl.BlockSpec((tm, tk), lambda i, j, kk: (i, kk))
        b_spec = pl.BlockSpec((tk, tn), lambda i, j, kk: (kk + ko, j))
        dims = NN
    elif mode == "nt":
        (m, k), n = a.shape, b.shape[0]
        k2 = k if b.shape[1] >= b_k0 + k else -1
        a_spec = pl.BlockSpec((tm, tk), lambda i, j, kk: (i, kk))
        b_spec = pl.BlockSpec((tn, tk), lambda i, j, kk: (j, kk + ko))
        dims = NT
    else:
        (k, m), (k2, n) = a.shape, b.shape
        a_spec = pl.BlockSpec((tk, tm), lambda i, j, kk: (kk, i))
        b_spec = pl.BlockSpec((tk, tn), lambda i, j, kk: (kk, j))
        dims = TN
    assert k == k2 and m % tm == 0 and n % tn == 0 and k % tk == 0, (name, a.shape, b.shape, tm, tn, tk)
    nk = k // tk
    has_res = res is not None

    def body(*refs):
        a_ref, b_ref = refs[:2]
        r_ref = refs[2] if has_res else None
        o_ref = refs[3] if has_res else refs[2]
        acc = refs[-1] if nk > 1 else None
        kk = pl.program_id(2)
        if precision is None:
            part = _dot(a_ref[...].astype(BF16), b_ref[...].astype(BF16), dims)
        else:
            part = lax.dot_general(a_ref[...], b_ref[...], (dims, ((), ())), precision=precision,
                                   preferred_element_type=F32)

        def finish(r):
            if has_res:
                r = r + r_ref[...].astype(F32)
            o_ref[...] = r.astype(out_dtype)

        if nk == 1:
            finish(part)
        else:
            @pl.when(kk == 0)
            def _():
                acc[...] = jnp.zeros_like(acc)

            acc[...] += part

            @pl.when(kk == nk - 1)
            def _():
                finish(acc[...])

    in_specs = [a_spec, b_spec]
    args = [a, b]
    if has_res:
        in_specs.append(pl.BlockSpec((tm, tn), lambda i, j, kk: (i, j)))
        args.append(res)
    return pl.pallas_call(
        body, name=name, grid=(m // tm, n // tn, nk),
        in_specs=in_specs, out_specs=pl.BlockSpec((tm, tn), lambda i, j, kk: (i, j)),
        out_shape=jax.ShapeDtypeStruct((m, n), out_dtype),
        scratch_shapes=[pltpu.VMEM((tm, tn), F32)] if nk > 1 else [],
        compiler_params=_params("parallel", "parallel", "arbitrary"),
    )(*args)


def _rms_matmul(x, nw, w, *, name, tm, tn, ncols=None):
    t, d = x.shape
    n = w.shape[1] if ncols is None else ncols
    assert t % tm == 0 and n % tn == 0

    def body(x_ref, nw_ref, w_ref, o_ref, h_ref):
        @pl.when(pl.program_id(1) == 0)
        def _():
            xv = x_ref[...]
            r = lax.rsqrt(jnp.mean(xv * xv, axis=-1, keepdims=True) + NORM_EPS)
            h_ref[...] = (xv * r * nw_ref[...]).astype(BF16)

        o_ref[...] = _dot(h_ref[...], w_ref[...].astype(BF16), NN).astype(BF16)

    return pl.pallas_call(
        body, name=name, grid=(t // tm, n // tn),
        in_specs=[pl.BlockSpec((tm, d), lambda i, j: (i, 0)),
                  pl.BlockSpec((1, d), lambda i, j: (0, 0)),
                  pl.BlockSpec((d, tn), lambda i, j: (0, j))],
        out_specs=[pl.BlockSpec((tm, tn), lambda i, j: (i, j)),
                   pl.BlockSpec((tm, d), lambda i, j: (i, 0))],
        out_shape=[jax.ShapeDtypeStruct((t, n), BF16), jax.ShapeDtypeStruct((t, d), BF16)],
        compiler_params=_params("parallel", "arbitrary"),
    )(x, nw, w)


def _zero_ext(v):
    z = jnp.zeros((SUBLANES, v.shape[1]), v.dtype)
    return jnp.concatenate([z, v, z], axis=0)


def _shifted(v_ext, offset, seq):
    if offset == 0:
        return v_ext[SUBLANES:SUBLANES + seq]
    return pltpu.roll(v_ext, (-offset) % (seq + 2 * SUBLANES), 0)[SUBLANES:SUBLANES + seq]


def _conv_taps(v, w_ref, taps, seq):
    pad = taps // 2
    v_ext = _zero_ext(v)
    acc = None
    for k in range(taps):
        term = _shifted(v_ext, k - pad, seq) * w_ref[k:k + 1, :]
        acc = term if acc is None else acc + term
    return acc


def _ssm_conv_fwd(proj, cw, cb, *, nb, seq):
    width = 512

    def body(x_ref, w_ref, b_ref, o_ref, g_ref):
        g = _conv_taps(x_ref[...].astype(F32), w_ref, SSM_TAPS, seq) + b_ref[...]
        o_ref[...] = g * _sigmoid(g)
        g_ref[...] = g.astype(BF16)

    def col(j):
        return jnp.where(j < 2, j + PXS // width, PB // width)

    return pl.pallas_call(
        body, name="ssm_conv_fwd", grid=(nb, CONV_CH // width),
        in_specs=[pl.BlockSpec((seq, width), lambda b, j: (b, col(j))),
                  pl.BlockSpec((SSM_TAPS, width), lambda b, j: (0, j)),
                  pl.BlockSpec((1, width), lambda b, j: (0, j))],
        out_specs=[pl.BlockSpec((seq, width), lambda b, j: (b, j))] * 2,
        out_shape=[jax.ShapeDtypeStruct((nb * seq, CONV_CH), F32), jax.ShapeDtypeStruct((nb * seq, CONV_CH), BF16)],
        compiler_params=_params("parallel", "parallel"),
    )(proj, cw, cb)


def _ffn_act_fwd(gu, cw, cb, *, nb, seq):
    width = 256
    nj = D_FF // width

    def body(g_ref, u_ref, w_ref, b_ref, o_ref, s_ref):
        g = _conv_taps(g_ref[...].astype(F32), w_ref, FFN_TAPS, seq) + b_ref[...]
        o_ref[...] = (g * _sigmoid(g) * u_ref[...].astype(F32)).astype(BF16)
        s_ref[...] = g.astype(BF16)

    return pl.pallas_call(
        body, name="ffn_act_fwd", grid=(nb, nj),
        in_specs=[pl.BlockSpec((seq, width), lambda b, j: (b, j)),
                  pl.BlockSpec((seq, width), lambda b, j: (b, j + nj)),
                  pl.BlockSpec((FFN_TAPS, width), lambda b, j: (0, j)),
                  pl.BlockSpec((1, width), lambda b, j: (0, j))],
        out_specs=[pl.BlockSpec((seq, width), lambda b, j: (b, j))] * 2,
        out_shape=[jax.ShapeDtypeStruct((nb * seq, D_FF), BF16)] * 2,
        compiler_params=_params("parallel", "parallel"),
    )(gu, gu, cw, cb)


def _scan_setup(d, dt_ref, dtb_ref, alog_ref, z_ref, zt_ref, *, lower_when_dir0, inclusive):
    is0 = d == 0
    dt_all = _softplus(dt_ref[...] + dtb_ref[...])
    adt_all = dt_all * (-jnp.exp(alog_ref[...]))
    li = lax.broadcasted_iota(jnp.int32, (CHUNK, CHUNK), 0)
    si = lax.broadcasted_iota(jnp.int32, (CHUNK, CHUNK), 1)
    lower = is0 if lower_when_dir0 else jnp.logical_not(is0)
    ahead = jnp.where(lower, li - si, si - li)
    mask = ahead >= 0
    tri = mask if inclusive else ahead > 0
    z_all = jnp.dot(tri.astype(F32), adt_all, precision=HIGHEST, preferred_element_type=F32)
    zt_all = z_all.T
    z_ref[...] = jnp.where(is0, z_all[:, 0:N_HEADS], z_all[:, N_HEADS:2 * N_HEADS])
    zt_ref[...] = jnp.where(is0, zt_all[0:N_HEADS, :], zt_all[N_HEADS:2 * N_HEADS, :])
    dt = jnp.where(is0, dt_all[:, 0:N_HEADS], dt_all[:, N_HEADS:2 * N_HEADS])
    adt = jnp.where(is0, adt_all[:, 0:N_HEADS], adt_all[:, N_HEADS:2 * N_HEADS])
    tot = jnp.sum(adt, axis=0, keepdims=True)
    return ahead, dt, tot, dt_all


def _chunk_index(nchunk, forward_when_dir0):
    def idx(d, b, c):
        fwd = (d == 0) if forward_when_dir0 else (d != 0)
        return b * nchunk + jnp.where(fwd, c, nchunk - 1 - c)
    return idx


def _ssd_fwd(xbc, dtraw, dtb, alog, *, nb, seq, gather=()):
    nchunk = seq // CHUNK
    t = nb * seq
    row = _chunk_index(nchunk, True)
    ng = len(gather)

    def body(*refs):
        xs_ref, bc_ref, dt_ref, dtb_ref, alog_ref = refs[:5]
        o_ref, hs_ref = refs[5 + ng:7 + ng]
        h_ref, z_ref, zt_ref, dts_ref = refs[7 + 2 * ng:11 + 2 * ng]
        d, c = pl.program_id(0), pl.program_id(2)
        if ng:
            step = (d * nb + pl.program_id(1)) * nchunk + c
            copies = _exchange_copies(refs[5:5 + ng], refs[7 + ng:7 + 2 * ng], *refs[11 + 2 * ng:], gather=True)
            pl.when(step == 0)(lambda: _exchange_start(copies))

        @pl.when(c == 0)
        def _():
            h_ref[...] = jnp.zeros_like(h_ref)

        ahead, dt, tot, _ = _scan_setup(d, dt_ref, dtb_ref, alog_ref, z_ref, zt_ref,
                                        lower_when_dir0=True, inclusive=True)
        mask = ahead >= 0
        dts_ref[...] = dt
        e_tot = jnp.exp(tot)
        for g in range(N_GROUPS):
            heads = range(g * HEADS_PER_GROUP, (g + 1) * HEADS_PER_GROUP)
            bg = bc_ref[:, g * N_STATE:(g + 1) * N_STATE]
            cg = bc_ref[:, BC_WIDTH + g * N_STATE:BC_WIDTH + (g + 1) * N_STATE]
            cb = _dot(cg.astype(BF16), bg.astype(BF16), NT)
            zc = {h: jnp.broadcast_to(z_ref[:, h:h + 1], (CHUNK, CHUNK)) for h in heads}
            decay = {h: jnp.exp(jnp.where(mask, zc[h] - zt_ref[h:h + 1, :], -jnp.inf)) for h in heads}
            u = {h: (xs_ref[:, h * HEAD_DIM:(h + 1) * HEAD_DIM] * dts_ref[:, h:h + 1]).astype(BF16) for h in heads}
            state = {h: h_ref[h] for h in heads}
            for h in heads:
                hs_ref[0, 0, h] = state[h]
            mix = {h: (cb * decay[h]).astype(BF16) for h in heads}
            cz = {h: (cg * jnp.exp(zc[h])).astype(BF16) for h in heads}
            bw = {h: (bg * jnp.exp(tot[:, h:h + 1] - zc[h])).astype(BF16) for h in heads}
            y = {h: _dot(mix[h], u[h], NN) + _dot(cz[h], state[h].astype(BF16), NT) for h in heads}
            new = {h: state[h] * e_tot[:, h:h + 1] + _dot(u[h], bw[h], TN) for h in heads}
            for h in heads:
                h_ref[h] = new[h]
                o_ref[0, :, h * HEAD_DIM:(h + 1) * HEAD_DIM] = y[h].astype(BF16)
        if ng:
            pl.when(step == 2 * nb * nchunk - 1)(lambda: _exchange_wait(copies))

    any_spec = pl.BlockSpec(memory_space=pl.ANY)
    res = pl.pallas_call(
        body, name="ssd_fwd_gather" if ng else "ssd_fwd", grid=(2, nb, nchunk),
        in_specs=[pl.BlockSpec((CHUNK, SSM_WIDTH), lambda d, b, c: (row(d, b, c), 0)),
                  pl.BlockSpec((CHUNK, 2 * BC_WIDTH), lambda d, b, c: (row(d, b, c), SSM_WIDTH // (2 * BC_WIDTH))),
                  pl.BlockSpec((CHUNK, LANES), lambda d, b, c: (row(d, b, c), 0)),
                  pl.BlockSpec((1, LANES), lambda d, b, c: (0, 0)),
                  pl.BlockSpec((1, LANES), lambda d, b, c: (0, 0))] + [any_spec] * ng,
        out_specs=[pl.BlockSpec((1, CHUNK, SSM_WIDTH), lambda d, b, c: (d, row(d, b, c), 0)),
                   pl.BlockSpec((1, 1, N_HEADS, HEAD_DIM, N_STATE), lambda d, b, c: (d, row(d, b, c), 0, 0, 0))]
        + [any_spec] * ng,
        out_shape=[jax.ShapeDtypeStruct((2, t, SSM_WIDTH), BF16),
                   jax.ShapeDtypeStruct((2, nb * nchunk, N_HEADS, HEAD_DIM, N_STATE), F32)]
        + _exchange_out_shapes(gather, gather=True),
        scratch_shapes=_SSD_SCRATCH + _exchange_sems(ng),
        compiler_params=_params("arbitrary", "arbitrary", "arbitrary"),
    )(xbc, xbc, dtraw, dtb, alog, *gather)
    return res[0], res[1], list(res[2:])


def _gate_norm_fwd(y2, xbc, proj, dvec, nw, *, tm):
    t = xbc.shape[0]
    half = SSM_WIDTH // N_GROUPS

    def body(y_ref, xs_ref, z_ref, d_ref, w_ref, o_ref):
        z = z_ref[...].astype(F32)
        p = (y_ref[0].astype(F32) + y_ref[1].astype(F32) + d_ref[...] * xs_ref[...]) * (z * _sigmoid(z))
        for g in range(N_GROUPS):
            pg = p[:, g * half:(g + 1) * half]
            r = lax.rsqrt(jnp.mean(pg * pg, axis=-1, keepdims=True) + NORM_EPS)
            o_ref[:, g * half:(g + 1) * half] = (pg * r * w_ref[:, g * half:(g + 1) * half]).astype(BF16)

    return pl.pallas_call(
        body, name="gate_norm_fwd", grid=(t // tm,),
        in_specs=[pl.BlockSpec((2, tm, SSM_WIDTH), lambda i: (0, i, 0)),
                  pl.BlockSpec((tm, SSM_WIDTH), lambda i: (i, 0)),
                  pl.BlockSpec((tm, SSM_WIDTH), lambda i: (i, PZ // SSM_WIDTH)),
                  pl.BlockSpec((1, SSM_WIDTH), lambda i: (0, 0)),
                  pl.BlockSpec((1, SSM_WIDTH), lambda i: (0, 0))],
        out_specs=pl.BlockSpec((tm, SSM_WIDTH), lambda i: (i, 0)),
        out_shape=jax.ShapeDtypeStruct((t, SSM_WIDTH), BF16),
        compiler_params=_params("parallel"),
    )(y2, xbc, proj, dvec, nw)


GROUP_ROWS = Q_PER_KV * CHUNK


def _keys_inside(n, nblk):
    kpos = (n - 1) * CHUNK + lax.broadcasted_iota(jnp.int32, (1, KEY_SPAN), 1)
    return (kpos >= 0) & (kpos < nblk * CHUNK)


def _per_head_column(ref, g):
    blk = lax.broadcasted_iota(jnp.int32, (GROUP_ROWS, 1), 0) // CHUNK
    col = jnp.zeros((GROUP_ROWS, 1), F32)
    for r in range(Q_PER_KV):
        h = g * Q_PER_KV + r
        col = jnp.where(blk == r, ref[:, h:h + 1], col)
    return col


def _stack_heads(ref, g, dtype):
    return jnp.concatenate([ref[:, (g * Q_PER_KV + r) * HEAD_DIM:(g * Q_PER_KV + r + 1) * HEAD_DIM].astype(dtype)
                            for r in range(Q_PER_KV)], axis=0)


def _kv_specs(nblk):
    kvb = PK // (2 * KV_WIDTH)

    def at(off):
        def idx(b, n):
            return (b * nblk + jnp.clip(n + off, 0, nblk - 1), kvb)
        return pl.BlockSpec((CHUNK, 2 * KV_WIDTH), idx)
    return [at(-1), at(0), at(1)]


def _attn_fwd(proj, bias, sink, *, nb, seq, gather=()):
    nblk = seq // CHUNK
    t = nb * seq
    scale = HEAD_DIM ** -0.5
    ng = len(gather)

    def body(*refs):
        q_ref, kp_ref, kc_ref, kn_ref, bias_ref, sink_ref = refs[:6]
        o_ref, lse_ref = refs[6 + ng:8 + ng]
        n = pl.program_id(1)
        if ng:
            step = pl.program_id(0) * nblk + n
            copies = _exchange_copies(refs[6:6 + ng], refs[8 + ng:8 + 2 * ng], *refs[8 + 2 * ng:], gather=True)
            pl.when(step == 0)(lambda: _exchange_start(copies))
        inside = _keys_inside(n, nblk)
        groups = range(KV_HEADS)

        def keys(g, off):
            cs = slice(off + g * HEAD_DIM, off + (g + 1) * HEAD_DIM)
            return jnp.concatenate([kp_ref[:, cs], kc_ref[:, cs], kn_ref[:, cs]], axis=0).astype(BF16)

        qs = [(_stack_heads(q_ref, g, F32) * scale).astype(BF16) for g in groups]
        ss = [jnp.where(inside, _dot(qs[g], keys(g, 0), NT)
                        + bias_ref[g * Q_PER_KV:(g + 1) * Q_PER_KV].reshape(GROUP_ROWS, KEY_SPAN), -jnp.inf) for g in groups]
        sks = [_per_head_column(sink_ref, g) for g in groups]
        ms = [jnp.maximum(jnp.max(ss[g], axis=-1, keepdims=True), sks[g]) for g in groups]
        ps = [jnp.exp(ss[g] - ms[g]) for g in groups]
        denoms = [jnp.sum(ps[g], axis=-1, keepdims=True) + jnp.exp(sks[g] - ms[g]) for g in groups]
        outs = [(_dot(ps[g].astype(BF16), keys(g, KV_WIDTH), NN) * (1.0 / denoms[g])).astype(BF16) for g in groups]
        lses = []
        for g in groups:
            lse = ms[g] + jnp.log(denoms[g])
            for r in range(Q_PER_KV):
                h = g * Q_PER_KV + r
                o_ref[:, h * HEAD_DIM:(h + 1) * HEAD_DIM] = outs[g][r * CHUNK:(r + 1) * CHUNK]
                lses.append(lse[r * CHUNK:(r + 1) * CHUNK])
        lse_ref[...] = jnp.concatenate(lses, axis=1)
        if ng:
            pl.when(step == nb * nblk - 1)(lambda: _exchange_wait(copies))

    any_spec = pl.BlockSpec(memory_space=pl.ANY)
    res = pl.pallas_call(
        body, name="attn_fwd_gather" if ng else "attn_fwd", grid=(nb, nblk),
        in_specs=[pl.BlockSpec((CHUNK, D_MODEL), lambda b, n: (b * nblk + n, PQ // D_MODEL))] + _kv_specs(nblk) + [
            pl.BlockSpec((N_HEADS, CHUNK, KEY_SPAN), lambda b, n: (0, 0, 0)),
            pl.BlockSpec((1, LANES), lambda b, n: (0, 0))] + [any_spec] * ng,
        out_specs=[pl.BlockSpec((CHUNK, D_MODEL), lambda b, n: (b * nblk + n, 0)),
                   pl.BlockSpec((CHUNK, N_HEADS), lambda b, n: (b * nblk + n, 0))] + [any_spec] * ng,
        out_shape=[jax.ShapeDtypeStruct((t, D_MODEL), BF16), jax.ShapeDtypeStruct((t, N_HEADS), F32)]
        + _exchange_out_shapes(gather, gather=True),
        scratch_shapes=_exchange_sems(ng),
        compiler_params=_params("arbitrary", "arbitrary"),
    )(proj, proj, proj, proj, bias, sink, *gather)
    return res[0], res[1], list(res[2:])


def _loss_head(x, tgt, nw, *, tm):
    t, d = x.shape

    def body(x_ref, t_ref, w_ref, dx_ref, dw_ref, l_ref):
        @pl.when(pl.program_id(0) == 0)
        def _():
            dw_ref[...] = jnp.zeros_like(dw_ref)
            l_ref[...] = jnp.zeros_like(l_ref)

        xv = x_ref[...]
        w = w_ref[...]
        r = lax.rsqrt(jnp.mean(xv * xv, axis=-1, keepdims=True) + NORM_EPS)
        xh = xv * r
        err = xh * w - t_ref[...]
        l_ref[...] += jnp.sum(err * err) * (0.5 / d)
        dy = err * (1.0 / d)
        gw = dy * w
        dx_ref[...] = r * (gw - xh * jnp.mean(gw * xh, axis=-1, keepdims=True))
        dw_ref[...] += jnp.sum(dy * xh, axis=0, keepdims=True)

    return pl.pallas_call(
        body, name="loss_head", grid=(t // tm,),
        in_specs=[pl.BlockSpec((tm, d), lambda i: (i, 0)), pl.BlockSpec((tm, d), lambda i: (i, 0)),
                  pl.BlockSpec((1, d), lambda i: (0, 0))],
        out_specs=[pl.BlockSpec((tm, d), lambda i: (i, 0)), pl.BlockSpec((1, d), lambda i: (0, 0)),
                   pl.BlockSpec((1, LANES), lambda i: (0, 0))],
        out_shape=[jax.ShapeDtypeStruct((t, d), F32), jax.ShapeDtypeStruct((1, d), F32),
                   jax.ShapeDtypeStruct((1, LANES), F32)],
        compiler_params=_params("arbitrary"),
    )(x, tgt, nw)


def _to_proj_layout(w):
    pad = jnp.zeros(w.shape[:-1] + (PROJ_W - IN_COLS,), w.dtype)
    return jnp.concatenate([w[..., OZ:OXBC], w[..., OXBC:OXBC + SSM_WIDTH], w[..., OQ:OK_],
                            w[..., OXBC + SSM_WIDTH:ODT], w[..., OK_:IN_COLS], w[..., ODT:OQ], pad], axis=-1)


def _from_proj_layout(g):
    return jnp.concatenate([g[..., PZ:PZ + 2 * SSM_WIDTH], g[..., PB:PB + 2 * BC_WIDTH], g[..., PDT:PDT + 2 * N_HEADS],
                            g[..., PQ:PQ + D_MODEL], g[..., PK:PK + 2 * KV_WIDTH]], axis=-1)


def _pad_lanes(v):
    return jnp.pad(v.reshape(1, -1), ((0, 0), (0, LANES - v.size)))


def _layer_fwd(x, p, band_bias, *, nb, seq, own_shards, next_shards=()):
    proj, h1 = _rms_matmul(x, p["n1"], p["w_in"], name="in_proj", tm=MM_ROWS, tn=1024, ncols=PDT)
    dtraw = _matmul(h1, p["w_in"][:, PDT:], "nn", name="in_proj_dt", tm=MM_ROWS, tn=LANES, tk=D_MODEL)
    xbc, gconv = _ssm_conv_fwd(proj, p["conv_w"], p["conv_b"], nb=nb, seq=seq)
    y2, states, (g_out, g_up, g_down) = _ssd_fwd(xbc, dtraw, p["dtb"], p["alog"], nb=nb, seq=seq, gather=own_shards)
    p = dict(p, w_out=_blocks_to_rows(g_out, 1)[0], w_up=_blocks_to_cols(g_up, 1)[0], w_down=_blocks_to_rows(g_down, 1)[0])
    y_ssm = _gate_norm_fwd(y2, xbc, proj, p["dvec"], p["ssm_nw"], tm=256)
    y_att, lse, gathered = _attn_fwd(proj, band_bias, p["sink"], nb=nb, seq=seq, gather=next_shards)
    x1 = _matmul(y_ssm, p["w_out"], "nn", name="out_proj_ssm", tm=MM_ROWS, tn=1024, tk=1024, res=x)
    x1 = _matmul(y_att, p["w_out"], "nn", name="out_proj_att", tm=MM_ROWS, tn=1024, tk=1024, res=x1, b_k0=SSM_WIDTH)
    gu, h2 = _rms_matmul(x1, p["n2"], p["w_up"], name="up_proj", tm=MM_ROWS, tn=1408)
    act, fconv = _ffn_act_fwd(gu, p["ffn_cw"], p["ffn_cb"], nb=nb, seq=seq)
    x2 = _matmul(act, p["w_down"], "nn", name="down_proj", tm=MM_ROWS, tn=1024, tk=D_FF, res=x1)
    saved = dict(x=x, proj=proj, dtraw=dtraw, h1=h1, xbc=xbc, gconv=gconv, y2=y2, states=states, y_ssm=y_ssm, y_att=y_att,
                 lse=lse, x1=x1, gu=gu, fconv=fconv, h2=h2, act=act)
    return x2, p, saved, gathered


def _rms_bwd(x, dh, nw, dres, *, tm, name):
    t, d = x.shape

    def body(x_ref, dh_ref, w_ref, r_ref, dx_ref, dw_ref):
        @pl.when(pl.program_id(0) == 0)
        def _():
            dw_ref[...] = jnp.zeros_like(dw_ref)

        xv = x_ref[...]
        dh_v = dh_ref[...].astype(F32)
        r = lax.rsqrt(jnp.mean(xv * xv, axis=-1, keepdims=True) + NORM_EPS)
        xh = xv * r
        gw = dh_v * w_ref[...]
        dx_ref[...] = r_ref[...] + r * (gw - xh * jnp.mean(gw * xh, axis=-1, keepdims=True))
        dw_ref[...] += jnp.sum(dh_v * xh, axis=0, keepdims=True)

    row = pl.BlockSpec((tm, d), lambda i: (i, 0))
    vec = pl.BlockSpec((1, d), lambda i: (0, 0))
    return pl.pallas_call(
        body, name=name, grid=(t // tm,), in_specs=[row, row, vec, row], out_specs=[row, vec],
        out_shape=[jax.ShapeDtypeStruct((t, d), F32), jax.ShapeDtypeStruct((1, d), F32)],
        compiler_params=_params("arbitrary"),
    )(x, dh, nw, dres)


def _dsilu(g, sg):
    return sg * (1.0 + g * (1.0 - sg))


def _conv_taps_bwd(gpre, dg, w_ref, dwb_ref, taps, seq):
    pad = taps // 2
    dg_ext, gpre_ext = _zero_ext(dg), _zero_ext(gpre)
    dpre = None
    for k in range(taps):
        term = _shifted(dg_ext, pad - k, seq) * w_ref[k:k + 1, :]
        dpre = term if dpre is None else dpre + term
        dwb_ref[k:k + 1, :] += jnp.sum(dg * _shifted(gpre_ext, k - pad, seq), axis=0, keepdims=True)
    dwb_ref[SUBLANES - 1:SUBLANES, :] += jnp.sum(dg, axis=0, keepdims=True)
    return dpre


def _ffn_act_bwd(gu, gconv, dact, cw, *, nb, seq):
    width = 256
    nj = D_FF // width

    def body(g_ref, u_ref, s_ref, da_ref, w_ref, dg_ref, du_ref, dwb_ref):
        @pl.when(pl.program_id(1) == 0)
        def _():
            dwb_ref[...] = jnp.zeros_like(dwb_ref)

        g = s_ref[...].astype(F32)
        sg = _sigmoid(g)
        da = da_ref[...].astype(F32)
        du_ref[...] = (da * g * sg).astype(BF16)
        dgc = da * u_ref[...].astype(F32) * _dsilu(g, sg)
        dg_ref[...] = _conv_taps_bwd(g_ref[...].astype(F32), dgc, w_ref, dwb_ref, FFN_TAPS, seq).astype(BF16)

    blk = lambda off: pl.BlockSpec((seq, width), lambda j, b: (b, j + off))
    return pl.pallas_call(
        body, name="ffn_act_bwd", grid=(nj, nb),
        in_specs=[blk(0), blk(nj), blk(0), blk(0), pl.BlockSpec((FFN_TAPS, width), lambda j, b: (0, j))],
        out_specs=[blk(0), blk(0), pl.BlockSpec((SUBLANES, width), lambda j, b: (0, j))],
        out_shape=[jax.ShapeDtypeStruct((nb * seq, D_FF), BF16), jax.ShapeDtypeStruct((nb * seq, D_FF), BF16),
                   jax.ShapeDtypeStruct((SUBLANES, D_FF), F32)],
        compiler_params=_params("parallel", "arbitrary"),
    )(gu, gu, gconv, dact, cw)


def _ssm_conv_bwd(proj, gconv, pair, cw, *, nb, seq, name, width, proj_col, conv_col, ncol, extra=None, scale=None):
    has_extra = extra is not None

    def body(*refs):
        if has_extra:
            x_ref, g_ref, p_ref, w_ref, e_ref, s_ref, dx_ref, dwb_ref = refs
        else:
            x_ref, g_ref, p_ref, w_ref, dx_ref, dwb_ref = refs

        @pl.when(pl.program_id(1) == 0)
        def _():
            dwb_ref[...] = jnp.zeros_like(dwb_ref)

        g = g_ref[...].astype(F32)
        da = p_ref[0] + p_ref[1]
        if has_extra:
            da = da + e_ref[...] * s_ref[...]
        dx_ref[...] = _conv_taps_bwd(x_ref[...].astype(F32), da * _dsilu(g, _sigmoid(g)), w_ref, dwb_ref, SSM_TAPS,
                                     seq).astype(BF16)

    in_specs = [pl.BlockSpec((seq, width), lambda j, b: (b, j + proj_col)),
                pl.BlockSpec((seq, width), lambda j, b: (b, j + conv_col)),
                pl.BlockSpec((2, seq, width), lambda j, b: (0, b, j)),
                pl.BlockSpec((SSM_TAPS, width), lambda j, b: (0, j + conv_col))]
    args = [proj, gconv, pair, cw]
    if has_extra:
        in_specs += [pl.BlockSpec((seq, width), lambda j, b: (b, j)), pl.BlockSpec((1, width), lambda j, b: (0, j))]
        args += [extra, scale]
    return pl.pallas_call(
        body, name=name, grid=(ncol, nb), in_specs=in_specs,
        out_specs=[pl.BlockSpec((seq, width), lambda j, b: (b, j)), pl.BlockSpec((SUBLANES, width), lambda j, b: (0, j))],
        out_shape=[jax.ShapeDtypeStruct((nb * seq, ncol * width), BF16), jax.ShapeDtypeStruct((SUBLANES, ncol * width), F32)],
        compiler_params=_params("parallel", "arbitrary"),
    )(*args)


def _attn_bwd(proj, dmix, y_att, lse, bias, sink, dbias_in, *, nb, seq):
    nblk = seq // CHUNK
    t = nb * seq
    scale = HEAD_DIM ** -0.5

    def body(q_ref, kp_ref, kc_ref, kn_ref, do_ref, o_ref, lse_ref, bias_ref, sink_ref, dbin_ref,
             dq_ref, dkv_ref, dbias_ref, dsink_ref):
        b, n = pl.program_id(0), pl.program_id(1)

        @pl.when(n == 0)
        def _():
            dkv_ref[...] = jnp.zeros_like(dkv_ref)

        @pl.when((n == 0) & (b == 0))
        def _():
            dbias_ref[...] = dbin_ref[...]
            dsink_ref[...] = jnp.zeros_like(dsink_ref)

        inside = _keys_inside(n, nblk)
        lane = lax.broadcasted_iota(jnp.int32, (1, LANES), 1)
        dsink = jnp.zeros((1, LANES), F32)
        rows = pl.ds(pl.multiple_of(n * CHUNK, CHUNK), KEY_SPAN)
        groups = range(KV_HEADS)

        def keys(g, off):
            cs = slice(off + g * HEAD_DIM, off + (g + 1) * HEAD_DIM)
            return jnp.concatenate([kp_ref[:, cs], kc_ref[:, cs], kn_ref[:, cs]], axis=0).astype(BF16)

        kcat = [keys(g, 0) for g in groups]
        vcat = [keys(g, KV_WIDTH) for g in groups]
        q = [(_stack_heads(q_ref, g, F32) * scale).astype(BF16) for g in groups]
        do = [_stack_heads(do_ref, g, F32) for g in groups]
        do16 = [do[g].astype(BF16) for g in groups]
        lse = [jnp.concatenate([lse_ref[:, g * Q_PER_KV + r:g * Q_PER_KV + r + 1] for r in range(Q_PER_KV)], axis=0)
               for g in groups]
        s = [jnp.where(inside, _dot(q[g], kcat[g], NT)
                       + bias_ref[g * Q_PER_KV:(g + 1) * Q_PER_KV].reshape(GROUP_ROWS, KEY_SPAN), -jnp.inf) for g in groups]
        p = [jnp.exp(s[g] - lse[g]) for g in groups]
        delta = [jnp.sum(do[g] * _stack_heads(o_ref, g, F32), axis=-1, keepdims=True) for g in groups]
        ds = [p[g] * (_dot(do16[g], vcat[g], NT) - delta[g]) for g in groups]
        ds16 = [ds[g].astype(BF16) for g in groups]
        sink_part = [jnp.exp(_per_head_column(sink_ref, g) - lse[g]) * delta[g] for g in groups]
        dq = [_dot(ds16[g], kcat[g], NN) * scale for g in groups]
        dk = [_dot(ds16[g], q[g], TN) for g in groups]
        dv = [_dot(p[g].astype(BF16), do16[g], TN) for g in groups]
        for g in groups:
            dbias_ref[g * Q_PER_KV:(g + 1) * Q_PER_KV] += ds[g].reshape(Q_PER_KV, CHUNK, KEY_SPAN)
            for r in range(Q_PER_KV):
                h = g * Q_PER_KV + r
                dq_ref[:, h * HEAD_DIM:(h + 1) * HEAD_DIM] = dq[g][r * CHUNK:(r + 1) * CHUNK].astype(BF16)
                dsink = dsink - jnp.where(lane == h, jnp.sum(sink_part[g][r * CHUNK:(r + 1) * CHUNK], axis=0, keepdims=True), 0.0)
            dkv_ref[0, rows, g * HEAD_DIM:(g + 1) * HEAD_DIM] += dk[g]
            dkv_ref[0, rows, KV_WIDTH + g * HEAD_DIM:KV_WIDTH + (g + 1) * HEAD_DIM] += dv[g]
        dsink_ref[...] += dsink

    blk = lambda cb: pl.BlockSpec((CHUNK, D_MODEL), lambda b, n: (b * nblk + n, cb))
    whole = pl.BlockSpec((N_HEADS, CHUNK, KEY_SPAN), lambda b, n: (0, 0, 0))
    vec = pl.BlockSpec((1, LANES), lambda b, n: (0, 0))
    return pl.pallas_call(
        body, name="attn_bwd", grid=(nb, nblk),
        in_specs=[blk(PQ // D_MODEL)] + _kv_specs(nblk) + [
            blk(1), blk(0), pl.BlockSpec((CHUNK, N_HEADS), lambda b, n: (b * nblk + n, 0)), whole, vec, whole],
        out_specs=[blk(0), pl.BlockSpec((1, seq + 2 * CHUNK, 2 * KV_WIDTH), lambda b, n: (b, 0, 0)), whole, vec],
        out_shape=[jax.ShapeDtypeStruct((t, D_MODEL), BF16),
                   jax.ShapeDtypeStruct((nb, seq + 2 * CHUNK, 2 * KV_WIDTH), F32),
                   jax.ShapeDtypeStruct((N_HEADS, CHUNK, KEY_SPAN), F32),
                   jax.ShapeDtypeStruct((1, LANES), F32)],
        compiler_params=_params("arbitrary", "arbitrary"),
    )(proj, proj, proj, proj, dmix, y_att, lse, bias, sink, dbias_in)


def _gate_norm_bwd(y2, xbc, proj, dmix, dvec, nw, *, tm):
    t = xbc.shape[0]
    half = SSM_WIDTH // N_GROUPS

    def body(y_ref, xs_ref, z_ref, do_ref, d_ref, w_ref, dyv_ref, dz_ref, dd_ref, dw_ref):
        @pl.when(pl.program_id(0) == 0)
        def _():
            dd_ref[...] = jnp.zeros_like(dd_ref)
            dw_ref[...] = jnp.zeros_like(dw_ref)

        z = z_ref[...].astype(F32)
        xs = xs_ref[...]
        sg = _sigmoid(z)
        gz = z * sg
        yv = y_ref[0].astype(F32) + y_ref[1].astype(F32) + d_ref[...] * xs
        p = yv * gz
        do = do_ref[...].astype(F32)
        for g in range(N_GROUPS):
            cs = slice(g * half, (g + 1) * half)
            pg = p[:, cs]
            r = lax.rsqrt(jnp.mean(pg * pg, axis=-1, keepdims=True) + NORM_EPS)
            ph = pg * r
            gw = do[:, cs] * w_ref[:, cs]
            dp = r * (gw - ph * jnp.mean(gw * ph, axis=-1, keepdims=True))
            dyv = dp * gz[:, cs]
            dyv_ref[:, cs] = dyv
            dz_ref[:, cs] = (dp * yv[:, cs] * _dsilu(z[:, cs], sg[:, cs])).astype(BF16)
            dw_ref[:, cs] += jnp.sum(do[:, cs] * ph, axis=0, keepdims=True)
            dd_ref[:, cs] += jnp.sum(dyv * xs[:, cs], axis=0, keepdims=True)

    row = lambda cb: pl.BlockSpec((tm, SSM_WIDTH), lambda i: (i, cb))
    vec = pl.BlockSpec((1, SSM_WIDTH), lambda i: (0, 0))
    return pl.pallas_call(
        body, name="gate_norm_bwd", grid=(t // tm,),
        in_specs=[pl.BlockSpec((2, tm, SSM_WIDTH), lambda i: (0, i, 0)), row(0), row(PZ // SSM_WIDTH), row(0), vec, vec],
        out_specs=[row(0), row(0), vec, vec],
        out_shape=[jax.ShapeDtypeStruct((t, SSM_WIDTH), F32), jax.ShapeDtypeStruct((t, SSM_WIDTH), BF16),
                   jax.ShapeDtypeStruct((1, SSM_WIDTH), F32), jax.ShapeDtypeStruct((1, SSM_WIDTH), F32)],
        compiler_params=_params("arbitrary"),
    )(y2, xbc, proj, dmix, dvec, nw)


def _ssd_specs(nchunk, row):
    return [pl.BlockSpec((CHUNK, SSM_WIDTH), lambda d, b, c: (row(d, b, c), 0)),
            pl.BlockSpec((CHUNK, 2 * BC_WIDTH), lambda d, b, c: (row(d, b, c), SSM_WIDTH // (2 * BC_WIDTH))),
            pl.BlockSpec((CHUNK, LANES), lambda d, b, c: (row(d, b, c), 0)),
            pl.BlockSpec((1, LANES), lambda d, b, c: (0, 0)),
            pl.BlockSpec((1, LANES), lambda d, b, c: (0, 0)),
            pl.BlockSpec((CHUNK, SSM_WIDTH), lambda d, b, c: (row(d, b, c), 0))]


_SSD_SCRATCH = [pltpu.VMEM((N_HEADS, HEAD_DIM, N_STATE), F32),
                pltpu.VMEM((CHUNK, N_HEADS), F32), pltpu.VMEM((N_HEADS, CHUNK), F32),
                pltpu.VMEM((CHUNK, N_HEADS), F32)]


def _ssd_bwd(xbc, dtraw, dtb, alog, dyv, states, *, nb, seq, scatter=()):
    nchunk = seq // CHUNK
    t = nb * seq
    row = _chunk_index(nchunk, False)
    ns = len(scatter)

    def body(*refs):
        xs_ref, bc_ref, dt_ref, dtb_ref, alog_ref, dy_ref, hs_ref = refs[:7]
        dx_ref, db_ref, dc_ref, draw_ref, da_ref, dbias_ref = refs[7 + ns:13 + ns]
        h_ref, z_ref, zt_ref, dts_ref, acc_ref, span_ref = refs[13 + 2 * ns:19 + 2 * ns]
        d, b, c = pl.program_id(0), pl.program_id(1), pl.program_id(2)
        if ns:
            step = (d * nb + b) * nchunk + c
            copies = _exchange_copies(refs[7:7 + ns], refs[13 + ns:13 + 2 * ns], *refs[19 + 2 * ns:], gather=False)
            pl.when(step == 0)(lambda: _exchange_start(copies))

        @pl.when(c == 0)
        def _():
            h_ref[...] = jnp.zeros_like(h_ref)

        @pl.when((c == 0) & (b == 0) & (d == 0))
        def _():
            da_ref[...] = jnp.zeros_like(da_ref)
            dbias_ref[...] = jnp.zeros_like(dbias_ref)

        ahead, dt, tot, dt_all = _scan_setup(d, dt_ref, dtb_ref, alog_ref, z_ref, zt_ref,
                                             lower_when_dir0=False, inclusive=False)
        mask = ahead >= 0
        dts_ref[...] = dt
        e_tot = jnp.exp(tot)
        acc_ref[...] = jnp.zeros_like(acc_ref)
        span_ref[...] = jnp.zeros_like(span_ref)
        for g in range(N_GROUPS):
            bg = bc_ref[:, g * N_STATE:(g + 1) * N_STATE]
            cg = bc_ref[:, BC_WIDTH + g * N_STATE:BC_WIDTH + (g + 1) * N_STATE]
            bg16 = bg.astype(BF16)
            cg16 = cg.astype(BF16)
            bc_t = _dot(bg16, cg16, NT)
            heads = range(g * HEADS_PER_GROUP, (g + 1) * HEADS_PER_GROUP)
            cols = {h: slice(h * HEAD_DIM, (h + 1) * HEAD_DIM) for h in heads}
            zc = {h: jnp.broadcast_to(z_ref[:, h:h + 1], (CHUNK, CHUNK)) for h in heads}
            decay = {h: jnp.exp(jnp.where(mask, zc[h] - zt_ref[h:h + 1, :], -jnp.inf)) for h in heads}
            e_z = {h: jnp.exp(zc[h]) for h in heads}
            e_tz = {h: jnp.exp(tot[:, h:h + 1] - zc[h]) for h in heads}
            x_h = {h: xs_ref[:, cols[h]] for h in heads}
            dt_h = {h: dts_ref[:, h:h + 1] for h in heads}
            u = {h: (x_h[h] * dt_h[h]).astype(BF16) for h in heads}
            dy = {h: dy_ref[:, cols[h]].astype(BF16) for h in heads}
            state = {h: h_ref[h] for h in heads}
            st16 = {h: state[h].astype(BF16) for h in heads}
            fstate = {h: hs_ref[0, 0, h] for h in heads}
            mix = {h: (bc_t * decay[h]).astype(BF16) for h in heads}
            bz = {h: (bg * e_z[h]).astype(BF16) for h in heads}
            du = {h: _dot(mix[h], dy[h], NN) + _dot(bz[h], st16[h], NT) for h in heads}
            w2f = {h: _dot(u[h], dy[h], NT) * decay[h] for h in heads}
            db_out = {h: e_z[h] * _dot(u[h], st16[h], NN) for h in heads}
            dc_out = {h: e_tz[h] * _dot(dy[h], fstate[h].astype(BF16), NN) for h in heads}
            w2_sum = w2f[heads[0]]
            for h in heads[1:]:
                w2_sum = w2_sum + w2f[h]
            w2_sum = w2_sum.astype(BF16)
            pairs = {h: w2f[h] * bc_t for h in heads}
            col_in = {h: jnp.sum(pairs[h], axis=-1, keepdims=True) for h in heads}
            row_in = {h: jnp.sum(pairs[h], axis=0, keepdims=True) for h in heads}
            row_out = {h: jnp.sum(dc_out[h] * cg, axis=-1, keepdims=True) for h in heads}
            col_out = {h: jnp.sum(db_out[h] * bg, axis=-1, keepdims=True) for h in heads}
            ddt_h = {h: jnp.sum(du[h] * x_h[h], axis=-1, keepdims=True) for h in heads}
            cz = {h: (cg * e_tz[h]).astype(BF16) for h in heads}
            new = {h: state[h] * e_tot[:, h:h + 1] + _dot(dy[h], cz[h], TN) for h in heads}
            dbg = _dot(w2_sum, cg16, NN)
            dcg = _dot(w2_sum, bg16, TN)
            for h in heads:
                dbg = dbg + db_out[h]
                dcg = dcg + dc_out[h]
                acc_ref[0, :, h:h + 1] = row_out[h] - col_in[h]
                acc_ref[1, :, h:h + 1] = col_out[h]
                acc_ref[2, :, h:h + 1] = ddt_h[h]
                acc_ref[3, h:h + 1, :] = row_in[h]
                span_ref[0:1, h:h + 1] = e_tot[:, h:h + 1] * jnp.sum(fstate[h] * state[h], keepdims=True)
                dx_ref[0, :, cols[h]] = du[h] * dt_h[h]
                h_ref[h] = new[h]
            db_ref[0, :, g * N_STATE:(g + 1) * N_STATE] = dbg
            dc_ref[0, :, g * N_STATE:(g + 1) * N_STATE] = dcg
        tri = mask.astype(F32)
        at_dir = lambda v: jnp.where(d == 0, v, pltpu.roll(v, N_HEADS, 1))
        dadt = at_dir(jnp.dot(tri, acc_ref[0] + acc_ref[3].T, precision=HIGHEST, preferred_element_type=F32)
                      + jnp.dot(1.0 - tri, acc_ref[1], precision=HIGHEST, preferred_element_type=F32) + span_ref[0:1, :])
        ddt = at_dir(acc_ref[2])
        a = -jnp.exp(alog_ref[...])
        draw = (ddt + a * dadt) * _sigmoid(dt_ref[...] + dtb_ref[...])
        draw_ref[0] = draw
        da_ref[...] += jnp.sum(dt_all * dadt, axis=0, keepdims=True) * a
        dbias_ref[...] += jnp.sum(draw, axis=0, keepdims=True)
        if ns:
            pl.when(step == 2 * nb * nchunk - 1)(lambda: _exchange_wait(copies))

    out_row = lambda w: pl.BlockSpec((1, CHUNK, w), lambda d, b, c: (d, row(d, b, c), 0))
    vec = pl.BlockSpec((1, LANES), lambda d, b, c: (0, 0))
    any_spec = pl.BlockSpec(memory_space=pl.ANY)
    res = pl.pallas_call(
        body, name="ssd_bwd_scatter" if ns else "ssd_bwd", grid=(2, nb, nchunk),
        in_specs=_ssd_specs(nchunk, row) + [
            pl.BlockSpec((1, 1, N_HEADS, HEAD_DIM, N_STATE), lambda d, b, c: (d, row(d, b, c), 0, 0, 0))]
        + [any_spec] * ns,
        out_specs=[out_row(SSM_WIDTH), out_row(BC_WIDTH), out_row(BC_WIDTH), out_row(LANES), vec, vec] + [any_spec] * ns,
        out_shape=[jax.ShapeDtypeStruct((2, t, SSM_WIDTH), F32), jax.ShapeDtypeStruct((2, t, BC_WIDTH), F32),
                   jax.ShapeDtypeStruct((2, t, BC_WIDTH), F32),
                   jax.ShapeDtypeStruct((2, t, LANES), F32), jax.ShapeDtypeStruct((1, LANES), F32),
                   jax.ShapeDtypeStruct((1, LANES), F32)] + _exchange_out_shapes(scatter, gather=False),
        scratch_shapes=_SSD_SCRATCH + [pltpu.VMEM((4, CHUNK, LANES), F32), pltpu.VMEM((SUBLANES, LANES), F32)]
        + _exchange_sems(ns),
        compiler_params=_params("arbitrary", "arbitrary", "arbitrary"),
    )(xbc, xbc, dtraw, dtb, alog, dyv, states, *scatter)
    return tuple(res[:6]) + (list(res[6:]),)


def _layer_bwd(dx2, p, s, band_bias, dbias_in, *, nb, seq, pending=()):
    t = nb * seq
    x, proj, xbc, x1 = s["x"], s["proj"], s["xbc"], s["x1"]
    dact = _matmul(dx2, p["w_down"], "nt", name="down_proj_dx", tm=MM_ROWS, tn=1408, tk=1024, out_dtype=BF16)
    g_w_down = _matmul(s["act"], dx2, "tn", name="down_proj_dw", tm=1408, tn=1024, tk=1024)
    dg, du, dwb_ffn = _ffn_act_bwd(s["gu"], s["fconv"], dact, p["ffn_cw"], nb=nb, seq=seq)
    dh2 = _matmul(dg, p["w_up"], "nt", name="up_proj_dx_g", tm=MM_ROWS, tn=1024, tk=D_FF)
    dh2 = _matmul(du, p["w_up"], "nt", name="up_proj_dx_u", tm=MM_ROWS, tn=1024, tk=D_FF, res=dh2, b_k0=D_FF)
    g_w_up = jnp.concatenate([_matmul(s["h2"], dg, "tn", name="up_proj_dw_g", tm=1024, tn=1408, tk=2 * MM_ROWS),
                              _matmul(s["h2"], du, "tn", name="up_proj_dw_u", tm=1024, tn=1408, tk=2 * MM_ROWS)], axis=1)
    dx1, g_n2 = _rms_bwd(x1, dh2, p["n2"], dx2, tm=512, name="norm2_bwd")
    dmix = _matmul(dx1, p["w_out"], "nt", name="out_proj_dx", tm=MM_ROWS, tn=1024, tk=1024, out_dtype=BF16)
    g_w_out = jnp.concatenate([_matmul(s["y_ssm"], dx1, "tn", name="out_proj_dw_ssm", tm=1024, tn=1024, tk=1024),
                               _matmul(s["y_att"], dx1, "tn", name="out_proj_dw_att", tm=1024, tn=1024, tk=1024)], axis=0)
    dq, dkv, dbias, dsink = _attn_bwd(proj, dmix, s["y_att"], s["lse"], band_bias, p["sink"], dbias_in, nb=nb, seq=seq)
    dkv = dkv[:, CHUNK:CHUNK + seq, :].reshape(t, 2 * KV_WIDTH)
    dyv, dz, g_dvec, g_ssm_nw = _gate_norm_bwd(s["y2"], xbc, proj, dmix, p["dvec"], p["ssm_nw"], tm=256)
    own = [_rows_to_blocks(g_w_down[None], 1).astype(BF16), _cols_to_blocks(g_w_up[None], 1).astype(BF16),
           _rows_to_blocks(g_w_out[None], 1).astype(BF16)]
    dxs2, db2, dc2, draw2, g_alog, g_dtb, exchanged = _ssd_bwd(xbc, s["dtraw"], p["dtb"], p["alog"], dyv, s["states"],
                                                               nb=nb, seq=seq, scatter=own + list(pending))
    ddt_raw = draw2[0] + draw2[1]
    conv = dict(nb=nb, seq=seq)
    dxs_pre, dwb_xs = _ssm_conv_bwd(proj, s["gconv"], dxs2, p["conv_w"], name="ssm_conv_bwd_x", width=256,
                                    proj_col=PXS // 256, conv_col=0, ncol=4, extra=dyv, scale=p["dvec"], **conv)
    db_pre, dwb_b = _ssm_conv_bwd(proj, s["gconv"], db2, p["conv_w"], name="ssm_conv_bwd_b", width=256,
                                  proj_col=PB // 256, conv_col=SSM_WIDTH // 256, ncol=1, **conv)
    dc_pre, dwb_c = _ssm_conv_bwd(proj, s["gconv"], dc2, p["conv_w"], name="ssm_conv_bwd_c", width=256,
                                  proj_col=PC // 256, conv_col=(SSM_WIDTH + BC_WIDTH) // 256, ncol=1, **conv)
    dwb_ssm = jnp.concatenate([dwb_xs, dwb_b, dwb_c], axis=1)
    dproj = jnp.concatenate([dz, dxs_pre, dq, db_pre, dc_pre, dkv.astype(BF16), ddt_raw.astype(BF16)], axis=1)
    dh1 = _matmul(dproj, p["w_in"], "nt", name="in_proj_dx", tm=MM_ROWS, tn=1024, tk=PROJ_W)
    g_w_in = _matmul(s["h1"], dproj, "tn", name="in_proj_dw", tm=1024, tn=1408, tk=2 * MM_ROWS)
    dx, g_n1 = _rms_bwd(x, dh1, p["n1"], dx1, tm=512, name="norm1_bwd")
    grads = dict(n1=g_n1, w_in=g_w_in, conv_w=dwb_ssm[:SSM_TAPS], conv_b=dwb_ssm[SUBLANES - 1], dtb=g_dtb, alog=g_alog,
                 dvec=g_dvec, ssm_nw=g_ssm_nw, sink=dsink, w_out=g_w_out, n2=g_n2, w_up=g_w_up,
                 ffn_cw=dwb_ffn[:FFN_TAPS], ffn_cb=dwb_ffn[SUBLANES - 1], w_down=g_w_down)
    return dx, dbias, grads, exchanged


def _band_bias(rel_bias, bucket):
    def body(rb_ref, b_ref, o_ref):
        o_ref[...] = jnp.zeros_like(o_ref)

        def per_bucket(k, carry):
            hit = b_ref[...] == k
            for h in range(N_HEADS):
                o_ref[h] = jnp.where(hit, rb_ref[k, h], o_ref[h])
            return carry

        lax.fori_loop(0, REL_BUCKETS, per_bucket, 0)
        qi = lax.broadcasted_iota(jnp.int32, (CHUNK, KEY_SPAN), 0)
        kj = lax.broadcasted_iota(jnp.int32, (CHUNK, KEY_SPAN), 1)
        band = jnp.abs(kj - CHUNK - qi) <= CHUNK
        for h in range(N_HEADS):
            o_ref[h] = jnp.where(band, o_ref[h], -jnp.inf)

    return pl.pallas_call(
        body, name="band_bias", out_shape=jax.ShapeDtypeStruct((N_HEADS, CHUNK, KEY_SPAN), F32),
        in_specs=[pl.BlockSpec(memory_space=pltpu.SMEM), pl.BlockSpec(memory_space=pltpu.VMEM)],
        out_specs=pl.BlockSpec(memory_space=pltpu.VMEM),
    )(rel_bias, bucket)


def _rel_bias_grad(dbias, bucket):
    def body(d_ref, b_ref, o_ref):
        o_ref[...] = jnp.zeros_like(o_ref)
        lane = lax.broadcasted_iota(jnp.int32, (1, LANES), 1)

        def per_bucket(k, carry):
            hit = b_ref[...] == k
            for h in range(N_HEADS):
                part = jnp.sum(jnp.where(hit, d_ref[h], 0.0), axis=1, keepdims=True)
                o_ref[h:h + 1, :] += jnp.where(lane == k, jnp.sum(part, axis=0, keepdims=True), 0.0)
            return carry

        lax.fori_loop(0, REL_BUCKETS, per_bucket, 0)

    return pl.pallas_call(
        body, name="rel_bias_grad", out_shape=jax.ShapeDtypeStruct((N_HEADS, LANES), F32),
        compiler_params=pltpu.CompilerParams(vmem_limit_bytes=VMEM_LIMIT_BYTES),
    )(dbias, bucket)


N_PEER = N_DEV - 1


def _exchange_copies(ins, outs, send_sems, recv_sems, local_sems, *, gather):
    x, y, c = lax.axis_index("x"), lax.axis_index("y"), lax.axis_index("c")
    me = 4 * x + 2 * y + c
    peers = []
    for k in range(1, N_DEV):
        px, py, pc = x ^ ((k >> 2) & 1), y ^ ((k >> 1) & 1), c ^ (k & 1)
        peers.append(((px, py, pc), 4 * px + 2 * py + pc))
    local, sends, recvs = [], [], []
    for i in range(len(ins)):
        mine = ins[i] if gather else ins[i].at[me]
        local.append(pltpu.make_async_copy(mine, outs[i].at[me], local_sems.at[i]))
        for k, (pid, pslot) in enumerate(peers):
            src = ins[i] if gather else ins[i].at[pslot]
            sem = i * N_PEER + k
            sends.append(pltpu.make_async_remote_copy(
                src_ref=src, dst_ref=outs[i].at[me], send_sem=send_sems.at[sem], recv_sem=recv_sems.at[sem],
                device_id=pid, device_id_type=pl.DeviceIdType.MESH))
            recvs.append(pltpu.make_async_remote_copy(
                src_ref=src, dst_ref=outs[i].at[pslot], send_sem=send_sems.at[sem], recv_sem=recv_sems.at[sem],
                device_id=pid, device_id_type=pl.DeviceIdType.MESH))
    return local, sends, recvs


def _exchange_start(copies):
    local, sends, _ = copies
    for cp in local + sends:
        cp.start()


def _exchange_wait(copies):
    local, sends, recvs = copies
    for cp in recvs:
        cp.wait_recv()
    for cp in sends:
        cp.wait_send()
    for cp in local:
        cp.wait()


def _exchange_out_shapes(arrs, *, gather):
    return [jax.ShapeDtypeStruct((N_DEV,) + (a.shape if gather else a.shape[1:]), a.dtype) for a in arrs]


def _exchange_sems(n):
    if not n:
        return []
    return [pltpu.SemaphoreType.DMA((n * N_PEER,)), pltpu.SemaphoreType.DMA((n * N_PEER,)), pltpu.SemaphoreType.DMA((n,))]


def _exchange(arrs, *, gather, name):
    n = len(arrs)

    def body(*refs):
        copies = _exchange_copies(refs[:n], refs[n:2 * n], *refs[2 * n:], gather=gather)
        _exchange_start(copies)
        _exchange_wait(copies)

    any_spec = pl.BlockSpec(memory_space=pl.ANY)
    return pl.pallas_call(
        body, name=name, in_specs=[any_spec] * n, out_specs=[any_spec] * n,
        out_shape=_exchange_out_shapes(arrs, gather=gather), scratch_shapes=_exchange_sems(n),
        compiler_params=pltpu.CompilerParams(has_side_effects=True),
    )(*arrs)


def _adamw(parts, w, m, v, *, name, tr):
    r, c = w.shape
    nparts = len(parts)
    rows = r // nparts
    assert rows * nparts == r and rows % tr == 0 and all(p.shape == (N_DEV, rows, c) for p in parts)
    per = rows // tr
    c1 = 1.0 - ADAM_B1 ** ADAM_STEP
    c2 = 1.0 - ADAM_B2 ** ADAM_STEP

    def body(*refs):
        p_refs = refs[:nparts]
        w_ref, m_ref, v_ref, g_ref, d_ref, nm_ref, nv_ref = refs[nparts:]
        which = pl.program_id(0) // per
        for k, p_ref in enumerate(p_refs):
            @pl.when(which == k)
            def _(p_ref=p_ref):
                acc = p_ref[0].astype(F32)
                for j in range(1, N_DEV):
                    acc = acc + p_ref[j].astype(F32)
                g_ref[...] = acc

        g = g_ref[...]
        nm = ADAM_B1 * m_ref[...] + (1.0 - ADAM_B1) * g
        nv = ADAM_B2 * v_ref[...] + (1.0 - ADAM_B2) * (g * g)
        nm_ref[...] = nm
        nv_ref[...] = nv
        d_ref[...] = -ADAM_LR * ((nm / c1) / (jnp.sqrt(nv / c2) + ADAM_EPS) + ADAM_WD * w_ref[...])

    def part_spec(k):
        return pl.BlockSpec((N_DEV, tr, c), lambda i: (0, jnp.clip(i - k * per, 0, per - 1), 0))

    blk = pl.BlockSpec((tr, c), lambda i: (i, 0))
    return pl.pallas_call(
        body, name=name, grid=(r // tr,),
        in_specs=[part_spec(k) for k in range(nparts)] + [blk, blk, blk],
        out_specs=[blk] * 4, out_shape=[jax.ShapeDtypeStruct((r, c), F32)] * 4,
        compiler_params=_params("arbitrary"),
    )(*parts, w, m, v)


def _t5_bucket(rel):
    half = REL_BUCKETS // 2
    max_exact = half // 2
    ret = jnp.where(rel > 0, half, 0)
    n = jnp.abs(rel)
    nf = jnp.maximum(n, 1).astype(F32)
    large = max_exact + (jnp.log(nf / max_exact) / math.log(CHUNK / max_exact) * (half - max_exact)).astype(jnp.int32)
    large = jnp.minimum(large, half - 1)
    return ret + jnp.where(n < max_exact, n, large)


def _split16(w):
    hi = w.astype(BF16)
    return hi, (w - hi.astype(F32)).astype(BF16)


def _cols_to_blocks(g, depth):
    _, r, c8 = g.shape
    return g.reshape(depth, r, N_DEV, c8 // N_DEV).transpose(2, 0, 1, 3).reshape(N_DEV, depth * r, c8 // N_DEV)


def _rows_to_blocks(g, depth):
    _, r8, c = g.shape
    return g.reshape(depth, N_DEV, r8 // N_DEV, c).transpose(1, 0, 2, 3).reshape(N_DEV, depth * r8 // N_DEV, c)


def _blocks_to_cols(a, depth):
    _, dr, c = a.shape
    r = dr // depth
    return a.reshape(N_DEV, depth, r, c).transpose(1, 2, 0, 3).reshape(depth, r, N_DEV * c)


def _blocks_to_rows(a, depth):
    _, dr, c = a.shape
    r = dr // depth
    return a.reshape(N_DEV, depth, r, c).transpose(1, 0, 2, 3).reshape(depth, N_DEV * r, c)


_SMALL = ("rel_bias", "norm1_w", "conv_b", "dt_bias", "a_log", "d_skip", "ssm_norm_w", "attn_sink", "norm2_w",
          "ffn_conv_b", "final_norm_w")
_SHARDED = ("w_in", "conv_w", "w_out", "w_up", "ffn_conv_w", "w_down")
_ORDER = ("rel_bias", "norm1_w", "w_in", "conv_w", "conv_b", "dt_bias", "a_log", "d_skip", "ssm_norm_w", "attn_sink",
          "w_out", "norm2_w", "w_up", "ffn_conv_w", "ffn_conv_b", "w_down", "final_norm_w")


def _pack_small(d):
    flat = jnp.concatenate([d[k].reshape(-1).astype(F32) for k in _SMALL])
    rows = -(-flat.size // (LANES * SUBLANES)) * SUBLANES
    return jnp.pad(flat, (0, rows * LANES - flat.size)).reshape(rows, LANES)


def _unpack_small(packed, like):
    flat = packed.reshape(-1)
    out, off = {}, 0
    for k in _SMALL:
        out[k] = flat[off:off + like[k].size].reshape(like[k].shape)
        off += like[k].size
    return out


def kernel(x, rel_bias, norm1_w, w_in, conv_w, conv_b, dt_bias, a_log, d_skip, ssm_norm_w, attn_sink, w_out, norm2_w, w_up, ffn_conv_w, ffn_conv_b, w_down, final_norm_w, loss_target, m_rel_bias, m_norm1_w, m_w_in, m_conv_w, m_conv_b, m_dt_bias, m_a_log, m_d_skip, m_ssm_norm_w, m_attn_sink, m_w_out, m_norm2_w, m_w_up, m_ffn_conv_w, m_ffn_conv_b, m_w_down, m_final_norm_w, v_rel_bias, v_norm1_w, v_w_in, v_conv_w, v_conv_b, v_dt_bias, v_a_log, v_d_skip, v_ssm_norm_w, v_attn_sink, v_w_out, v_norm2_w, v_w_up, v_ffn_conv_w, v_ffn_conv_b, v_w_down, v_final_norm_w):
    w = dict(rel_bias=rel_bias, norm1_w=norm1_w, w_in=w_in, conv_w=conv_w, conv_b=conv_b, dt_bias=dt_bias, a_log=a_log,
             d_skip=d_skip, ssm_norm_w=ssm_norm_w, attn_sink=attn_sink, w_out=w_out, norm2_w=norm2_w, w_up=w_up,
             ffn_conv_w=ffn_conv_w, ffn_conv_b=ffn_conv_b, w_down=w_down, final_norm_w=final_norm_w)
    m = dict(rel_bias=m_rel_bias, norm1_w=m_norm1_w, w_in=m_w_in, conv_w=m_conv_w, conv_b=m_conv_b, dt_bias=m_dt_bias,
             a_log=m_a_log, d_skip=m_d_skip, ssm_norm_w=m_ssm_norm_w, attn_sink=m_attn_sink, w_out=m_w_out,
             norm2_w=m_norm2_w, w_up=m_w_up, ffn_conv_w=m_ffn_conv_w, ffn_conv_b=m_ffn_conv_b, w_down=m_w_down,
             final_norm_w=m_final_norm_w)
    v = dict(rel_bias=v_rel_bias, norm1_w=v_norm1_w, w_in=v_w_in, conv_w=v_conv_w, conv_b=v_conv_b, dt_bias=v_dt_bias,
             a_log=v_a_log, d_skip=v_d_skip, ssm_norm_w=v_ssm_norm_w, attn_sink=v_attn_sink, w_out=v_w_out,
             norm2_w=v_norm2_w, w_up=v_w_up, ffn_conv_w=v_ffn_conv_w, ffn_conv_b=v_ffn_conv_b, w_down=v_w_down,
             final_norm_w=v_final_norm_w)
    nb, seq, _ = x.shape
    t = nb * seq
    depth = w_in.shape[0]

    flat2 = lambda a: a.reshape(-1, a.shape[-1])
    own_shards = lambda i: [w_out[i].astype(BF16), w_up[i].astype(BF16), w_down[i].astype(BF16)]
    cw_hi, cw_lo = _split16(flat2(conv_w))
    fw_hi, fw_lo = _split16(flat2(ffn_conv_w))
    g_in, g_cwh, g_cwl, g_fwh, g_fwl = _exchange([w_in[0].astype(BF16), cw_hi, cw_lo, fw_hi, fw_lo], gather=True,
                                                 name="gather_weights")
    full_conv_w = _blocks_to_cols(g_cwh.astype(F32) + g_cwl.astype(F32), depth)
    full_ffn_cw = _blocks_to_cols(g_fwh.astype(F32) + g_fwl.astype(F32), depth)

    rel = jnp.arange(KEY_SPAN)[None, :] - CHUNK - jnp.arange(CHUNK)[:, None]
    bucket = _t5_bucket(rel)
    band_bias = _band_bias(rel_bias, bucket)

    def layer_params(i, g_in):
        return dict(n1=norm1_w[i][None], w_in=_to_proj_layout(_blocks_to_cols(g_in, 1)[0]), conv_w=full_conv_w[i],
                    conv_b=conv_b[i][None], dtb=_pad_lanes(dt_bias[i].reshape(-1)), alog=_pad_lanes(a_log[i].reshape(-1)),
                    dvec=jnp.repeat(d_skip[i], HEAD_DIM)[None], ssm_nw=ssm_norm_w[i][None], sink=_pad_lanes(attn_sink[i]),
                    n2=norm2_w[i][None], ffn_cw=full_ffn_cw[i], ffn_cb=ffn_conv_b[i][None])

    h = x.reshape(t, D_MODEL)
    params, saved = [None] * depth, [None] * depth
    for i in range(depth):
        h, params[i], saved[i], nxt = _layer_fwd(h, layer_params(i, g_in), band_bias, nb=nb, seq=seq, own_shards=own_shards(i),
                                                 next_shards=[w_in[i + 1].astype(BF16)] if i + 1 < depth else ())
        if nxt:
            (g_in,) = nxt
    dh, g_final, loss_part = _loss_head(h, loss_target.reshape(t, D_MODEL), final_norm_w[None], tm=512)
    loss = lax.psum(loss_part[0, 0], ("x", "y", "c"))

    w_in_blocks = lambda g: _cols_to_blocks(_from_proj_layout(g["w_in"])[None], 1).astype(BF16)
    dbias = jnp.zeros((N_HEADS, CHUNK, KEY_SPAN), F32)
    grads, pending = [None] * depth, ()
    parts = dict(w_in=[None] * depth, w_out=[None] * depth, w_up=[None] * depth, w_down=[None] * depth)
    for i in reversed(range(depth)):
        dh, dbias, grads[i], arrived = _layer_bwd(dh, params[i], saved[i], band_bias, dbias, nb=nb, seq=seq, pending=pending)
        parts["w_down"][i], parts["w_up"][i], parts["w_out"][i] = arrived[:3]
        if pending:
            parts["w_in"][i + 1] = arrived[3]
        pending = [w_in_blocks(grads[i])]
    grad_x = dh.reshape(nb, seq, D_MODEL)
    stack = lambda k: jnp.stack([g[k] for g in grads])
    last = _exchange(pending + [_cols_to_blocks(stack("conv_w"), depth).astype(BF16),
                                _cols_to_blocks(stack("ffn_cw"), depth).astype(BF16)], gather=False, name="scatter_grads")
    parts["w_in"][0] = last[0]

    out = {}
    for k in ("w_in", "w_out", "w_up", "w_down"):
        rows = parts[k][0].shape[1]
        tr = max(d for d in range(16, 129, 16) if rows % d == 0)
        res = _adamw(parts[k], flat2(w[k]), flat2(m[k]), flat2(v[k]), name="adamw_" + k, tr=tr)
        out[k] = [a.reshape(w[k].shape) for a in res]
    for k, p8 in zip(("conv_w", "ffn_conv_w"), last[1:]):
        res = _adamw([p8], flat2(w[k]), flat2(m[k]), flat2(v[k]), name="adamw_" + k, tr=p8.shape[1])
        out[k] = [a.reshape(w[k].shape) for a in res]

    small = dict(rel_bias=_rel_bias_grad(dbias, bucket)[:, :REL_BUCKETS].T, norm1_w=stack("n1"), conv_b=stack("conv_b"),
                 dt_bias=stack("dtb")[:, 0, :2 * N_HEADS], a_log=stack("alog")[:, 0, :2 * N_HEADS],
                 d_skip=stack("dvec").reshape(depth, N_HEADS, HEAD_DIM).sum(-1), ssm_norm_w=stack("ssm_nw"),
                 attn_sink=stack("sink")[:, 0, :N_HEADS], norm2_w=stack("n2"), ffn_conv_b=stack("ffn_cb"),
                 final_norm_w=g_final)
    (small_parts,) = _exchange([_pack_small(small)], gather=True, name="gather_small_grads")
    res = _adamw([small_parts], _pack_small(w), _pack_small(m), _pack_small(v), name="adamw_small", tr=small_parts.shape[1])
    unpacked = [_unpack_small(a, w) for a in res]
    for k in _SMALL:
        out[k] = [u[k] for u in unpacked]

    return (loss, grad_x, *[out[k][0] for k in _ORDER], *[out[k][1] for k in _ORDER],
            *[out[k][2] for k in _ORDER], *[out[k][3] for k in _ORDER])
```

```python
import math

import numpy as np
import jax
import jax.numpy as jnp
from jax import lax
from jax.experimental import pallas as pl
from jax.experimental.pallas import tpu as pltpu

F32, BF16 = jnp.float32, jnp.bfloat16
HIGHEST = lax.Precision.HIGHEST

D_MODEL = 1024
HEAD_DIM = 64
N_HEADS = 16
N_GROUPS = 2
HEADS_PER_GROUP = N_HEADS // N_GROUPS
N_STATE = 128
SSM_WIDTH = 1024
BC_WIDTH = 256
CONV_CH = SSM_WIDTH + 2 * BC_WIDTH
SSM_TAPS = 7
CHUNK = 128
KV_HEADS = 4
KV_WIDTH = 256
Q_PER_KV = N_HEADS // KV_HEADS
KEY_SPAN = 3 * CHUNK
REL_BUCKETS = 32
D_FF = 2816
FFN_TAPS = 3
IN_COLS = 4128
NORM_EPS = 1e-6
N_DEV = 8

LANES = 128
SUBLANES = 8
VMEM_LIMIT_BYTES = 56 * 1024 * 1024
MM_ROWS = 1024

PZ, PXS, PQ, PB, PC, PK, PV, PDT, PROJ_W = 0, 1024, 2048, 3072, 3328, 3584, 3840, 4096, 4224
OZ, OXBC, ODT, OQ, OK_, OV = 0, 1024, 2560, 2592, 3616, 3872

ADAM_LR, ADAM_B1, ADAM_B2, ADAM_EPS, ADAM_WD, ADAM_STEP = 0.001, 0.9, 0.999, 1e-08, 0.01, 10


def _params(*sem):
    return pltpu.CompilerParams(dimension_semantics=sem, vmem_limit_bytes=VMEM_LIMIT_BYTES)


def _sigmoid(x):
    return 0.5 * jnp.tanh(0.5 * x) + 0.5


def _softplus(x):
    return jnp.maximum(x, 0.0) + jnp.log(1.0 + jnp.exp(-jnp.abs(x)))


def _dot(a, b, dims):
    return lax.dot_general(a, b, (dims, ((), ())), preferred_element_type=F32)


NN = ((1,), (0,))
NT = ((1,), (1,))
TN = ((0,), (0,))


def _matmul(a, b, mode, *, name, tm, tn, tk, res=None, out_dtype=F32, precision=None, b_k0=0):
    assert b_k0 % tk == 0
    ko = b_k0 // tk
    if mode == "nn":
        (m, k), n = a.shape, b.shape[1]
        k2 = k if b.shape[0] >= b_k0 + k else -1
        a_spec = pl.BlockSpec((tm, tk), lambda i, j, kk: (i, kk))
        b_spec = pl.BlockSpec((tk, tn), lambda i, j, kk: (kk + ko, j))
        dims = NN
    elif mode == "nt":
        (m, k), n = a.shape, b.shape[0]
        k2 = k if b.shape[1] >= b_k0 + k else -1
        a_spec = pl.BlockSpec((tm, tk), lambda i, j, kk: (i, kk))
        b_spec = pl.BlockSpec((tn, tk), lambda i, j, kk: (j, kk + ko))
        dims = NT
    else:
        (k, m), (k2, n) = a.shape, b.shape
        a_spec = pl.BlockSpec((tk, tm), lambda i, j, kk: (kk, i))
        b_spec = pl.BlockSpec((tk, tn), lambda i, j, kk: (kk, j))
        dims = TN
    assert k == k2 and m % tm == 0 and n % tn == 0 and k % tk == 0, (name, a.shape, b.shape, tm, tn, tk)
    nk = k // tk
    has_res = res is not None

    def body(*refs):
        a_ref, b_ref = refs[:2]
        r_ref = refs[2] if has_res else None
        o_ref = refs[3] if has_res else refs[2]
        acc = refs[-1] if nk > 1 else None
        kk = pl.program_id(2)
        if precision is None:
            part = _dot(a_ref[...].astype(BF16), b_ref[...].astype(BF16), dims)
        else:
            part = lax.dot_general(a_ref[...], b_ref[...], (dims, ((), ())), precision=precision,
                                   preferred_element_type=F32)

        def finish(r):
            if has_res:
                r = r + r_ref[...].astype(F32)
            o_ref[...] = r.astype(out_dtype)

        if nk == 1:
            finish(part)
        else:
            @pl.when(kk == 0)
            def _():
                acc[...] = jnp.zeros_like(acc)

            acc[...] += part

            @pl.when(kk == nk - 1)
            def _():
                finish(acc[...])

    in_specs = [a_spec, b_spec]
    args = [a, b]
    if has_res:
        in_specs.append(pl.BlockSpec((tm, tn), lambda i, j, kk: (i, j)))
        args.append(res)
    return pl.pallas_call(
        body, name=name, grid=(m // tm, n // tn, nk),
        in_specs=in_specs, out_specs=pl.BlockSpec((tm, tn), lambda i, j, kk: (i, j)),
        out_shape=jax.ShapeDtypeStruct((m, n), out_dtype),
        scratch_shapes=[pltpu.VMEM((tm, tn), F32)] if nk > 1 else [],
        compiler_params=_params("parallel", "parallel", "arbitrary"),
    )(*args)


def _rms_matmul(x, nw, w, *, name, tm, tn, ncols=None):
    t, d = x.shape
    n = w.shape[1] if ncols is None else ncols
    assert t % tm == 0 and n % tn == 0

    def body(x_ref, nw_ref, w_ref, o_ref, h_ref):
        @pl.when(pl.program_id(1) == 0)
        def _():
            xv = x_ref[...]
            r = lax.rsqrt(jnp.mean(xv * xv, axis=-1, keepdims=True) + NORM_EPS)
            h_ref[...] = (xv * r * nw_ref[...]).astype(BF16)

        o_ref[...] = _dot(h_ref[...], w_ref[...].astype(BF16), NN).astype(BF16)

    return pl.pallas_call(
        body, name=name, grid=(t // tm, n // tn),
        in_specs=[pl.BlockSpec((tm, d), lambda i, j: (i, 0)),
                  pl.BlockSpec((1, d), lambda i, j: (0, 0)),
                  pl.BlockSpec((d, tn), lambda i, j: (0, j))],
        out_specs=[pl.BlockSpec((tm, tn), lambda i, j: (i, j)),
                   pl.BlockSpec((tm, d), lambda i, j: (i, 0))],
        out_shape=[jax.ShapeDtypeStruct((t, n), BF16), jax.ShapeDtypeStruct((t, d), BF16)],
        compiler_params=_params("parallel", "arbitrary"),
    )(x, nw, w)


def _zero_ext(v):
    z = jnp.zeros((SUBLANES, v.shape[1]), v.dtype)
    return jnp.concatenate([z, v, z], axis=0)


def _shifted(v_ext, offset, seq):
    if offset == 0:
        return v_ext[SUBLANES:SUBLANES + seq]
    return pltpu.roll(v_ext, (-offset) % (seq + 2 * SUBLANES), 0)[SUBLANES:SUBLANES + seq]


def _conv_taps(v, w_ref, taps, seq):
    pad = taps // 2
    v_ext = _zero_ext(v)
    acc = None
    for k in range(taps):
        term = _shifted(v_ext, k - pad, seq) * w_ref[k:k + 1, :]
        acc = term if acc is None else acc + term
    return acc


def _ssm_conv_fwd(proj, cw, cb, *, nb, seq):
    width = 512

    def body(x_ref, w_ref, b_ref, o_ref, g_ref):
        g = _conv_taps(x_ref[...].astype(F32), w_ref, SSM_TAPS, seq) + b_ref[...]
        o_ref[...] = g * _sigmoid(g)
        g_ref[...] = g.astype(BF16)

    def col(j):
        return jnp.where(j < 2, j + PXS // width, PB // width)

    return pl.pallas_call(
        body, name="ssm_conv_fwd", grid=(nb, CONV_CH // width),
        in_specs=[pl.BlockSpec((seq, width), lambda b, j: (b, col(j))),
                  pl.BlockSpec((SSM_TAPS, width), lambda b, j: (0, j)),
                  pl.BlockSpec((1, width), lambda b, j: (0, j))],
        out_specs=[pl.BlockSpec((seq, width), lambda b, j: (b, j))] * 2,
        out_shape=[jax.ShapeDtypeStruct((nb * seq, CONV_CH), F32), jax.ShapeDtypeStruct((nb * seq, CONV_CH), BF16)],
        compiler_params=_params("parallel", "parallel"),
    )(proj, cw, cb)


def _ffn_act_fwd(gu, cw, cb, *, nb, seq):
    width = 256
    nj = D_FF // width

    def body(g_ref, u_ref, w_ref, b_ref, o_ref, s_ref):
        g = _conv_taps(g_ref[...].astype(F32), w_ref, FFN_TAPS, seq) + b_ref[...]
        o_ref[...] = (g * _sigmoid(g) * u_ref[...].astype(F32)).astype(BF16)
        s_ref[...] = g.astype(BF16)

    return pl.pallas_call(
        body, name="ffn_act_fwd", grid=(nb, nj),
        in_specs=[pl.BlockSpec((seq, width), lambda b, j: (b, j)),
                  pl.BlockSpec((seq, width), lambda b, j: (b, j + nj)),
                  pl.BlockSpec((FFN_TAPS, width), lambda b, j: (0, j)),
                  pl.BlockSpec((1, width), lambda b, j: (0, j))],
        out_specs=[pl.BlockSpec((seq, width), lambda b, j: (b, j))] * 2,
        out_shape=[jax.ShapeDtypeStruct((nb * seq, D_FF), BF16)] * 2,
        compiler_params=_params("parallel", "parallel"),
    )(gu, gu, cw, cb)


def _scan_setup(d, dt_ref, dtb_ref, alog_ref, z_ref, zt_ref, *, lower_when_dir0, inclusive):
    is0 = d == 0
    dt_all = _softplus(dt_ref[...] + dtb_ref[...])
    adt_all = dt_all * (-jnp.exp(alog_ref[...]))
    li = lax.broadcasted_iota(jnp.int32, (CHUNK, CHUNK), 0)
    si = lax.broadcasted_iota(jnp.int32, (CHUNK, CHUNK), 1)
    lower = is0 if lower_when_dir0 else jnp.logical_not(is0)
    ahead = jnp.where(lower, li - si, si - li)
    mask = ahead >= 0
    tri = mask if inclusive else ahead > 0
    z_all = jnp.dot(tri.astype(F32), adt_all, precision=HIGHEST, preferred_element_type=F32)
    zt_all = z_all.T
    z_ref[...] = jnp.where(is0, z_all[:, 0:N_HEADS], z_all[:, N_HEADS:2 * N_HEADS])
    zt_ref[...] = jnp.where(is0, zt_all[0:N_HEADS, :], zt_all[N_HEADS:2 * N_HEADS, :])
    dt = jnp.where(is0, dt_all[:, 0:N_HEADS], dt_all[:, N_HEADS:2 * N_HEADS])
    adt = jnp.where(is0, adt_all[:, 0:N_HEADS], adt_all[:, N_HEADS:2 * N_HEADS])
    tot = jnp.sum(adt, axis=0, keepdims=True)
    return ahead, dt, tot, dt_all


def _chunk_index(nchunk, forward_when_dir0):
    def idx(d, b, c):
        fwd = (d == 0) if forward_when_dir0 else (d != 0)
        return b * nchunk + jnp.where(fwd, c, nchunk - 1 - c)
    return idx


def _ssd_fwd(xbc, dtraw, dtb, alog, *, nb, seq, gather=()):
    nchunk = seq // CHUNK
    t = nb * seq
    row = _chunk_index(nchunk, True)
    ng = len(gather)

    def body(*refs):
        xs_ref, bc_ref, dt_ref, dtb_ref, alog_ref = refs[:5]
        o_ref, hs_ref = refs[5 + ng:7 + ng]
        h_ref, z_ref, zt_ref, dts_ref = refs[7 + 2 * ng:11 + 2 * ng]
        d, c = pl.program_id(0), pl.program_id(2)
        if ng:
            step = (d * nb + pl.program_id(1)) * nchunk + c
            copies = _exchange_copies(refs[5:5 + ng], refs[7 + ng:7 + 2 * ng], *refs[11 + 2 * ng:], gather=True)
            pl.when(step == 0)(lambda: _exchange_start(copies))

        @pl.when(c == 0)
        def _():
            h_ref[...] = jnp.zeros_like(h_ref)

        ahead, dt, tot, _ = _scan_setup(d, dt_ref, dtb_ref, alog_ref, z_ref, zt_ref,
                                        lower_when_dir0=True, inclusive=True)
        mask = ahead >= 0
        dts_ref[...] = dt
        e_tot = jnp.exp(tot)
        for g in range(N_GROUPS):
            heads = range(g * HEADS_PER_GROUP, (g + 1) * HEADS_PER_GROUP)
            bg = bc_ref[:, g * N_STATE:(g + 1) * N_STATE]
            cg = bc_ref[:, BC_WIDTH + g * N_STATE:BC_WIDTH + (g + 1) * N_STATE]
            cb = _dot(cg.astype(BF16), bg.astype(BF16), NT)
            zc = {h: jnp.broadcast_to(z_ref[:, h:h + 1], (CHUNK, CHUNK)) for h in heads}
            decay = {h: jnp.exp(jnp.where(mask, zc[h] - zt_ref[h:h + 1, :], -jnp.inf)) for h in heads}
            u = {h: (xs_ref[:, h * HEAD_DIM:(h + 1) * HEAD_DIM] * dts_ref[:, h:h + 1]).astype(BF16) for h in heads}
            state = {h: h_ref[h] for h in heads}
            for h in heads:
                hs_ref[0, 0, h] = state[h]
            mix = {h: (cb * decay[h]).astype(BF16) for h in heads}
            cz = {h: (cg * jnp.exp(zc[h])).astype(BF16) for h in heads}
            bw = {h: (bg * jnp.exp(tot[:, h:h + 1] - zc[h])).astype(BF16) for h in heads}
            y = {h: _dot(mix[h], u[h], NN) + _dot(cz[h], state[h].astype(BF16), NT) for h in heads}
            new = {h: state[h] * e_tot[:, h:h + 1] + _dot(u[h], bw[h], TN) for h in heads}
            for h in heads:
                h_ref[h] = new[h]
                o_ref[0, :, h * HEAD_DIM:(h + 1) * HEAD_DIM] = y[h].astype(BF16)
        if ng:
            pl.when(step == 2 * nb * nchunk - 1)(lambda: _exchange_wait(copies))

    any_spec = pl.BlockSpec(memory_space=pl.ANY)
    res = pl.pallas_call(
        body, name="ssd_fwd_gather" if ng else "ssd_fwd", grid=(2, nb, nchunk),
        in_specs=[pl.BlockSpec((CHUNK, SSM_WIDTH), lambda d, b, c: (row(d, b, c), 0)),
                  pl.BlockSpec((CHUNK, 2 * BC_WIDTH), lambda d, b, c: (row(d, b, c), SSM_WIDTH // (2 * BC_WIDTH))),
                  pl.BlockSpec((CHUNK, LANES), lambda d, b, c: (row(d, b, c), 0)),
                  pl.BlockSpec((1, LANES), lambda d, b, c: (0, 0)),
                  pl.BlockSpec((1, LANES), lambda d, b, c: (0, 0))] + [any_spec] * ng,
        out_specs=[pl.BlockSpec((1, CHUNK, SSM_WIDTH), lambda d, b, c: (d, row(d, b, c), 0)),
                   pl.BlockSpec((1, 1, N_HEADS, HEAD_DIM, N_STATE), lambda d, b, c: (d, row(d, b, c), 0, 0, 0))]
        + [any_spec] * ng,
        out_shape=[jax.ShapeDtypeStruct((2, t, SSM_WIDTH), BF16),
                   jax.ShapeDtypeStruct((2, nb * nchunk, N_HEADS, HEAD_DIM, N_STATE), F32)]
        + _exchange_out_shapes(gather, gather=True),
        scratch_shapes=_SSD_SCRATCH + _exchange_sems(ng),
        compiler_params=_params("arbitrary", "arbitrary", "arbitrary"),
    )(xbc, xbc, dtraw, dtb, alog, *gather)
    return res[0], res[1], list(res[2:])


def _gate_norm_fwd(y2, xbc, proj, dvec, nw, *, tm):
    t = xbc.shape[0]
    half = SSM_WIDTH // N_GROUPS

    def body(y_ref, xs_ref, z_ref, d_ref, w_ref, o_ref):
        z = z_ref[...].astype(F32)
        p = (y_ref[0].astype(F32) + y_ref[1].astype(F32) + d_ref[...] * xs_ref[...]) * (z * _sigmoid(z))
        for g in range(N_GROUPS):
            pg = p[:, g * half:(g + 1) * half]
            r = lax.rsqrt(jnp.mean(pg * pg, axis=-1, keepdims=True) + NORM_EPS)
            o_ref[:, g * half:(g + 1) * half] = (pg * r * w_ref[:, g * half:(g + 1) * half]).astype(BF16)

    return pl.pallas_call(
        body, name="gate_norm_fwd", grid=(t // tm,),
        in_specs=[pl.BlockSpec((2, tm, SSM_WIDTH), lambda i: (0, i, 0)),
                  pl.BlockSpec((tm, SSM_WIDTH), lambda i: (i, 0)),
                  pl.BlockSpec((tm, SSM_WIDTH), lambda i: (i, PZ // SSM_WIDTH)),
                  pl.BlockSpec((1, SSM_WIDTH), lambda i: (0, 0)),
                  pl.BlockSpec((1, SSM_WIDTH), lambda i: (0, 0))],
        out_specs=pl.BlockSpec((tm, SSM_WIDTH), lambda i: (i, 0)),
        out_shape=jax.ShapeDtypeStruct((t, SSM_WIDTH), BF16),
        compiler_params=_params("parallel"),
    )(y2, xbc, proj, dvec, nw)


GROUP_ROWS = Q_PER_KV * CHUNK


def _keys_inside(n, nblk):
    kpos = (n - 1) * CHUNK + lax.broadcasted_iota(jnp.int32, (1, KEY_SPAN), 1)
    return (kpos >= 0) & (kpos < nblk * CHUNK)


def _per_head_column(ref, g):
    blk = lax.broadcasted_iota(jnp.int32, (GROUP_ROWS, 1), 0) // CHUNK
    col = jnp.zeros((GROUP_ROWS, 1), F32)
    for r in range(Q_PER_KV):
        h = g * Q_PER_KV + r
        col = jnp.where(blk == r, ref[:, h:h + 1], col)
    return col


def _stack_heads(ref, g, dtype):
    return jnp.concatenate([ref[:, (g * Q_PER_KV + r) * HEAD_DIM:(g * Q_PER_KV + r + 1) * HEAD_DIM].astype(dtype)
                            for r in range(Q_PER_KV)], axis=0)


def _kv_specs(nblk):
    kvb = PK // (2 * KV_WIDTH)

    def at(off):
        def idx(b, n):
            return (b * nblk + jnp.clip(n + off, 0, nblk - 1), kvb)
        return pl.BlockSpec((CHUNK, 2 * KV_WIDTH), idx)
    return [at(-1), at(0), at(1)]


def _attn_fwd(proj, bias, sink, *, nb, seq, gather=()):
    nblk = seq // CHUNK
    t = nb * seq
    scale = HEAD_DIM ** -0.5
    ng = len(gather)

    def body(*refs):
        q_ref, kp_ref, kc_ref, kn_ref, bias_ref, sink_ref = refs[:6]
        o_ref, lse_ref = refs[6 + ng:8 + ng]
        n = pl.program_id(1)
        if ng:
            step = pl.program_id(0) * nblk + n
            copies = _exchange_copies(refs[6:6 + ng], refs[8 + ng:8 + 2 * ng], *refs[8 + 2 * ng:], gather=True)
            pl.when(step == 0)(lambda: _exchange_start(copies))
        inside = _keys_inside(n, nblk)
        groups = range(KV_HEADS)

        def keys(g, off):
            cs = slice(off + g * HEAD_DIM, off + (g + 1) * HEAD_DIM)
            return jnp.concatenate([kp_ref[:, cs], kc_ref[:, cs], kn_ref[:, cs]], axis=0).astype(BF16)

        qs = [(_stack_heads(q_ref, g, F32) * scale).astype(BF16) for g in groups]
        ss = [jnp.where(inside, _dot(qs[g], keys(g, 0), NT)
                        + bias_ref[g * Q_PER_KV:(g + 1) * Q_PER_KV].reshape(GROUP_ROWS, KEY_SPAN), -jnp.inf) for g in groups]
        sks = [_per_head_column(sink_ref, g) for g in groups]
        ms = [jnp.maximum(jnp.max(ss[g], axis=-1, keepdims=True), sks[g]) for g in groups]
        ps = [jnp.exp(ss[g] - ms[g]) for g in groups]
        denoms = [jnp.sum(ps[g], axis=-1, keepdims=True) + jnp.exp(sks[g] - ms[g]) for g in groups]
        outs = [(_dot(ps[g].astype(BF16), keys(g, KV_WIDTH), NN) * (1.0 / denoms[g])).astype(BF16) for g in groups]
        lses = []
        for g in groups:
            lse = ms[g] + jnp.log(denoms[g])
            for r in range(Q_PER_KV):
                h = g * Q_PER_KV + r
                o_ref[:, h * HEAD_DIM:(h + 1) * HEAD_DIM] = outs[g][r * CHUNK:(r + 1) * CHUNK]
                lses.append(lse[r * CHUNK:(r + 1) * CHUNK])
        lse_ref[...] = jnp.concatenate(lses, axis=1)
        if ng:
            pl.when(step == nb * nblk - 1)(lambda: _exchange_wait(copies))

    any_spec = pl.BlockSpec(memory_space=pl.ANY)
    res = pl.pallas_call(
        body, name="attn_fwd_gather" if ng else "attn_fwd", grid=(nb, nblk),
        in_specs=[pl.BlockSpec((CHUNK, D_MODEL), lambda b, n: (b * nblk + n, PQ // D_MODEL))] + _kv_specs(nblk) + [
            pl.BlockSpec((N_HEADS, CHUNK, KEY_SPAN), lambda b, n: (0, 0, 0)),
            pl.BlockSpec((1, LANES), lambda b, n: (0, 0))] + [any_spec] * ng,
        out_specs=[pl.BlockSpec((CHUNK, D_MODEL), lambda b, n: (b * nblk + n, 0)),
                   pl.BlockSpec((CHUNK, N_HEADS), lambda b, n: (b * nblk + n, 0))] + [any_spec] * ng,
        out_shape=[jax.ShapeDtypeStruct((t, D_MODEL), BF16), jax.ShapeDtypeStruct((t, N_HEADS), F32)]
        + _exchange_out_shapes(gather, gather=True),
        scratch_shapes=_exchange_sems(ng),
        compiler_params=_params("arbitrary", "arbitrary"),
    )(proj, proj, proj, proj, bias, sink, *gather)
    return res[0], res[1], list(res[2:])


def _loss_head(x, tgt, nw, *, tm):
    t, d = x.shape

    def body(x_ref, t_ref, w_ref, dx_ref, dw_ref, l_ref):
        @pl.when(pl.program_id(0) == 0)
        def _():
            dw_ref[...] = jnp.zeros_like(dw_ref)
            l_ref[...] = jnp.zeros_like(l_ref)

        xv = x_ref[...]
        w = w_ref[...]
        r = lax.rsqrt(jnp.mean(xv * xv, axis=-1, keepdims=True) + NORM_EPS)
        xh = xv * r
        err = xh * w - t_ref[...]
        l_ref[...] += jnp.sum(err * err) * (0.5 / d)
        dy = err * (1.0 / d)
        gw = dy * w
        dx_ref[...] = r * (gw - xh * jnp.mean(gw * xh, axis=-1, keepdims=True))
        dw_ref[...] += jnp.sum(dy * xh, axis=0, keepdims=True)

    return pl.pallas_call(
        body, name="loss_head", grid=(t // tm,),
        in_specs=[pl.BlockSpec((tm, d), lambda i: (i, 0)), pl.BlockSpec((tm, d), lambda i: (i, 0)),
                  pl.BlockSpec((1, d), lambda i: (0, 0))],
        out_specs=[pl.BlockSpec((tm, d), lambda i: (i, 0)), pl.BlockSpec((1, d), lambda i: (0, 0)),
                   pl.BlockSpec((1, LANES), lambda i: (0, 0))],
        out_shape=[jax.ShapeDtypeStruct((t, d), F32), jax.ShapeDtypeStruct((1, d), F32),
                   jax.ShapeDtypeStruct((1, LANES), F32)],
        compiler_params=_params("arbitrary"),
    )(x, tgt, nw)


def _to_proj_layout(w):
    pad = jnp.zeros(w.shape[:-1] + (PROJ_W - IN_COLS,), w.dtype)
    return jnp.concatenate([w[..., OZ:OXBC], w[..., OXBC:OXBC + SSM_WIDTH], w[..., OQ:OK_],
                            w[..., OXBC + SSM_WIDTH:ODT], w[..., OK_:IN_COLS], w[..., ODT:OQ], pad], axis=-1)


def _from_proj_layout(g):
    return jnp.concatenate([g[..., PZ:PZ + 2 * SSM_WIDTH], g[..., PB:PB + 2 * BC_WIDTH], g[..., PDT:PDT + 2 * N_HEADS],
                            g[..., PQ:PQ + D_MODEL], g[..., PK:PK + 2 * KV_WIDTH]], axis=-1)


def _pad_lanes(v):
    return jnp.pad(v.reshape(1, -1), ((0, 0), (0, LANES - v.size)))


def _layer_fwd(x, p, band_bias, *, nb, seq, own_shards, next_shards=()):
    proj, h1 = _rms_matmul(x, p["n1"], p["w_in"], name="in_proj", tm=MM_ROWS, tn=1024, ncols=PDT)
    dtraw = _matmul(h1, p["w_in"][:, PDT:], "nn", name="in_proj_dt", tm=MM_ROWS, tn=LANES, tk=D_MODEL)
    xbc, gconv = _ssm_conv_fwd(proj, p["conv_w"], p["conv_b"], nb=nb, seq=seq)
    y2, states, (g_out, g_up, g_down) = _ssd_fwd(xbc, dtraw, p["dtb"], p["alog"], nb=nb, seq=seq, gather=own_shards)
    p = dict(p, w_out=_blocks_to_rows(g_out, 1)[0], w_up=_blocks_to_cols(g_up, 1)[0], w_down=_blocks_to_rows(g_down, 1)[0])
    y_ssm = _gate_norm_fwd(y2, xbc, proj, p["dvec"], p["ssm_nw"], tm=256)
    y_att, lse, gathered = _attn_fwd(proj, band_bias, p["sink"], nb=nb, seq=seq, gather=next_shards)
    x1 = _matmul(y_ssm, p["w_out"], "nn", name="out_proj_ssm", tm=MM_ROWS, tn=1024, tk=1024, res=x)
    x1 = _matmul(y_att, p["w_out"], "nn", name="out_proj_att", tm=MM_ROWS, tn=1024, tk=1024, res=x1, b_k0=SSM_WIDTH)
    gu, h2 = _rms_matmul(x1, p["n2"], p["w_up"], name="up_proj", tm=MM_ROWS, tn=1408)
    act, fconv = _ffn_act_fwd(gu, p["ffn_cw"], p["ffn_cb"], nb=nb, seq=seq)
    x2 = _matmul(act, p["w_down"], "nn", name="down_proj", tm=MM_ROWS, tn=1024, tk=D_FF, res=x1)
    saved = dict(x=x, proj=proj, dtraw=dtraw, h1=h1, xbc=xbc, gconv=gconv, y2=y2, states=states, y_ssm=y_ssm, y_att=y_att,
                 lse=lse, x1=x1, gu=gu, fconv=fconv, h2=h2, act=act)
    return x2, p, saved, gathered


def _rms_bwd(x, dh, nw, dres, *, tm, name):
    t, d = x.shape

    def body(x_ref, dh_ref, w_ref, r_ref, dx_ref, dw_ref):
        @pl.when(pl.program_id(0) == 0)
        def _():
            dw_ref[...] = jnp.zeros_like(dw_ref)

        xv = x_ref[...]
        dh_v = dh_ref[...].astype(F32)
        r = lax.rsqrt(jnp.mean(xv * xv, axis=-1, keepdims=True) + NORM_EPS)
        xh = xv * r
        gw = dh_v * w_ref[...]
        dx_ref[...] = r_ref[...] + r * (gw - xh * jnp.mean(gw * xh, axis=-1, keepdims=True))
        dw_ref[...] += jnp.sum(dh_v * xh, axis=0, keepdims=True)

    row = pl.BlockSpec((tm, d), lambda i: (i, 0))
    vec = pl.BlockSpec((1, d), lambda i: (0, 0))
    return pl.pallas_call(
        body, name=name, grid=(t // tm,), in_specs=[row, row, vec, row], out_specs=[row, vec],
        out_shape=[jax.ShapeDtypeStruct((t, d), F32), jax.ShapeDtypeStruct((1, d), F32)],
        compiler_params=_params("arbitrary"),
    )(x, dh, nw, dres)


def _dsilu(g, sg):
    return sg * (1.0 + g * (1.0 - sg))


def _conv_taps_bwd(gpre, dg, w_ref, dwb_ref, taps, seq):
    pad = taps // 2
    dg_ext, gpre_ext = _zero_ext(dg), _zero_ext(gpre)
    dpre = None
    for k in range(taps):
        term = _shifted(dg_ext, pad - k, seq) * w_ref[k:k + 1, :]
        dpre = term if dpre is None else dpre + term
        dwb_ref[k:k + 1, :] += jnp.sum(dg * _shifted(gpre_ext, k - pad, seq), axis=0, keepdims=True)
    dwb_ref[SUBLANES - 1:SUBLANES, :] += jnp.sum(dg, axis=0, keepdims=True)
    return dpre


def _ffn_act_bwd(gu, gconv, dact, cw, *, nb, seq):
    width = 256
    nj = D_FF // width

    def body(g_ref, u_ref, s_ref, da_ref, w_ref, dg_ref, du_ref, dwb_ref):
        @pl.when(pl.program_id(1) == 0)
        def _():
            dwb_ref[...] = jnp.zeros_like(dwb_ref)

        g = s_ref[...].astype(F32)
        sg = _sigmoid(g)
        da = da_ref[...].astype(F32)
        du_ref[...] = (da * g * sg).astype(BF16)
        dgc = da * u_ref[...].astype(F32) * _dsilu(g, sg)
        dg_ref[...] = _conv_taps_bwd(g_ref[...].astype(F32), dgc, w_ref, dwb_ref, FFN_TAPS, seq).astype(BF16)

    blk = lambda off: pl.BlockSpec((seq, width), lambda j, b: (b, j + off))
    return pl.pallas_call(
        body, name="ffn_act_bwd", grid=(nj, nb),
        in_specs=[blk(0), blk(nj), blk(0), blk(0), pl.BlockSpec((FFN_TAPS, width), lambda j, b: (0, j))],
        out_specs=[blk(0), blk(0), pl.BlockSpec((SUBLANES, width), lambda j, b: (0, j))],
        out_shape=[jax.ShapeDtypeStruct((nb * seq, D_FF), BF16), jax.ShapeDtypeStruct((nb * seq, D_FF), BF16),
                   jax.ShapeDtypeStruct((SUBLANES, D_FF), F32)],
        compiler_params=_params("parallel", "arbitrary"),
    )(gu, gu, gconv, dact, cw)


def _ssm_conv_bwd(proj, gconv, pair, cw, *, nb, seq, name, width, proj_col, conv_col, ncol, extra=None, scale=None):
    has_extra = extra is not None

    def body(*refs):
        if has_extra:
            x_ref, g_ref, p_ref, w_ref, e_ref, s_ref, dx_ref, dwb_ref = refs
        else:
            x_ref, g_ref, p_ref, w_ref, dx_ref, dwb_ref = refs

        @pl.when(pl.program_id(1) == 0)
        def _():
            dwb_ref[...] = jnp.zeros_like(dwb_ref)

        g = g_ref[...].astype(F32)
        da = p_ref[0] + p_ref[1]
        if has_extra:
            da = da + e_ref[...] * s_ref[...]
        dx_ref[...] = _conv_taps_bwd(x_ref[...].astype(F32), da * _dsilu(g, _sigmoid(g)), w_ref, dwb_ref, SSM_TAPS,
                                     seq).astype(BF16)

    in_specs = [pl.BlockSpec((seq, width), lambda j, b: (b, j + proj_col)),
                pl.BlockSpec((seq, width), lambda j, b: (b, j + conv_col)),
                pl.BlockSpec((2, seq, width), lambda j, b: (0, b, j)),
                pl.BlockSpec((SSM_TAPS, width), lambda j, b: (0, j + conv_col))]
    args = [proj, gconv, pair, cw]
    if has_extra:
        in_specs += [pl.BlockSpec((seq, width), lambda j, b: (b, j)), pl.BlockSpec((1, width), lambda j, b: (0, j))]
        args += [extra, scale]
    return pl.pallas_call(
        body, name=name, grid=(ncol, nb), in_specs=in_specs,
        out_specs=[pl.BlockSpec((seq, width), lambda j, b: (b, j)), pl.BlockSpec((SUBLANES, width), lambda j, b: (0, j))],
        out_shape=[jax.ShapeDtypeStruct((nb * seq, ncol * width), BF16), jax.ShapeDtypeStruct((SUBLANES, ncol * width), F32)],
        compiler_params=_params("parallel", "arbitrary"),
    )(*args)


def _attn_bwd(proj, dmix, y_att, lse, bias, sink, dbias_in, *, nb, seq):
    nblk = seq // CHUNK
    t = nb * seq
    scale = HEAD_DIM ** -0.5

    def body(q_ref, kp_ref, kc_ref, kn_ref, do_ref, o_ref, lse_ref, bias_ref, sink_ref, dbin_ref,
             dq_ref, dkv_ref, dbias_ref, dsink_ref):
        b, n = pl.program_id(0), pl.program_id(1)

        @pl.when(n == 0)
        def _():
            dkv_ref[...] = jnp.zeros_like(dkv_ref)

        @pl.when((n == 0) & (b == 0))
        def _():
            dbias_ref[...] = dbin_ref[...]
            dsink_ref[...] = jnp.zeros_like(dsink_ref)

        inside = _keys_inside(n, nblk)
        lane = lax.broadcasted_iota(jnp.int32, (1, LANES), 1)
        dsink = jnp.zeros((1, LANES), F32)
        rows = pl.ds(pl.multiple_of(n * CHUNK, CHUNK), KEY_SPAN)
        groups = range(KV_HEADS)

        def keys(g, off):
            cs = slice(off + g * HEAD_DIM, off + (g + 1) * HEAD_DIM)
            return jnp.concatenate([kp_ref[:, cs], kc_ref[:, cs], kn_ref[:, cs]], axis=0).astype(BF16)

        kcat = [keys(g, 0) for g in groups]
        vcat = [keys(g, KV_WIDTH) for g in groups]
        q = [(_stack_heads(q_ref, g, F32) * scale).astype(BF16) for g in groups]
        do = [_stack_heads(do_ref, g, F32) for g in groups]
        do16 = [do[g].astype(BF16) for g in groups]
        lse = [jnp.concatenate([lse_ref[:, g * Q_PER_KV + r:g * Q_PER_KV + r + 1] for r in range(Q_PER_KV)], axis=0)
               for g in groups]
        s = [jnp.where(inside, _dot(q[g], kcat[g], NT)
                       + bias_ref[g * Q_PER_KV:(g + 1) * Q_PER_KV].reshape(GROUP_ROWS, KEY_SPAN), -jnp.inf) for g in groups]
        p = [jnp.exp(s[g] - lse[g]) for g in groups]
        delta = [jnp.sum(do[g] * _stack_heads(o_ref, g, F32), axis=-1, keepdims=True) for g in groups]
        ds = [p[g] * (_dot(do16[g], vcat[g], NT) - delta[g]) for g in groups]
        ds16 = [ds[g].astype(BF16) for g in groups]
        sink_part = [jnp.exp(_per_head_column(sink_ref, g) - lse[g]) * delta[g] for g in groups]
        dq = [_dot(ds16[g], kcat[g], NN) * scale for g in groups]
        dk = [_dot(ds16[g], q[g], TN) for g in groups]
        dv = [_dot(p[g].astype(BF16), do16[g], TN) for g in groups]
        for g in groups:
            dbias_ref[g * Q_PER_KV:(g + 1) * Q_PER_KV] += ds[g].reshape(Q_PER_KV, CHUNK, KEY_SPAN)
            for r in range(Q_PER_KV):
                h = g * Q_PER_KV + r
                dq_ref[:, h * HEAD_DIM:(h + 1) * HEAD_DIM] = dq[g][r * CHUNK:(r + 1) * CHUNK].astype(BF16)
                dsink = dsink - jnp.where(lane == h, jnp.sum(sink_part[g][r * CHUNK:(r + 1) * CHUNK], axis=0, keepdims=True), 0.0)
            dkv_ref[0, rows, g * HEAD_DIM:(g + 1) * HEAD_DIM] += dk[g]
            dkv_ref[0, rows, KV_WIDTH + g * HEAD_DIM:KV_WIDTH + (g + 1) * HEAD_DIM] += dv[g]
        dsink_ref[...] += dsink

    blk = lambda cb: pl.BlockSpec((CHUNK, D_MODEL), lambda b, n: (b * nblk + n, cb))
    whole = pl.BlockSpec((N_HEADS, CHUNK, KEY_SPAN), lambda b, n: (0, 0, 0))
    vec = pl.BlockSpec((1, LANES), lambda b, n: (0, 0))
    return pl.pallas_call(
        body, name="attn_bwd", grid=(nb, nblk),
        in_specs=[blk(PQ // D_MODEL)] + _kv_specs(nblk) + [
            blk(1), blk(0), pl.BlockSpec((CHUNK, N_HEADS), lambda b, n: (b * nblk + n, 0)), whole, vec, whole],
        out_specs=[blk(0), pl.BlockSpec((1, seq + 2 * CHUNK, 2 * KV_WIDTH), lambda b, n: (b, 0, 0)), whole, vec],
        out_shape=[jax.ShapeDtypeStruct((t, D_MODEL), BF16),
                   jax.ShapeDtypeStruct((nb, seq + 2 * CHUNK, 2 * KV_WIDTH), F32),
                   jax.ShapeDtypeStruct((N_HEADS, CHUNK, KEY_SPAN), F32),
                   jax.ShapeDtypeStruct((1, LANES), F32)],
        compiler_params=_params("arbitrary", "arbitrary"),
    )(proj, proj, proj, proj, dmix, y_att, lse, bias, sink, dbias_in)


def _gate_norm_bwd(y2, xbc, proj, dmix, dvec, nw, *, tm):
    t = xbc.shape[0]
    half = SSM_WIDTH // N_GROUPS

    def body(y_ref, xs_ref, z_ref, do_ref, d_ref, w_ref, dyv_ref, dz_ref, dd_ref, dw_ref):
        @pl.when(pl.program_id(0) == 0)
        def _():
            dd_ref[...] = jnp.zeros_like(dd_ref)
            dw_ref[...] = jnp.zeros_like(dw_ref)

        z = z_ref[...].astype(F32)
        xs = xs_ref[...]
        sg = _sigmoid(z)
        gz = z * sg
        yv = y_ref[0].astype(F32) + y_ref[1].astype(F32) + d_ref[...] * xs
        p = yv * gz
        do = do_ref[...].astype(F32)
        for g in range(N_GROUPS):
            cs = slice(g * half, (g + 1) * half)
            pg = p[:, cs]
            r = lax.rsqrt(jnp.mean(pg * pg, axis=-1, keepdims=True) + NORM_EPS)
            ph = pg * r
            gw = do[:, cs] * w_ref[:, cs]
            dp = r * (gw - ph * jnp.mean(gw * ph, axis=-1, keepdims=True))
            dyv = dp * gz[:, cs]
            dyv_ref[:, cs] = dyv
            dz_ref[:, cs] = (dp * yv[:, cs] * _dsilu(z[:, cs], sg[:, cs])).astype(BF16)
            dw_ref[:, cs] += jnp.sum(do[:, cs] * ph, axis=0, keepdims=True)
            dd_ref[:, cs] += jnp.sum(dyv * xs[:, cs], axis=0, keepdims=True)

    row = lambda cb: pl.BlockSpec((tm, SSM_WIDTH), lambda i: (i, cb))
    vec = pl.BlockSpec((1, SSM_WIDTH), lambda i: (0, 0))
    return pl.pallas_call(
        body, name="gate_norm_bwd", grid=(t // tm,),
        in_specs=[pl.BlockSpec((2, tm, SSM_WIDTH), lambda i: (0, i, 0)), row(0), row(PZ // SSM_WIDTH), row(0), vec, vec],
        out_specs=[row(0), row(0), vec, vec],
        out_shape=[jax.ShapeDtypeStruct((t, SSM_WIDTH), F32), jax.ShapeDtypeStruct((t, SSM_WIDTH), BF16),
                   jax.ShapeDtypeStruct((1, SSM_WIDTH), F32), jax.ShapeDtypeStruct((1, SSM_WIDTH), F32)],
        compiler_params=_params("arbitrary"),
    )(y2, xbc, proj, dmix, dvec, nw)


def _ssd_specs(nchunk, row):
    return [pl.BlockSpec((CHUNK, SSM_WIDTH), lambda d, b, c: (row(d, b, c), 0)),
            pl.BlockSpec((CHUNK, 2 * BC_WIDTH), lambda d, b, c: (row(d, b, c), SSM_WIDTH // (2 * BC_WIDTH))),
            pl.BlockSpec((CHUNK, LANES), lambda d, b, c: (row(d, b, c), 0)),
            pl.BlockSpec((1, LANES), lambda d, b, c: (0, 0)),
            pl.BlockSpec((1, LANES), lambda d, b, c: (0, 0)),
            pl.BlockSpec((CHUNK, SSM_WIDTH), lambda d, b, c: (row(d, b, c), 0))]


_SSD_SCRATCH = [pltpu.VMEM((N_HEADS, HEAD_DIM, N_STATE), F32),
                pltpu.VMEM((CHUNK, N_HEADS), F32), pltpu.VMEM((N_HEADS, CHUNK), F32),
                pltpu.VMEM((CHUNK, N_HEADS), F32)]


def _ssd_bwd(xbc, dtraw, dtb, alog, dyv, states, *, nb, seq, scatter=()):
    nchunk = seq // CHUNK
    t = nb * seq
    row = _chunk_index(nchunk, False)
    ns = len(scatter)

    def body(*refs):
        xs_ref, bc_ref, dt_ref, dtb_ref, alog_ref, dy_ref, hs_ref = refs[:7]
        dx_ref, db_ref, dc_ref, draw_ref, da_ref, dbias_ref = refs[7 + ns:13 + ns]
        h_ref, z_ref, zt_ref, dts_ref, acc_ref, span_ref = refs[13 + 2 * ns:19 + 2 * ns]
        d, b, c = pl.program_id(0), pl.program_id(1), pl.program_id(2)
        if ns:
            step = (d * nb + b) * nchunk + c
            copies = _exchange_copies(refs[7:7 + ns], refs[13 + ns:13 + 2 * ns], *refs[19 + 2 * ns:], gather=False)
            pl.when(step == 0)(lambda: _exchange_start(copies))

        @pl.when(c == 0)
        def _():
            h_ref[...] = jnp.zeros_like(h_ref)

        @pl.when((c == 0) & (b == 0) & (d == 0))
        def _():
            da_ref[...] = jnp.zeros_like(da_ref)
            dbias_ref[...] = jnp.zeros_like(dbias_ref)

        ahead, dt, tot, dt_all = _scan_setup(d, dt_ref, dtb_ref, alog_ref, z_ref, zt_ref,
                                             lower_when_dir0=False, inclusive=False)
        mask = ahead >= 0
        dts_ref[...] = dt
        e_tot = jnp.exp(tot)
        acc_ref[...] = jnp.zeros_like(acc_ref)
        span_ref[...] = jnp.zeros_like(span_ref)
        for g in range(N_GROUPS):
            bg = bc_ref[:, g * N_STATE:(g + 1) * N_STATE]
            cg = bc_ref[:, BC_WIDTH + g * N_STATE:BC_WIDTH + (g + 1) * N_STATE]
            bg16 = bg.astype(BF16)
            cg16 = cg.astype(BF16)
            bc_t = _dot(bg16, cg16, NT)
            heads = range(g * HEADS_PER_GROUP, (g + 1) * HEADS_PER_GROUP)
            cols = {h: slice(h * HEAD_DIM, (h + 1) * HEAD_DIM) for h in heads}
            zc = {h: jnp.broadcast_to(z_ref[:, h:h + 1], (CHUNK, CHUNK)) for h in heads}
            decay = {h: jnp.exp(jnp.where(mask, zc[h] - zt_ref[h:h + 1, :], -jnp.inf)) for h in heads}
            e_z = {h: jnp.exp(zc[h]) for h in heads}
            e_tz = {h: jnp.exp(tot[:, h:h + 1] - zc[h]) for h in heads}
            x_h = {h: xs_ref[:, cols[h]] for h in heads}
            dt_h = {h: dts_ref[:, h:h + 1] for h in heads}
            u = {h: (x_h[h] * dt_h[h]).astype(BF16) for h in heads}
            dy = {h: dy_ref[:, cols[h]].astype(BF16) for h in heads}
            state = {h: h_ref[h] for h in heads}
            st16 = {h: state[h].astype(BF16) for h in heads}
            fstate = {h: hs_ref[0, 0, h] for h in heads}
            mix = {h: (bc_t * decay[h]).astype(BF16) for h in heads}
            bz = {h: (bg * e_z[h]).astype(BF16) for h in heads}
            du = {h: _dot(mix[h], dy[h], NN) + _dot(bz[h], st16[h], NT) for h in heads}
            w2f = {h: _dot(u[h], dy[h], NT) * decay[h] for h in heads}
            db_out = {h: e_z[h] * _dot(u[h], st16[h], NN) for h in heads}
            dc_out = {h: e_tz[h] * _dot(dy[h], fstate[h].astype(BF16), NN) for h in heads}
            w2_sum = w2f[heads[0]]
            for h in heads[1:]:
                w2_sum = w2_sum + w2f[h]
            w2_sum = w2_sum.astype(BF16)
            pairs = {h: w2f[h] * bc_t for h in heads}
            col_in = {h: jnp.sum(pairs[h], axis=-1, keepdims=True) for h in heads}
            row_in = {h: jnp.sum(pairs[h], axis=0, keepdims=True) for h in heads}
            row_out = {h: jnp.sum(dc_out[h] * cg, axis=-1, keepdims=True) for h in heads}
            col_out = {h: jnp.sum(db_out[h] * bg, axis=-1, keepdims=True) for h in heads}
            ddt_h = {h: jnp.sum(du[h] * x_h[h], axis=-1, keepdims=True) for h in heads}
            cz = {h: (cg * e_tz[h]).astype(BF16) for h in heads}
            new = {h: state[h] * e_tot[:, h:h + 1] + _dot(dy[h], cz[h], TN) for h in heads}
            dbg = _dot(w2_sum, cg16, NN)
            dcg = _dot(w2_sum, bg16, TN)
            for h in heads:
                dbg = dbg + db_out[h]
                dcg = dcg + dc_out[h]
                acc_ref[0, :, h:h + 1] = row_out[h] - col_in[h]
                acc_ref[1, :, h:h + 1] = col_out[h]
                acc_ref[2, :, h:h + 1] = ddt_h[h]
                acc_ref[3, h:h + 1, :] = row_in[h]
                span_ref[0:1, h:h + 1] = e_tot[:, h:h + 1] * jnp.sum(fstate[h] * state[h], keepdims=True)
                dx_ref[0, :, cols[h]] = du[h] * dt_h[h]
                h_ref[h] = new[h]
            db_ref[0, :, g * N_STATE:(g + 1) * N_STATE] = dbg
            dc_ref[0, :, g * N_STATE:(g + 1) * N_STATE] = dcg
        tri = mask.astype(F32)
        at_dir = lambda v: jnp.where(d == 0, v, pltpu.roll(v, N_HEADS, 1))
        dadt = at_dir(jnp.dot(tri, acc_ref[0] + acc_ref[3].T, precision=HIGHEST, preferred_element_type=F32)
                      + jnp.dot(1.0 - tri, acc_ref[1], precision=HIGHEST, preferred_element_type=F32) + span_ref[0:1, :])
        ddt = at_dir(acc_ref[2])
        a = -jnp.exp(alog_ref[...])
        draw = (ddt + a * dadt) * _sigmoid(dt_ref[...] + dtb_ref[...])
        draw_ref[0] = draw
        da_ref[...] += jnp.sum(dt_all * dadt, axis=0, keepdims=True) * a
        dbias_ref[...] += jnp.sum(draw, axis=0, keepdims=True)
        if ns:
            pl.when(step == 2 * nb * nchunk - 1)(lambda: _exchange_wait(copies))

    out_row = lambda w: pl.BlockSpec((1, CHUNK, w), lambda d, b, c: (d, row(d, b, c), 0))
    vec = pl.BlockSpec((1, LANES), lambda d, b, c: (0, 0))
    any_spec = pl.BlockSpec(memory_space=pl.ANY)
    res = pl.pallas_call(
        body, name="ssd_bwd_scatter" if ns else "ssd_bwd", grid=(2, nb, nchunk),
        in_specs=_ssd_specs(nchunk, row) + [
            pl.BlockSpec((1, 1, N_HEADS, HEAD_DIM, N_STATE), lambda d, b, c: (d, row(d, b, c), 0, 0, 0))]
        + [any_spec] * ns,
        out_specs=[out_row(SSM_WIDTH), out_row(BC_WIDTH), out_row(BC_WIDTH), out_row(LANES), vec, vec] + [any_spec] * ns,
        out_shape=[jax.ShapeDtypeStruct((2, t, SSM_WIDTH), F32), jax.ShapeDtypeStruct((2, t, BC_WIDTH), F32),
                   jax.ShapeDtypeStruct((2, t, BC_WIDTH), F32),
                   jax.ShapeDtypeStruct((2, t, LANES), F32), jax.ShapeDtypeStruct((1, LANES), F32),
                   jax.ShapeDtypeStruct((1, LANES), F32)] + _exchange_out_shapes(scatter, gather=False),
        scratch_shapes=_SSD_SCRATCH + [pltpu.VMEM((4, CHUNK, LANES), F32), pltpu.VMEM((SUBLANES, LANES), F32)]
        + _exchange_sems(ns),
        compiler_params=_params("arbitrary", "arbitrary", "arbitrary"),
    )(xbc, xbc, dtraw, dtb, alog, dyv, states, *scatter)
    return tuple(res[:6]) + (list(res[6:]),)


def _layer_bwd(dx2, p, s, band_bias, dbias_in, *, nb, seq, pending=()):
    t = nb * seq
    x, proj, xbc, x1 = s["x"], s["proj"], s["xbc"], s["x1"]
    dact = _matmul(dx2, p["w_down"], "nt", name="down_proj_dx", tm=MM_ROWS, tn=1408, tk=1024, out_dtype=BF16)
    g_w_down = _matmul(s["act"], dx2, "tn", name="down_proj_dw", tm=1408, tn=1024, tk=2 * MM_ROWS)
    dg, du, dwb_ffn = _ffn_act_bwd(s["gu"], s["fconv"], dact, p["ffn_cw"], nb=nb, seq=seq)
    dh2 = _matmul(dg, p["w_up"], "nt", name="up_proj_dx_g", tm=MM_ROWS, tn=1024, tk=D_FF)
    dh2 = _matmul(du, p["w_up"], "nt", name="up_proj_dx_u", tm=MM_ROWS, tn=1024, tk=D_FF, res=dh2, b_k0=D_FF)
    g_w_up = jnp.concatenate([_matmul(s["h2"], dg, "tn", name="up_proj_dw_g", tm=1024, tn=1408, tk=2 * MM_ROWS),
                              _matmul(s["h2"], du, "tn", name="up_proj_dw_u", tm=1024, tn=1408, tk=2 * MM_ROWS)], axis=1)
    dx1, g_n2 = _rms_bwd(x1, dh2, p["n2"], dx2, tm=512, name="norm2_bwd")
    dmix = _matmul(dx1, p["w_out"], "nt", name="out_proj_dx", tm=MM_ROWS, tn=1024, tk=1024, out_dtype=BF16)
    g_w_out = jnp.concatenate([_matmul(s["y_ssm"], dx1, "tn", name="out_proj_dw_ssm", tm=1024, tn=1024, tk=2 * MM_ROWS),
                               _matmul(s["y_att"], dx1, "tn", name="out_proj_dw_att", tm=1024, tn=1024, tk=2 * MM_ROWS)],
                              axis=0)
    dq, dkv, dbias, dsink = _attn_bwd(proj, dmix, s["y_att"], s["lse"], band_bias, p["sink"], dbias_in, nb=nb, seq=seq)
    dkv = dkv[:, CHUNK:CHUNK + seq, :].reshape(t, 2 * KV_WIDTH)
    dyv, dz, g_dvec, g_ssm_nw = _gate_norm_bwd(s["y2"], xbc, proj, dmix, p["dvec"], p["ssm_nw"], tm=256)
    own = [_rows_to_blocks(g_w_down[None], 1).astype(BF16), _cols_to_blocks(g_w_up[None], 1).astype(BF16),
           _rows_to_blocks(g_w_out[None], 1).astype(BF16)]
    dxs2, db2, dc2, draw2, g_alog, g_dtb, exchanged = _ssd_bwd(xbc, s["dtraw"], p["dtb"], p["alog"], dyv, s["states"],
                                                               nb=nb, seq=seq, scatter=own + list(pending))
    ddt_raw = draw2[0] + draw2[1]
    conv = dict(nb=nb, seq=seq)
    dxs_pre, dwb_xs = _ssm_conv_bwd(proj, s["gconv"], dxs2, p["conv_w"], name="ssm_conv_bwd_x", width=256,
                                    proj_col=PXS // 256, conv_col=0, ncol=4, extra=dyv, scale=p["dvec"], **conv)
    db_pre, dwb_b = _ssm_conv_bwd(proj, s["gconv"], db2, p["conv_w"], name="ssm_conv_bwd_b", width=256,
                                  proj_col=PB // 256, conv_col=SSM_WIDTH // 256, ncol=1, **conv)
    dc_pre, dwb_c = _ssm_conv_bwd(proj, s["gconv"], dc2, p["conv_w"], name="ssm_conv_bwd_c", width=256,
                                  proj_col=PC // 256, conv_col=(SSM_WIDTH + BC_WIDTH) // 256, ncol=1, **conv)
    dwb_ssm = jnp.concatenate([dwb_xs, dwb_b, dwb_c], axis=1)
    dproj = jnp.concatenate([dz, dxs_pre, dq, db_pre, dc_pre, dkv.astype(BF16), ddt_raw.astype(BF16)], axis=1)
    dh1 = _matmul(dproj, p["w_in"], "nt", name="in_proj_dx", tm=MM_ROWS, tn=1024, tk=PROJ_W)
    g_w_in = _matmul(s["h1"], dproj, "tn", name="in_proj_dw", tm=1024, tn=1408, tk=2 * MM_ROWS)
    dx, g_n1 = _rms_bwd(x, dh1, p["n1"], dx1, tm=512, name="norm1_bwd")
    grads = dict(n1=g_n1, w_in=g_w_in, conv_w=dwb_ssm[:SSM_TAPS], conv_b=dwb_ssm[SUBLANES - 1], dtb=g_dtb, alog=g_alog,
                 dvec=g_dvec, ssm_nw=g_ssm_nw, sink=dsink, w_out=g_w_out, n2=g_n2, w_up=g_w_up,
                 ffn_cw=dwb_ffn[:FFN_TAPS], ffn_cb=dwb_ffn[SUBLANES - 1], w_down=g_w_down)
    return dx, dbias, grads, exchanged


def _band_bias(rel_bias, bucket):
    def body(rb_ref, b_ref, o_ref):
        o_ref[...] = jnp.zeros_like(o_ref)

        def per_bucket(k, carry):
            hit = b_ref[...] == k
            for h in range(N_HEADS):
                o_ref[h] = jnp.where(hit, rb_ref[k, h], o_ref[h])
            return carry

        lax.fori_loop(0, REL_BUCKETS, per_bucket, 0)
        qi = lax.broadcasted_iota(jnp.int32, (CHUNK, KEY_SPAN), 0)
        kj = lax.broadcasted_iota(jnp.int32, (CHUNK, KEY_SPAN), 1)
        band = jnp.abs(kj - CHUNK - qi) <= CHUNK
        for h in range(N_HEADS):
            o_ref[h] = jnp.where(band, o_ref[h], -jnp.inf)

    return pl.pallas_call(
        body, name="band_bias", out_shape=jax.ShapeDtypeStruct((N_HEADS, CHUNK, KEY_SPAN), F32),
        in_specs=[pl.BlockSpec(memory_space=pltpu.SMEM), pl.BlockSpec(memory_space=pltpu.VMEM)],
        out_specs=pl.BlockSpec(memory_space=pltpu.VMEM),
    )(rel_bias, bucket)


def _rel_bias_grad(dbias, bucket):
    def body(d_ref, b_ref, o_ref):
        o_ref[...] = jnp.zeros_like(o_ref)
        lane = lax.broadcasted_iota(jnp.int32, (1, LANES), 1)

        def per_bucket(k, carry):
            hit = b_ref[...] == k
            for h in range(N_HEADS):
                part = jnp.sum(jnp.where(hit, d_ref[h], 0.0), axis=1, keepdims=True)
                o_ref[h:h + 1, :] += jnp.where(lane == k, jnp.sum(part, axis=0, keepdims=True), 0.0)
            return carry

        lax.fori_loop(0, REL_BUCKETS, per_bucket, 0)

    return pl.pallas_call(
        body, name="rel_bias_grad", out_shape=jax.ShapeDtypeStruct((N_HEADS, LANES), F32),
        compiler_params=pltpu.CompilerParams(vmem_limit_bytes=VMEM_LIMIT_BYTES),
    )(dbias, bucket)


N_PEER = N_DEV - 1


def _exchange_copies(ins, outs, send_sems, recv_sems, local_sems, *, gather):
    x, y, c = lax.axis_index("x"), lax.axis_index("y"), lax.axis_index("c")
    me = 4 * x + 2 * y + c
    peers = []
    for k in range(1, N_DEV):
        px, py, pc = x ^ ((k >> 2) & 1), y ^ ((k >> 1) & 1), c ^ (k & 1)
        peers.append(((px, py, pc), 4 * px + 2 * py + pc))
    local, sends, recvs = [], [], []
    for i in range(len(ins)):
        mine = ins[i] if gather else ins[i].at[me]
        local.append(pltpu.make_async_copy(mine, outs[i].at[me], local_sems.at[i]))
        for k, (pid, pslot) in enumerate(peers):
            src = ins[i] if gather else ins[i].at[pslot]
            sem = i * N_PEER + k
            sends.append(pltpu.make_async_remote_copy(
                src_ref=src, dst_ref=outs[i].at[me], send_sem=send_sems.at[sem], recv_sem=recv_sems.at[sem],
                device_id=pid, device_id_type=pl.DeviceIdType.MESH))
            recvs.append(pltpu.make_async_remote_copy(
                src_ref=src, dst_ref=outs[i].at[pslot], send_sem=send_sems.at[sem], recv_sem=recv_sems.at[sem],
                device_id=pid, device_id_type=pl.DeviceIdType.MESH))
    return local, sends, recvs


def _exchange_start(copies):
    local, sends, _ = copies
    for cp in local + sends:
        cp.start()


def _exchange_wait(copies):
    local, sends, recvs = copies
    for cp in recvs:
        cp.wait_recv()
    for cp in sends:
        cp.wait_send()
    for cp in local:
        cp.wait()


def _exchange_out_shapes(arrs, *, gather):
    return [jax.ShapeDtypeStruct((N_DEV,) + (a.shape if gather else a.shape[1:]), a.dtype) for a in arrs]


def _exchange_sems(n):
    if not n:
        return []
    return [pltpu.SemaphoreType.DMA((n * N_PEER,)), pltpu.SemaphoreType.DMA((n * N_PEER,)), pltpu.SemaphoreType.DMA((n,))]


def _exchange(arrs, *, gather, name):
    n = len(arrs)

    def body(*refs):
        copies = _exchange_copies(refs[:n], refs[n:2 * n], *refs[2 * n:], gather=gather)
        _exchange_start(copies)
        _exchange_wait(copies)

    any_spec = pl.BlockSpec(memory_space=pl.ANY)
    return pl.pallas_call(
        body, name=name, in_specs=[any_spec] * n, out_specs=[any_spec] * n,
        out_shape=_exchange_out_shapes(arrs, gather=gather), scratch_shapes=_exchange_sems(n),
        compiler_params=pltpu.CompilerParams(has_side_effects=True),
    )(*arrs)


def _adamw(parts, w, m, v, *, name, tr):
    r, c = w.shape
    nparts = len(parts)
    rows = r // nparts
    assert rows * nparts == r and rows % tr == 0 and all(p.shape == (N_DEV, rows, c) for p in parts)
    per = rows // tr
    c1 = 1.0 - ADAM_B1 ** ADAM_STEP
    c2 = 1.0 - ADAM_B2 ** ADAM_STEP

    def body(*refs):
        p_refs = refs[:nparts]
        w_ref, m_ref, v_ref, g_ref, d_ref, nm_ref, nv_ref = refs[nparts:]
        which = pl.program_id(0) // per
        for k, p_ref in enumerate(p_refs):
            @pl.when(which == k)
            def _(p_ref=p_ref):
                acc = p_ref[0].astype(F32)
                for j in range(1, N_DEV):
                    acc = acc + p_ref[j].astype(F32)
                g_ref[...] = acc

        g = g_ref[...]
        nm = ADAM_B1 * m_ref[...] + (1.0 - ADAM_B1) * g
        nv = ADAM_B2 * v_ref[...] + (1.0 - ADAM_B2) * (g * g)
        nm_ref[...] = nm
        nv_ref[...] = nv
        d_ref[...] = -ADAM_LR * ((nm / c1) / (jnp.sqrt(nv / c2) + ADAM_EPS) + ADAM_WD * w_ref[...])

    def part_spec(k):
        return pl.BlockSpec((N_DEV, tr, c), lambda i: (0, jnp.clip(i - k * per, 0, per - 1), 0))

    blk = pl.BlockSpec((tr, c), lambda i: (i, 0))
    return pl.pallas_call(
        body, name=name, grid=(r // tr,),
        in_specs=[part_spec(k) for k in range(nparts)] + [blk, blk, blk],
        out_specs=[blk] * 4, out_shape=[jax.ShapeDtypeStruct((r, c), F32)] * 4,
        compiler_params=_params("arbitrary"),
    )(*parts, w, m, v)


def _t5_bucket(rel):
    half = REL_BUCKETS // 2
    max_exact = half // 2
    ret = jnp.where(rel > 0, half, 0)
    n = jnp.abs(rel)
    nf = jnp.maximum(n, 1).astype(F32)
    large = max_exact + (jnp.log(nf / max_exact) / math.log(CHUNK / max_exact) * (half - max_exact)).astype(jnp.int32)
    large = jnp.minimum(large, half - 1)
    return ret + jnp.where(n < max_exact, n, large)


def _split16(w):
    hi = w.astype(BF16)
    return hi, (w - hi.astype(F32)).astype(BF16)


def _cols_to_blocks(g, depth):
    _, r, c8 = g.shape
    return g.reshape(depth, r, N_DEV, c8 // N_DEV).transpose(2, 0, 1, 3).reshape(N_DEV, depth * r, c8 // N_DEV)


def _rows_to_blocks(g, depth):
    _, r8, c = g.shape
    return g.reshape(depth, N_DEV, r8 // N_DEV, c).transpose(1, 0, 2, 3).reshape(N_DEV, depth * r8 // N_DEV, c)


def _blocks_to_cols(a, depth):
    _, dr, c = a.shape
    r = dr // depth
    return a.reshape(N_DEV, depth, r, c).transpose(1, 2, 0, 3).reshape(depth, r, N_DEV * c)


def _blocks_to_rows(a, depth):
    _, dr, c = a.shape
    r = dr // depth
    return a.reshape(N_DEV, depth, r, c).transpose(1, 0, 2, 3).reshape(depth, N_DEV * r, c)


_SMALL = ("rel_bias", "norm1_w", "conv_b", "dt_bias", "a_log", "d_skip", "ssm_norm_w", "attn_sink", "norm2_w",
          "ffn_conv_b", "final_norm_w")
_SHARDED = ("w_in", "conv_w", "w_out", "w_up", "ffn_conv_w", "w_down")
_ORDER = ("rel_bias", "norm1_w", "w_in", "conv_w", "conv_b", "dt_bias", "a_log", "d_skip", "ssm_norm_w", "attn_sink",
          "w_out", "norm2_w", "w_up", "ffn_conv_w", "ffn_conv_b", "w_down", "final_norm_w")


def _pack_small(d):
    flat = jnp.concatenate([d[k].reshape(-1).astype(F32) for k in _SMALL])
    rows = -(-flat.size // (LANES * SUBLANES)) * SUBLANES
    return jnp.pad(flat, (0, rows * LANES - flat.size)).reshape(rows, LANES)


def _unpack_small(packed, like):
    flat = packed.reshape(-1)
    out, off = {}, 0
    for k in _SMALL:
        out[k] = flat[off:off + like[k].size].reshape(like[k].shape)
        off += like[k].size
    return out


def kernel(x, rel_bias, norm1_w, w_in, conv_w, conv_b, dt_bias, a_log, d_skip, ssm_norm_w, attn_sink, w_out, norm2_w, w_up, ffn_conv_w, ffn_conv_b, w_down, final_norm_w, loss_target, m_rel_bias, m_norm1_w, m_w_in, m_conv_w, m_conv_b, m_dt_bias, m_a_log, m_d_skip, m_ssm_norm_w, m_attn_sink, m_w_out, m_norm2_w, m_w_up, m_ffn_conv_w, m_ffn_conv_b, m_w_down, m_final_norm_w, v_rel_bias, v_norm1_w, v_w_in, v_conv_w, v_conv_b, v_dt_bias, v_a_log, v_d_skip, v_ssm_norm_w, v_attn_sink, v_w_out, v_norm2_w, v_w_up, v_ffn_conv_w, v_ffn_conv_b, v_w_down, v_final_norm_w):
    w = dict(rel_bias=rel_bias, norm1_w=norm1_w, w_in=w_in, conv_w=conv_w, conv_b=conv_b, dt_bias=dt_bias, a_log=a_log,
             d_skip=d_skip, ssm_norm_w=ssm_norm_w, attn_sink=attn_sink, w_out=w_out, norm2_w=norm2_w, w_up=w_up,
             ffn_conv_w=ffn_conv_w, ffn_conv_b=ffn_conv_b, w_down=w_down, final_norm_w=final_norm_w)
    m = dict(rel_bias=m_rel_bias, norm1_w=m_norm1_w, w_in=m_w_in, conv_w=m_conv_w, conv_b=m_conv_b, dt_bias=m_dt_bias,
             a_log=m_a_log, d_skip=m_d_skip, ssm_norm_w=m_ssm_norm_w, attn_sink=m_attn_sink, w_out=m_w_out,
             norm2_w=m_norm2_w, w_up=m_w_up, ffn_conv_w=m_ffn_conv_w, ffn_conv_b=m_ffn_conv_b, w_down=m_w_down,
             final_norm_w=m_final_norm_w)
    v = dict(rel_bias=v_rel_bias, norm1_w=v_norm1_w, w_in=v_w_in, conv_w=v_conv_w, conv_b=v_conv_b, dt_bias=v_dt_bias,
             a_log=v_a_log, d_skip=v_d_skip, ssm_norm_w=v_ssm_norm_w, attn_sink=v_attn_sink, w_out=v_w_out,
             norm2_w=v_norm2_w, w_up=v_w_up, ffn_conv_w=v_ffn_conv_w, ffn_conv_b=v_ffn_conv_b, w_down=v_w_down,
             final_norm_w=v_final_norm_w)
    nb, seq, _ = x.shape
    t = nb * seq
    depth = w_in.shape[0]

    flat2 = lambda a: a.reshape(-1, a.shape[-1])
    own_shards = lambda i: [w_out[i].astype(BF16), w_up[i].astype(BF16), w_down[i].astype(BF16)]
    cw_hi, cw_lo = _split16(flat2(conv_w))
    fw_hi, fw_lo = _split16(flat2(ffn_conv_w))
    g_in, g_cwh, g_cwl, g_fwh, g_fwl = _exchange([w_in[0].astype(BF16), cw_hi, cw_lo, fw_hi, fw_lo], gather=True,
                                                 name="gather_weights")
    full_conv_w = _blocks_to_cols(g_cwh.astype(F32) + g_cwl.astype(F32), depth)
    full_ffn_cw = _blocks_to_cols(g_fwh.astype(F32) + g_fwl.astype(F32), depth)

    rel = jnp.arange(KEY_SPAN)[None, :] - CHUNK - jnp.arange(CHUNK)[:, None]
    bucket = _t5_bucket(rel)
    band_bias = _band_bias(rel_bias, bucket)

    def layer_params(i, g_in):
        return dict(n1=norm1_w[i][None], w_in=_to_proj_layout(_blocks_to_cols(g_in, 1)[0]), conv_w=full_conv_w[i],
                    conv_b=conv_b[i][None], dtb=_pad_lanes(dt_bias[i].reshape(-1)), alog=_pad_lanes(a_log[i].reshape(-1)),
                    dvec=jnp.repeat(d_skip[i], HEAD_DIM)[None], ssm_nw=ssm_norm_w[i][None], sink=_pad_lanes(attn_sink[i]),
                    n2=norm2_w[i][None], ffn_cw=full_ffn_cw[i], ffn_cb=ffn_conv_b[i][None])

    h = x.reshape(t, D_MODEL)
    params, saved = [None] * depth, [None] * depth
    for i in range(depth):
        h, params[i], saved[i], nxt = _layer_fwd(h, layer_params(i, g_in), band_bias, nb=nb, seq=seq, own_shards=own_shards(i),
                                                 next_shards=[w_in[i + 1].astype(BF16)] if i + 1 < depth else ())
        if nxt:
            (g_in,) = nxt
    dh, g_final, loss_part = _loss_head(h, loss_target.reshape(t, D_MODEL), final_norm_w[None], tm=512)
    loss = lax.psum(loss_part[0, 0], ("x", "y", "c"))

    w_in_blocks = lambda g: _cols_to_blocks(_from_proj_layout(g["w_in"])[None], 1).astype(BF16)
    dbias = jnp.zeros((N_HEADS, CHUNK, KEY_SPAN), F32)
    grads, pending = [None] * depth, ()
    parts = dict(w_in=[None] * depth, w_out=[None] * depth, w_up=[None] * depth, w_down=[None] * depth)
    for i in reversed(range(depth)):
        dh, dbias, grads[i], arrived = _layer_bwd(dh, params[i], saved[i], band_bias, dbias, nb=nb, seq=seq, pending=pending)
        parts["w_down"][i], parts["w_up"][i], parts["w_out"][i] = arrived[:3]
        if pending:
            parts["w_in"][i + 1] = arrived[3]
        pending = [w_in_blocks(grads[i])]
    grad_x = dh.reshape(nb, seq, D_MODEL)
    stack = lambda k: jnp.stack([g[k] for g in grads])
    last = _exchange(pending + [_cols_to_blocks(stack("conv_w"), depth).astype(BF16),
                                _cols_to_blocks(stack("ffn_cw"), depth).astype(BF16)], gather=False, name="scatter_grads")
    parts["w_in"][0] = last[0]

    out = {}
    for k in ("w_in", "w_out", "w_up", "w_down"):
        rows = parts[k][0].shape[1]
        tr = max(d for d in range(16, 129, 16) if rows % d == 0)
        res = _adamw(parts[k], flat2(w[k]), flat2(m[k]), flat2(v[k]), name="adamw_" + k, tr=tr)
        out[k] = [a.reshape(w[k].shape) for a in res]
    for k, p8 in zip(("conv_w", "ffn_conv_w"), last[1:]):
        res = _adamw([p8], flat2(w[k]), flat2(m[k]), flat2(v[k]), name="adamw_" + k, tr=p8.shape[1])
        out[k] = [a.reshape(w[k].shape) for a in res]

    small = dict(rel_bias=_rel_bias_grad(dbias, bucket)[:, :REL_BUCKETS].T, norm1_w=stack("n1"), conv_b=stack("conv_b"),
                 dt_bias=stack("dtb")[:, 0, :2 * N_HEADS], a_log=stack("alog")[:, 0, :2 * N_HEADS],
                 d_skip=stack("dvec").reshape(depth, N_HEADS, HEAD_DIM).sum(-1), ssm_norm_w=stack("ssm_nw"),
                 attn_sink=stack("sink")[:, 0, :N_HEADS], norm2_w=stack("n2"), ffn_conv_b=stack("ffn_cb"),
                 final_norm_w=g_final)
    (small_parts,) = _exchange([_pack_small(small)], gather=True, name="gather_small_grads")
    res = _adamw([small_parts], _pack_small(w), _pack_small(m), _pack_small(v), name="adamw_small", tr=small_parts.shape[1])
    unpacked = [_unpack_small(a, w) for a in res]
    for k in _SMALL:
        out[k] = [u[k] for u in unpacked]

    return (loss, grad_x, *[out[k][0] for k in _ORDER], *[out[k][1] for k in _ORDER],
            *[out[k][2] for k in _ORDER], *[out[k][3] for k in _ORDER])
```
